```python
import jax, jax.numpy as jnp
from jax import lax
import numpy as np

D_MODEL = 2048
BATCH = 8
SEQ = 8192
DEPTH = 4

CHUNK = 64
EPS = 1e-6
POOL_WIDTH = 1024
POOL_WINDOWS = (2, 4, 8, 16)
N_POOL_GROUPS = 4
POOL_GROUP = POOL_WIDTH // N_POOL_GROUPS
N_Q_HEADS = 16
N_KV_HEADS = 4
HEAD_DIM = 64
ATTN_WIDTH = N_Q_HEADS * HEAD_DIM
KV_WIDTH = N_KV_HEADS * HEAD_DIM
WINDOW = 128
WINDOW_CHUNKS = WINDOW // CHUNK
CONV_WIDTH = 1024
CONV_KERNEL = 31
N_BRANCHES = 3
IN_SPLITS = (POOL_WIDTH, POOL_WIDTH, ATTN_WIDTH, KV_WIDTH, KV_WIDTH, ATTN_WIDTH,
             CONV_WIDTH, CONV_WIDTH, CONV_WIDTH, N_BRANCHES * D_MODEL)
IN_WIDTH = 2 * POOL_WIDTH + 2 * ATTN_WIDTH + 2 * KV_WIDTH + 3 * CONV_WIDTH + N_BRANCHES * D_MODEL

kernel_name = "hybrid_pool_swa_conformer_parallel"


def rms_norm(x, g):
    xf = x.astype(jnp.float32)
    y = xf * lax.rsqrt(jnp.mean(xf * xf, axis=-1, keepdims=True) + EPS)
    return y.astype(x.dtype) * g


def layer_norm(x, g, b):
    xf = x.astype(jnp.float32)
    mu = jnp.mean(xf, axis=-1, keepdims=True)
    var = jnp.mean(jnp.square(xf - mu), axis=-1, keepdims=True)
    return ((xf - mu) * lax.rsqrt(var + EPS)).astype(x.dtype) * g + b


def multiscale_pool(u, pool_w, pool_scale):
    B, S, _ = u.shape
    ug = u.astype(jnp.float32).reshape(B, S, N_POOL_GROUPS, POOL_GROUP)
    cs = jnp.cumsum(ug, axis=1)
    t = jnp.arange(S)
    means = []
    for gi, w in enumerate(POOL_WINDOWS):
        c_g = cs[:, :, gi]
        prev = jnp.pad(c_g, ((0, 0), (w, 0), (0, 0)))[:, :S]
        cnt = jnp.minimum(t + 1, w).astype(jnp.float32)[None, :, None]
        means.append((c_g - prev) / cnt)
    mixed = (jnp.stack(means, axis=2) - ug).astype(u.dtype)
    y = jnp.einsum('bsgc,gcd->bsgd', mixed, pool_w)
    return y.reshape(B, S, POOL_WIDTH) * pool_scale


def window_attention(q, k, v, sink):
    B, S = q.shape[:2]
    nc = S // CHUNK
    grp = N_Q_HEADS // N_KV_HEADS
    qc = q.reshape(B, nc, CHUNK, N_KV_HEADS, grp, HEAD_DIM)
    pad = WINDOW_CHUNKS * CHUNK

    def band(t):
        tp = jnp.pad(t, ((0, 0), (pad, 0), (0, 0), (0, 0)))
        tp = tp.reshape(B, nc + WINDOW_CHUNKS, CHUNK, N_KV_HEADS, HEAD_DIM)
        return jnp.concatenate([tp[:, i:i + nc] for i in range(WINDOW_CHUNKS + 1)], axis=2)

    kb, vb = band(k), band(v)
    s = jnp.einsum('bnqhgd,bnkhd->bnhgqk', qc, kb).astype(jnp.float32) * (HEAD_DIM ** -0.5)
    n_keys = (WINDOW_CHUNKS + 1) * CHUNK
    key_chunk = jnp.arange(nc)[:, None] - WINDOW_CHUNKS + jnp.arange(n_keys)[None, :] // CHUNK
    valid = (key_chunk >= 0)[None, :, None, None, None, :]
    s = jnp.where(valid, s, -jnp.inf)
    sk = sink.astype(jnp.float32).reshape(1, 1, N_KV_HEADS, grp, 1, 1)
    m = jnp.maximum(jnp.max(s, axis=-1, keepdims=True), sk)
    p = jnp.exp(s - m)
    denom = jnp.sum(p, axis=-1, keepdims=True) + jnp.exp(sk - m)
    p = (p / denom).astype(v.dtype)
    o = jnp.einsum('bnhgqk,bnkhd->bnqhgd', p, vb)
    return o.reshape(B, S, ATTN_WIDTH)


def conformer_conv(a, b, dw, dw_b, ln_g, ln_b, pw):
    g = a * jax.nn.sigmoid(b)
    gp = jnp.pad(g, ((0, 0), (CONV_KERNEL - 1, 0), (0, 0)))
    y = lax.conv_general_dilated(gp, dw[:, None, :], window_strides=(1,), padding='VALID',
                                 dimension_numbers=('NWC', 'WIO', 'NWC'),
                                 feature_group_count=CONV_WIDTH) + dw_b
    y = jax.nn.silu(layer_norm(y, ln_g, ln_b))
    return y @ pw


def _fwd_setup_inputs(seed: int = 0) -> dict:
    key = jax.random.key(seed)
    ks = jax.random.split(key, 20)
    f32 = jnp.float32

    def nrm(k, shape, scale):
        return jax.random.normal(k, shape, f32) * scale

    L, D = DEPTH, D_MODEL
    return {
        "x": nrm(ks[0], (BATCH, SEQ, D), 1.0),
        "c": nrm(ks[1], (BATCH, D), 1.0),
        "norm_g": 1.0 + nrm(ks[2], (L, D), 0.05),
        "w_ada": nrm(ks[3], (L, D, 3 * D), 0.5 * D ** -0.5),
        "b_ada": nrm(ks[4], (L, 3 * D), 0.01),
        "w_in": nrm(ks[5], (L, D, IN_WIDTH), D ** -0.5),
        "pool_w": nrm(ks[6], (L, N_POOL_GROUPS, POOL_GROUP, POOL_GROUP), POOL_GROUP ** -0.5),
        "pool_scale": 1.0 + nrm(ks[7], (L, POOL_WIDTH), 0.1),
        "attn_sink": nrm(ks[8], (L, N_Q_HEADS), 1.0),
        "conv_dw": nrm(ks[9], (L, CONV_KERNEL, CONV_WIDTH), CONV_KERNEL ** -0.5),
        "conv_dw_b": nrm(ks[10], (L, CONV_WIDTH), 0.01),
        "conv_ln_g": 1.0 + nrm(ks[11], (L, CONV_WIDTH), 0.05),
        "conv_ln_b": nrm(ks[12], (L, CONV_WIDTH), 0.01),
        "conv_pw": nrm(ks[13], (L, CONV_WIDTH, CONV_WIDTH), CONV_WIDTH ** -0.5),
        "w_branch_pool": nrm(ks[14], (L, POOL_WIDTH, D), POOL_WIDTH ** -0.5),
        "w_branch_attn": nrm(ks[15], (L, ATTN_WIDTH, D), ATTN_WIDTH ** -0.5),
        "w_branch_conv": nrm(ks[16], (L, CONV_WIDTH, D), CONV_WIDTH ** -0.5),
        "w_out": nrm(ks[17], (L, D, D), D ** -0.5),
        "final_g": 1.0 + nrm(ks[18], (D,), 0.05),
    }


def _fwd_reference(x, c, norm_g, w_ada, b_ada, w_in, pool_w, pool_scale, attn_sink, conv_dw, conv_dw_b,
              conv_ln_g, conv_ln_b, conv_pw, w_branch_pool, w_branch_attn, w_branch_conv, w_out,
              final_g):
    B, S, _ = x.shape
    split_idx = np.cumsum(IN_SPLITS)[:-1].tolist()
    c_act = jax.nn.silu(c)
    for l in range(DEPTH):
        mod = c_act @ w_ada[l] + b_ada[l]
        shift, scale, gate = jnp.split(mod, 3, axis=-1)
        h = rms_norm(x, norm_g[l]) * (1 + scale[:, None]) + shift[:, None]
        proj = h @ w_in[l]
        pool_u, pool_z, q, k, v, attn_z, conv_a, conv_b, conv_z, gates = jnp.split(proj, split_idx, axis=-1)
        y_pool = multiscale_pool(pool_u, pool_w[l], pool_scale[l]) * jax.nn.silu(pool_z)
        y_attn = window_attention(q.reshape(B, S, N_Q_HEADS, HEAD_DIM),
                                  k.reshape(B, S, N_KV_HEADS, HEAD_DIM),
                                  v.reshape(B, S, N_KV_HEADS, HEAD_DIM),
                                  attn_sink[l]) * jax.nn.silu(attn_z)
        y_conv = conformer_conv(conv_a, conv_b, conv_dw[l], conv_dw_b[l], conv_ln_g[l], conv_ln_b[l],
                                conv_pw[l]) * jax.nn.silu(conv_z)
        g_pool, g_attn, g_conv = jnp.split(jax.nn.sigmoid(gates), 3, axis=-1)
        merged = (g_pool * (y_pool @ w_branch_pool[l])
                  + g_attn * (y_attn @ w_branch_attn[l])
                  + g_conv * (y_conv @ w_branch_conv[l]))
        x = x + gate[:, None] * (merged @ w_out[l])
    return rms_norm(x, final_g)


import jax as _jax
import jax.numpy as _jnp

TWIN_FORMAT = 'train_step'
FWD_PARAMS = ['x', 'c', 'norm_g', 'w_ada', 'b_ada', 'w_in', 'pool_w', 'pool_scale', 'attn_sink', 'conv_dw', 'conv_dw_b', 'conv_ln_g', 'conv_ln_b', 'conv_pw', 'w_branch_pool', 'w_branch_attn', 'w_branch_conv', 'w_out', 'final_g']
TWIN_WEIGHTS = ['norm_g', 'w_ada', 'b_ada', 'w_in', 'pool_w', 'pool_scale', 'attn_sink', 'conv_dw', 'conv_dw_b', 'conv_ln_g', 'conv_ln_b', 'conv_pw', 'w_branch_pool', 'w_branch_attn', 'w_branch_conv', 'w_out', 'final_g']
TWIN_DIFF_INPUT = 'x'
TWIN_INPUTS = ['x', 'c', 'norm_g', 'w_ada', 'b_ada', 'w_in', 'pool_w', 'pool_scale', 'attn_sink', 'conv_dw', 'conv_dw_b', 'conv_ln_g', 'conv_ln_b', 'conv_pw', 'w_branch_pool', 'w_branch_attn', 'w_branch_conv', 'w_out', 'final_g', 'loss_target', 'm_norm_g', 'm_w_ada', 'm_b_ada', 'm_w_in', 'm_pool_w', 'm_pool_scale', 'm_attn_sink', 'm_conv_dw', 'm_conv_dw_b', 'm_conv_ln_g', 'm_conv_ln_b', 'm_conv_pw', 'm_w_branch_pool', 'm_w_branch_attn', 'm_w_branch_conv', 'm_w_out', 'm_final_g', 'v_norm_g', 'v_w_ada', 'v_b_ada', 'v_w_in', 'v_pool_w', 'v_pool_scale', 'v_attn_sink', 'v_conv_dw', 'v_conv_dw_b', 'v_conv_ln_g', 'v_conv_ln_b', 'v_conv_pw', 'v_w_branch_pool', 'v_w_branch_attn', 'v_w_branch_conv', 'v_w_out', 'v_final_g']
TWIN_OUTPUTS = ['loss', 'grad_x', 'grad_norm_g', 'grad_w_ada', 'grad_b_ada', 'grad_w_in', 'grad_pool_w', 'grad_pool_scale', 'grad_attn_sink', 'grad_conv_dw', 'grad_conv_dw_b', 'grad_conv_ln_g', 'grad_conv_ln_b', 'grad_conv_pw', 'grad_w_branch_pool', 'grad_w_branch_attn', 'grad_w_branch_conv', 'grad_w_out', 'grad_final_g', 'delta_norm_g', 'delta_w_ada', 'delta_b_ada', 'delta_w_in', 'delta_pool_w', 'delta_pool_scale', 'delta_attn_sink', 'delta_conv_dw', 'delta_conv_dw_b', 'delta_conv_ln_g', 'delta_conv_ln_b', 'delta_conv_pw', 'delta_w_branch_pool', 'delta_w_branch_attn', 'delta_w_branch_conv', 'delta_w_out', 'delta_final_g', 'new_m_norm_g', 'new_m_w_ada', 'new_m_b_ada', 'new_m_w_in', 'new_m_pool_w', 'new_m_pool_scale', 'new_m_attn_sink', 'new_m_conv_dw', 'new_m_conv_dw_b', 'new_m_conv_ln_g', 'new_m_conv_ln_b', 'new_m_conv_pw', 'new_m_w_branch_pool', 'new_m_w_branch_attn', 'new_m_w_branch_conv', 'new_m_w_out', 'new_m_final_g', 'new_v_norm_g', 'new_v_w_ada', 'new_v_b_ada', 'new_v_w_in', 'new_v_pool_w', 'new_v_pool_scale', 'new_v_attn_sink', 'new_v_conv_dw', 'new_v_conv_dw_b', 'new_v_conv_ln_g', 'new_v_conv_ln_b', 'new_v_conv_pw', 'new_v_w_branch_pool', 'new_v_w_branch_attn', 'new_v_w_branch_conv', 'new_v_w_out', 'new_v_final_g']
TWIN_LEAF_KINDS = {'loss': 'loss', 'grad_x': 'grad_x', 'grad_norm_g': 'grad_w', 'grad_w_ada': 'grad_w', 'grad_b_ada': 'grad_w', 'grad_w_in': 'grad_w', 'grad_pool_w': 'grad_w', 'grad_pool_scale': 'grad_w', 'grad_attn_sink': 'grad_w', 'grad_conv_dw': 'grad_w', 'grad_conv_dw_b': 'grad_w', 'grad_conv_ln_g': 'grad_w', 'grad_conv_ln_b': 'grad_w', 'grad_conv_pw': 'grad_w', 'grad_w_branch_pool': 'grad_w', 'grad_w_branch_attn': 'grad_w', 'grad_w_branch_conv': 'grad_w', 'grad_w_out': 'grad_w', 'grad_final_g': 'grad_w', 'delta_norm_g': 'delta_w', 'delta_w_ada': 'delta_w', 'delta_b_ada': 'delta_w', 'delta_w_in': 'delta_w', 'delta_pool_w': 'delta_w', 'delta_pool_scale': 'delta_w', 'delta_attn_sink': 'delta_w', 'delta_conv_dw': 'delta_w', 'delta_conv_dw_b': 'delta_w', 'delta_conv_ln_g': 'delta_w', 'delta_conv_ln_b': 'delta_w', 'delta_conv_pw': 'delta_w', 'delta_w_branch_pool': 'delta_w', 'delta_w_branch_attn': 'delta_w', 'delta_w_branch_conv': 'delta_w', 'delta_w_out': 'delta_w', 'delta_final_g': 'delta_w', 'new_m_norm_g': 'new_m', 'new_m_w_ada': 'new_m', 'new_m_b_ada': 'new_m', 'new_m_w_in': 'new_m', 'new_m_pool_w': 'new_m', 'new_m_pool_scale': 'new_m', 'new_m_attn_sink': 'new_m', 'new_m_conv_dw': 'new_m', 'new_m_conv_dw_b': 'new_m', 'new_m_conv_ln_g': 'new_m', 'new_m_conv_ln_b': 'new_m', 'new_m_conv_pw': 'new_m', 'new_m_w_branch_pool': 'new_m', 'new_m_w_branch_attn': 'new_m', 'new_m_w_branch_conv': 'new_m', 'new_m_w_out': 'new_m', 'new_m_final_g': 'new_m', 'new_v_norm_g': 'new_v', 'new_v_w_ada': 'new_v', 'new_v_b_ada': 'new_v', 'new_v_w_in': 'new_v', 'new_v_pool_w': 'new_v', 'new_v_pool_scale': 'new_v', 'new_v_attn_sink': 'new_v', 'new_v_conv_dw': 'new_v', 'new_v_conv_dw_b': 'new_v', 'new_v_conv_ln_g': 'new_v', 'new_v_conv_ln_b': 'new_v', 'new_v_conv_pw': 'new_v', 'new_v_w_branch_pool': 'new_v', 'new_v_w_branch_attn': 'new_v', 'new_v_w_branch_conv': 'new_v', 'new_v_w_out': 'new_v', 'new_v_final_g': 'new_v'}


def _forward(args):
    return _fwd_reference(*[args[k] for k in FWD_PARAMS])


def _output_shape():
    def fwd():
        inp = _fwd_setup_inputs(0)
        return _fwd_reference(*[inp[k] for k in FWD_PARAMS])
    out = _jax.eval_shape(fwd)
    return out.shape, out.dtype

N_MICROBATCH = 1
ADAM_LR = 0.001
ADAM_B1 = 0.9
ADAM_B2 = 0.999
ADAM_EPS = 1e-08
ADAM_WD = 0.01
ADAM_STEP = 10
PER_EXAMPLE_BATCH_AXIS = {'x': 0, 'c': 0, 'loss_target': 0}
SHARED_INPUTS = []
_WEIGHT_DTYPES = {'norm_g': _jnp.float32, 'w_ada': _jnp.float32, 'b_ada': _jnp.float32, 'w_in': _jnp.float32, 'pool_w': _jnp.float32, 'pool_scale': _jnp.float32, 'attn_sink': _jnp.float32, 'conv_dw': _jnp.float32, 'conv_dw_b': _jnp.float32, 'conv_ln_g': _jnp.float32, 'conv_ln_b': _jnp.float32, 'conv_pw': _jnp.float32, 'w_branch_pool': _jnp.float32, 'w_branch_attn': _jnp.float32, 'w_branch_conv': _jnp.float32, 'w_out': _jnp.float32, 'final_g': _jnp.float32}
MOMENT_SCALE = {'norm_g': 2.281437e-02, 'w_ada': 2.317671e-02, 'b_ada': 4.164238e-02, 'w_in': 9.165486e-03, 'pool_w': 1.756508e-02, 'pool_scale': 1.747156e-02, 'attn_sink': 4.202253e-04, 'conv_dw': 1.193279e-02, 'conv_dw_b': 2.319937e-02, 'conv_ln_g': 1.421245e-02, 'conv_ln_b': 1.221345e-02, 'conv_pw': 1.157743e-02, 'w_branch_pool': 1.242360e-02, 'w_branch_attn': 4.613973e-03, 'w_branch_conv': 8.139488e-03, 'w_out': 1.548388e-02, 'final_g': 3.202572e+01}


def _to_microbatches(a, axis):
    t = _jnp.moveaxis(a, axis, 0)
    t = t.reshape((N_MICROBATCH, t.shape[0] // N_MICROBATCH) + t.shape[1:])
    return _jnp.moveaxis(t, 1, axis + 1)


def setup_inputs(seed: int = 0) -> dict:
    inp = _fwd_setup_inputs(seed)
    key = _jax.random.fold_in(_jax.random.key(seed), 7919)
    shape, _ = _output_shape()
    out = dict(inp)
    out["loss_target"] = _jax.random.normal(_jax.random.fold_in(key, 0), shape, _jnp.float32)
    for i, name in enumerate(TWIN_WEIGHTS):
        w = inp[name].astype(_jnp.float32)
        if MOMENT_SCALE is None:
            s = _jnp.sqrt(_jnp.mean(_jnp.square(w)) + 1e-30)
        else:
            s = MOMENT_SCALE[name]
        km, kv = _jax.random.split(_jax.random.fold_in(key, i + 1))
        out[name] = w
        out["m_" + name] = s * _jax.random.normal(km, w.shape, _jnp.float32)
        out["v_" + name] = (s * s) * _jax.random.uniform(kv, w.shape, _jnp.float32, 0.5, 1.5)
    if N_MICROBATCH > 1:
        for name, axis in PER_EXAMPLE_BATCH_AXIS.items():
            out[name] = _to_microbatches(out[name], axis)
    return {'x': out['x'], 'c': out['c'], 'norm_g': out['norm_g'], 'w_ada': out['w_ada'], 'b_ada': out['b_ada'], 'w_in': out['w_in'], 'pool_w': out['pool_w'], 'pool_scale': out['pool_scale'], 'attn_sink': out['attn_sink'], 'conv_dw': out['conv_dw'], 'conv_dw_b': out['conv_dw_b'], 'conv_ln_g': out['conv_ln_g'], 'conv_ln_b': out['conv_ln_b'], 'conv_pw': out['conv_pw'], 'w_branch_pool': out['w_branch_pool'], 'w_branch_attn': out['w_branch_attn'], 'w_branch_conv': out['w_branch_conv'], 'w_out': out['w_out'], 'final_g': out['final_g'], 'loss_target': out['loss_target'], 'm_norm_g': out['m_norm_g'], 'm_w_ada': out['m_w_ada'], 'm_b_ada': out['m_b_ada'], 'm_w_in': out['m_w_in'], 'm_pool_w': out['m_pool_w'], 'm_pool_scale': out['m_pool_scale'], 'm_attn_sink': out['m_attn_sink'], 'm_conv_dw': out['m_conv_dw'], 'm_conv_dw_b': out['m_conv_dw_b'], 'm_conv_ln_g': out['m_conv_ln_g'], 'm_conv_ln_b': out['m_conv_ln_b'], 'm_conv_pw': out['m_conv_pw'], 'm_w_branch_pool': out['m_w_branch_pool'], 'm_w_branch_attn': out['m_w_branch_attn'], 'm_w_branch_conv': out['m_w_branch_conv'], 'm_w_out': out['m_w_out'], 'm_final_g': out['m_final_g'], 'v_norm_g': out['v_norm_g'], 'v_w_ada': out['v_w_ada'], 'v_b_ada': out['v_b_ada'], 'v_w_in': out['v_w_in'], 'v_pool_w': out['v_pool_w'], 'v_pool_scale': out['v_pool_scale'], 'v_attn_sink': out['v_attn_sink'], 'v_conv_dw': out['v_conv_dw'], 'v_conv_dw_b': out['v_conv_dw_b'], 'v_conv_ln_g': out['v_conv_ln_g'], 'v_conv_ln_b': out['v_conv_ln_b'], 'v_conv_pw': out['v_conv_pw'], 'v_w_branch_pool': out['v_w_branch_pool'], 'v_w_branch_attn': out['v_w_branch_attn'], 'v_w_branch_conv': out['v_w_branch_conv'], 'v_w_out': out['v_w_out'], 'v_final_g': out['v_final_g']}


def _loss(weights, diff, rest, loss_target):
    with _jax.named_scope("forward"):
        args = {**rest, TWIN_DIFF_INPUT: diff, **{k: w.astype(_WEIGHT_DTYPES[k]) for k, w in weights.items()}}
        y = _forward(args)
    with _jax.named_scope("loss_head"):
        err = _jnp.square(y.astype(_jnp.float32) - loss_target)
        return 0.5 * _jnp.sum(_jnp.mean(err, axis=-1)) if err.ndim else 0.5 * err


def _adamw(w, g, m, v):
    m = ADAM_B1 * m + (1.0 - ADAM_B1) * g
    v = ADAM_B2 * v + (1.0 - ADAM_B2) * _jnp.square(g)
    m_hat = m / (1.0 - ADAM_B1 ** ADAM_STEP)
    v_hat = v / (1.0 - ADAM_B2 ** ADAM_STEP)
    delta = -ADAM_LR * (m_hat / (_jnp.sqrt(v_hat) + ADAM_EPS) + ADAM_WD * w)
    return delta, m, v


def reference(x, c, norm_g, w_ada, b_ada, w_in, pool_w, pool_scale, attn_sink, conv_dw, conv_dw_b, conv_ln_g, conv_ln_b, conv_pw, w_branch_pool, w_branch_attn, w_branch_conv, w_out, final_g, loss_target, m_norm_g, m_w_ada, m_b_ada, m_w_in, m_pool_w, m_pool_scale, m_attn_sink, m_conv_dw, m_conv_dw_b, m_conv_ln_g, m_conv_ln_b, m_conv_pw, m_w_branch_pool, m_w_branch_attn, m_w_branch_conv, m_w_out, m_final_g, v_norm_g, v_w_ada, v_b_ada, v_w_in, v_pool_w, v_pool_scale, v_attn_sink, v_conv_dw, v_conv_dw_b, v_conv_ln_g, v_conv_ln_b, v_conv_pw, v_w_branch_pool, v_w_branch_attn, v_w_branch_conv, v_w_out, v_final_g):
    given = dict(x=x, c=c, norm_g=norm_g, w_ada=w_ada, b_ada=b_ada, w_in=w_in, pool_w=pool_w, pool_scale=pool_scale, attn_sink=attn_sink, conv_dw=conv_dw, conv_dw_b=conv_dw_b, conv_ln_g=conv_ln_g, conv_ln_b=conv_ln_b, conv_pw=conv_pw, w_branch_pool=w_branch_pool, w_branch_attn=w_branch_attn, w_branch_conv=w_branch_conv, w_out=w_out, final_g=final_g, loss_target=loss_target, m_norm_g=m_norm_g, m_w_ada=m_w_ada, m_b_ada=m_b_ada, m_w_in=m_w_in, m_pool_w=m_pool_w, m_pool_scale=m_pool_scale, m_attn_sink=m_attn_sink, m_conv_dw=m_conv_dw, m_conv_dw_b=m_conv_dw_b, m_conv_ln_g=m_conv_ln_g, m_conv_ln_b=m_conv_ln_b, m_conv_pw=m_conv_pw, m_w_branch_pool=m_w_branch_pool, m_w_branch_attn=m_w_branch_attn, m_w_branch_conv=m_w_branch_conv, m_w_out=m_w_out, m_final_g=m_final_g, v_norm_g=v_norm_g, v_w_ada=v_w_ada, v_b_ada=v_b_ada, v_w_in=v_w_in, v_pool_w=v_pool_w, v_pool_scale=v_pool_scale, v_attn_sink=v_attn_sink, v_conv_dw=v_conv_dw, v_conv_dw_b=v_conv_dw_b, v_conv_ln_g=v_conv_ln_g, v_conv_ln_b=v_conv_ln_b, v_conv_pw=v_conv_pw, v_w_branch_pool=v_w_branch_pool, v_w_branch_attn=v_w_branch_attn, v_w_branch_conv=v_w_branch_conv, v_w_out=v_w_out, v_final_g=v_final_g)
    weights = {n: given[n] for n in TWIN_WEIGHTS}
    shared = {n: given[n] for n in SHARED_INPUTS}
    per_example = {n: given[n] for n in ['x', 'c']}
    grad_fn = _jax.value_and_grad(_loss, argnums=(0, 1))

    def one_microbatch(ex, loss_target):
        ex = dict(ex)
        diff = ex.pop(TWIN_DIFF_INPUT)
        return grad_fn(weights, diff, {**shared, **ex}, loss_target)

    if N_MICROBATCH == 1:
        loss, (grad_w, grad_x) = one_microbatch(per_example, given["loss_target"])
    else:
        def body(carry, xs):
            loss_sum, grad_sum = carry
            l_k, (gw_k, gx_k) = one_microbatch(xs[0], xs[1])
            with _jax.named_scope("update"):
                return (loss_sum + l_k, _jax.tree.map(_jnp.add, grad_sum, gw_k)), gx_k

        init = (_jnp.zeros((), _jnp.float32), _jax.tree.map(_jnp.zeros_like, weights))
        (loss, grad_w), grad_x = _jax.lax.scan(body, init, (per_example, given["loss_target"]))
    with _jax.named_scope("update"):
        delta_w, new_m, new_v = {}, {}, {}
        for n in TWIN_WEIGHTS:
            delta_w[n], new_m[n], new_v[n] = _adamw(weights[n], grad_w[n], given["m_" + n], given["v_" + n])
    return (loss, grad_x, *[grad_w[n] for n in TWIN_WEIGHTS], *[delta_w[n] for n in TWIN_WEIGHTS],
            *[new_m[n] for n in TWIN_WEIGHTS], *[new_v[n] for n in TWIN_WEIGHTS])
```

```python
import functools

import jax
import jax.numpy as jnp
from jax import lax
from jax.experimental import pallas as pl
from jax.experimental.pallas import tpu as pltpu

F32, BF16 = jnp.float32, jnp.bfloat16
MESH = pl.DeviceIdType.MESH
ANY = pl.BlockSpec(memory_space=pl.ANY)

N_DEV = 8
D = 2048
DEPTH = 4
EPS = 1e-6
IN_WIDTH = 13824
POOL_WINDOWS = (2, 4, 8, 16)
POOL_GROUP = 256
POOL_HALO = 16
CONV_K = 31
CONV_HALO = 32
N_HEADS, N_KV, HEAD_DIM = 16, 4, 64
ATTN_TQ = 256
ATTN_BACK = 128
LANE = 128

CB_U, CB_Z, CB_Q, CB_AZ, CB_CA, CB_CB, CB_CZ, CB_GP, CB_GA, CB_GC = 0, 1, 2, 3, 4, 5, 6, 7, 9, 11
COL_K, COL_V = 13312, 13568

ADAM_LR, ADAM_B1, ADAM_B2, ADAM_EPS, ADAM_WD, ADAM_STEP = 0.001, 0.9, 0.999, 1e-08, 0.01, 10


def _cp(sem=None, vmem=None):
    return pltpu.CompilerParams(dimension_semantics=sem, vmem_limit_bytes=vmem)


def _sig(x):
    return jax.nn.sigmoid(x)


def _silu(x):
    return x * _sig(x)


def _dsilu(x):
    s = _sig(x)
    return s * (1.0 + x * (1.0 - s))


def _dot(a, b):
    return jnp.dot(a, b, preferred_element_type=F32)


def _dot_tn(a, b):
    return lax.dot_general(a, b, (((0,), (0,)), ((), ())), preferred_element_type=F32)


def _dot_nt(a, b):
    return lax.dot_general(a, b, (((1,), (1,)), ((), ())), preferred_element_type=F32)


def _full(shape):
    n = len(shape)
    return pl.BlockSpec(shape, lambda *_: (0,) * n)


def _my_pos():
    return lax.axis_index("x"), lax.axis_index("y"), lax.axis_index("c")


def _all_gather(xs, name):
    n = len(xs)

    def body(*refs):
        x_refs, o_refs = refs[:n], refs[n:2 * n]
        send_sems, recv_sems, local_sems = refs[2 * n:]
        x, y, c = _my_pos()
        sibling = (x, y, 1 - c)
        chips = [(1 - x, y), (x, 1 - y), (1 - x, 1 - y)]
        me = 4 * x + 2 * y + c

        def slot(px, py, pc):
            return 4 * px + 2 * py + pc

        def copy(t, k, block, to, src=None):
            dst = o_refs[t].at[block]
            return pltpu.make_async_remote_copy(
                src_ref=dst if src is None else src, dst_ref=dst,
                send_sem=send_sems.at[t, k], recv_sem=recv_sems.at[t, k],
                device_id=to, device_id_type=MESH)

        mine = [pltpu.make_async_copy(x_refs[t], o_refs[t].at[me], local_sems.at[t]) for t in range(n)]
        for cp in mine:
            cp.start()
        first = []
        for t in range(n):
            first.append(copy(t, 0, me, sibling, src=x_refs[t]))
            for j, chip in enumerate(chips):
                first.append(copy(t, 1 + j, me, (*chip, c), src=x_refs[t]))
        for cp in first:
            cp.start()
        passed = []
        for j, chip in enumerate(chips):
            for t in range(n):
                copy(t, 1 + j, slot(*chip, c), (x, y, c)).wait_recv()
                fwd = copy(t, 4 + j, slot(*chip, c), sibling)
                fwd.start()
                passed.append(fwd)
        for t in range(n):
            copy(t, 0, slot(x, y, 1 - c), (x, y, c)).wait_recv()
            for j, chip in enumerate(chips):
                copy(t, 4 + j, slot(*chip, 1 - c), (x, y, c)).wait_recv()
        for cp in first + passed:
            cp.wait_send()
        for cp in mine:
            cp.wait()

    return pl.pallas_call(
        body, name=name,
        out_shape=[jax.ShapeDtypeStruct((N_DEV,) + a.shape, a.dtype) for a in xs],
        in_specs=[ANY] * n, out_specs=[ANY] * n,
        scratch_shapes=[pltpu.SemaphoreType.DMA((n, 7)), pltpu.SemaphoreType.DMA((n, 7)),
                        pltpu.SemaphoreType.DMA((n,))],
    )(*xs)


def _exchange_sibling(gs, name):
    n = len(gs)

    def body(*refs):
        g_refs, r_refs = refs[:n], refs[n:2 * n]
        send_sems, recv_sems = refs[2 * n:]
        x, y, c = _my_pos()
        copies = []
        for t in range(n):
            for ch in range(4):
                copies.append(pltpu.make_async_remote_copy(
                    src_ref=g_refs[t].at[2 * ch + (1 - c)], dst_ref=r_refs[t].at[ch],
                    send_sem=send_sems.at[t, ch], recv_sem=recv_sems.at[t, ch],
                    device_id=(x, y, 1 - c), device_id_type=MESH))
        for cp in copies:
            cp.start()
        for cp in copies:
            cp.wait()

    return pl.pallas_call(
        body, name=name,
        out_shape=[jax.ShapeDtypeStruct((4,) + g.shape[1:], g.dtype) for g in gs],
        in_specs=[ANY] * n, out_specs=[ANY] * n,
        scratch_shapes=[pltpu.SemaphoreType.DMA((n, 4)), pltpu.SemaphoreType.DMA((n, 4))],
    )(*gs)


def _exchange_chips(hs, name):
    n = len(hs)

    def body(*refs):
        h_refs, r_refs = refs[:n], refs[n:2 * n]
        send_sems, recv_sems = refs[2 * n:]
        x, y, c = _my_pos()
        peers = [(x, 1 - y), (1 - x, y), (1 - x, 1 - y)]
        copies = []
        for t in range(n):
            for r, (px, py) in enumerate(peers):
                copies.append(pltpu.make_async_remote_copy(
                    src_ref=h_refs[t].at[2 * px + py], dst_ref=r_refs[t].at[r],
                    send_sem=send_sems.at[t, r], recv_sem=recv_sems.at[t, r],
                    device_id=(px, py, c), device_id_type=MESH))
        for cp in copies:
            cp.start()
        for cp in copies:
            cp.wait()

    return pl.pallas_call(
        body, name=name,
        out_shape=[jax.ShapeDtypeStruct((3,) + h.shape[1:], h.dtype) for h in hs],
        in_specs=[ANY] * n, out_specs=[ANY] * n,
        scratch_shapes=[pltpu.SemaphoreType.DMA((n, 3)), pltpu.SemaphoreType.DMA((n, 3))],
    )(*hs)


def _mm(a, b, out_dtype, tm, tn, tk=None, name="mm", vmem=None):
    M, K = a.shape
    _, N = b.shape
    tk = K if tk is None else tk
    nk = K // tk
    assert M % tm == 0 and N % tn == 0 and K % tk == 0

    def body(a_ref, b_ref, o_ref, *acc):
        prod = _dot(a_ref[...].astype(BF16), b_ref[...])
        if nk == 1:
            o_ref[...] = prod.astype(out_dtype)
            return
        acc_ref = acc[0] if acc else o_ref
        k = pl.program_id(2)

        @pl.when(k == 0)
        def _():
            acc_ref[...] = prod

        @pl.when(k > 0)
        def _():
            acc_ref[...] += prod

        if acc:
            @pl.when(k == nk - 1)
            def _():
                o_ref[...] = acc_ref[...].astype(out_dtype)

    scratch = [pltpu.VMEM((tm, tn), F32)] if (nk > 1 and out_dtype != F32) else []
    return pl.pallas_call(
        body, name=name, grid=(N // tn, M // tm, nk),
        in_specs=[pl.BlockSpec((tm, tk), lambda j, i, k: (i, k)), pl.BlockSpec((tk, tn), lambda j, i, k: (k, j))],
        out_specs=pl.BlockSpec((tm, tn), lambda j, i, k: (i, j)),
        out_shape=jax.ShapeDtypeStruct((M, N), out_dtype), scratch_shapes=scratch,
        compiler_params=_cp(("parallel", "parallel", "arbitrary"), vmem),
    )(a, b)


def _mm_tn(a, b, tm, tn, ts, name="mm_tn", vmem=None):
    S, Ka = a.shape
    _, N = b.shape
    assert Ka % tm == 0 and N % tn == 0 and S % ts == 0

    def body(a_ref, b_ref, o_ref):
        prod = _dot_tn(a_ref[...].astype(BF16), b_ref[...].astype(BF16))
        k = pl.program_id(2)

        @pl.when(k == 0)
        def _():
            o_ref[...] = prod

        @pl.when(k > 0)
        def _():
            o_ref[...] += prod

    return pl.pallas_call(
        body, name=name, grid=(Ka // tm, N // tn, S // ts),
        in_specs=[pl.BlockSpec((ts, tm), lambda i, j, k: (k, i)), pl.BlockSpec((ts, tn), lambda i, j, k: (k, j))],
        out_specs=pl.BlockSpec((tm, tn), lambda i, j, k: (i, j)),
        out_shape=jax.ShapeDtypeStruct((Ka, N), F32),
        compiler_params=_cp(("parallel", "parallel", "arbitrary"), vmem),
    )(a, b)


def _mod_fwd(c_all, w_ada):
    L, _, n = w_ada.shape

    def body(c_ref, w_ref, o_ref):
        ca = _silu(c_ref[...])
        o_ref[0] = jnp.dot(ca, w_ref[0], preferred_element_type=F32, precision=lax.Precision.HIGHEST)

    return pl.pallas_call(
        body, name="mod_fwd", grid=(L,),
        in_specs=[_full((N_DEV, D)), pl.BlockSpec((1, D, n), lambda l: (l, 0, 0))],
        out_specs=pl.BlockSpec((1, N_DEV, n), lambda l: (l, 0, 0)),
        out_shape=jax.ShapeDtypeStruct((L, N_DEV, n), F32),
        compiler_params=_cp(("parallel",)),
    )(c_all, w_ada)


def _adam_math(g, w, m, v):
    m2 = ADAM_B1 * m + (1.0 - ADAM_B1) * g
    v2 = ADAM_B2 * v + (1.0 - ADAM_B2) * (g * g)
    m_hat = m2 / (1.0 - ADAM_B1 ** ADAM_STEP)
    v_hat = v2 / (1.0 - ADAM_B2 ** ADAM_STEP)
    delta = -ADAM_LR * (m_hat / (jnp.sqrt(v_hat) + ADAM_EPS) + ADAM_WD * w)
    return delta, m2, v2


def _wada_bwd(c_all_t, dmod, w, m, v, tr=256):
    L, _, n = w.shape

    def body(c_ref, d_ref, w_ref, m_ref, v_ref, g_ref, dl_ref, m2_ref, v2_ref):
        ca = _silu(c_ref[...])
        dm = d_ref[0]
        g = ca[:, 0:1] * dm[0:1, :]
        for b in range(1, N_DEV):
            g = g + ca[:, b:b + 1] * dm[b:b + 1, :]
        delta, m2, v2 = _adam_math(g, w_ref[0], m_ref[0], v_ref[0])
        g_ref[0], dl_ref[0], m2_ref[0], v2_ref[0] = g, delta, m2, v2

    blk = pl.BlockSpec((1, tr, n), lambda l, i: (l, i, 0))
    return pl.pallas_call(
        body, name="wada_bwd", grid=(L, D // tr),
        in_specs=[pl.BlockSpec((tr, N_DEV), lambda l, i: (i, 0)), pl.BlockSpec((1, N_DEV, n), lambda l, i: (l, 0, 0)),
                  blk, blk, blk],
        out_specs=[blk] * 4, out_shape=[jax.ShapeDtypeStruct(w.shape, F32)] * 4,
        compiler_params=_cp(("parallel", "parallel")),
    )(c_all_t, dmod, w, m, v)


def _norm_mod(x, g, scale, shift, tb=256):
    S = x.shape[0]

    def body(x_ref, g_ref, sc_ref, sh_ref, h_ref):
        xv = x_ref[...]
        r = lax.rsqrt(jnp.mean(xv * xv, axis=-1, keepdims=True) + EPS)
        h_ref[...] = (xv * r * (g_ref[...] * (1.0 + sc_ref[...])) + sh_ref[...]).astype(BF16)

    row = pl.BlockSpec((tb, D), lambda i: (i, 0))
    vec = _full((1, D))
    return pl.pallas_call(
        body, name="norm_mod", grid=(S // tb,), in_specs=[row, vec, vec, vec], out_specs=row,
        out_shape=jax.ShapeDtypeStruct((S, D), BF16), compiler_params=_cp(("parallel",)),
    )(x, g, scale, shift)


def _pool_mix(u_ref, uh_ref, ubuf, i, tb):
    H = POOL_HALO
    ubuf[H:, :] = u_ref[...].astype(F32)
    ubuf[:H, :] = jnp.where(i > 0, uh_ref[...].astype(F32), 0.0)
    t = i * tb + lax.broadcasted_iota(jnp.int32, (tb, 1), 0)
    mixed = []
    for g, w in enumerate(POOL_WINDOWS):
        cs = slice(g * POOL_GROUP, (g + 1) * POOL_GROUP)
        cur = ubuf[H:, cs]
        acc = cur
        for j in range(1, w):
            acc = acc + ubuf[pl.ds(H - j, tb), cs]
        cnt = jnp.minimum(t + 1, w).astype(F32)
        mixed.append(acc / cnt - cur)
    return mixed


def _pool_specs(tb):
    H = POOL_HALO
    u = pl.BlockSpec((tb, 1024), lambda i: (i, CB_U))
    uh = pl.BlockSpec((H, 1024), lambda i: (jnp.maximum(i * (tb // H) - 1, 0), CB_U))
    z = pl.BlockSpec((tb, 1024), lambda i: (i, CB_Z))
    return u, uh, z


def _pool_fwd(proj, pool_w, pool_scale, tb=256):
    S = proj.shape[0]

    def body(u_ref, uh_ref, z_ref, w_ref, sc_ref, y_ref, ubuf):
        i = pl.program_id(0)
        mixed = _pool_mix(u_ref, uh_ref, ubuf, i, tb)
        m = jnp.concatenate([_dot(mixed[g].astype(BF16), w_ref[g]) for g in range(4)], axis=1)
        y_ref[...] = (m * sc_ref[...] * _silu(z_ref[...].astype(F32))).astype(BF16)

    u, uh, z = _pool_specs(tb)
    return pl.pallas_call(
        body, name="pool_fwd", grid=(S // tb,),
        in_specs=[u, uh, z, _full((4, 256, 256)), _full((1, 1024))],
        out_specs=pl.BlockSpec((tb, 1024), lambda i: (i, 0)),
        out_shape=jax.ShapeDtypeStruct((S, 1024), BF16),
        scratch_shapes=[pltpu.VMEM((tb + POOL_HALO, 1024), F32)],
        compiler_params=_cp(("parallel",)),
    )(proj, proj, proj, pool_w, pool_scale)


def _attn_mask(i):
    TQ, NK = ATTN_TQ, ATTN_TQ + ATTN_BACK
    qc = lax.broadcasted_iota(jnp.int32, (TQ, NK), 0) // 64
    col = lax.broadcasted_iota(jnp.int32, (TQ, NK), 1)
    kc = col // 64
    return (kc >= qc) & (kc <= qc + 2) & ((col >= ATTN_BACK) | (i > 0))


def _attn_probs(q_ref, kw, sink_ref, valid, h):
    qh = q_ref[:, h * HEAD_DIM:(h + 1) * HEAD_DIM]
    s = _dot_nt(qh, kw) * (HEAD_DIM ** -0.5)
    s = jnp.where(valid, s, -jnp.inf)
    sk = sink_ref[h:h + 1, 0:1]
    mx = jnp.maximum(jnp.max(s, axis=-1, keepdims=True), sk)
    p = jnp.exp(s - mx)
    es = jnp.exp(sk - mx)
    den = jnp.sum(p, axis=-1, keepdims=True) + es
    return p / den, es / den


def _attn_fwd(proj, kpad, vpad, sink_b):
    S = proj.shape[0]
    TQ, NK = ATTN_TQ, ATTN_TQ + ATTN_BACK

    def body(q_ref, z_ref, k_ref, v_ref, sink_ref, o_ref, y_ref):
        i = pl.program_id(0)
        start = pl.multiple_of(i * TQ, TQ)
        valid = _attn_mask(i)
        for h in range(N_HEADS):
            kh = h // (N_HEADS // N_KV)
            ks = slice(kh * HEAD_DIM, (kh + 1) * HEAD_DIM)
            hs = slice(h * HEAD_DIM, (h + 1) * HEAD_DIM)
            pn, _ = _attn_probs(q_ref, k_ref[pl.ds(start, NK), ks], sink_ref, valid, h)
            o = _dot(pn.astype(BF16), v_ref[pl.ds(start, NK), ks])
            o_ref[:, hs] = o.astype(BF16)
            y_ref[:, hs] = (o * _silu(z_ref[:, hs].astype(F32))).astype(BF16)

    out = pl.BlockSpec((TQ, 1024), lambda i: (i, 0))
    return pl.pallas_call(
        body, name="attn_fwd", grid=(S // TQ,),
        in_specs=[pl.BlockSpec((TQ, 1024), lambda i: (i, CB_Q)), pl.BlockSpec((TQ, 1024), lambda i: (i, CB_AZ)),
                  _full(kpad.shape), _full(vpad.shape), _full((N_HEADS, LANE))],
        out_specs=[out, out], out_shape=[jax.ShapeDtypeStruct((S, 1024), BF16)] * 2,
        compiler_params=_cp(("parallel",)),
    )(proj, proj, kpad, vpad, sink_b)


def _conv_specs(tb):
    H = CONV_HALO
    prev = lambda i: jnp.maximum(i * (tb // H) - 1, 0)
    a = pl.BlockSpec((tb, 1024), lambda i: (i, CB_CA))
    ah = pl.BlockSpec((H, 1024), lambda i: (prev(i), CB_CA))
    b = pl.BlockSpec((tb, 1024), lambda i: (i, CB_CB))
    bh = pl.BlockSpec((H, 1024), lambda i: (prev(i), CB_CB))
    z = pl.BlockSpec((tb, 1024), lambda i: (i, CB_CZ))
    return a, ah, b, bh, z


def _conv_glu_dw(a_ref, ah_ref, b_ref, bh_ref, dw_ref, gbuf, ybuf, i, tb):
    H = CONV_HALO
    gbuf[H:, :] = a_ref[...].astype(F32) * _sig(b_ref[...].astype(F32))
    gh = ah_ref[...].astype(F32) * _sig(bh_ref[...].astype(F32))
    gbuf[:H, :] = jnp.where(i > 0, gh, 0.0)
    base = H - (CONV_K - 1)
    for c in range(1024 // LANE):
        cs = slice(c * LANE, (c + 1) * LANE)
        for r in range(tb // 128):
            acc = jnp.zeros((128, LANE), F32)
            for j in range(CONV_K):
                acc = acc + dw_ref[j:j + 1, cs] * gbuf[pl.ds(r * 128 + base + j, 128), cs]
            ybuf[r * 128:(r + 1) * 128, cs] = acc


def _layer_norm_fwd(y, g, b):
    mu = jnp.mean(y, axis=-1, keepdims=True)
    yc = y - mu
    rstd = lax.rsqrt(jnp.mean(yc * yc, axis=-1, keepdims=True) + EPS)
    xh = yc * rstd
    return xh, rstd, xh * g + b


def _conv_fwd(proj, dw, dw_b, ln_g, ln_b, pw, tb=256):
    S = proj.shape[0]

    def body(a_ref, ah_ref, b_ref, bh_ref, z_ref, dw_ref, dwb_ref, lg_ref, lb_ref, pw_ref, y_ref, gbuf, ybuf):
        i = pl.program_id(0)
        _conv_glu_dw(a_ref, ah_ref, b_ref, bh_ref, dw_ref, gbuf, ybuf, i, tb)
        _, _, yn = _layer_norm_fwd(ybuf[...] + dwb_ref[...], lg_ref[...], lb_ref[...])
        out = _dot(_silu(yn).astype(BF16), pw_ref[...])
        y_ref[...] = (out * _silu(z_ref[...].astype(F32))).astype(BF16)

    vec = _full((1, 1024))
    return pl.pallas_call(
        body, name="conv_fwd", grid=(S // tb,),
        in_specs=[*_conv_specs(tb), _full((32, 1024)), vec, vec, vec, _full((1024, 1024))],
        out_specs=pl.BlockSpec((tb, 1024), lambda i: (i, 0)),
        out_shape=jax.ShapeDtypeStruct((S, 1024), BF16),
        scratch_shapes=[pltpu.VMEM((tb + CONV_HALO, 1024), F32), pltpu.VMEM((tb, 1024), F32)],
        compiler_params=_cp(("parallel",)),
    )(proj, proj, proj, proj, proj, dw, dw_b, ln_g, ln_b, pw)


def _merge_fwd(yp, ya, yc, wbp, wba, wbc, proj, tm=512, tn=1024):
    S = yp.shape[0]

    def body(yp_ref, ya_ref, yc_ref, wp_ref, wa_ref, wc_ref, gp_ref, ga_ref, gc_ref, m_ref, pp_ref, pa_ref, pc_ref):
        pp = _dot(yp_ref[...], wp_ref[...])
        pa = _dot(ya_ref[...], wa_ref[...])
        pc = _dot(yc_ref[...], wc_ref[...])
        m = (_sig(gp_ref[...].astype(F32)) * pp + _sig(ga_ref[...].astype(F32)) * pa
             + _sig(gc_ref[...].astype(F32)) * pc)
        m_ref[...] = m.astype(BF16)
        pp_ref[...], pa_ref[...], pc_ref[...] = pp.astype(BF16), pa.astype(BF16), pc.astype(BF16)

    yb = pl.BlockSpec((tm, 1024), lambda j, i: (i, 0))
    wb = pl.BlockSpec((1024, tn), lambda j, i: (0, j))
    gate = lambda cb: pl.BlockSpec((tm, tn), lambda j, i: (i, cb + j))
    out = pl.BlockSpec((tm, tn), lambda j, i: (i, j))
    return pl.pallas_call(
        body, name="merge_fwd", grid=(D // tn, S // tm),
        in_specs=[yb, yb, yb, wb, wb, wb, gate(CB_GP), gate(CB_GA), gate(CB_GC)],
        out_specs=[out] * 4, out_shape=[jax.ShapeDtypeStruct((S, D), BF16)] * 4,
        compiler_params=_cp(("parallel", "parallel")),
    )(yp, ya, yc, wbp, wba, wbc, proj, proj, proj)


def _out_fwd(x, merged, w_out, gate, tm=512, tn=1024):
    S = x.shape[0]

    def body(x_ref, m_ref, w_ref, g_ref, o_ref):
        o_ref[...] = x_ref[...] + g_ref[...] * _dot(m_ref[...], w_ref[...])

    xb = pl.BlockSpec((tm, tn), lambda j, i: (i, j))
    return pl.pallas_call(
        body, name="out_fwd", grid=(D // tn, S // tm),
        in_specs=[xb, pl.BlockSpec((tm, D), lambda j, i: (i, 0)), pl.BlockSpec((D, tn), lambda j, i: (0, j)),
                  pl.BlockSpec((1, tn), lambda j, i: (0, j))],
        out_specs=xb, out_shape=jax.ShapeDtypeStruct((S, D), F32),
        compiler_params=_cp(("parallel", "parallel")),
    )(x, merged, w_out, gate)


def _final_loss(x, target, final_g, tb=256):
    S = x.shape[0]

    def body(x_ref, t_ref, g_ref, dx_ref, gg_ref, ls_ref):
        i = pl.program_id(0)
        xv, g = x_ref[...], g_ref[...]
        r = lax.rsqrt(jnp.mean(xv * xv, axis=-1, keepdims=True) + EPS)
        xh = xv * r
        e = xh * g - t_ref[...]
        dy = e * (1.0 / D)
        gy = dy * g
        dx_ref[...] = r * (gy - xh * jnp.mean(gy * xh, axis=-1, keepdims=True))
        gg = jnp.sum(dy * xh, axis=0, keepdims=True)
        ls = jnp.sum(e * e, axis=0, keepdims=True) * (0.5 / D)

        @pl.when(i == 0)
        def _():
            gg_ref[...], ls_ref[...] = gg, ls

        @pl.when(i > 0)
        def _():
            gg_ref[...] += gg
            ls_ref[...] += ls

    row = pl.BlockSpec((tb, D), lambda i: (i, 0))
    vec = _full((1, D))
    return pl.pallas_call(
        body, name="final_loss", grid=(S // tb,), in_specs=[row, row, vec], out_specs=[row, vec, vec],
        out_shape=[jax.ShapeDtypeStruct((S, D), F32), jax.ShapeDtypeStruct((1, D), F32),
                   jax.ShapeDtypeStruct((1, D), F32)],
        compiler_params=_cp(("arbitrary",)),
    )(x, target, final_g)


def _out_bwd(dx, gate, w_out_t, pp, pa, pc, proj, tm=512, tn=1024):
    S = dx.shape[0]

    def body(dx_ref, g_ref, w_ref, pp_ref, pa_ref, pc_ref, gp_ref, ga_ref, gc_ref,
             dpp_ref, dpa_ref, dpc_ref, dgp_ref, dga_ref, dgc_ref):
        dm = _dot((dx_ref[...] * g_ref[...]).astype(BF16), w_ref[...])
        for p_ref, gl_ref, dp_ref, dg_ref in ((pp_ref, gp_ref, dpp_ref, dgp_ref), (pa_ref, ga_ref, dpa_ref, dga_ref),
                                              (pc_ref, gc_ref, dpc_ref, dgc_ref)):
            s = _sig(gl_ref[...].astype(F32))
            dp_ref[...] = (dm * s).astype(BF16)
            dg_ref[...] = (dm * p_ref[...].astype(F32) * s * (1.0 - s)).astype(BF16)

    out = pl.BlockSpec((tm, tn), lambda j, i: (i, j))
    gate_spec = lambda cb: pl.BlockSpec((tm, tn), lambda j, i: (i, cb + j))
    return pl.pallas_call(
        body, name="out_bwd", grid=(D // tn, S // tm),
        in_specs=[pl.BlockSpec((tm, D), lambda j, i: (i, 0)), _full((1, D)), pl.BlockSpec((D, tn), lambda j, i: (0, j)),
                  out, out, out, gate_spec(CB_GP), gate_spec(CB_GA), gate_spec(CB_GC)],
        out_specs=[out] * 6, out_shape=[jax.ShapeDtypeStruct((S, D), BF16)] * 6,
        compiler_params=_cp(("parallel", "parallel")),
    )(dx, gate, w_out_t, pp, pa, pc, proj, proj, proj)


def _wout_post(gmat, w_out, gate, tr=256):
    def body(g_ref, w_ref, gate_ref, dw_ref, dg_ref):
        i = pl.program_id(0)
        gm = g_ref[...]
        dw_ref[...] = gm * gate_ref[...]
        part = jnp.sum(gm * w_ref[...].astype(F32), axis=0, keepdims=True)

        @pl.when(i == 0)
        def _():
            dg_ref[...] = part

        @pl.when(i > 0)
        def _():
            dg_ref[...] += part

    row = pl.BlockSpec((tr, D), lambda i: (i, 0))
    return pl.pallas_call(
        body, name="wout_post", grid=(D // tr,), in_specs=[row, row, _full((1, D))], out_specs=[row, _full((1, D))],
        out_shape=[jax.ShapeDtypeStruct((D, D), F32), jax.ShapeDtypeStruct((1, D), F32)],
        compiler_params=_cp(("arbitrary",)),
    )(gmat, w_out, gate)


def _pool_bwd_a(dy, proj, pool_w, pool_w_t, pool_scale, tb=256):
    S = proj.shape[0]

    def body(dy_ref, u_ref, uh_ref, z_ref, w_ref, wt_ref, sc_ref, dmix_ref, dz_ref, dsc_ref, dw_ref, ubuf):
        i = pl.program_id(0)
        mixed = [m.astype(BF16) for m in _pool_mix(u_ref, uh_ref, ubuf, i, tb)]
        m = jnp.concatenate([_dot(mixed[g], w_ref[g]) for g in range(4)], axis=1)
        dyv, z, sc = dy_ref[...].astype(F32), z_ref[...].astype(F32), sc_ref[...]
        dyp = dyv * _silu(z)
        dz_ref[...] = (dyv * (m * sc) * _dsilu(z)).astype(BF16)
        dsc = jnp.sum(dyp * m, axis=0, keepdims=True)
        dmm = (dyp * sc).astype(BF16)
        dws = []
        for g in range(4):
            cs = slice(g * POOL_GROUP, (g + 1) * POOL_GROUP)
            dmix_ref[:, cs] = _dot(dmm[:, cs], wt_ref[g])
            dws.append(_dot_tn(mixed[g], dmm[:, cs]))

        @pl.when(i == 0)
        def _():
            dsc_ref[...] = dsc
            for g in range(4):
                dw_ref[g] = dws[g]

        @pl.when(i > 0)
        def _():
            dsc_ref[...] += dsc
            for g in range(4):
                dw_ref[g] += dws[g]

    u, uh, z = _pool_specs(tb)
    row = pl.BlockSpec((tb, 1024), lambda i: (i, 0))
    wfull = _full((4, 256, 256))
    return pl.pallas_call(
        body, name="pool_bwd_a", grid=(S // tb,),
        in_specs=[row, u, uh, z, wfull, wfull, _full((1, 1024))],
        out_specs=[row, row, _full((1, 1024)), wfull],
        out_shape=[jax.ShapeDtypeStruct((S, 1024), F32), jax.ShapeDtypeStruct((S, 1024), BF16),
                   jax.ShapeDtypeStruct((1, 1024), F32), jax.ShapeDtypeStruct((4, 256, 256), F32)],
        scratch_shapes=[pltpu.VMEM((tb + POOL_HALO, 1024), F32)],
        compiler_params=_cp(("arbitrary",)),
    )(dy, proj, proj, proj, pool_w, pool_w_t, pool_scale)


def _pool_bwd_b(dmix, tb=256):
    S = dmix.shape[0]
    H = POOL_HALO
    nb = S // tb

    def body(dm_ref, dh_ref, du_ref, ebuf):
        i = pl.program_id(0)
        t = i * tb + lax.broadcasted_iota(jnp.int32, (tb, 1), 0)
        th = (i + 1) * tb + lax.broadcasted_iota(jnp.int32, (H, 1), 0)
        for g, w in enumerate(POOL_WINDOWS):
            cs = slice(g * POOL_GROUP, (g + 1) * POOL_GROUP)
            ebuf[:tb, cs] = dm_ref[:, cs] / jnp.minimum(t + 1, w).astype(F32)
            eh = dh_ref[:, cs] / jnp.minimum(th + 1, w).astype(F32)
            ebuf[tb:, cs] = jnp.where(i < nb - 1, eh, 0.0)
        for g, w in enumerate(POOL_WINDOWS):
            cs = slice(g * POOL_GROUP, (g + 1) * POOL_GROUP)
            acc = ebuf[:tb, cs]
            for j in range(1, w):
                acc = acc + ebuf[pl.ds(j, tb), cs]
            du_ref[:, cs] = (acc - dm_ref[:, cs]).astype(BF16)

    row = pl.BlockSpec((tb, 1024), lambda i: (i, 0))
    nxt = pl.BlockSpec((H, 1024), lambda i: (jnp.minimum((i + 1) * (tb // H), S // H - 1), 0))
    return pl.pallas_call(
        body, name="pool_bwd_b", grid=(nb,), in_specs=[row, nxt], out_specs=row,
        out_shape=jax.ShapeDtypeStruct((S, 1024), BF16),
        scratch_shapes=[pltpu.VMEM((tb + H, 1024), F32)],
        compiler_params=_cp(("parallel",)),
    )(dmix, dmix)


def _attn_bwd(dy, o, proj, kpad, vpad, sink_b):
    S = proj.shape[0]
    TQ, NK = ATTN_TQ, ATTN_TQ + ATTN_BACK
    nb = S // TQ
    G = N_HEADS // N_KV

    def body(dy_ref, o_ref, q_ref, z_ref, k_ref, v_ref, sink_ref, dq_ref, dz_ref, dk_hbm, dv_hbm, ds_ref, dk_acc, dv_acc):
        i = pl.program_id(0)

        @pl.when(i == 0)
        def _():
            dk_acc[...] = jnp.zeros_like(dk_acc)
            dv_acc[...] = jnp.zeros_like(dv_acc)
            ds_ref[...] = jnp.zeros_like(ds_ref)

        start = pl.multiple_of(i * TQ, TQ)
        valid = _attn_mask(i)
        dks, dvs = [], []
        for kh in range(N_KV):
            ks = slice(kh * HEAD_DIM, (kh + 1) * HEAD_DIM)
            kw = k_ref[pl.ds(start, NK), ks]
            vw = v_ref[pl.ds(start, NK), ks]
            dk_sum = jnp.zeros((NK, HEAD_DIM), F32)
            dv_sum = jnp.zeros((NK, HEAD_DIM), F32)
            for gi in range(G):
                h = kh * G + gi
                hs = slice(h * HEAD_DIM, (h + 1) * HEAD_DIM)
                pn, psink = _attn_probs(q_ref, kw, sink_ref, valid, h)
                z = z_ref[:, hs].astype(F32)
                dyv = dy_ref[:, hs].astype(F32)
                ov = o_ref[:, hs].astype(F32)
                do = dyv * _silu(z)
                dz_ref[:, hs] = (dyv * ov * _dsilu(z)).astype(BF16)
                delta = jnp.sum(do * ov, axis=-1, keepdims=True)
                dob = do.astype(BF16)
                dp = _dot_nt(dob, vw)
                ds = (pn * (dp - delta)).astype(BF16)
                dsink = -jnp.sum(psink * delta, axis=0, keepdims=True)
                ds_ref[h:h + 1, :] += jnp.broadcast_to(dsink, (1, LANE))
                dq_ref[:, hs] = (_dot(ds, kw) * (HEAD_DIM ** -0.5)).astype(BF16)
                dk_sum = dk_sum + _dot_tn(ds, q_ref[:, hs])
                dv_sum = dv_sum + _dot_tn(pn.astype(BF16), dob)
            dks.append(dk_sum * (HEAD_DIM ** -0.5))
            dvs.append(dv_sum)
        dk_acc[pl.ds(start, NK), :] += jnp.concatenate(dks, axis=1)
        dv_acc[pl.ds(start, NK), :] += jnp.concatenate(dvs, axis=1)

        @pl.when(i == nb - 1)
        def _():
            pltpu.sync_copy(dk_acc, dk_hbm)
            pltpu.sync_copy(dv_acc, dv_hbm)

    row = pl.BlockSpec((TQ, 1024), lambda i: (i, 0))
    return pl.pallas_call(
        body, name="attn_bwd", grid=(nb,),
        in_specs=[row, row, pl.BlockSpec((TQ, 1024), lambda i: (i, CB_Q)), pl.BlockSpec((TQ, 1024), lambda i: (i, CB_AZ)),
                  _full(kpad.shape), _full(vpad.shape), _full((N_HEADS, LANE))],
        out_specs=[row, row, ANY, ANY, _full((N_HEADS, LANE))],
        out_shape=[jax.ShapeDtypeStruct((S, 1024), BF16), jax.ShapeDtypeStruct((S, 1024), BF16),
                   jax.ShapeDtypeStruct(kpad.shape, F32), jax.ShapeDtypeStruct(vpad.shape, F32),
                   jax.ShapeDtypeStruct((N_HEADS, LANE), F32)],
        scratch_shapes=[pltpu.VMEM(kpad.shape, F32), pltpu.VMEM(vpad.shape, F32)],
        compiler_params=_cp(("arbitrary",), 56 * 1024 * 1024),
    )(dy, o, proj, proj, kpad, vpad, sink_b)


def _conv_bwd_a(dy, proj, dw, dw_b, ln_g, ln_b, pw, pw_t, tb=256):
    S = proj.shape[0]
    H = CONV_HALO
    base = H - (CONV_K - 1)

    def body(dy_ref, a_ref, ah_ref, b_ref, bh_ref, z_ref, dw_ref, dwb_ref, lg_ref, lb_ref, pw_ref, pwt_ref,
             dcv_ref, dz_ref, dpw_ref, dlg_ref, dlb_ref, ddwb_ref, ddw_ref, gbuf, ybuf):
        i = pl.program_id(0)
        _conv_glu_dw(a_ref, ah_ref, b_ref, bh_ref, dw_ref, gbuf, ybuf, i, tb)
        lg = lg_ref[...]
        xh, rstd, yn = _layer_norm_fwd(ybuf[...] + dwb_ref[...], lg, lb_ref[...])
        u = _silu(yn).astype(BF16)
        out = _dot(u, pw_ref[...])
        dyv, z = dy_ref[...].astype(F32), z_ref[...].astype(F32)
        dz_ref[...] = (dyv * out * _dsilu(z)).astype(BF16)
        dout = (dyv * _silu(z)).astype(BF16)
        dpw = _dot_tn(u, dout)
        dyn = _dot(dout, pwt_ref[...]) * _dsilu(yn)
        dlg = jnp.sum(dyn * xh, axis=0, keepdims=True)
        dlb = jnp.sum(dyn, axis=0, keepdims=True)
        dxh = dyn * lg
        dcv = rstd * (dxh - jnp.mean(dxh, axis=-1, keepdims=True) - xh * jnp.mean(dxh * xh, axis=-1, keepdims=True))
        dcv_ref[...] = dcv
        ddwb = jnp.sum(dcv, axis=0, keepdims=True)
        ybuf[...] = dcv

        @pl.when(i == 0)
        def _():
            dpw_ref[...], dlg_ref[...], dlb_ref[...], ddwb_ref[...] = dpw, dlg, dlb, ddwb
            ddw_ref[...] = jnp.zeros_like(ddw_ref)

        @pl.when(i > 0)
        def _():
            dpw_ref[...] += dpw
            dlg_ref[...] += dlg
            dlb_ref[...] += dlb
            ddwb_ref[...] += ddwb

        for c in range(1024 // LANE):
            cs = slice(c * LANE, (c + 1) * LANE)
            for j in range(CONV_K):
                acc = jnp.zeros((1, LANE), F32)
                for r in range(tb // 128):
                    acc = acc + jnp.sum(ybuf[r * 128:(r + 1) * 128, cs] * gbuf[pl.ds(r * 128 + base + j, 128), cs],
                                        axis=0, keepdims=True)
                ddw_ref[j:j + 1, cs] += acc

    vec = _full((1, 1024))
    row = pl.BlockSpec((tb, 1024), lambda i: (i, 0))
    big = _full((1024, 1024))
    return pl.pallas_call(
        body, name="conv_bwd_a", grid=(S // tb,),
        in_specs=[row, *_conv_specs(tb), _full((32, 1024)), vec, vec, vec, big, big],
        out_specs=[row, row, big, vec, vec, vec, _full((32, 1024))],
        out_shape=[jax.ShapeDtypeStruct((S, 1024), F32), jax.ShapeDtypeStruct((S, 1024), BF16),
                   jax.ShapeDtypeStruct((1024, 1024), F32), jax.ShapeDtypeStruct((1, 1024), F32),
                   jax.ShapeDtypeStruct((1, 1024), F32), jax.ShapeDtypeStruct((1, 1024), F32),
                   jax.ShapeDtypeStruct((32, 1024), F32)],
        scratch_shapes=[pltpu.VMEM((tb + H, 1024), F32), pltpu.VMEM((tb, 1024), F32)],
        compiler_params=_cp(("arbitrary",)),
    )(dy, proj, proj, proj, proj, proj, dw, dw_b, ln_g, ln_b, pw, pw_t)


def _conv_bwd_b(dcv, proj, dw, tb=256):
    S = proj.shape[0]
    H = CONV_HALO
    nb = S // tb

    def body(d_ref, dn_ref, a_ref, b_ref, dw_ref, da_ref, db_ref, dbuf, gbuf):
        i = pl.program_id(0)
        dbuf[:tb, :] = d_ref[...]
        dbuf[tb:, :] = jnp.where(i < nb - 1, dn_ref[...], 0.0)
        for c in range(1024 // LANE):
            cs = slice(c * LANE, (c + 1) * LANE)
            for r in range(tb // 128):
                acc = jnp.zeros((128, LANE), F32)
                for j in range(CONV_K):
                    acc = acc + dw_ref[j:j + 1, cs] * dbuf[pl.ds(r * 128 + (CONV_K - 1) - j, 128), cs]
                gbuf[r * 128:(r + 1) * 128, cs] = acc
        dg = gbuf[...]
        a, s = a_ref[...].astype(F32), _sig(b_ref[...].astype(F32))
        da_ref[...] = (dg * s).astype(BF16)
        db_ref[...] = (dg * a * s * (1.0 - s)).astype(BF16)

    row = pl.BlockSpec((tb, 1024), lambda i: (i, 0))
    nxt = pl.BlockSpec((H, 1024), lambda i: (jnp.minimum((i + 1) * (tb // H), S // H - 1), 0))
    return pl.pallas_call(
        body, name="conv_bwd_b", grid=(nb,),
        in_specs=[row, nxt, pl.BlockSpec((tb, 1024), lambda i: (i, CB_CA)), pl.BlockSpec((tb, 1024), lambda i: (i, CB_CB)),
                  _full((32, 1024))],
        out_specs=[row, row], out_shape=[jax.ShapeDtypeStruct((S, 1024), BF16)] * 2,
        scratch_shapes=[pltpu.VMEM((tb + H, 1024), F32), pltpu.VMEM((tb, 1024), F32)],
        compiler_params=_cp(("parallel",)),
    )(dcv, dcv, proj, proj, dw)


def _norm_bwd(dh, x, dx_out, g, scale, tb=256):
    S = x.shape[0]

    def body(dh_ref, x_ref, dxo_ref, g_ref, sc_ref, dx_ref, dsh_ref, da_ref):
        i = pl.program_id(0)
        xv, dhv = x_ref[...], dh_ref[...]
        r = lax.rsqrt(jnp.mean(xv * xv, axis=-1, keepdims=True) + EPS)
        xh = xv * r
        gy = dhv * (g_ref[...] * (1.0 + sc_ref[...]))
        dx_ref[...] = dxo_ref[...] + r * (gy - xh * jnp.mean(gy * xh, axis=-1, keepdims=True))
        dsh = jnp.sum(dhv, axis=0, keepdims=True)
        da = jnp.sum(dhv * xh, axis=0, keepdims=True)

        @pl.when(i == 0)
        def _():
            dsh_ref[...], da_ref[...] = dsh, da

        @pl.when(i > 0)
        def _():
            dsh_ref[...] += dsh
            da_ref[...] += da

    row = pl.BlockSpec((tb, D), lambda i: (i, 0))
    vec = _full((1, D))
    return pl.pallas_call(
        body, name="norm_bwd", grid=(S // tb,), in_specs=[row, row, row, vec, vec], out_specs=[row, vec, vec],
        out_shape=[jax.ShapeDtypeStruct((S, D), F32), jax.ShapeDtypeStruct((1, D), F32), jax.ShapeDtypeStruct((1, D), F32)],
        compiler_params=_cp(("arbitrary",)),
    )(dh, x, dx_out, g, scale)


def _mod_bwd(d_a, norm_g, scale):
    def body(da_ref, g_ref, sc_ref, dg_ref, dsc_ref):
        dg_ref[...] = da_ref[...] * (1.0 + sc_ref[...])
        dsc_ref[...] = da_ref[...] * g_ref[...]

    return pl.pallas_call(body, name="mod_bwd", out_shape=[jax.ShapeDtypeStruct(d_a.shape, F32)] * 2)(d_a, norm_g, scale)


def _rows_block(rows):
    return 256 if rows % 256 == 0 else rows


def _pair_sum(g, r1, c_idx, name):
    _, rows, C = g.shape
    tr = _rows_block(rows)

    def body(c_ref, g_ref, r_ref, h_ref):
        h_ref[...] = (g_ref[...].astype(F32) + r_ref[...].astype(F32)).astype(BF16)

    blk = pl.BlockSpec((1, tr, C), lambda ch, i, c: (ch, i, 0))
    return pl.pallas_call(
        body, name=name,
        grid_spec=pltpu.PrefetchScalarGridSpec(
            num_scalar_prefetch=1, grid=(4, rows // tr),
            in_specs=[pl.BlockSpec((1, tr, C), lambda ch, i, c: (2 * ch + c[0], i, 0)), blk], out_specs=blk),
        out_shape=jax.ShapeDtypeStruct((4, rows, C), BF16),
        compiler_params=_cp(("parallel", "parallel")),
    )(c_idx, g, r1)


def _adamw_final(h, r2, chip_idx, w, m, v, name):
    _, rows, C = h.shape
    tr = _rows_block(rows)

    def body(q_ref, h_ref, r0_ref, r1_ref, r2_ref, w_ref, m_ref, v_ref, g_ref, dl_ref, m2_ref, v2_ref):
        g = ((h_ref[0].astype(F32) + r0_ref[0].astype(F32)) + r1_ref[0].astype(F32)) + r2_ref[0].astype(F32)
        delta, m2, v2 = _adam_math(g, w_ref[...], m_ref[...], v_ref[...])
        g_ref[...], dl_ref[...], m2_ref[...], v2_ref[...] = g, delta, m2, v2

    blk = pl.BlockSpec((tr, C), lambda i, q: (i, 0))
    rspec = lambda r: pl.BlockSpec((1, tr, C), lambda i, q: (r, i, 0))
    return pl.pallas_call(
        body, name=name,
        grid_spec=pltpu.PrefetchScalarGridSpec(
            num_scalar_prefetch=1, grid=(rows // tr,),
            in_specs=[pl.BlockSpec((1, tr, C), lambda i, q: (q[0], i, 0)), rspec(0), rspec(1), rspec(2), blk, blk, blk],
            out_specs=[blk] * 4),
        out_shape=[jax.ShapeDtypeStruct((rows, C), F32)] * 4,
        compiler_params=_cp(("parallel",)),
    )(chip_idx, h, r2, r2, r2, w, m, v)


def _small_final(parts, w, m, v):
    R = w.shape[0]

    def body(p_ref, w_ref, m_ref, v_ref, g_ref, dl_ref, m2_ref, v2_ref):
        g = p_ref[0]
        for k in range(1, N_DEV):
            g = g + p_ref[k]
        delta, m2, v2 = _adam_math(g, w_ref[...], m_ref[...], v_ref[...])
        g_ref[...], dl_ref[...], m2_ref[...], v2_ref[...] = g, delta, m2, v2

    return pl.pallas_call(body, name="small_final", out_shape=[jax.ShapeDtypeStruct((R, LANE), F32)] * 4)(parts, w, m, v)


def _layer_fwd(x, mod, small, W):
    shift, scale, gate = mod
    h = _norm_mod(x, small["norm_g"], scale, shift)
    proj = _mm(h, W["w_in"], BF16, 512, 1536, name="proj_mm")
    y_pool = _pool_fwd(proj, W["pool_w"], small["pool_scale"])
    kpad = jnp.pad(proj[:, COL_K:COL_K + 256], ((ATTN_BACK, 0), (0, 0)))
    vpad = jnp.pad(proj[:, COL_V:COL_V + 256], ((ATTN_BACK, 0), (0, 0)))
    o, y_attn = _attn_fwd(proj, kpad, vpad, small["sink_b"])
    y_conv = _conv_fwd(proj, W["conv_dw"], small["conv_dw_b"], small["conv_ln_g"], small["conv_ln_b"], W["conv_pw"])
    merged, pp, pa, pc = _merge_fwd(y_pool, y_attn, y_conv, W["wbp"], W["wba"], W["wbc"], proj)
    x_new = _out_fwd(x, merged, W["w_out"], gate)
    stash = dict(x=x, h=h, proj=proj, kpad=kpad, vpad=vpad, o=o, y_pool=y_pool, y_attn=y_attn, y_conv=y_conv,
                 merged=merged, pp=pp, pa=pa, pc=pc)
    return x_new, stash


def _layer_bwd(dx, st, mod, small, W):
    shift, scale, gate = mod
    proj = st["proj"]
    gmat = _mm_tn(st["merged"], dx, 1024, 1024, 512, name="wout_tn")
    d_w_out, d_gate = _wout_post(gmat, W["w_out"], gate)
    dpp, dpa, dpc, dgp, dga, dgc = _out_bwd(dx, gate, W["w_out_t"], st["pp"], st["pa"], st["pc"], proj)
    dy_pool = _mm(dpp, W["wbp_t"], BF16, 512, 1024, name="branch_bwd_mm")
    dy_attn = _mm(dpa, W["wba_t"], BF16, 512, 1024, name="branch_bwd_mm")
    dy_conv = _mm(dpc, W["wbc_t"], BF16, 512, 1024, name="branch_bwd_mm")
    d_wbp = _mm_tn(st["y_pool"], dpp, 1024, 1024, 512, name="branch_tn")
    d_wba = _mm_tn(st["y_attn"], dpa, 1024, 1024, 512, name="branch_tn")
    d_wbc = _mm_tn(st["y_conv"], dpc, 1024, 1024, 512, name="branch_tn")

    dmix, dz_pool, d_pool_scale, d_pool_w = _pool_bwd_a(dy_pool, proj, W["pool_w"], W["pool_w_t"], small["pool_scale"])
    du = _pool_bwd_b(dmix)
    dq, dz_attn, dk, dv, d_sink = _attn_bwd(dy_attn, st["o"], proj, st["kpad"], st["vpad"], small["sink_b"])
    dcv, dz_conv, d_pw, d_ln_g, d_ln_b, d_dw_b, d_dw = _conv_bwd_a(
        dy_conv, proj, W["conv_dw"], small["conv_dw_b"], small["conv_ln_g"], small["conv_ln_b"], W["conv_pw"], W["conv_pw_t"])
    da, db = _conv_bwd_b(dcv, proj, W["conv_dw"])

    dproj = jnp.concatenate([du, dz_pool, dq, dz_attn, da, db, dz_conv, dgp, dga, dgc,
                             dk[ATTN_BACK:].astype(BF16), dv[ATTN_BACK:].astype(BF16)], axis=1)
    dh = _mm(dproj, W["w_in_t"], F32, 512, 2048, 1536, name="dh_mm", vmem=56 * 1024 * 1024)
    d_w_in = _mm_tn(st["h"], dproj, 1024, 1536, 512, name="win_tn")
    dx_in, d_shift, d_a = _norm_bwd(dh, st["x"], dx, small["norm_g"], scale)
    big = dict(w_in=d_w_in, pool_w=d_pool_w, conv_dw=d_dw[:CONV_K], conv_pw=d_pw, wbp=d_wbp, wba=d_wba, wbc=d_wbc,
               w_out=d_w_out)
    sm = dict(d_a=d_a, d_shift=d_shift, d_gate=d_gate, pool_scale=d_pool_scale, attn_sink=d_sink[:, 0],
              conv_dw_b=d_dw_b, conv_ln_g=d_ln_g, conv_ln_b=d_ln_b)
    return dx_in, big, sm


def _local_step(x, target, mods, smalls, Ws, final_g):
    stashes = []
    for l in range(DEPTH):
        x, st = _layer_fwd(x, mods[l], smalls[l], Ws[l])
        stashes.append(st)
    dx, d_final_g, loss_lanes = _final_loss(x, target, final_g)
    bigs, sms = [None] * DEPTH, [None] * DEPTH
    for l in reversed(range(DEPTH)):
        dx, bigs[l], sms[l] = _layer_bwd(dx, stashes[l], mods[l], smalls[l], Ws[l])
    return loss_lanes, dx, d_final_g, bigs, sms


BIG = ("w_in", "pool_w", "conv_pw", "wbp", "wba", "wbc", "w_out")
GRADS = ("w_in", "pool_w", "conv_dw", "conv_pw", "wbp", "wba", "wbc", "w_out")


def _to_internal(w):
    return jnp.concatenate([w[..., :3072], w[..., 3584:], w[..., 3072:3584]], axis=-1)


def _from_internal(w):
    return jnp.concatenate([w[..., :3072], w[..., COL_K:], w[..., 3072:COL_K]], axis=-1)


def _unpack_layer(g, conv_dw_all, l):
    t = lambda a: jnp.swapaxes(a, -1, -2)
    w_in = _to_internal(jnp.transpose(g["w_in"][:, l], (1, 0, 2)).reshape(D, IN_WIDTH))
    pool_w = jnp.transpose(g["pool_w"][:, l], (1, 0, 2, 3)).reshape(4, 256, 256)
    conv_pw = g["conv_pw"][:, l].reshape(1024, 1024)
    cols = lambda a: jnp.transpose(a, (1, 0, 2)).reshape(a.shape[1], -1)
    wbp, wba, wbc = cols(g["wbp"][:, l]), cols(g["wba"][:, l]), cols(g["wbc"][:, l])
    w_out = g["w_out"][:, l].reshape(D, D)
    conv_dw = jnp.pad(cols(conv_dw_all[:, l]), ((0, 32 - CONV_K), (0, 0)))
    return dict(w_in=w_in, w_in_t=t(w_in), pool_w=pool_w, pool_w_t=t(pool_w), conv_pw=conv_pw, conv_pw_t=t(conv_pw),
                wbp=wbp, wba=wba, wbc=wbc, wbp_t=t(wbp), wba_t=t(wba), wbc_t=t(wbc), w_out=w_out, w_out_t=t(w_out),
                conv_dw=conv_dw)


def _pieces(name, g):
    L = g.shape[0]
    if name == "w_in":
        p = jnp.transpose(_from_internal(g).reshape(L, D, 8, IN_WIDTH // 8), (2, 0, 1, 3))
    elif name == "pool_w":
        p = jnp.transpose(g.reshape(L, 4, 8, 32, 256), (2, 0, 1, 3, 4))
    elif name == "conv_dw":
        p = jnp.transpose(g.reshape(L, CONV_K, 8, 128), (2, 0, 1, 3))
    elif name == "conv_pw":
        p = jnp.transpose(g.reshape(L, 8, 128, 1024), (1, 0, 2, 3))
    elif name in ("wbp", "wba", "wbc"):
        p = jnp.transpose(g.reshape(L, 1024, 8, 256), (2, 0, 1, 3))
    else:
        p = jnp.transpose(g.reshape(L, 8, 256, D), (1, 0, 2, 3))
    return p.reshape(8, -1, p.shape[-1]).astype(BF16)


def _pack_small(items, rows):
    flat = jnp.concatenate([a.reshape(-1).astype(F32) for a in items])
    return jnp.pad(flat, (0, rows * LANE - flat.shape[0])).reshape(rows, LANE)


def _unpack_small(packed, shapes):
    flat, out, off = packed.reshape(-1), [], 0
    for s in shapes:
        n = 1
        for d in s:
            n *= d
        out.append(flat[off:off + n].reshape(s))
        off += n
    return out


def kernel(x, c, norm_g, w_ada, b_ada, w_in, pool_w, pool_scale, attn_sink, conv_dw, conv_dw_b, conv_ln_g, conv_ln_b, conv_pw, w_branch_pool, w_branch_attn, w_branch_conv, w_out, final_g, loss_target, m_norm_g, m_w_ada, m_b_ada, m_w_in, m_pool_w, m_pool_scale, m_attn_sink, m_conv_dw, m_conv_dw_b, m_conv_ln_g, m_conv_ln_b, m_conv_pw, m_w_branch_pool, m_w_branch_attn, m_w_branch_conv, m_w_out, m_final_g, v_norm_g, v_w_ada, v_b_ada, v_w_in, v_pool_w, v_pool_scale, v_attn_sink, v_conv_dw, v_conv_dw_b, v_conv_ln_g, v_conv_ln_b, v_conv_pw, v_w_branch_pool, v_w_branch_attn, v_w_branch_conv, v_w_out, v_final_g):
    L = DEPTH
    me = 4 * lax.axis_index("x") + 2 * lax.axis_index("y") + lax.axis_index("c")
    c_idx = lax.axis_index("c").astype(jnp.int32).reshape(1)
    chip_idx = (2 * lax.axis_index("x") + lax.axis_index("y")).astype(jnp.int32).reshape(1)
    shards = dict(w_in=w_in, pool_w=pool_w, conv_dw=conv_dw, conv_pw=conv_pw, wbp=w_branch_pool, wba=w_branch_attn,
                  wbc=w_branch_conv, w_out=w_out)
    moms = dict(w_in=(m_w_in, v_w_in), pool_w=(m_pool_w, v_pool_w), conv_dw=(m_conv_dw, v_conv_dw),
                conv_pw=(m_conv_pw, v_conv_pw), wbp=(m_w_branch_pool, v_w_branch_pool),
                wba=(m_w_branch_attn, v_w_branch_attn), wbc=(m_w_branch_conv, v_w_branch_conv), w_out=(m_w_out, v_w_out))

    gathered = dict(zip(BIG, _all_gather([shards[k].astype(BF16) for k in BIG], "gather_weights")))
    n_cd = L * CONV_K * 128
    first = _all_gather([_pack_small([c, conv_dw], 144)], "gather_c")[0].reshape(N_DEV, -1)
    c_all = first[:, :D]
    conv_dw_all = first[:, D:D + n_cd].reshape(N_DEV, L, CONV_K, 128)

    mod_part = _mod_fwd(c_all, w_ada)
    mod_all = _all_gather([mod_part.reshape(-1, LANE)], "gather_mod")[0].reshape(N_DEV, L, N_DEV, -1)
    mod = jnp.transpose(lax.dynamic_index_in_dim(mod_all, me, axis=2, keepdims=False), (1, 0, 2)).reshape(L, 3 * D)
    mod = mod + b_ada
    mods = [(mod[l:l + 1, :D], mod[l:l + 1, D:2 * D], mod[l:l + 1, 2 * D:]) for l in range(L)]

    sink_b = jnp.broadcast_to(attn_sink[:, :, None], (L, N_HEADS, LANE))
    smalls = [dict(norm_g=norm_g[l:l + 1], pool_scale=pool_scale[l:l + 1], sink_b=sink_b[l], conv_dw_b=conv_dw_b[l:l + 1],
                   conv_ln_g=conv_ln_g[l:l + 1], conv_ln_b=conv_ln_b[l:l + 1]) for l in range(L)]
    Ws = [_unpack_layer(gathered, conv_dw_all, l) for l in range(L)]

    loss_lanes, grad_x, d_final_g, bigs, sms = _local_step(x[0], loss_target[0], mods, smalls, Ws, final_g.reshape(1, D))

    stack = lambda k: jnp.concatenate([sms[l][k].reshape(1, -1) for l in range(L)], axis=0)
    scale_all = jnp.concatenate([mods[l][1] for l in range(L)], axis=0)
    d_norm_g, d_scale = _mod_bwd(stack("d_a"), norm_g, scale_all)
    dmod = jnp.concatenate([stack("d_shift"), d_scale, stack("d_gate")], axis=1)
    small_names = ("norm_g", "b_ada", "pool_scale", "attn_sink", "conv_dw_b", "conv_ln_g", "conv_ln_b", "final_g")
    small_g = (d_norm_g, dmod, stack("pool_scale"), stack("attn_sink"), stack("conv_dw_b"), stack("conv_ln_g"),
               stack("conv_ln_b"), d_final_g.reshape(D))
    small_w = (norm_g, b_ada, pool_scale, attn_sink, conv_dw_b, conv_ln_g, conv_ln_b, final_g)
    small_m = (m_norm_g, m_b_ada, m_pool_scale, m_attn_sink, m_conv_dw_b, m_conv_ln_g, m_conv_ln_b, m_final_g)
    small_v = (v_norm_g, v_b_ada, v_pool_scale, v_attn_sink, v_conv_dw_b, v_conv_ln_g, v_conv_ln_b, v_final_g)
    shapes = [a.shape for a in small_w] + [(D,)]
    n_small = sum(a.size for a in small_w) + D
    R = -(-n_small // (8 * LANE)) * 8
    zero = jnp.zeros((D,), F32)
    parts = _all_gather([_pack_small(small_g + (loss_lanes,), R)], "gather_small")[0]
    sg, sd, sm2, sv2 = _small_final(parts, _pack_small(small_w + (zero,), R), _pack_small(small_m + (zero,), R),
                                    _pack_small(small_v + (zero + 1.0,), R))
    sg, sd, sm2, sv2 = (_unpack_small(a, shapes) for a in (sg, sd, sm2, sv2))
    loss = jnp.sum(sg[-1])
    out_small = {n: (sg[i], sd[i], sm2[i], sv2[i]) for i, n in enumerate(small_names)}

    off = sum(a.size for a in small_w[:1])
    dmod_all = parts.reshape(N_DEV, -1)[:, off:off + L * 3 * D].reshape(N_DEV, L, 3 * D)
    dmod_mine = jnp.transpose(lax.dynamic_slice_in_dim(dmod_all, me * (3 * D // N_DEV), 3 * D // N_DEV, axis=2), (1, 0, 2))
    out_wada = _wada_bwd(c_all.T, dmod_mine, w_ada, m_w_ada, v_w_ada)

    pieces = [_pieces(k, jnp.stack([bigs[l][k] for l in range(L)])) for k in GRADS]
    r1 = _exchange_sibling(pieces, "rs_sibling")
    hs = [_pair_sum(g, r, c_idx, "pair_sum_" + k) for k, g, r in zip(GRADS, pieces, r1)]
    r2 = _exchange_chips(hs, "rs_chips")
    out_big = {}
    for k, h, r in zip(GRADS, hs, r2):
        shp = shards[k].shape
        to2d = lambda a: a.reshape(h.shape[1], h.shape[2])
        res = _adamw_final(h, r, chip_idx, to2d(shards[k]), to2d(moms[k][0]), to2d(moms[k][1]), "adamw_" + k)
        out_big[k] = tuple(a.reshape(shp) for a in res)

    order = ("norm_g", "w_ada", "b_ada", "w_in", "pool_w", "pool_scale", "attn_sink", "conv_dw", "conv_dw_b", "conv_ln_g",
             "conv_ln_b", "conv_pw", "wbp", "wba", "wbc", "w_out", "final_g")
    res = {**out_small, **out_big, "w_ada": out_wada}
    outs = [loss, grad_x[None]]
    for j in range(4):
        outs += [res[n][j] for n in order]
    return tuple(outs)
```

```python
import functools

import jax
import jax.numpy as jnp
from jax import lax
from jax.experimental import pallas as pl
from jax.experimental.pallas import tpu as pltpu

F32, BF16 = jnp.float32, jnp.bfloat16
MESH = pl.DeviceIdType.MESH
ANY = pl.BlockSpec(memory_space=pl.ANY)

N_DEV = 8
D = 2048
DEPTH = 4
EPS = 1e-6
IN_WIDTH = 13824
POOL_WINDOWS = (2, 4, 8, 16)
POOL_GROUP = 256
POOL_HALO = 16
CONV_K = 31
CONV_HALO = 32
N_HEADS, N_KV, HEAD_DIM = 16, 4, 64
ATTN_TQ = 256
ATTN_BACK = 128
LANE = 128
VMEM_BIG = 56 * 1024 * 1024

CB_U, CB_Z, CB_Q, CB_AZ, CB_CA, CB_CB, CB_CZ, CB_GP, CB_GA, CB_GC = 0, 1, 2, 3, 4, 5, 6, 7, 9, 11
COL_K, COL_V = 13312, 13568

ADAM_LR, ADAM_B1, ADAM_B2, ADAM_EPS, ADAM_WD, ADAM_STEP = 0.001, 0.9, 0.999, 1e-08, 0.01, 10


def _cp(sem=None, vmem=None):
    return pltpu.CompilerParams(dimension_semantics=sem, vmem_limit_bytes=vmem)


def _sig(x):
    return jax.nn.sigmoid(x)


def _silu(x):
    return x * _sig(x)


def _dsilu(x):
    s = _sig(x)
    return s * (1.0 + x * (1.0 - s))


def _dot(a, b):
    return jnp.dot(a, b, preferred_element_type=F32)


def _dot_tn(a, b):
    return lax.dot_general(a, b, (((0,), (0,)), ((), ())), preferred_element_type=F32)


def _dot_nt(a, b):
    return lax.dot_general(a, b, (((1,), (1,)), ((), ())), preferred_element_type=F32)


def _full(shape):
    n = len(shape)
    return pl.BlockSpec(shape, lambda *_: (0,) * n)


def _my_pos():
    return lax.axis_index("x"), lax.axis_index("y"), lax.axis_index("c")


def _all_gather(xs, name):
    n = len(xs)

    def body(*refs):
        x_refs, o_refs = refs[:n], refs[n:2 * n]
        send_sems, recv_sems, local_sems = refs[2 * n:]
        x, y, c = _my_pos()
        sibling = (x, y, 1 - c)
        chips = [(1 - x, y), (x, 1 - y), (1 - x, 1 - y)]
        me = 4 * x + 2 * y + c

        def slot(px, py, pc):
            return 4 * px + 2 * py + pc

        def copy(t, k, block, to, src=None):
            dst = o_refs[t].at[block]
            return pltpu.make_async_remote_copy(
                src_ref=dst if src is None else src, dst_ref=dst,
                send_sem=send_sems.at[t, k], recv_sem=recv_sems.at[t, k],
                device_id=to, device_id_type=MESH)

        mine = [pltpu.make_async_copy(x_refs[t], o_refs[t].at[me], local_sems.at[t]) for t in range(n)]
        for cp in mine:
            cp.start()
        first = []
        for t in range(n):
            first.append(copy(t, 0, me, sibling, src=x_refs[t]))
            for j, chip in enumerate(chips):
                first.append(copy(t, 1 + j, me, (*chip, c), src=x_refs[t]))
        for cp in first:
            cp.start()
        passed = []
        for j, chip in enumerate(chips):
            for t in range(n):
                copy(t, 1 + j, slot(*chip, c), (x, y, c)).wait_recv()
                fwd = copy(t, 4 + j, slot(*chip, c), sibling)
                fwd.start()
                passed.append(fwd)
        for t in range(n):
            copy(t, 0, slot(x, y, 1 - c), (x, y, c)).wait_recv()
            for j, chip in enumerate(chips):
                copy(t, 4 + j, slot(*chip, 1 - c), (x, y, c)).wait_recv()
        for cp in first + passed:
            cp.wait_send()
        for cp in mine:
            cp.wait()

    return pl.pallas_call(
        body, name=name,
        out_shape=[jax.ShapeDtypeStruct((N_DEV,) + a.shape, a.dtype) for a in xs],
        in_specs=[ANY] * n, out_specs=[ANY] * n,
        scratch_shapes=[pltpu.SemaphoreType.DMA((n, 7)), pltpu.SemaphoreType.DMA((n, 7)),
                        pltpu.SemaphoreType.DMA((n,))],
    )(*xs)


N_PEER = N_DEV - 1
HBM = pl.BlockSpec(memory_space=pltpu.HBM)
SEM = pl.BlockSpec(memory_space=pltpu.SEMAPHORE)
EFFECT = pltpu.SideEffectType.DATAFLOW_SIDE_EFFECTING


def _peer(k):
    x, y, c = _my_pos()
    flip = lambda v, bit: 1 - v if bit else v
    return flip(x, (k >> 2) & 1), flip(y, (k >> 1) & 1), flip(c, k & 1)


def _spread_copies(v_ref, land_ref, send_sems, recv_sems, per_peer):
    x, y, c = _my_pos()
    me = 4 * x + 2 * y + c
    copies = []
    for k in range(1, N_DEV):
        px, py, pc = _peer(k)
        src = v_ref.at[4 * px + 2 * py + pc] if per_peer else v_ref
        copies.append(pltpu.make_async_remote_copy(
            src_ref=src, dst_ref=land_ref.at[me], send_sem=send_sems[k - 1], recv_sem=recv_sems[k - 1],
            device_id=(px, py, pc), device_id_type=MESH))
    return copies


def _spread_start(vs, per_peer, name):
    n = len(vs)
    lands = [(N_DEV,) + (v.shape[1:] if per_peer else v.shape) for v in vs]
    n_sem = 2 * N_PEER * n

    def body(*refs):
        v_refs, land_refs, outs = refs[:n], refs[n:2 * n], refs[2 * n:]
        for t in range(n):
            sems = outs[2 * N_PEER * t:2 * N_PEER * (t + 1)]
            for cp in _spread_copies(v_refs[t], land_refs[t], sems[:N_PEER], sems[N_PEER:], per_peer):
                cp.start()
        token = outs[n_sem + 2 * n]
        token[...] = jnp.zeros_like(token)

    hbm = lambda a: pltpu.with_memory_space_constraint(a, pltpu.HBM)
    return pl.pallas_call(
        body, name=name,
        out_shape=((pltpu.SemaphoreType.DMA(()),) * n_sem + tuple(pltpu.HBM(v.shape, v.dtype) for v in vs)
                   + tuple(pltpu.HBM(s, v.dtype) for s, v in zip(lands, vs)) + (jax.ShapeDtypeStruct((8, LANE), F32),)),
        in_specs=(HBM,) * (2 * n), out_specs=(SEM,) * n_sem + (HBM,) * (2 * n) + (pl.BlockSpec(memory_space=pltpu.VMEM),),
        input_output_aliases={t: n_sem + t for t in range(2 * n)},
        compiler_params=pltpu.CompilerParams(has_side_effects=EFFECT),
    )(*[hbm(v) for v in vs], *[hbm(lax.empty(s, v.dtype)) for s, v in zip(lands, vs)])


def _spread_wait(started, after, per_peer, name):
    n = (len(started) - 1) // (2 * N_PEER + 2)
    n_sem = 2 * N_PEER * n
    sems, thru = started[:n_sem], started[n_sem:n_sem + 2 * n]

    def body(*refs):
        v_refs, land_refs, rest = refs[:n], refs[n:2 * n], refs[2 * n:]
        for t in range(n):
            s = rest[2 * N_PEER * t:2 * N_PEER * (t + 1)]
            for cp in _spread_copies(v_refs[t], land_refs[t], s[:N_PEER], s[N_PEER:], per_peer):
                cp.wait_send()
                cp.wait_recv()

    return pl.pallas_call(
        body, name=name,
        out_shape=tuple(pltpu.HBM(a.shape, a.dtype) for a in thru),
        in_specs=(HBM,) * (2 * n) + (SEM,) * n_sem + (ANY,), out_specs=(HBM,) * (2 * n),
        input_output_aliases={t: t for t in range(2 * n)},
        compiler_params=pltpu.CompilerParams(has_side_effects=EFFECT),
    )(*thru, *sems, after)[n:]


def _mm(a, b, out_dtype, tm, tn, tk=None, name="mm", vmem=None):
    M, K = a.shape
    _, N = b.shape
    tk = K if tk is None else tk
    nk = K // tk
    assert M % tm == 0 and N % tn == 0 and K % tk == 0

    def body(a_ref, b_ref, o_ref, *acc):
        prod = _dot(a_ref[...].astype(BF16), b_ref[...])
        if nk == 1:
            o_ref[...] = prod.astype(out_dtype)
            return
        acc_ref = acc[0] if acc else o_ref
        k = pl.program_id(2)

        @pl.when(k == 0)
        def _():
            acc_ref[...] = prod

        @pl.when(k > 0)
        def _():
            acc_ref[...] += prod

        if acc:
            @pl.when(k == nk - 1)
            def _():
                o_ref[...] = acc_ref[...].astype(out_dtype)

    scratch = [pltpu.VMEM((tm, tn), F32)] if (nk > 1 and out_dtype != F32) else []
    return pl.pallas_call(
        body, name=name, grid=(N // tn, M // tm, nk),
        in_specs=[pl.BlockSpec((tm, tk), lambda j, i, k: (i, k)), pl.BlockSpec((tk, tn), lambda j, i, k: (k, j))],
        out_specs=pl.BlockSpec((tm, tn), lambda j, i, k: (i, j)),
        out_shape=jax.ShapeDtypeStruct((M, N), out_dtype), scratch_shapes=scratch,
        compiler_params=_cp(("parallel", "parallel", "arbitrary"), vmem),
    )(a, b)


def _mm_tn(a, b, tm, tn, ts, name="mm_tn", vmem=None):
    S, Ka = a.shape
    _, N = b.shape
    assert Ka % tm == 0 and N % tn == 0 and S % ts == 0

    def body(a_ref, b_ref, o_ref):
        prod = _dot_tn(a_ref[...].astype(BF16), b_ref[...].astype(BF16))
        k = pl.program_id(2)

        @pl.when(k == 0)
        def _():
            o_ref[...] = prod

        @pl.when(k > 0)
        def _():
            o_ref[...] += prod

    return pl.pallas_call(
        body, name=name, grid=(Ka // tm, N // tn, S // ts),
        in_specs=[pl.BlockSpec((ts, tm), lambda i, j, k: (k, i)), pl.BlockSpec((ts, tn), lambda i, j, k: (k, j))],
        out_specs=pl.BlockSpec((tm, tn), lambda i, j, k: (i, j)),
        out_shape=jax.ShapeDtypeStruct((Ka, N), F32),
        compiler_params=_cp(("parallel", "parallel", "arbitrary"), vmem),
    )(a, b)


def _mod_fwd(c_all, w_ada):
    L, _, n = w_ada.shape

    def body(c_ref, w_ref, o_ref):
        ca = _silu(c_ref[...])
        o_ref[0] = jnp.dot(ca, w_ref[0], preferred_element_type=F32, precision=lax.Precision.HIGHEST)

    return pl.pallas_call(
        body, name="mod_fwd", grid=(L,),
        in_specs=[_full((N_DEV, D)), pl.BlockSpec((1, D, n), lambda l: (l, 0, 0))],
        out_specs=pl.BlockSpec((1, N_DEV, n), lambda l: (l, 0, 0)),
        out_shape=jax.ShapeDtypeStruct((L, N_DEV, n), F32),
        compiler_params=_cp(("parallel",)),
    )(c_all, w_ada)


def _adam_math(g, w, m, v):
    m2 = ADAM_B1 * m + (1.0 - ADAM_B1) * g
    v2 = ADAM_B2 * v + (1.0 - ADAM_B2) * (g * g)
    m_hat = m2 / (1.0 - ADAM_B1 ** ADAM_STEP)
    v_hat = v2 / (1.0 - ADAM_B2 ** ADAM_STEP)
    delta = -ADAM_LR * (m_hat / (jnp.sqrt(v_hat) + ADAM_EPS) + ADAM_WD * w)
    return delta, m2, v2


def _wada_bwd(c_all_t, dmod, w, m, v, tr=256):
    L, _, n = w.shape

    def body(c_ref, d_ref, w_ref, m_ref, v_ref, g_ref, dl_ref, m2_ref, v2_ref):
        ca = _silu(c_ref[...])
        dm = d_ref[0]
        g = ca[:, 0:1] * dm[0:1, :]
        for b in range(1, N_DEV):
            g = g + ca[:, b:b + 1] * dm[b:b + 1, :]
        delta, m2, v2 = _adam_math(g, w_ref[0], m_ref[0], v_ref[0])
        g_ref[0], dl_ref[0], m2_ref[0], v2_ref[0] = g, delta, m2, v2

    blk = pl.BlockSpec((1, tr, n), lambda l, i: (l, i, 0))
    return pl.pallas_call(
        body, name="wada_bwd", grid=(L, D // tr),
        in_specs=[pl.BlockSpec((tr, N_DEV), lambda l, i: (i, 0)), pl.BlockSpec((1, N_DEV, n), lambda l, i: (l, 0, 0)),
                  blk, blk, blk],
        out_specs=[blk] * 4, out_shape=[jax.ShapeDtypeStruct(w.shape, F32)] * 4,
        compiler_params=_cp(("parallel", "parallel")),
    )(c_all_t, dmod, w, m, v)


def _norm_mod(x, g, scale, shift, tb=256):
    S = x.shape[0]

    def body(x_ref, g_ref, sc_ref, sh_ref, h_ref):
        xv = x_ref[...]
        r = lax.rsqrt(jnp.mean(xv * xv, axis=-1, keepdims=True) + EPS)
        h_ref[...] = (xv * r * (g_ref[...] * (1.0 + sc_ref[...])) + sh_ref[...]).astype(BF16)

    row = pl.BlockSpec((tb, D), lambda i: (i, 0))
    vec = _full((1, D))
    return pl.pallas_call(
        body, name="norm_mod", grid=(S // tb,), in_specs=[row, vec, vec, vec], out_specs=row,
        out_shape=jax.ShapeDtypeStruct((S, D), BF16), compiler_params=_cp(("parallel",)),
    )(x, g, scale, shift)


def _pool_mix(u_ref, uh_ref, ubuf, i, tb):
    H = POOL_HALO
    ubuf[H:, :] = u_ref[...].astype(F32)
    ubuf[:H, :] = jnp.where(i > 0, uh_ref[...].astype(F32), 0.0)
    t = i * tb + lax.broadcasted_iota(jnp.int32, (tb, 1), 0)
    mixed = []
    for g, w in enumerate(POOL_WINDOWS):
        cs = slice(g * POOL_GROUP, (g + 1) * POOL_GROUP)
        cur = ubuf[H:, cs]
        acc = cur
        for j in range(1, w):
            acc = acc + ubuf[pl.ds(H - j, tb), cs]
        cnt = jnp.minimum(t + 1, w).astype(F32)
        mixed.append(acc / cnt - cur)
    return mixed


def _pool_specs(tb):
    H = POOL_HALO
    u = pl.BlockSpec((tb, 1024), lambda i: (i, CB_U))
    uh = pl.BlockSpec((H, 1024), lambda i: (jnp.maximum(i * (tb // H) - 1, 0), CB_U))
    z = pl.BlockSpec((tb, 1024), lambda i: (i, CB_Z))
    return u, uh, z


def _pool_fwd(proj, pool_w, pool_scale, tb=256):
    S = proj.shape[0]

    def body(u_ref, uh_ref, z_ref, w_ref, sc_ref, y_ref, ubuf):
        i = pl.program_id(0)
        mixed = _pool_mix(u_ref, uh_ref, ubuf, i, tb)
        m = jnp.concatenate([_dot(mixed[g].astype(BF16), w_ref[g]) for g in range(4)], axis=1)
        y_ref[...] = (m * sc_ref[...] * _silu(z_ref[...].astype(F32))).astype(BF16)

    u, uh, z = _pool_specs(tb)
    return pl.pallas_call(
        body, name="pool_fwd", grid=(S // tb,),
        in_specs=[u, uh, z, _full((4, 256, 256)), _full((1, 1024))],
        out_specs=pl.BlockSpec((tb, 1024), lambda i: (i, 0)),
        out_shape=jax.ShapeDtypeStruct((S, 1024), BF16),
        scratch_shapes=[pltpu.VMEM((tb + POOL_HALO, 1024), F32)],
        compiler_params=_cp(("parallel",)),
    )(proj, proj, proj, pool_w, pool_scale)


def _attn_mask(i):
    TQ, NK = ATTN_TQ, ATTN_TQ + ATTN_BACK
    qc = lax.broadcasted_iota(jnp.int32, (TQ, NK), 0) // 64
    col = lax.broadcasted_iota(jnp.int32, (TQ, NK), 1)
    kc = col // 64
    return (kc >= qc) & (kc <= qc + 2) & ((col >= ATTN_BACK) | (i > 0))


def _attn_probs(q_ref, kw, sink_ref, valid, h):
    qh = q_ref[:, h * HEAD_DIM:(h + 1) * HEAD_DIM]
    s = _dot_nt(qh, kw) * (HEAD_DIM ** -0.5)
    s = jnp.where(valid, s, -jnp.inf)
    sk = sink_ref[h:h + 1, 0:1]
    mx = jnp.maximum(jnp.max(s, axis=-1, keepdims=True), sk)
    p = jnp.exp(s - mx)
    es = jnp.exp(sk - mx)
    den = jnp.sum(p, axis=-1, keepdims=True) + es
    return p / den, es / den


def _attn_fwd(proj, kpad, vpad, sink_b):
    S = proj.shape[0]
    TQ, NK = ATTN_TQ, ATTN_TQ + ATTN_BACK

    def body(q_ref, z_ref, k_ref, v_ref, sink_ref, o_ref, y_ref):
        i = pl.program_id(0)
        start = pl.multiple_of(i * TQ, TQ)
        valid = _attn_mask(i)
        for h in range(N_HEADS):
            kh = h // (N_HEADS // N_KV)
            ks = slice(kh * HEAD_DIM, (kh + 1) * HEAD_DIM)
            hs = slice(h * HEAD_DIM, (h + 1) * HEAD_DIM)
            pn, _ = _attn_probs(q_ref, k_ref[pl.ds(start, NK), ks], sink_ref, valid, h)
            o = _dot(pn.astype(BF16), v_ref[pl.ds(start, NK), ks])
            o_ref[:, hs] = o.astype(BF16)
            y_ref[:, hs] = (o * _silu(z_ref[:, hs].astype(F32))).astype(BF16)

    out = pl.BlockSpec((TQ, 1024), lambda i: (i, 0))
    return pl.pallas_call(
        body, name="attn_fwd", grid=(S // TQ,),
        in_specs=[pl.BlockSpec((TQ, 1024), lambda i: (i, CB_Q)), pl.BlockSpec((TQ, 1024), lambda i: (i, CB_AZ)),
                  _full(kpad.shape), _full(vpad.shape), _full((N_HEADS, LANE))],
        out_specs=[out, out], out_shape=[jax.ShapeDtypeStruct((S, 1024), BF16)] * 2,
        compiler_params=_cp(("parallel",)),
    )(proj, proj, kpad, vpad, sink_b)


def _conv_specs(tb):
    H = CONV_HALO
    prev = lambda i: jnp.maximum(i * (tb // H) - 1, 0)
    a = pl.BlockSpec((tb, 1024), lambda i: (i, CB_CA))
    ah = pl.BlockSpec((H, 1024), lambda i: (prev(i), CB_CA))
    b = pl.BlockSpec((tb, 1024), lambda i: (i, CB_CB))
    bh = pl.BlockSpec((H, 1024), lambda i: (prev(i), CB_CB))
    z = pl.BlockSpec((tb, 1024), lambda i: (i, CB_CZ))
    return a, ah, b, bh, z


def _conv_glu_dw(a_ref, ah_ref, b_ref, bh_ref, dw_ref, gbuf, ybuf, i, tb):
    H = CONV_HALO
    gbuf[H:, :] = a_ref[...].astype(F32) * _sig(b_ref[...].astype(F32))
    gh = ah_ref[...].astype(F32) * _sig(bh_ref[...].astype(F32))
    gbuf[:H, :] = jnp.where(i > 0, gh, 0.0)
    base = H - (CONV_K - 1)
    for c in range(1024 // LANE):
        cs = slice(c * LANE, (c + 1) * LANE)
        for r in range(tb // 128):
            acc = jnp.zeros((128, LANE), F32)
            for j in range(CONV_K):
                acc = acc + dw_ref[j:j + 1, cs] * gbuf[pl.ds(r * 128 + base + j, 128), cs]
            ybuf[r * 128:(r + 1) * 128, cs] = acc


def _layer_norm_fwd(y, g, b):
    mu = jnp.mean(y, axis=-1, keepdims=True)
    yc = y - mu
    rstd = lax.rsqrt(jnp.mean(yc * yc, axis=-1, keepdims=True) + EPS)
    xh = yc * rstd
    return xh, rstd, xh * g + b


def _conv_fwd(proj, dw, dw_b, ln_g, ln_b, pw, tb=256):
    S = proj.shape[0]

    def body(a_ref, ah_ref, b_ref, bh_ref, z_ref, dw_ref, dwb_ref, lg_ref, lb_ref, pw_ref, y_ref, gbuf, ybuf):
        i = pl.program_id(0)
        _conv_glu_dw(a_ref, ah_ref, b_ref, bh_ref, dw_ref, gbuf, ybuf, i, tb)
        _, _, yn = _layer_norm_fwd(ybuf[...] + dwb_ref[...], lg_ref[...], lb_ref[...])
        out = _dot(_silu(yn).astype(BF16), pw_ref[...])
        y_ref[...] = (out * _silu(z_ref[...].astype(F32))).astype(BF16)

    vec = _full((1, 1024))
    return pl.pallas_call(
        body, name="conv_fwd", grid=(S // tb,),
        in_specs=[*_conv_specs(tb), _full((32, 1024)), vec, vec, vec, _full((1024, 1024))],
        out_specs=pl.BlockSpec((tb, 1024), lambda i: (i, 0)),
        out_shape=jax.ShapeDtypeStruct((S, 1024), BF16),
        scratch_shapes=[pltpu.VMEM((tb + CONV_HALO, 1024), F32), pltpu.VMEM((tb, 1024), F32)],
        compiler_params=_cp(("parallel",)),
    )(proj, proj, proj, proj, proj, dw, dw_b, ln_g, ln_b, pw)


def _merge_fwd(yp, ya, yc, wbp, wba, wbc, proj, tm=512, tn=1024):
    S = yp.shape[0]

    def body(yp_ref, ya_ref, yc_ref, wp_ref, wa_ref, wc_ref, gp_ref, ga_ref, gc_ref, m_ref, pp_ref, pa_ref, pc_ref):
        pp = _dot(yp_ref[...], wp_ref[...])
        pa = _dot(ya_ref[...], wa_ref[...])
        pc = _dot(yc_ref[...], wc_ref[...])
        m = (_sig(gp_ref[...].astype(F32)) * pp + _sig(ga_ref[...].astype(F32)) * pa
             + _sig(gc_ref[...].astype(F32)) * pc)
        m_ref[...] = m.astype(BF16)
        pp_ref[...], pa_ref[...], pc_ref[...] = pp.astype(BF16), pa.astype(BF16), pc.astype(BF16)

    yb = pl.BlockSpec((tm, 1024), lambda j, i: (i, 0))
    wb = pl.BlockSpec((1024, tn), lambda j, i: (0, j))
    gate = lambda cb: pl.BlockSpec((tm, tn), lambda j, i: (i, cb + j))
    out = pl.BlockSpec((tm, tn), lambda j, i: (i, j))
    return pl.pallas_call(
        body, name="merge_fwd", grid=(D // tn, S // tm),
        in_specs=[yb, yb, yb, wb, wb, wb, gate(CB_GP), gate(CB_GA), gate(CB_GC)],
        out_specs=[out] * 4, out_shape=[jax.ShapeDtypeStruct((S, D), BF16)] * 4,
        compiler_params=_cp(("parallel", "parallel")),
    )(yp, ya, yc, wbp, wba, wbc, proj, proj, proj)


def _out_fwd(x, merged, w_out, gate, tm=512, tn=1024):
    S = x.shape[0]

    def body(x_ref, m_ref, w_ref, g_ref, o_ref):
        o_ref[...] = x_ref[...] + g_ref[...] * _dot(m_ref[...], w_ref[...])

    xb = pl.BlockSpec((tm, tn), lambda j, i: (i, j))
    return pl.pallas_call(
        body, name="out_fwd", grid=(D // tn, S // tm),
        in_specs=[xb, pl.BlockSpec((tm, D), lambda j, i: (i, 0)), pl.BlockSpec((D, tn), lambda j, i: (0, j)),
                  pl.BlockSpec((1, tn), lambda j, i: (0, j))],
        out_specs=xb, out_shape=jax.ShapeDtypeStruct((S, D), F32),
        compiler_params=_cp(("parallel", "parallel")),
    )(x, merged, w_out, gate)


def _final_loss(x, target, final_g, tb=256):
    S = x.shape[0]

    def body(x_ref, t_ref, g_ref, dx_ref, gg_ref, ls_ref):
        i = pl.program_id(0)
        xv, g = x_ref[...], g_ref[...]
        r = lax.rsqrt(jnp.mean(xv * xv, axis=-1, keepdims=True) + EPS)
        xh = xv * r
        e = xh * g - t_ref[...]
        dy = e * (1.0 / D)
        gy = dy * g
        dx_ref[...] = r * (gy - xh * jnp.mean(gy * xh, axis=-1, keepdims=True))
        gg = jnp.sum(dy * xh, axis=0, keepdims=True)
        ls = jnp.sum(e * e, axis=0, keepdims=True) * (0.5 / D)

        @pl.when(i == 0)
        def _():
            gg_ref[...], ls_ref[...] = gg, ls

        @pl.when(i > 0)
        def _():
            gg_ref[...] += gg
            ls_ref[...] += ls

    row = pl.BlockSpec((tb, D), lambda i: (i, 0))
    vec = _full((1, D))
    return pl.pallas_call(
        body, name="final_loss", grid=(S // tb,), in_specs=[row, row, vec], out_specs=[row, vec, vec],
        out_shape=[jax.ShapeDtypeStruct((S, D), F32), jax.ShapeDtypeStruct((1, D), F32),
                   jax.ShapeDtypeStruct((1, D), F32)],
        compiler_params=_cp(("arbitrary",)),
    )(x, target, final_g)


def _out_bwd(dx, gate, w_out_t, pp, pa, pc, proj, tm=512, tn=1024):
    S = dx.shape[0]

    def body(dx_ref, g_ref, w_ref, pp_ref, pa_ref, pc_ref, gp_ref, ga_ref, gc_ref,
             dpp_ref, dpa_ref, dpc_ref, dgp_ref, dga_ref, dgc_ref):
        dm = _dot((dx_ref[...] * g_ref[...]).astype(BF16), w_ref[...])
        for p_ref, gl_ref, dp_ref, dg_ref in ((pp_ref, gp_ref, dpp_ref, dgp_ref), (pa_ref, ga_ref, dpa_ref, dga_ref),
                                              (pc_ref, gc_ref, dpc_ref, dgc_ref)):
            s = _sig(gl_ref[...].astype(F32))
            dp_ref[...] = (dm * s).astype(BF16)
            dg_ref[...] = (dm * p_ref[...].astype(F32) * s * (1.0 - s)).astype(BF16)

    out = pl.BlockSpec((tm, tn), lambda j, i: (i, j))
    gate_spec = lambda cb: pl.BlockSpec((tm, tn), lambda j, i: (i, cb + j))
    return pl.pallas_call(
        body, name="out_bwd", grid=(D // tn, S // tm),
        in_specs=[pl.BlockSpec((tm, D), lambda j, i: (i, 0)), _full((1, D)), pl.BlockSpec((D, tn), lambda j, i: (0, j)),
                  out, out, out, gate_spec(CB_GP), gate_spec(CB_GA), gate_spec(CB_GC)],
        out_specs=[out] * 6, out_shape=[jax.ShapeDtypeStruct((S, D), BF16)] * 6,
        compiler_params=_cp(("parallel", "parallel")),
    )(dx, gate, w_out_t, pp, pa, pc, proj, proj, proj)


def _wout_post(gmat, w_out, gate, tr=256):
    def body(g_ref, w_ref, gate_ref, dw_ref, dg_ref):
        i = pl.program_id(0)
        gm = g_ref[...]
        dw_ref[...] = gm * gate_ref[...]
        part = jnp.sum(gm * w_ref[...].astype(F32), axis=0, keepdims=True)

        @pl.when(i == 0)
        def _():
            dg_ref[...] = part

        @pl.when(i > 0)
        def _():
            dg_ref[...] += part

    row = pl.BlockSpec((tr, D), lambda i: (i, 0))
    return pl.pallas_call(
        body, name="wout_post", grid=(D // tr,), in_specs=[row, row, _full((1, D))], out_specs=[row, _full((1, D))],
        out_shape=[jax.ShapeDtypeStruct((D, D), F32), jax.ShapeDtypeStruct((1, D), F32)],
        compiler_params=_cp(("arbitrary",)),
    )(gmat, w_out, gate)


def _pool_bwd_a(dy, proj, pool_w, pool_w_t, pool_scale, tb=256):
    S = proj.shape[0]

    def body(dy_ref, u_ref, uh_ref, z_ref, w_ref, wt_ref, sc_ref, dmix_ref, dz_ref, dsc_ref, dw_ref, ubuf):
        i = pl.program_id(0)
        mixed = [m.astype(BF16) for m in _pool_mix(u_ref, uh_ref, ubuf, i, tb)]
        m = jnp.concatenate([_dot(mixed[g], w_ref[g]) for g in range(4)], axis=1)
        dyv, z, sc = dy_ref[...].astype(F32), z_ref[...].astype(F32), sc_ref[...]
        dyp = dyv * _silu(z)
        dz_ref[...] = (dyv * (m * sc) * _dsilu(z)).astype(BF16)
        dsc = jnp.sum(dyp * m, axis=0, keepdims=True)
        dmm = (dyp * sc).astype(BF16)
        dws = []
        for g in range(4):
            cs = slice(g * POOL_GROUP, (g + 1) * POOL_GROUP)
            dmix_ref[:, cs] = _dot(dmm[:, cs], wt_ref[g])
            dws.append(_dot_tn(mixed[g], dmm[:, cs]))

        @pl.when(i == 0)
        def _():
            dsc_ref[...] = dsc
            for g in range(4):
                dw_ref[g] = dws[g]

        @pl.when(i > 0)
        def _():
            dsc_ref[...] += dsc
            for g in range(4):
                dw_ref[g] += dws[g]

    u, uh, z = _pool_specs(tb)
    row = pl.BlockSpec((tb, 1024), lambda i: (i, 0))
    wfull = _full((4, 256, 256))
    return pl.pallas_call(
        body, name="pool_bwd_a", grid=(S // tb,),
        in_specs=[row, u, uh, z, wfull, wfull, _full((1, 1024))],
        out_specs=[row, row, _full((1, 1024)), wfull],
        out_shape=[jax.ShapeDtypeStruct((S, 1024), F32), jax.ShapeDtypeStruct((S, 1024), BF16),
                   jax.ShapeDtypeStruct((1, 1024), F32), jax.ShapeDtypeStruct((4, 256, 256), F32)],
        scratch_shapes=[pltpu.VMEM((tb + POOL_HALO, 1024), F32)],
        compiler_params=_cp(("arbitrary",)),
    )(dy, proj, proj, proj, pool_w, pool_w_t, pool_scale)


def _pool_bwd_b(dmix, tb=256):
    S = dmix.shape[0]
    H = POOL_HALO
    nb = S // tb

    def body(dm_ref, dh_ref, du_ref, ebuf):
        i = pl.program_id(0)
        t = i * tb + lax.broadcasted_iota(jnp.int32, (tb, 1), 0)
        th = (i + 1) * tb + lax.broadcasted_iota(jnp.int32, (H, 1), 0)
        for g, w in enumerate(POOL_WINDOWS):
            cs = slice(g * POOL_GROUP, (g + 1) * POOL_GROUP)
            ebuf[:tb, cs] = dm_ref[:, cs] / jnp.minimum(t + 1, w).astype(F32)
            eh = dh_ref[:, cs] / jnp.minimum(th + 1, w).astype(F32)
            ebuf[tb:, cs] = jnp.where(i < nb - 1, eh, 0.0)
        for g, w in enumerate(POOL_WINDOWS):
            cs = slice(g * POOL_GROUP, (g + 1) * POOL_GROUP)
            acc = ebuf[:tb, cs]
            for j in range(1, w):
                acc = acc + ebuf[pl.ds(j, tb), cs]
            du_ref[:, cs] = (acc - dm_ref[:, cs]).astype(BF16)

    row = pl.BlockSpec((tb, 1024), lambda i: (i, 0))
    nxt = pl.BlockSpec((H, 1024), lambda i: (jnp.minimum((i + 1) * (tb // H), S // H - 1), 0))
    return pl.pallas_call(
        body, name="pool_bwd_b", grid=(nb,), in_specs=[row, nxt], out_specs=row,
        out_shape=jax.ShapeDtypeStruct((S, 1024), BF16),
        scratch_shapes=[pltpu.VMEM((tb + H, 1024), F32)],
        compiler_params=_cp(("parallel",)),
    )(dmix, dmix)


def _attn_bwd(dy, o, proj, kpad, vpad, sink_b):
    S = proj.shape[0]
    TQ, NK = ATTN_TQ, ATTN_TQ + ATTN_BACK
    nb = S // TQ
    G = N_HEADS // N_KV

    def body(dy_ref, o_ref, q_ref, z_ref, k_ref, v_ref, sink_ref, dq_ref, dz_ref, dk_hbm, dv_hbm, ds_ref, dk_acc, dv_acc):
        i = pl.program_id(0)

        @pl.when(i == 0)
        def _():
            dk_acc[...] = jnp.zeros_like(dk_acc)
            dv_acc[...] = jnp.zeros_like(dv_acc)
            ds_ref[...] = jnp.zeros_like(ds_ref)

        start = pl.multiple_of(i * TQ, TQ)
        valid = _attn_mask(i)
        dks, dvs = [], []
        for kh in range(N_KV):
            ks = slice(kh * HEAD_DIM, (kh + 1) * HEAD_DIM)
            kw = k_ref[pl.ds(start, NK), ks]
            vw = v_ref[pl.ds(start, NK), ks]
            dk_sum = jnp.zeros((NK, HEAD_DIM), F32)
            dv_sum = jnp.zeros((NK, HEAD_DIM), F32)
            for gi in range(G):
                h = kh * G + gi
                hs = slice(h * HEAD_DIM, (h + 1) * HEAD_DIM)
                pn, psink = _attn_probs(q_ref, kw, sink_ref, valid, h)
                z = z_ref[:, hs].astype(F32)
                dyv = dy_ref[:, hs].astype(F32)
                ov = o_ref[:, hs].astype(F32)
                do = dyv * _silu(z)
                dz_ref[:, hs] = (dyv * ov * _dsilu(z)).astype(BF16)
                delta = jnp.sum(do * ov, axis=-1, keepdims=True)
                dob = do.astype(BF16)
                dp = _dot_nt(dob, vw)
                ds = (pn * (dp - delta)).astype(BF16)
                dsink = -jnp.sum(psink * delta, axis=0, keepdims=True)
                ds_ref[h:h + 1, :] += jnp.broadcast_to(dsink, (1, LANE))
                dq_ref[:, hs] = (_dot(ds, kw) * (HEAD_DIM ** -0.5)).astype(BF16)
                dk_sum = dk_sum + _dot_tn(ds, q_ref[:, hs])
                dv_sum = dv_sum + _dot_tn(pn.astype(BF16), dob)
            dks.append(dk_sum * (HEAD_DIM ** -0.5))
            dvs.append(dv_sum)
        dk_acc[pl.ds(start, NK), :] += jnp.concatenate(dks, axis=1)
        dv_acc[pl.ds(start, NK), :] += jnp.concatenate(dvs, axis=1)

        @pl.when(i == nb - 1)
        def _():
            pltpu.sync_copy(dk_acc, dk_hbm)
            pltpu.sync_copy(dv_acc, dv_hbm)

    row = pl.BlockSpec((TQ, 1024), lambda i: (i, 0))
    return pl.pallas_call(
        body, name="attn_bwd", grid=(nb,),
        in_specs=[row, row, pl.BlockSpec((TQ, 1024), lambda i: (i, CB_Q)), pl.BlockSpec((TQ, 1024), lambda i: (i, CB_AZ)),
                  _full(kpad.shape), _full(vpad.shape), _full((N_HEADS, LANE))],
        out_specs=[row, row, ANY, ANY, _full((N_HEADS, LANE))],
        out_shape=[jax.ShapeDtypeStruct((S, 1024), BF16), jax.ShapeDtypeStruct((S, 1024), BF16),
                   jax.ShapeDtypeStruct(kpad.shape, F32), jax.ShapeDtypeStruct(vpad.shape, F32),
                   jax.ShapeDtypeStruct((N_HEADS, LANE), F32)],
        scratch_shapes=[pltpu.VMEM(kpad.shape, F32), pltpu.VMEM(vpad.shape, F32)],
        compiler_params=_cp(("arbitrary",), VMEM_BIG),
    )(dy, o, proj, proj, kpad, vpad, sink_b)


def _conv_bwd_a(dy, proj, dw, dw_b, ln_g, ln_b, pw, pw_t, tb=256):
    S = proj.shape[0]
    H = CONV_HALO
    base = H - (CONV_K - 1)

    def body(dy_ref, a_ref, ah_ref, b_ref, bh_ref, z_ref, dw_ref, dwb_ref, lg_ref, lb_ref, pw_ref, pwt_ref,
             dcv_ref, dz_ref, dpw_ref, dlg_ref, dlb_ref, ddwb_ref, ddw_ref, gbuf, ybuf):
        i = pl.program_id(0)
        _conv_glu_dw(a_ref, ah_ref, b_ref, bh_ref, dw_ref, gbuf, ybuf, i, tb)
        lg = lg_ref[...]
        xh, rstd, yn = _layer_norm_fwd(ybuf[...] + dwb_ref[...], lg, lb_ref[...])
        u = _silu(yn).astype(BF16)
        out = _dot(u, pw_ref[...])
        dyv, z = dy_ref[...].astype(F32), z_ref[...].astype(F32)
        dz_ref[...] = (dyv * out * _dsilu(z)).astype(BF16)
        dout = (dyv * _silu(z)).astype(BF16)
        dpw = _dot_tn(u, dout)
        dyn = _dot(dout, pwt_ref[...]) * _dsilu(yn)
        dlg = jnp.sum(dyn * xh, axis=0, keepdims=True)
        dlb = jnp.sum(dyn, axis=0, keepdims=True)
        dxh = dyn * lg
        dcv = rstd * (dxh - jnp.mean(dxh, axis=-1, keepdims=True) - xh * jnp.mean(dxh * xh, axis=-1, keepdims=True))
        dcv_ref[...] = dcv
        ddwb = jnp.sum(dcv, axis=0, keepdims=True)
        ybuf[...] = dcv

        @pl.when(i == 0)
        def _():
            dpw_ref[...], dlg_ref[...], dlb_ref[...], ddwb_ref[...] = dpw, dlg, dlb, ddwb
            ddw_ref[...] = jnp.zeros_like(ddw_ref)

        @pl.when(i > 0)
        def _():
            dpw_ref[...] += dpw
            dlg_ref[...] += dlg
            dlb_ref[...] += dlb
            ddwb_ref[...] += ddwb

        for c in range(1024 // LANE):
            cs = slice(c * LANE, (c + 1) * LANE)
            for j in range(CONV_K):
                acc = jnp.zeros((1, LANE), F32)
                for r in range(tb // 128):
                    acc = acc + jnp.sum(ybuf[r * 128:(r + 1) * 128, cs] * gbuf[pl.ds(r * 128 + base + j, 128), cs],
                                        axis=0, keepdims=True)
                ddw_ref[j:j + 1, cs] += acc

    vec = _full((1, 1024))
    row = pl.BlockSpec((tb, 1024), lambda i: (i, 0))
    big = _full((1024, 1024))
    return pl.pallas_call(
        body, name="conv_bwd_a", grid=(S // tb,),
        in_specs=[row, *_conv_specs(tb), _full((32, 1024)), vec, vec, vec, big, big],
        out_specs=[row, row, big, vec, vec, vec, _full((32, 1024))],
        out_shape=[jax.ShapeDtypeStruct((S, 1024), F32), jax.ShapeDtypeStruct((S, 1024), BF16),
                   jax.ShapeDtypeStruct((1024, 1024), F32), jax.ShapeDtypeStruct((1, 1024), F32),
                   jax.ShapeDtypeStruct((1, 1024), F32), jax.ShapeDtypeStruct((1, 1024), F32),
                   jax.ShapeDtypeStruct((32, 1024), F32)],
        scratch_shapes=[pltpu.VMEM((tb + H, 1024), F32), pltpu.VMEM((tb, 1024), F32)],
        compiler_params=_cp(("arbitrary",)),
    )(dy, proj, proj, proj, proj, proj, dw, dw_b, ln_g, ln_b, pw, pw_t)


def _conv_bwd_b(dcv, proj, dw, tb=256):
    S = proj.shape[0]
    H = CONV_HALO
    nb = S // tb

    def body(d_ref, dn_ref, a_ref, b_ref, dw_ref, da_ref, db_ref, dbuf, gbuf):
        i = pl.program_id(0)
        dbuf[:tb, :] = d_ref[...]
        dbuf[tb:, :] = jnp.where(i < nb - 1, dn_ref[...], 0.0)
        for c in range(1024 // LANE):
            cs = slice(c * LANE, (c + 1) * LANE)
            for r in range(tb // 128):
                acc = jnp.zeros((128, LANE), F32)
                for j in range(CONV_K):
                    acc = acc + dw_ref[j:j + 1, cs] * dbuf[pl.ds(r * 128 + (CONV_K - 1) - j, 128), cs]
                gbuf[r * 128:(r + 1) * 128, cs] = acc
        dg = gbuf[...]
        a, s = a_ref[...].astype(F32), _sig(b_ref[...].astype(F32))
        da_ref[...] = (dg * s).astype(BF16)
        db_ref[...] = (dg * a * s * (1.0 - s)).astype(BF16)

    row = pl.BlockSpec((tb, 1024), lambda i: (i, 0))
    nxt = pl.BlockSpec((H, 1024), lambda i: (jnp.minimum((i + 1) * (tb // H), S // H - 1), 0))
    return pl.pallas_call(
        body, name="conv_bwd_b", grid=(nb,),
        in_specs=[row, nxt, pl.BlockSpec((tb, 1024), lambda i: (i, CB_CA)), pl.BlockSpec((tb, 1024), lambda i: (i, CB_CB)),
                  _full((32, 1024))],
        out_specs=[row, row], out_shape=[jax.ShapeDtypeStruct((S, 1024), BF16)] * 2,
        scratch_shapes=[pltpu.VMEM((tb + H, 1024), F32), pltpu.VMEM((tb, 1024), F32)],
        compiler_params=_cp(("parallel",)),
    )(dcv, dcv, proj, proj, dw)


def _norm_bwd(dh, x, dx_out, g, scale, tb=256):
    S = x.shape[0]

    def body(dh_ref, x_ref, dxo_ref, g_ref, sc_ref, dx_ref, dsh_ref, da_ref):
        i = pl.program_id(0)
        xv, dhv = x_ref[...], dh_ref[...]
        r = lax.rsqrt(jnp.mean(xv * xv, axis=-1, keepdims=True) + EPS)
        xh = xv * r
        gy = dhv * (g_ref[...] * (1.0 + sc_ref[...]))
        dx_ref[...] = dxo_ref[...] + r * (gy - xh * jnp.mean(gy * xh, axis=-1, keepdims=True))
        dsh = jnp.sum(dhv, axis=0, keepdims=True)
        da = jnp.sum(dhv * xh, axis=0, keepdims=True)

        @pl.when(i == 0)
        def _():
            dsh_ref[...], da_ref[...] = dsh, da

        @pl.when(i > 0)
        def _():
            dsh_ref[...] += dsh
            da_ref[...] += da

    row = pl.BlockSpec((tb, D), lambda i: (i, 0))
    vec = _full((1, D))
    return pl.pallas_call(
        body, name="norm_bwd", grid=(S // tb,), in_specs=[row, row, row, vec, vec], out_specs=[row, vec, vec],
        out_shape=[jax.ShapeDtypeStruct((S, D), F32), jax.ShapeDtypeStruct((1, D), F32), jax.ShapeDtypeStruct((1, D), F32)],
        compiler_params=_cp(("arbitrary",)),
    )(dh, x, dx_out, g, scale)


def _mod_bwd(d_a, norm_g, scale):
    def body(da_ref, g_ref, sc_ref, dg_ref, dsc_ref):
        dg_ref[...] = da_ref[...] * (1.0 + sc_ref[...])
        dsc_ref[...] = da_ref[...] * g_ref[...]

    return pl.pallas_call(body, name="mod_bwd", out_shape=[jax.ShapeDtypeStruct(d_a.shape, F32)] * 2)(d_a, norm_g, scale)


def _reduce_adamw(parts, w, m, v, name):
    L, rows, C = w.shape
    tr = rows if rows % 64 else 64

    def body(*refs):
        p_refs, (w_ref, m_ref, v_ref, g_ref, dl_ref, m2_ref, v2_ref) = refs[:L * N_DEV], refs[L * N_DEV:]
        for l in range(L):
            g = p_refs[l * N_DEV][0].astype(F32)
            for k in range(1, N_DEV):
                g = g + p_refs[l * N_DEV + k][0].astype(F32)
            g_ref[l] = g
            dl_ref[l], m2_ref[l], v2_ref[l] = _adam_math(g, w_ref[l], m_ref[l], v_ref[l])

    slot = lambda k: pl.BlockSpec((1, tr, C), lambda i: (k, i, 0))
    blk = pl.BlockSpec((L, tr, C), lambda i: (0, i, 0))
    return pl.pallas_call(
        body, name=name, grid=(rows // tr,), in_specs=[slot(k) for _ in range(L) for k in range(N_DEV)] + [blk] * 3,
        out_specs=[blk] * 4, out_shape=[jax.ShapeDtypeStruct((L, rows, C), F32)] * 4,
        compiler_params=_cp(("parallel",)),
    )(*[p for p in parts for _ in range(N_DEV)], w, m, v)


def _small_final(parts, w, m, v):
    R = w.shape[0]

    def body(p_ref, w_ref, m_ref, v_ref, g_ref, dl_ref, m2_ref, v2_ref):
        g = p_ref[0]
        for k in range(1, N_DEV):
            g = g + p_ref[k]
        delta, m2, v2 = _adam_math(g, w_ref[...], m_ref[...], v_ref[...])
        g_ref[...], dl_ref[...], m2_ref[...], v2_ref[...] = g, delta, m2, v2

    return pl.pallas_call(body, name="small_final", out_shape=[jax.ShapeDtypeStruct((R, LANE), F32)] * 4)(parts, w, m, v)


def _layer_fwd(x, mod, small, W):
    shift, scale, gate = mod
    h = _norm_mod(x, small["norm_g"], scale, shift)
    proj = _mm(h, W["w_in"], BF16, 512, 1536, name="proj_mm")
    y_pool = _pool_fwd(proj, W["pool_w"], small["pool_scale"])
    kpad = jnp.pad(proj[:, COL_K:COL_K + 256], ((ATTN_BACK, 0), (0, 0)))
    vpad = jnp.pad(proj[:, COL_V:COL_V + 256], ((ATTN_BACK, 0), (0, 0)))
    o, y_attn = _attn_fwd(proj, kpad, vpad, small["sink_b"])
    y_conv = _conv_fwd(proj, W["conv_dw"], small["conv_dw_b"], small["conv_ln_g"], small["conv_ln_b"], W["conv_pw"])
    merged, pp, pa, pc = _merge_fwd(y_pool, y_attn, y_conv, W["wbp"], W["wba"], W["wbc"], proj)
    x_new = _out_fwd(x, merged, W["w_out"], gate)
    stash = dict(x=x, h=h, proj=proj, kpad=kpad, vpad=vpad, o=o, y_pool=y_pool, y_attn=y_attn, y_conv=y_conv,
                 merged=merged, pp=pp, pa=pa, pc=pc)
    return x_new, stash


def _layer_bwd(dx, st, mod, small, W):
    shift, scale, gate = mod
    proj = st["proj"]
    gmat = _mm_tn(st["merged"], dx, 1024, 1024, 1024, name="wout_tn")
    d_w_out, d_gate = _wout_post(gmat, W["w_out"], gate)
    dpp, dpa, dpc, dgp, dga, dgc = _out_bwd(dx, gate, W["w_out_t"], st["pp"], st["pa"], st["pc"], proj)
    dy_pool = _mm(dpp, W["wbp_t"], BF16, 512, 1024, name="branch_bwd_mm")
    dy_attn = _mm(dpa, W["wba_t"], BF16, 512, 1024, name="branch_bwd_mm")
    dy_conv = _mm(dpc, W["wbc_t"], BF16, 512, 1024, name="branch_bwd_mm")
    d_wbp = _mm_tn(st["y_pool"], dpp, 1024, 1024, 1024, name="branch_tn")
    d_wba = _mm_tn(st["y_attn"], dpa, 1024, 1024, 1024, name="branch_tn")
    d_wbc = _mm_tn(st["y_conv"], dpc, 1024, 1024, 1024, name="branch_tn")

    dmix, dz_pool, d_pool_scale, d_pool_w = _pool_bwd_a(dy_pool, proj, W["pool_w"], W["pool_w_t"], small["pool_scale"])
    du = _pool_bwd_b(dmix)
    dq, dz_attn, dk, dv, d_sink = _attn_bwd(dy_attn, st["o"], proj, st["kpad"], st["vpad"], small["sink_b"])
    dcv, dz_conv, d_pw, d_ln_g, d_ln_b, d_dw_b, d_dw = _conv_bwd_a(
        dy_conv, proj, W["conv_dw"], small["conv_dw_b"], small["conv_ln_g"], small["conv_ln_b"], W["conv_pw"], W["conv_pw_t"])
    da, db = _conv_bwd_b(dcv, proj, W["conv_dw"])

    dproj = jnp.concatenate([du, dz_pool, dq, dz_attn, da, db, dz_conv, dgp, dga, dgc,
                             dk[ATTN_BACK:].astype(BF16), dv[ATTN_BACK:].astype(BF16)], axis=1)
    dh = _mm(dproj, W["w_in_t"], F32, 1024, 2048, 1536, name="dh_mm", vmem=VMEM_BIG)
    d_w_in = _mm_tn(st["h"], dproj, 1024, 1536, 1024, name="win_tn", vmem=VMEM_BIG)
    dx_in, d_shift, d_a = _norm_bwd(dh, st["x"], dx, small["norm_g"], scale)
    big = dict(w_in=d_w_in, pool_w=d_pool_w, conv_dw=d_dw[:CONV_K], conv_pw=d_pw, wbp=d_wbp, wba=d_wba, wbc=d_wbc,
               w_out=d_w_out)
    sm = dict(d_a=d_a, d_shift=d_shift, d_gate=d_gate, pool_scale=d_pool_scale, attn_sink=d_sink[:, 0],
              conv_dw_b=d_dw_b, conv_ln_g=d_ln_g, conv_ln_b=d_ln_b)
    return dx_in, big, sm


def _local_step(x, target, mods, smalls, get_w, final_g, put_g):
    stashes, Ws = [], []
    for l in range(DEPTH):
        Ws.append(get_w(l, x))
        x, st = _layer_fwd(x, mods[l], smalls[l], Ws[l])
        stashes.append(st)
    dx, d_final_g, loss_lanes = _final_loss(x, target, final_g)
    sms = [None] * DEPTH
    tok = None
    for l in reversed(range(DEPTH)):
        shift, scale, gate = mods[l]
        if tok is not None:
            gate = gate + tok
        dx, big, sms[l] = _layer_bwd(dx, stashes[l], (shift, scale, gate), smalls[l], Ws[l])
        tok = put_g(l, big, dx)
    return loss_lanes, dx, d_final_g, sms


BIG = ("w_in", "pool_w", "conv_pw", "wbp", "wba", "wbc", "w_out")
GRADS = BIG + ("conv_dw",)


def _to_internal(w):
    return jnp.concatenate([w[..., :3072], w[..., 3584:], w[..., 3072:3584]], axis=-1)


def _from_internal(w):
    return jnp.concatenate([w[..., :3072], w[..., COL_K:], w[..., 3072:COL_K]], axis=-1)


def _full_weights(g, conv_dw):
    t = lambda a: jnp.swapaxes(a, -1, -2)
    cols = lambda a: jnp.transpose(a, (1, 0, 2)).reshape(a.shape[1], -1)
    w_in = _to_internal(cols(g["w_in"]))
    pool_w = jnp.transpose(g["pool_w"], (1, 0, 2, 3)).reshape(4, 256, 256)
    conv_pw = g["conv_pw"].reshape(1024, 1024)
    wbp, wba, wbc = cols(g["wbp"]), cols(g["wba"]), cols(g["wbc"])
    w_out = g["w_out"].reshape(D, D)
    conv_dw = jnp.pad(cols(conv_dw), ((0, 32 - CONV_K), (0, 0)))
    return dict(w_in=w_in, w_in_t=t(w_in), pool_w=pool_w, pool_w_t=t(pool_w), conv_pw=conv_pw, conv_pw_t=t(conv_pw),
                wbp=wbp, wba=wba, wbc=wbc, wbp_t=t(wbp), wba_t=t(wba), wbc_t=t(wbc), w_out=w_out, w_out_t=t(w_out),
                conv_dw=conv_dw)


def _pieces(name, g):
    if name == "w_in":
        return jnp.transpose(_from_internal(g).reshape(D, 8, IN_WIDTH // 8), (1, 0, 2))
    if name == "pool_w":
        return jnp.transpose(g.reshape(4, 8, 32, 256), (1, 0, 2, 3))
    if name == "conv_dw":
        return jnp.transpose(g.reshape(CONV_K, 8, 128), (1, 0, 2))
    if name == "conv_pw":
        return g.reshape(8, 128, 1024)
    if name in ("wbp", "wba", "wbc"):
        return jnp.transpose(g.reshape(1024, 8, 256), (1, 0, 2))
    return g.reshape(8, 256, D)


def _pack_small(items, rows):
    flat = jnp.concatenate([a.reshape(-1).astype(F32) for a in items])
    return jnp.pad(flat, (0, rows * LANE - flat.shape[0])).reshape(rows, LANE)


def _unpack_small(packed, shapes):
    flat, out, off = packed.reshape(-1), [], 0
    for s in shapes:
        n = 1
        for d in s:
            n *= d
        out.append(flat[off:off + n].reshape(s))
        off += n
    return out


def kernel(x, c, norm_g, w_ada, b_ada, w_in, pool_w, pool_scale, attn_sink, conv_dw, conv_dw_b, conv_ln_g, conv_ln_b, conv_pw, w_branch_pool, w_branch_attn, w_branch_conv, w_out, final_g, loss_target, m_norm_g, m_w_ada, m_b_ada, m_w_in, m_pool_w, m_pool_scale, m_attn_sink, m_conv_dw, m_conv_dw_b, m_conv_ln_g, m_conv_ln_b, m_conv_pw, m_w_branch_pool, m_w_branch_attn, m_w_branch_conv, m_w_out, m_final_g, v_norm_g, v_w_ada, v_b_ada, v_w_in, v_pool_w, v_pool_scale, v_attn_sink, v_conv_dw, v_conv_dw_b, v_conv_ln_g, v_conv_ln_b, v_conv_pw, v_w_branch_pool, v_w_branch_attn, v_w_branch_conv, v_w_out, v_final_g):
    L = DEPTH
    me = 4 * lax.axis_index("x") + 2 * lax.axis_index("y") + lax.axis_index("c")
    shards = dict(w_in=w_in, pool_w=pool_w, conv_dw=conv_dw, conv_pw=conv_pw, wbp=w_branch_pool, wba=w_branch_attn,
                  wbc=w_branch_conv, w_out=w_out)
    moms = dict(w_in=(m_w_in, v_w_in), pool_w=(m_pool_w, v_pool_w), conv_dw=(m_conv_dw, v_conv_dw),
                conv_pw=(m_conv_pw, v_conv_pw), wbp=(m_w_branch_pool, v_w_branch_pool),
                wba=(m_w_branch_attn, v_w_branch_attn), wbc=(m_w_branch_conv, v_w_branch_conv), w_out=(m_w_out, v_w_out))

    gathers, tok = [], 0.0
    for l in range(L):
        mine = [(shards[k][l] + tok if k == BIG[0] else shards[k][l]).astype(BF16) for k in BIG]
        started = _spread_start(mine, False, f"gather_start_{l}")
        gathers.append((mine, started))
        tok = started[-1][0, 0]

    n_cd = L * CONV_K * 128
    first = _all_gather([_pack_small([c + tok, conv_dw], 144)], "gather_c")[0].reshape(N_DEV, -1)
    c_all = first[:, :D]
    conv_dw_all = first[:, D:D + n_cd].reshape(N_DEV, L, CONV_K, 128)

    mod_part = _mod_fwd(c_all, w_ada)
    mod_all = _all_gather([mod_part.reshape(-1, LANE)], "gather_mod")[0].reshape(N_DEV, L, N_DEV, -1)
    mod = jnp.transpose(lax.dynamic_index_in_dim(mod_all, me, axis=2, keepdims=False), (1, 0, 2)).reshape(L, 3 * D)
    mod = mod + b_ada
    mods = [(mod[l:l + 1, :D], mod[l:l + 1, D:2 * D], mod[l:l + 1, 2 * D:]) for l in range(L)]

    sink_b = jnp.broadcast_to(attn_sink[:, :, None], (L, N_HEADS, LANE))
    smalls = [dict(norm_g=norm_g[l:l + 1], pool_scale=pool_scale[l:l + 1], sink_b=sink_b[l], conv_dw_b=conv_dw_b[l:l + 1],
                   conv_ln_g=conv_ln_g[l:l + 1], conv_ln_b=conv_ln_b[l:l + 1]) for l in range(L)]

    def with_mine(landed, mine):
        return lax.dynamic_update_slice(landed, mine, (me,) + (0,) * (landed.ndim - 1))

    def get_w(l, x_in):
        mine, started = gathers[l]
        landed = _spread_wait(started, x_in, False, f"gather_wait_{l}")
        g = {k: with_mine(a, b[None]) for k, a, b in zip(BIG, landed, mine)}
        return _full_weights(g, conv_dw_all[:, l])

    pending, parts = {}, [None] * L

    def finish(l, after):
        pieces, started = pending.pop(l)
        landed = _spread_wait(started, after, True, f"scatter_wait_{l}")
        parts[l] = {k: with_mine(a, lax.dynamic_slice_in_dim(b, me, 1, axis=0)) for k, a, b in zip(GRADS, landed, pieces)}

    def put_g(l, big, dx):
        if l + 1 in pending:
            finish(l + 1, dx)
        pieces = [_pieces(k, big[k]).astype(BF16) for k in GRADS]
        started = pending.setdefault(l, (pieces, _spread_start(pieces, True, f"scatter_start_{l}")))[1]
        return started[-1][0:1, 0:1]

    loss_lanes, grad_x, d_final_g, sms = _local_step(x[0], loss_target[0], mods, smalls, get_w, final_g.reshape(1, D), put_g)
    finish(0, grad_x)

    stack = lambda k: jnp.concatenate([sms[l][k].reshape(1, -1) for l in range(L)], axis=0)
    scale_all = jnp.concatenate([mods[l][1] for l in range(L)], axis=0)
    d_norm_g, d_scale = _mod_bwd(stack("d_a"), norm_g, scale_all)
    dmod = jnp.concatenate([stack("d_shift"), d_scale, stack("d_gate")], axis=1)
    small_names = ("norm_g", "b_ada", "pool_scale", "attn_sink", "conv_dw_b", "conv_ln_g", "conv_ln_b", "final_g")
    small_g = (d_norm_g, dmod, stack("pool_scale"), stack("attn_sink"), stack("conv_dw_b"), stack("conv_ln_g"),
               stack("conv_ln_b"), d_final_g.reshape(D))
    small_w = (norm_g, b_ada, pool_scale, attn_sink, conv_dw_b, conv_ln_g, conv_ln_b, final_g)
    small_m = (m_norm_g, m_b_ada, m_pool_scale, m_attn_sink, m_conv_dw_b, m_conv_ln_g, m_conv_ln_b, m_final_g)
    small_v = (v_norm_g, v_b_ada, v_pool_scale, v_attn_sink, v_conv_dw_b, v_conv_ln_g, v_conv_ln_b, v_final_g)
    shapes = [a.shape for a in small_w] + [(D,)]
    n_small = sum(a.size for a in small_w) + D
    R = -(-n_small // (8 * LANE)) * 8
    zero = jnp.zeros((D,), F32)
    small_parts = _all_gather([_pack_small(small_g + (loss_lanes,), R)], "gather_small")[0]
    sg, sd, sm2, sv2 = _small_final(small_parts, _pack_small(small_w + (zero,), R), _pack_small(small_m + (zero,), R),
                                    _pack_small(small_v + (zero + 1.0,), R))
    sg, sd, sm2, sv2 = (_unpack_small(a, shapes) for a in (sg, sd, sm2, sv2))
    loss = jnp.sum(sg[-1])
    res = {n: (sg[i], sd[i], sm2[i], sv2[i]) for i, n in enumerate(small_names)}

    off = norm_g.size
    dmod_all = small_parts.reshape(N_DEV, -1)[:, off:off + L * 3 * D].reshape(N_DEV, L, 3 * D)
    dmod_mine = jnp.transpose(lax.dynamic_slice_in_dim(dmod_all, me * (3 * D // N_DEV), 3 * D // N_DEV, axis=2), (1, 0, 2))
    res["w_ada"] = _wada_bwd(c_all.T, dmod_mine, w_ada, m_w_ada, v_w_ada)

    for k in GRADS:
        shp = shards[k].shape
        to3d = lambda a: a.reshape(L, -1, shp[-1])
        out = _reduce_adamw([parts[l][k].reshape(N_DEV, -1, shp[-1]) for l in range(L)], to3d(shards[k]),
                            to3d(moms[k][0]), to3d(moms[k][1]), "adamw_" + k)
        res[k] = tuple(a.reshape(shp) for a in out)

    order = ("norm_g", "w_ada", "b_ada", "w_in", "pool_w", "pool_scale", "attn_sink", "conv_dw", "conv_dw_b", "conv_ln_g",
             "conv_ln_b", "conv_pw", "wbp", "wba", "wbc", "w_out", "final_g")
    outs = [loss, grad_x[None]]
    for j in range(4):
        outs += [res[n][j] for n in order]
    return tuple(outs)
```

```python
import functools

import jax
import jax.numpy as jnp
from jax import lax
from jax.experimental import pallas as pl
from jax.experimental.pallas import tpu as pltpu

F32, BF16 = jnp.float32, jnp.bfloat16
MESH = pl.DeviceIdType.MESH
ANY = pl.BlockSpec(memory_space=pl.ANY)

N_DEV = 8
D = 2048
DEPTH = 4
EPS = 1e-6
IN_WIDTH = 13824
POOL_WINDOWS = (2, 4, 8, 16)
POOL_GROUP = 256
POOL_HALO = 16
CONV_K = 31
CONV_HALO = 32
N_HEADS, N_KV, HEAD_DIM = 16, 4, 64
ATTN_TQ = 256
ATTN_BACK = 128
LANE = 128
VMEM_BIG = 56 * 1024 * 1024

CB_U, CB_Z, CB_Q, CB_AZ, CB_CA, CB_CB, CB_CZ, CB_GP, CB_GA, CB_GC = 0, 1, 2, 3, 4, 5, 6, 7, 9, 11
COL_K, COL_V = 13312, 13568

ADAM_LR, ADAM_B1, ADAM_B2, ADAM_EPS, ADAM_WD, ADAM_STEP = 0.001, 0.9, 0.999, 1e-08, 0.01, 10


def _cp(sem=None, vmem=None):
    return pltpu.CompilerParams(dimension_semantics=sem, vmem_limit_bytes=vmem)


def _sig(x):
    return jax.nn.sigmoid(x)


def _silu(x):
    return x * _sig(x)


def _dsilu(x):
    s = _sig(x)
    return s * (1.0 + x * (1.0 - s))


def _dot(a, b):
    return jnp.dot(a, b, preferred_element_type=F32)


def _dot_tn(a, b):
    return lax.dot_general(a, b, (((0,), (0,)), ((), ())), preferred_element_type=F32)


def _dot_nt(a, b):
    return lax.dot_general(a, b, (((1,), (1,)), ((), ())), preferred_element_type=F32)


def _full(shape):
    n = len(shape)
    return pl.BlockSpec(shape, lambda *_: (0,) * n)


def _my_pos():
    return lax.axis_index("x"), lax.axis_index("y"), lax.axis_index("c")


def _all_gather(xs, name):
    n = len(xs)

    def body(*refs):
        x_refs, o_refs = refs[:n], refs[n:2 * n]
        send_sems, recv_sems, local_sems = refs[2 * n:]
        x, y, c = _my_pos()
        sibling = (x, y, 1 - c)
        chips = [(1 - x, y), (x, 1 - y), (1 - x, 1 - y)]
        me = 4 * x + 2 * y + c

        def slot(px, py, pc):
            return 4 * px + 2 * py + pc

        def copy(t, k, block, to, src=None):
            dst = o_refs[t].at[block]
            return pltpu.make_async_remote_copy(
                src_ref=dst if src is None else src, dst_ref=dst,
                send_sem=send_sems.at[t, k], recv_sem=recv_sems.at[t, k],
                device_id=to, device_id_type=MESH)

        mine = [pltpu.make_async_copy(x_refs[t], o_refs[t].at[me], local_sems.at[t]) for t in range(n)]
        for cp in mine:
            cp.start()
        first = []
        for t in range(n):
            first.append(copy(t, 0, me, sibling, src=x_refs[t]))
            for j, chip in enumerate(chips):
                first.append(copy(t, 1 + j, me, (*chip, c), src=x_refs[t]))
        for cp in first:
            cp.start()
        passed = []
        for j, chip in enumerate(chips):
            for t in range(n):
                copy(t, 1 + j, slot(*chip, c), (x, y, c)).wait_recv()
                fwd = copy(t, 4 + j, slot(*chip, c), sibling)
                fwd.start()
                passed.append(fwd)
        for t in range(n):
            copy(t, 0, slot(x, y, 1 - c), (x, y, c)).wait_recv()
            for j, chip in enumerate(chips):
                copy(t, 4 + j, slot(*chip, 1 - c), (x, y, c)).wait_recv()
        for cp in first + passed:
            cp.wait_send()
        for cp in mine:
            cp.wait()

    return pl.pallas_call(
        body, name=name,
        out_shape=[jax.ShapeDtypeStruct((N_DEV,) + a.shape, a.dtype) for a in xs],
        in_specs=[ANY] * n, out_specs=[ANY] * n,
        scratch_shapes=[pltpu.SemaphoreType.DMA((n, 7)), pltpu.SemaphoreType.DMA((n, 7)),
                        pltpu.SemaphoreType.DMA((n,))],
    )(*xs)


N_PEER = N_DEV - 1
HBM = pl.BlockSpec(memory_space=pltpu.HBM)
SEM = pl.BlockSpec(memory_space=pltpu.SEMAPHORE)
EFFECT = pltpu.SideEffectType.DATAFLOW_SIDE_EFFECTING


def _peer(k):
    x, y, c = _my_pos()
    flip = lambda v, bit: 1 - v if bit else v
    return flip(x, (k >> 2) & 1), flip(y, (k >> 1) & 1), flip(c, k & 1)


def _spread_copies(v_ref, land_ref, send_sems, recv_sems, per_peer):
    x, y, c = _my_pos()
    me = 4 * x + 2 * y + c
    copies = []
    for k in range(1, N_DEV):
        px, py, pc = _peer(k)
        src = v_ref.at[4 * px + 2 * py + pc] if per_peer else v_ref
        copies.append(pltpu.make_async_remote_copy(
            src_ref=src, dst_ref=land_ref.at[me], send_sem=send_sems[k - 1], recv_sem=recv_sems[k - 1],
            device_id=(px, py, pc), device_id_type=MESH))
    return copies


def _spread_start(vs, per_peer, name):
    n = len(vs)
    lands = [(N_DEV,) + (v.shape[1:] if per_peer else v.shape) for v in vs]
    n_sem = 2 * N_PEER * n

    def body(*refs):
        v_refs, land_refs, outs = refs[:n], refs[n:2 * n], refs[2 * n:]
        for t in range(n):
            sems = outs[2 * N_PEER * t:2 * N_PEER * (t + 1)]
            for cp in _spread_copies(v_refs[t], land_refs[t], sems[:N_PEER], sems[N_PEER:], per_peer):
                cp.start()
        token = outs[n_sem + 2 * n]
        token[...] = jnp.zeros_like(token)

    hbm = lambda a: pltpu.with_memory_space_constraint(a, pltpu.HBM)
    return pl.pallas_call(
        body, name=name,
        out_shape=((pltpu.SemaphoreType.DMA(()),) * n_sem + tuple(pltpu.HBM(v.shape, v.dtype) for v in vs)
                   + tuple(pltpu.HBM(s, v.dtype) for s, v in zip(lands, vs)) + (jax.ShapeDtypeStruct((8, LANE), F32),)),
        in_specs=(HBM,) * (2 * n), out_specs=(SEM,) * n_sem + (HBM,) * (2 * n) + (pl.BlockSpec(memory_space=pltpu.VMEM),),
        input_output_aliases={t: n_sem + t for t in range(2 * n)},
        compiler_params=pltpu.CompilerParams(has_side_effects=EFFECT),
    )(*[hbm(v) for v in vs], *[hbm(lax.empty(s, v.dtype)) for s, v in zip(lands, vs)])


def _spread_wait(started, after, per_peer, name):
    n = (len(started) - 1) // (2 * N_PEER + 2)
    n_sem = 2 * N_PEER * n
    sems, thru = started[:n_sem], started[n_sem:n_sem + 2 * n]

    def body(*refs):
        v_refs, land_refs, rest = refs[:n], refs[n:2 * n], refs[2 * n:]
        for t in range(n):
            s = rest[2 * N_PEER * t:2 * N_PEER * (t + 1)]
            for cp in _spread_copies(v_refs[t], land_refs[t], s[:N_PEER], s[N_PEER:], per_peer):
                cp.wait_send()
                cp.wait_recv()

    return pl.pallas_call(
        body, name=name,
        out_shape=tuple(pltpu.HBM(a.shape, a.dtype) for a in thru),
        in_specs=(HBM,) * (2 * n) + (SEM,) * n_sem + (ANY,), out_specs=(HBM,) * (2 * n),
        input_output_aliases={t: t for t in range(2 * n)},
        compiler_params=pltpu.CompilerParams(has_side_effects=EFFECT),
    )(*thru, *sems, after)[n:]


def _mm(a, b, out_dtype, tm, tn, tk=None, name="mm", vmem=None, after=None):
    M, K = a.shape
    _, N = b.shape
    tk = K if tk is None else tk
    nk = K // tk
    assert M % tm == 0 and N % tn == 0 and K % tk == 0
    dep = () if after is None else (after,)

    def body(*refs):
        a_ref, b_ref, o_ref, *acc = refs[len(dep):]
        prod = _dot(a_ref[...].astype(BF16), b_ref[...])
        if nk == 1:
            o_ref[...] = prod.astype(out_dtype)
            return
        acc_ref = acc[0] if acc else o_ref
        k = pl.program_id(2)

        @pl.when(k == 0)
        def _():
            acc_ref[...] = prod

        @pl.when(k > 0)
        def _():
            acc_ref[...] += prod

        if acc:
            @pl.when(k == nk - 1)
            def _():
                o_ref[...] = acc_ref[...].astype(out_dtype)

    scratch = [pltpu.VMEM((tm, tn), F32)] if (nk > 1 and out_dtype != F32) else []
    return pl.pallas_call(
        body, name=name, grid=(N // tn, M // tm, nk),
        in_specs=[_full((1, 1))] * len(dep) + [pl.BlockSpec((tm, tk), lambda j, i, k: (i, k)),
                                               pl.BlockSpec((tk, tn), lambda j, i, k: (k, j))],
        out_specs=pl.BlockSpec((tm, tn), lambda j, i, k: (i, j)),
        out_shape=jax.ShapeDtypeStruct((M, N), out_dtype), scratch_shapes=scratch,
        compiler_params=_cp(("parallel", "parallel", "arbitrary"), vmem),
    )(*dep, a, b)


def _mm_tn(a, b, tm, tn, ts, name="mm_tn", vmem=None, after=None):
    S, Ka = a.shape
    _, N = b.shape
    assert Ka % tm == 0 and N % tn == 0 and S % ts == 0
    dep = () if after is None else (after,)

    def body(*refs):
        a_ref, b_ref, o_ref = refs[len(dep):]
        prod = _dot_tn(a_ref[...].astype(BF16), b_ref[...].astype(BF16))
        k = pl.program_id(2)

        @pl.when(k == 0)
        def _():
            o_ref[...] = prod

        @pl.when(k > 0)
        def _():
            o_ref[...] += prod

    return pl.pallas_call(
        body, name=name, grid=(Ka // tm, N // tn, S // ts),
        in_specs=[_full((1, 1))] * len(dep) + [pl.BlockSpec((ts, tm), lambda i, j, k: (k, i)),
                                               pl.BlockSpec((ts, tn), lambda i, j, k: (k, j))],
        out_specs=pl.BlockSpec((tm, tn), lambda i, j, k: (i, j)),
        out_shape=jax.ShapeDtypeStruct((Ka, N), F32),
        compiler_params=_cp(("parallel", "parallel", "arbitrary"), vmem),
    )(*dep, a, b)


def _mod_fwd(c_all, w_ada):
    L, _, n = w_ada.shape

    def body(c_ref, w_ref, o_ref):
        ca = _silu(c_ref[...])
        o_ref[0] = jnp.dot(ca, w_ref[0], preferred_element_type=F32, precision=lax.Precision.HIGHEST)

    return pl.pallas_call(
        body, name="mod_fwd", grid=(L,),
        in_specs=[_full((N_DEV, D)), pl.BlockSpec((1, D, n), lambda l: (l, 0, 0))],
        out_specs=pl.BlockSpec((1, N_DEV, n), lambda l: (l, 0, 0)),
        out_shape=jax.ShapeDtypeStruct((L, N_DEV, n), F32),
        compiler_params=_cp(("parallel",)),
    )(c_all, w_ada)


def _adam_math(g, w, m, v):
    m2 = ADAM_B1 * m + (1.0 - ADAM_B1) * g
    v2 = ADAM_B2 * v + (1.0 - ADAM_B2) * (g * g)
    m_hat = m2 / (1.0 - ADAM_B1 ** ADAM_STEP)
    v_hat = v2 / (1.0 - ADAM_B2 ** ADAM_STEP)
    delta = -ADAM_LR * (m_hat / (jnp.sqrt(v_hat) + ADAM_EPS) + ADAM_WD * w)
    return delta, m2, v2


def _wada_bwd(c_all_t, dmod, w, m, v, tr=256):
    L, _, n = w.shape

    def body(c_ref, d_ref, w_ref, m_ref, v_ref, g_ref, dl_ref, m2_ref, v2_ref):
        ca = _silu(c_ref[...])
        dm = d_ref[0]
        g = ca[:, 0:1] * dm[0:1, :]
        for b in range(1, N_DEV):
            g = g + ca[:, b:b + 1] * dm[b:b + 1, :]
        delta, m2, v2 = _adam_math(g, w_ref[0], m_ref[0], v_ref[0])
        g_ref[0], dl_ref[0], m2_ref[0], v2_ref[0] = g, delta, m2, v2

    blk = pl.BlockSpec((1, tr, n), lambda l, i: (l, i, 0))
    return pl.pallas_call(
        body, name="wada_bwd", grid=(L, D // tr),
        in_specs=[pl.BlockSpec((tr, N_DEV), lambda l, i: (i, 0)), pl.BlockSpec((1, N_DEV, n), lambda l, i: (l, 0, 0)),
                  blk, blk, blk],
        out_specs=[blk] * 4, out_shape=[jax.ShapeDtypeStruct(w.shape, F32)] * 4,
        compiler_params=_cp(("parallel", "parallel")),
    )(c_all_t, dmod, w, m, v)


def _norm_mod(x, g, scale, shift, tb=256):
    S = x.shape[0]

    def body(x_ref, g_ref, sc_ref, sh_ref, h_ref):
        xv = x_ref[...]
        r = lax.rsqrt(jnp.mean(xv * xv, axis=-1, keepdims=True) + EPS)
        h_ref[...] = (xv * r * (g_ref[...] * (1.0 + sc_ref[...])) + sh_ref[...]).astype(BF16)

    row = pl.BlockSpec((tb, D), lambda i: (i, 0))
    vec = _full((1, D))
    return pl.pallas_call(
        body, name="norm_mod", grid=(S // tb,), in_specs=[row, vec, vec, vec], out_specs=row,
        out_shape=jax.ShapeDtypeStruct((S, D), BF16), compiler_params=_cp(("parallel",)),
    )(x, g, scale, shift)


def _pool_mix(u_ref, uh_ref, ubuf, i, tb):
    H = POOL_HALO
    ubuf[H:, :] = u_ref[...].astype(F32)
    ubuf[:H, :] = jnp.where(i > 0, uh_ref[...].astype(F32), 0.0)
    t = i * tb + lax.broadcasted_iota(jnp.int32, (tb, 1), 0)
    mixed = []
    for g, w in enumerate(POOL_WINDOWS):
        cs = slice(g * POOL_GROUP, (g + 1) * POOL_GROUP)
        cur = ubuf[H:, cs]
        acc = cur
        for j in range(1, w):
            acc = acc + ubuf[pl.ds(H - j, tb), cs]
        cnt = jnp.minimum(t + 1, w).astype(F32)
        mixed.append(acc / cnt - cur)
    return mixed


def _pool_specs(tb):
    H = POOL_HALO
    u = pl.BlockSpec((tb, 1024), lambda i: (i, CB_U))
    uh = pl.BlockSpec((H, 1024), lambda i: (jnp.maximum(i * (tb // H) - 1, 0), CB_U))
    z = pl.BlockSpec((tb, 1024), lambda i: (i, CB_Z))
    return u, uh, z


def _pool_fwd(proj, pool_w, pool_scale, tb=256):
    S = proj.shape[0]

    def body(u_ref, uh_ref, z_ref, w_ref, sc_ref, y_ref, ubuf):
        i = pl.program_id(0)
        mixed = _pool_mix(u_ref, uh_ref, ubuf, i, tb)
        m = jnp.concatenate([_dot(mixed[g].astype(BF16), w_ref[g]) for g in range(4)], axis=1)
        y_ref[...] = (m * sc_ref[...] * _silu(z_ref[...].astype(F32))).astype(BF16)

    u, uh, z = _pool_specs(tb)
    return pl.pallas_call(
        body, name="pool_fwd", grid=(S // tb,),
        in_specs=[u, uh, z, _full((4, 256, 256)), _full((1, 1024))],
        out_specs=pl.BlockSpec((tb, 1024), lambda i: (i, 0)),
        out_shape=jax.ShapeDtypeStruct((S, 1024), BF16),
        scratch_shapes=[pltpu.VMEM((tb + POOL_HALO, 1024), F32)],
        compiler_params=_cp(("parallel",)),
    )(proj, proj, proj, pool_w, pool_scale)


def _attn_mask(i):
    TQ, NK = ATTN_TQ, ATTN_TQ + ATTN_BACK
    qc = lax.broadcasted_iota(jnp.int32, (TQ, NK), 0) // 64
    col = lax.broadcasted_iota(jnp.int32, (TQ, NK), 1)
    kc = col // 64
    return (kc >= qc) & (kc <= qc + 2) & ((col >= ATTN_BACK) | (i > 0))


def _attn_probs(q_ref, kw, sink_ref, valid, h):
    qh = q_ref[:, h * HEAD_DIM:(h + 1) * HEAD_DIM]
    s = _dot_nt(qh, kw) * (HEAD_DIM ** -0.5)
    s = jnp.where(valid, s, -jnp.inf)
    sk = sink_ref[h:h + 1, 0:1]
    mx = jnp.maximum(jnp.max(s, axis=-1, keepdims=True), sk)
    p = jnp.exp(s - mx)
    es = jnp.exp(sk - mx)
    den = jnp.sum(p, axis=-1, keepdims=True) + es
    return p / den, es / den


def _attn_fwd(proj, kpad, vpad, sink_b):
    S = proj.shape[0]
    TQ, NK = ATTN_TQ, ATTN_TQ + ATTN_BACK

    def body(q_ref, z_ref, k_ref, v_ref, sink_ref, o_ref, y_ref):
        i = pl.program_id(0)
        start = pl.multiple_of(i * TQ, TQ)
        valid = _attn_mask(i)
        for h in range(N_HEADS):
            kh = h // (N_HEADS // N_KV)
            ks = slice(kh * HEAD_DIM, (kh + 1) * HEAD_DIM)
            hs = slice(h * HEAD_DIM, (h + 1) * HEAD_DIM)
            pn, _ = _attn_probs(q_ref, k_ref[pl.ds(start, NK), ks], sink_ref, valid, h)
            o = _dot(pn.astype(BF16), v_ref[pl.ds(start, NK), ks])
            o_ref[:, hs] = o.astype(BF16)
            y_ref[:, hs] = (o * _silu(z_ref[:, hs].astype(F32))).astype(BF16)

    out = pl.BlockSpec((TQ, 1024), lambda i: (i, 0))
    return pl.pallas_call(
        body, name="attn_fwd", grid=(S // TQ,),
        in_specs=[pl.BlockSpec((TQ, 1024), lambda i: (i, CB_Q)), pl.BlockSpec((TQ, 1024), lambda i: (i, CB_AZ)),
                  _full(kpad.shape), _full(vpad.shape), _full((N_HEADS, LANE))],
        out_specs=[out, out], out_shape=[jax.ShapeDtypeStruct((S, 1024), BF16)] * 2,
        compiler_params=_cp(("parallel",)),
    )(proj, proj, kpad, vpad, sink_b)


def _conv_specs(tb):
    H = CONV_HALO
    prev = lambda i: jnp.maximum(i * (tb // H) - 1, 0)
    a = pl.BlockSpec((tb, 1024), lambda i: (i, CB_CA))
    ah = pl.BlockSpec((H, 1024), lambda i: (prev(i), CB_CA))
    b = pl.BlockSpec((tb, 1024), lambda i: (i, CB_CB))
    bh = pl.BlockSpec((H, 1024), lambda i: (prev(i), CB_CB))
    z = pl.BlockSpec((tb, 1024), lambda i: (i, CB_CZ))
    return a, ah, b, bh, z


SUBLANES = 8


def _shift_copies(bufs, tb):
    n = tb + CONV_HALO - SUBLANES
    for s in range(1, SUBLANES):
        bufs[s, :n, :] = bufs[0, pl.ds(s, n), :]


def _rows_at(bufs, start, offset, cs):
    return bufs[offset % SUBLANES, pl.ds(start + offset - offset % SUBLANES, 128), cs]


def _conv_glu_dw(a_ref, ah_ref, b_ref, bh_ref, dw_ref, gbuf, ybuf, i, tb):
    H = CONV_HALO
    gbuf[0, H:, :] = a_ref[...].astype(F32) * _sig(b_ref[...].astype(F32))
    gh = ah_ref[...].astype(F32) * _sig(bh_ref[...].astype(F32))
    gbuf[0, :H, :] = jnp.where(i > 0, gh, 0.0)
    _shift_copies(gbuf, tb)
    base = H - (CONV_K - 1)
    for c in range(1024 // LANE):
        cs = slice(c * LANE, (c + 1) * LANE)
        for r in range(tb // 128):
            acc = jnp.zeros((128, LANE), F32)
            for j in range(CONV_K):
                acc = acc + dw_ref[j:j + 1, cs] * _rows_at(gbuf, r * 128, base + j, cs)
            ybuf[r * 128:(r + 1) * 128, cs] = acc


def _layer_norm_fwd(y, g, b):
    mu = jnp.mean(y, axis=-1, keepdims=True)
    yc = y - mu
    rstd = lax.rsqrt(jnp.mean(yc * yc, axis=-1, keepdims=True) + EPS)
    xh = yc * rstd
    return xh, rstd, xh * g + b


def _conv_fwd(proj, dw, dw_b, ln_g, ln_b, pw, tb=256):
    S = proj.shape[0]

    def body(a_ref, ah_ref, b_ref, bh_ref, z_ref, dw_ref, dwb_ref, lg_ref, lb_ref, pw_ref, y_ref, gbuf, ybuf):
        i = pl.program_id(0)
        _conv_glu_dw(a_ref, ah_ref, b_ref, bh_ref, dw_ref, gbuf, ybuf, i, tb)
        _, _, yn = _layer_norm_fwd(ybuf[...] + dwb_ref[...], lg_ref[...], lb_ref[...])
        out = _dot(_silu(yn).astype(BF16), pw_ref[...])
        y_ref[...] = (out * _silu(z_ref[...].astype(F32))).astype(BF16)

    vec = _full((1, 1024))
    return pl.pallas_call(
        body, name="conv_fwd", grid=(S // tb,),
        in_specs=[*_conv_specs(tb), _full((32, 1024)), vec, vec, vec, _full((1024, 1024))],
        out_specs=pl.BlockSpec((tb, 1024), lambda i: (i, 0)),
        out_shape=jax.ShapeDtypeStruct((S, 1024), BF16),
        scratch_shapes=[pltpu.VMEM((SUBLANES, tb + CONV_HALO, 1024), F32), pltpu.VMEM((tb, 1024), F32)],
        compiler_params=_cp(("parallel",)),
    )(proj, proj, proj, proj, proj, dw, dw_b, ln_g, ln_b, pw)


def _merge_fwd(yp, ya, yc, wbp, wba, wbc, proj, tm=512, tn=1024):
    S = yp.shape[0]

    def body(yp_ref, ya_ref, yc_ref, wp_ref, wa_ref, wc_ref, gp_ref, ga_ref, gc_ref, m_ref, pp_ref, pa_ref, pc_ref):
        pp = _dot(yp_ref[...], wp_ref[...])
        pa = _dot(ya_ref[...], wa_ref[...])
        pc = _dot(yc_ref[...], wc_ref[...])
        m = (_sig(gp_ref[...].astype(F32)) * pp + _sig(ga_ref[...].astype(F32)) * pa
             + _sig(gc_ref[...].astype(F32)) * pc)
        m_ref[...] = m.astype(BF16)
        pp_ref[...], pa_ref[...], pc_ref[...] = pp.astype(BF16), pa.astype(BF16), pc.astype(BF16)

    yb = pl.BlockSpec((tm, 1024), lambda j, i: (i, 0))
    wb = pl.BlockSpec((1024, tn), lambda j, i: (0, j))
    gate = lambda cb: pl.BlockSpec((tm, tn), lambda j, i: (i, cb + j))
    out = pl.BlockSpec((tm, tn), lambda j, i: (i, j))
    return pl.pallas_call(
        body, name="merge_fwd", grid=(D // tn, S // tm),
        in_specs=[yb, yb, yb, wb, wb, wb, gate(CB_GP), gate(CB_GA), gate(CB_GC)],
        out_specs=[out] * 4, out_shape=[jax.ShapeDtypeStruct((S, D), BF16)] * 4,
        compiler_params=_cp(("parallel", "parallel")),
    )(yp, ya, yc, wbp, wba, wbc, proj, proj, proj)


def _out_fwd(x, merged, w_out, gate, tm=512, tn=1024):
    S = x.shape[0]

    def body(x_ref, m_ref, w_ref, g_ref, o_ref):
        o_ref[...] = x_ref[...] + g_ref[...] * _dot(m_ref[...], w_ref[...])

    xb = pl.BlockSpec((tm, tn), lambda j, i: (i, j))
    return pl.pallas_call(
        body, name="out_fwd", grid=(D // tn, S // tm),
        in_specs=[xb, pl.BlockSpec((tm, D), lambda j, i: (i, 0)), pl.BlockSpec((D, tn), lambda j, i: (0, j)),
                  pl.BlockSpec((1, tn), lambda j, i: (0, j))],
        out_specs=xb, out_shape=jax.ShapeDtypeStruct((S, D), F32),
        compiler_params=_cp(("parallel", "parallel")),
    )(x, merged, w_out, gate)


def _final_loss(x, target, final_g, tb=256):
    S = x.shape[0]

    def body(x_ref, t_ref, g_ref, dx_ref, gg_ref, ls_ref):
        i = pl.program_id(0)
        xv, g = x_ref[...], g_ref[...]
        r = lax.rsqrt(jnp.mean(xv * xv, axis=-1, keepdims=True) + EPS)
        xh = xv * r
        e = xh * g - t_ref[...]
        dy = e * (1.0 / D)
        gy = dy * g
        dx_ref[...] = r * (gy - xh * jnp.mean(gy * xh, axis=-1, keepdims=True))
        gg = jnp.sum(dy * xh, axis=0, keepdims=True)
        ls = jnp.sum(e * e, axis=0, keepdims=True) * (0.5 / D)

        @pl.when(i == 0)
        def _():
            gg_ref[...], ls_ref[...] = gg, ls

        @pl.when(i > 0)
        def _():
            gg_ref[...] += gg
            ls_ref[...] += ls

    row = pl.BlockSpec((tb, D), lambda i: (i, 0))
    vec = _full((1, D))
    return pl.pallas_call(
        body, name="final_loss", grid=(S // tb,), in_specs=[row, row, vec], out_specs=[row, vec, vec],
        out_shape=[jax.ShapeDtypeStruct((S, D), F32), jax.ShapeDtypeStruct((1, D), F32),
                   jax.ShapeDtypeStruct((1, D), F32)],
        compiler_params=_cp(("arbitrary",)),
    )(x, target, final_g)


def _out_bwd(dx, gate, w_out_t, pp, pa, pc, proj, tm=512, tn=1024):
    S = dx.shape[0]

    def body(dx_ref, g_ref, w_ref, pp_ref, pa_ref, pc_ref, gp_ref, ga_ref, gc_ref,
             dpp_ref, dpa_ref, dpc_ref, dgp_ref, dga_ref, dgc_ref):
        dm = _dot((dx_ref[...] * g_ref[...]).astype(BF16), w_ref[...])
        for p_ref, gl_ref, dp_ref, dg_ref in ((pp_ref, gp_ref, dpp_ref, dgp_ref), (pa_ref, ga_ref, dpa_ref, dga_ref),
                                              (pc_ref, gc_ref, dpc_ref, dgc_ref)):
            s = _sig(gl_ref[...].astype(F32))
            dp_ref[...] = (dm * s).astype(BF16)
            dg_ref[...] = (dm * p_ref[...].astype(F32) * s * (1.0 - s)).astype(BF16)

    out = pl.BlockSpec((tm, tn), lambda j, i: (i, j))
    gate_spec = lambda cb: pl.BlockSpec((tm, tn), lambda j, i: (i, cb + j))
    return pl.pallas_call(
        body, name="out_bwd", grid=(D // tn, S // tm),
        in_specs=[pl.BlockSpec((tm, D), lambda j, i: (i, 0)), _full((1, D)), pl.BlockSpec((D, tn), lambda j, i: (0, j)),
                  out, out, out, gate_spec(CB_GP), gate_spec(CB_GA), gate_spec(CB_GC)],
        out_specs=[out] * 6, out_shape=[jax.ShapeDtypeStruct((S, D), BF16)] * 6,
        compiler_params=_cp(("parallel", "parallel")),
    )(dx, gate, w_out_t, pp, pa, pc, proj, proj, proj)


def _wout_post(gmat, w_out, gate, tr=256):
    def body(g_ref, w_ref, gate_ref, dw_ref, dg_ref):
        i = pl.program_id(0)
        gm = g_ref[...]
        dw_ref[...] = gm * gate_ref[...]
        part = jnp.sum(gm * w_ref[...].astype(F32), axis=0, keepdims=True)

        @pl.when(i == 0)
        def _():
            dg_ref[...] = part

        @pl.when(i > 0)
        def _():
            dg_ref[...] += part

    row = pl.BlockSpec((tr, D), lambda i: (i, 0))
    return pl.pallas_call(
        body, name="wout_post", grid=(D // tr,), in_specs=[row, row, _full((1, D))], out_specs=[row, _full((1, D))],
        out_shape=[jax.ShapeDtypeStruct((D, D), F32), jax.ShapeDtypeStruct((1, D), F32)],
        compiler_params=_cp(("arbitrary",)),
    )(gmat, w_out, gate)


def _pool_bwd_a(dy, proj, pool_w, pool_w_t, pool_scale, tb=256):
    S = proj.shape[0]

    def body(dy_ref, u_ref, uh_ref, z_ref, w_ref, wt_ref, sc_ref, dmix_ref, dz_ref, dsc_ref, dw_ref, ubuf):
        i = pl.program_id(0)
        mixed = [m.astype(BF16) for m in _pool_mix(u_ref, uh_ref, ubuf, i, tb)]
        m = jnp.concatenate([_dot(mixed[g], w_ref[g]) for g in range(4)], axis=1)
        dyv, z, sc = dy_ref[...].astype(F32), z_ref[...].astype(F32), sc_ref[...]
        dyp = dyv * _silu(z)
        dz_ref[...] = (dyv * (m * sc) * _dsilu(z)).astype(BF16)
        dsc = jnp.sum(dyp * m, axis=0, keepdims=True)
        dmm = (dyp * sc).astype(BF16)
        dws = []
        for g in range(4):
            cs = slice(g * POOL_GROUP, (g + 1) * POOL_GROUP)
            dmix_ref[:, cs] = _dot(dmm[:, cs], wt_ref[g])
            dws.append(_dot_tn(mixed[g], dmm[:, cs]))

        @pl.when(i == 0)
        def _():
            dsc_ref[...] = dsc
            for g in range(4):
                dw_ref[g] = dws[g]

        @pl.when(i > 0)
        def _():
            dsc_ref[...] += dsc
            for g in range(4):
                dw_ref[g] += dws[g]

    u, uh, z = _pool_specs(tb)
    row = pl.BlockSpec((tb, 1024), lambda i: (i, 0))
    wfull = _full((4, 256, 256))
    return pl.pallas_call(
        body, name="pool_bwd_a", grid=(S // tb,),
        in_specs=[row, u, uh, z, wfull, wfull, _full((1, 1024))],
        out_specs=[row, row, _full((1, 1024)), wfull],
        out_shape=[jax.ShapeDtypeStruct((S, 1024), F32), jax.ShapeDtypeStruct((S, 1024), BF16),
                   jax.ShapeDtypeStruct((1, 1024), F32), jax.ShapeDtypeStruct((4, 256, 256), F32)],
        scratch_shapes=[pltpu.VMEM((tb + POOL_HALO, 1024), F32)],
        compiler_params=_cp(("arbitrary",)),
    )(dy, proj, proj, proj, pool_w, pool_w_t, pool_scale)


def _pool_bwd_b(dmix, tb=256):
    S = dmix.shape[0]
    H = POOL_HALO
    nb = S // tb

    def body(dm_ref, dh_ref, du_ref, ebuf):
        i = pl.program_id(0)
        t = i * tb + lax.broadcasted_iota(jnp.int32, (tb, 1), 0)
        th = (i + 1) * tb + lax.broadcasted_iota(jnp.int32, (H, 1), 0)
        for g, w in enumerate(POOL_WINDOWS):
            cs = slice(g * POOL_GROUP, (g + 1) * POOL_GROUP)
            ebuf[:tb, cs] = dm_ref[:, cs] / jnp.minimum(t + 1, w).astype(F32)
            eh = dh_ref[:, cs] / jnp.minimum(th + 1, w).astype(F32)
            ebuf[tb:, cs] = jnp.where(i < nb - 1, eh, 0.0)
        for g, w in enumerate(POOL_WINDOWS):
            cs = slice(g * POOL_GROUP, (g + 1) * POOL_GROUP)
            acc = ebuf[:tb, cs]
            for j in range(1, w):
                acc = acc + ebuf[pl.ds(j, tb), cs]
            du_ref[:, cs] = (acc - dm_ref[:, cs]).astype(BF16)

    row = pl.BlockSpec((tb, 1024), lambda i: (i, 0))
    nxt = pl.BlockSpec((H, 1024), lambda i: (jnp.minimum((i + 1) * (tb // H), S // H - 1), 0))
    return pl.pallas_call(
        body, name="pool_bwd_b", grid=(nb,), in_specs=[row, nxt], out_specs=row,
        out_shape=jax.ShapeDtypeStruct((S, 1024), BF16),
        scratch_shapes=[pltpu.VMEM((tb + H, 1024), F32)],
        compiler_params=_cp(("parallel",)),
    )(dmix, dmix)


def _attn_bwd(dy, o, proj, kpad, vpad, sink_b):
    S = proj.shape[0]
    TQ, NK = ATTN_TQ, ATTN_TQ + ATTN_BACK
    nb = S // TQ
    G = N_HEADS // N_KV

    def body(dy_ref, o_ref, q_ref, z_ref, k_ref, v_ref, sink_ref, dq_ref, dz_ref, dk_hbm, dv_hbm, ds_ref, dk_acc, dv_acc):
        i = pl.program_id(0)

        @pl.when(i == 0)
        def _():
            dk_acc[...] = jnp.zeros_like(dk_acc)
            dv_acc[...] = jnp.zeros_like(dv_acc)
            ds_ref[...] = jnp.zeros_like(ds_ref)

        start = pl.multiple_of(i * TQ, TQ)
        valid = _attn_mask(i)
        dks, dvs = [], []
        for kh in range(N_KV):
            ks = slice(kh * HEAD_DIM, (kh + 1) * HEAD_DIM)
            kw = k_ref[pl.ds(start, NK), ks]
            vw = v_ref[pl.ds(start, NK), ks]
            dk_sum = jnp.zeros((NK, HEAD_DIM), F32)
            dv_sum = jnp.zeros((NK, HEAD_DIM), F32)
            for gi in range(G):
                h = kh * G + gi
                hs = slice(h * HEAD_DIM, (h + 1) * HEAD_DIM)
                pn, psink = _attn_probs(q_ref, kw, sink_ref, valid, h)
                z = z_ref[:, hs].astype(F32)
                dyv = dy_ref[:, hs].astype(F32)
                ov = o_ref[:, hs].astype(F32)
                do = dyv * _silu(z)
                dz_ref[:, hs] = (dyv * ov * _dsilu(z)).astype(BF16)
                delta = jnp.sum(do * ov, axis=-1, keepdims=True)
                dob = do.astype(BF16)
                dp = _dot_nt(dob, vw)
                ds = (pn * (dp - delta)).astype(BF16)
                dsink = -jnp.sum(psink * delta, axis=0, keepdims=True)
                ds_ref[h:h + 1, :] += jnp.broadcast_to(dsink, (1, LANE))
                dq_ref[:, hs] = (_dot(ds, kw) * (HEAD_DIM ** -0.5)).astype(BF16)
                dk_sum = dk_sum + _dot_tn(ds, q_ref[:, hs])
                dv_sum = dv_sum + _dot_tn(pn.astype(BF16), dob)
            dks.append(dk_sum * (HEAD_DIM ** -0.5))
            dvs.append(dv_sum)
        dk_acc[pl.ds(start, NK), :] += jnp.concatenate(dks, axis=1)
        dv_acc[pl.ds(start, NK), :] += jnp.concatenate(dvs, axis=1)

        @pl.when(i == nb - 1)
        def _():
            pltpu.sync_copy(dk_acc, dk_hbm)
            pltpu.sync_copy(dv_acc, dv_hbm)

    row = pl.BlockSpec((TQ, 1024), lambda i: (i, 0))
    return pl.pallas_call(
        body, name="attn_bwd", grid=(nb,),
        in_specs=[row, row, pl.BlockSpec((TQ, 1024), lambda i: (i, CB_Q)), pl.BlockSpec((TQ, 1024), lambda i: (i, CB_AZ)),
                  _full(kpad.shape), _full(vpad.shape), _full((N_HEADS, LANE))],
        out_specs=[row, row, ANY, ANY, _full((N_HEADS, LANE))],
        out_shape=[jax.ShapeDtypeStruct((S, 1024), BF16), jax.ShapeDtypeStruct((S, 1024), BF16),
                   jax.ShapeDtypeStruct(kpad.shape, F32), jax.ShapeDtypeStruct(vpad.shape, F32),
                   jax.ShapeDtypeStruct((N_HEADS, LANE), F32)],
        scratch_shapes=[pltpu.VMEM(kpad.shape, F32), pltpu.VMEM(vpad.shape, F32)],
        compiler_params=_cp(("arbitrary",), VMEM_BIG),
    )(dy, o, proj, proj, kpad, vpad, sink_b)


def _conv_bwd_a(dy, proj, dw, dw_b, ln_g, ln_b, pw, pw_t, tb=256):
    S = proj.shape[0]
    H = CONV_HALO
    base = H - (CONV_K - 1)

    def body(dy_ref, a_ref, ah_ref, b_ref, bh_ref, z_ref, dw_ref, dwb_ref, lg_ref, lb_ref, pw_ref, pwt_ref,
             dcv_ref, dz_ref, dpw_ref, dlg_ref, dlb_ref, ddwb_ref, ddw_ref, gbuf, ybuf):
        i = pl.program_id(0)
        _conv_glu_dw(a_ref, ah_ref, b_ref, bh_ref, dw_ref, gbuf, ybuf, i, tb)
        lg = lg_ref[...]
        xh, rstd, yn = _layer_norm_fwd(ybuf[...] + dwb_ref[...], lg, lb_ref[...])
        u = _silu(yn).astype(BF16)
        out = _dot(u, pw_ref[...])
        dyv, z = dy_ref[...].astype(F32), z_ref[...].astype(F32)
        dz_ref[...] = (dyv * out * _dsilu(z)).astype(BF16)
        dout = (dyv * _silu(z)).astype(BF16)
        dpw = _dot_tn(u, dout)
        dyn = _dot(dout, pwt_ref[...]) * _dsilu(yn)
        dlg = jnp.sum(dyn * xh, axis=0, keepdims=True)
        dlb = jnp.sum(dyn, axis=0, keepdims=True)
        dxh = dyn * lg
        dcv = rstd * (dxh - jnp.mean(dxh, axis=-1, keepdims=True) - xh * jnp.mean(dxh * xh, axis=-1, keepdims=True))
        dcv_ref[...] = dcv
        ddwb = jnp.sum(dcv, axis=0, keepdims=True)
        ybuf[...] = dcv

        @pl.when(i == 0)
        def _():
            dpw_ref[...], dlg_ref[...], dlb_ref[...], ddwb_ref[...] = dpw, dlg, dlb, ddwb
            ddw_ref[...] = jnp.zeros_like(ddw_ref)

        @pl.when(i > 0)
        def _():
            dpw_ref[...] += dpw
            dlg_ref[...] += dlg
            dlb_ref[...] += dlb
            ddwb_ref[...] += ddwb

        for c in range(1024 // LANE):
            cs = slice(c * LANE, (c + 1) * LANE)
            for j in range(CONV_K):
                acc = jnp.zeros((1, LANE), F32)
                for r in range(tb // 128):
                    acc = acc + jnp.sum(ybuf[r * 128:(r + 1) * 128, cs] * _rows_at(gbuf, r * 128, base + j, cs),
                                        axis=0, keepdims=True)
                ddw_ref[j:j + 1, cs] += acc

    vec = _full((1, 1024))
    row = pl.BlockSpec((tb, 1024), lambda i: (i, 0))
    big = _full((1024, 1024))
    return pl.pallas_call(
        body, name="conv_bwd_a", grid=(S // tb,),
        in_specs=[row, *_conv_specs(tb), _full((32, 1024)), vec, vec, vec, big, big],
        out_specs=[row, row, big, vec, vec, vec, _full((32, 1024))],
        out_shape=[jax.ShapeDtypeStruct((S, 1024), F32), jax.ShapeDtypeStruct((S, 1024), BF16),
                   jax.ShapeDtypeStruct((1024, 1024), F32), jax.ShapeDtypeStruct((1, 1024), F32),
                   jax.ShapeDtypeStruct((1, 1024), F32), jax.ShapeDtypeStruct((1, 1024), F32),
                   jax.ShapeDtypeStruct((32, 1024), F32)],
        scratch_shapes=[pltpu.VMEM((SUBLANES, tb + H, 1024), F32), pltpu.VMEM((tb, 1024), F32)],
        compiler_params=_cp(("arbitrary",)),
    )(dy, proj, proj, proj, proj, proj, dw, dw_b, ln_g, ln_b, pw, pw_t)


def _conv_bwd_b(dcv, proj, dw, tb=256):
    S = proj.shape[0]
    H = CONV_HALO
    nb = S // tb

    def body(d_ref, dn_ref, a_ref, b_ref, dw_ref, da_ref, db_ref, dbuf, gbuf):
        i = pl.program_id(0)
        dbuf[0, :tb, :] = d_ref[...]
        dbuf[0, tb:, :] = jnp.where(i < nb - 1, dn_ref[...], 0.0)
        _shift_copies(dbuf, tb)
        for c in range(1024 // LANE):
            cs = slice(c * LANE, (c + 1) * LANE)
            for r in range(tb // 128):
                acc = jnp.zeros((128, LANE), F32)
                for j in range(CONV_K):
                    acc = acc + dw_ref[j:j + 1, cs] * _rows_at(dbuf, r * 128, (CONV_K - 1) - j, cs)
                gbuf[r * 128:(r + 1) * 128, cs] = acc
        dg = gbuf[...]
        a, s = a_ref[...].astype(F32), _sig(b_ref[...].astype(F32))
        da_ref[...] = (dg * s).astype(BF16)
        db_ref[...] = (dg * a * s * (1.0 - s)).astype(BF16)

    row = pl.BlockSpec((tb, 1024), lambda i: (i, 0))
    nxt = pl.BlockSpec((H, 1024), lambda i: (jnp.minimum((i + 1) * (tb // H), S // H - 1), 0))
    return pl.pallas_call(
        body, name="conv_bwd_b", grid=(nb,),
        in_specs=[row, nxt, pl.BlockSpec((tb, 1024), lambda i: (i, CB_CA)), pl.BlockSpec((tb, 1024), lambda i: (i, CB_CB)),
                  _full((32, 1024))],
        out_specs=[row, row], out_shape=[jax.ShapeDtypeStruct((S, 1024), BF16)] * 2,
        scratch_shapes=[pltpu.VMEM((SUBLANES, tb + H, 1024), F32), pltpu.VMEM((tb, 1024), F32)],
        compiler_params=_cp(("parallel",)),
    )(dcv, dcv, proj, proj, dw)


def _norm_bwd(dh, x, dx_out, g, scale, tb=256):
    S = x.shape[0]

    def body(dh_ref, x_ref, dxo_ref, g_ref, sc_ref, dx_ref, dsh_ref, da_ref):
        i = pl.program_id(0)
        xv, dhv = x_ref[...], dh_ref[...]
        r = lax.rsqrt(jnp.mean(xv * xv, axis=-1, keepdims=True) + EPS)
        xh = xv * r
        gy = dhv * (g_ref[...] * (1.0 + sc_ref[...]))
        dx_ref[...] = dxo_ref[...] + r * (gy - xh * jnp.mean(gy * xh, axis=-1, keepdims=True))
        dsh = jnp.sum(dhv, axis=0, keepdims=True)
        da = jnp.sum(dhv * xh, axis=0, keepdims=True)

        @pl.when(i == 0)
        def _():
            dsh_ref[...], da_ref[...] = dsh, da

        @pl.when(i > 0)
        def _():
            dsh_ref[...] += dsh
            da_ref[...] += da

    row = pl.BlockSpec((tb, D), lambda i: (i, 0))
    vec = _full((1, D))
    return pl.pallas_call(
        body, name="norm_bwd", grid=(S // tb,), in_specs=[row, row, row, vec, vec], out_specs=[row, vec, vec],
        out_shape=[jax.ShapeDtypeStruct((S, D), F32), jax.ShapeDtypeStruct((1, D), F32), jax.ShapeDtypeStruct((1, D), F32)],
        compiler_params=_cp(("arbitrary",)),
    )(dh, x, dx_out, g, scale)


def _mod_bwd(d_a, norm_g, scale):
    def body(da_ref, g_ref, sc_ref, dg_ref, dsc_ref):
        dg_ref[...] = da_ref[...] * (1.0 + sc_ref[...])
        dsc_ref[...] = da_ref[...] * g_ref[...]

    return pl.pallas_call(body, name="mod_bwd", out_shape=[jax.ShapeDtypeStruct(d_a.shape, F32)] * 2)(d_a, norm_g, scale)


def _reduce_adamw(parts, w, m, v, name):
    L, rows, C = w.shape
    tr = rows if rows % 64 else 64

    def body(*refs):
        p_refs, (w_ref, m_ref, v_ref, g_ref, dl_ref, m2_ref, v2_ref) = refs[:L * N_DEV], refs[L * N_DEV:]
        for l in range(L):
            g = p_refs[l * N_DEV][0].astype(F32)
            for k in range(1, N_DEV):
                g = g + p_refs[l * N_DEV + k][0].astype(F32)
            g_ref[l] = g
            dl_ref[l], m2_ref[l], v2_ref[l] = _adam_math(g, w_ref[l], m_ref[l], v_ref[l])

    slot = lambda k: pl.BlockSpec((1, tr, C), lambda i: (k, i, 0))
    blk = pl.BlockSpec((L, tr, C), lambda i: (0, i, 0))
    return pl.pallas_call(
        body, name=name, grid=(rows // tr,), in_specs=[slot(k) for _ in range(L) for k in range(N_DEV)] + [blk] * 3,
        out_specs=[blk] * 4, out_shape=[jax.ShapeDtypeStruct((L, rows, C), F32)] * 4,
        compiler_params=_cp(("parallel",)),
    )(*[p for p in parts for _ in range(N_DEV)], w, m, v)


def _small_final(parts, w, m, v):
    R = w.shape[0]

    def body(p_ref, w_ref, m_ref, v_ref, g_ref, dl_ref, m2_ref, v2_ref):
        g = p_ref[0]
        for k in range(1, N_DEV):
            g = g + p_ref[k]
        delta, m2, v2 = _adam_math(g, w_ref[...], m_ref[...], v_ref[...])
        g_ref[...], dl_ref[...], m2_ref[...], v2_ref[...] = g, delta, m2, v2

    return pl.pallas_call(body, name="small_final", out_shape=[jax.ShapeDtypeStruct((R, LANE), F32)] * 4)(parts, w, m, v)


def _layer_fwd(x, mod, small, W, more_w):
    shift, scale, gate = mod
    h = _norm_mod(x, small["norm_g"], scale, shift)
    proj = _mm(h, W["w_in"], BF16, 512, 1536, name="proj_mm")
    W.update(more_w(proj))
    y_pool = _pool_fwd(proj, W["pool_w"], small["pool_scale"])
    kpad = jnp.pad(proj[:, COL_K:COL_K + 256], ((ATTN_BACK, 0), (0, 0)))
    vpad = jnp.pad(proj[:, COL_V:COL_V + 256], ((ATTN_BACK, 0), (0, 0)))
    o, y_attn = _attn_fwd(proj, kpad, vpad, small["sink_b"])
    y_conv = _conv_fwd(proj, W["conv_dw"], small["conv_dw_b"], small["conv_ln_g"], small["conv_ln_b"], W["conv_pw"])
    merged, pp, pa, pc = _merge_fwd(y_pool, y_attn, y_conv, W["wbp"], W["wba"], W["wbc"], proj)
    x_new = _out_fwd(x, merged, W["w_out"], gate)
    stash = dict(x=x, h=h, proj=proj, kpad=kpad, vpad=vpad, o=o, y_pool=y_pool, y_attn=y_attn, y_conv=y_conv,
                 merged=merged, pp=pp, pa=pa, pc=pc)
    return x_new, stash


def _layer_bwd(dx, st, mod, small, W, put):
    shift, scale, gate = mod
    proj = st["proj"]
    gmat = _mm_tn(st["merged"], dx, 1024, 1024, 1024, name="wout_tn")
    d_w_out, d_gate = _wout_post(gmat, W["w_out"], gate)
    dpp, dpa, dpc, dgp, dga, dgc = _out_bwd(dx, gate, W["w_out_t"], st["pp"], st["pa"], st["pc"], proj)
    dy_pool = _mm(dpp, W["wbp_t"], BF16, 512, 1024, name="branch_bwd_mm")
    dy_attn = _mm(dpa, W["wba_t"], BF16, 512, 1024, name="branch_bwd_mm")
    dy_conv = _mm(dpc, W["wbc_t"], BF16, 512, 1024, name="branch_bwd_mm")
    d_wbp = _mm_tn(st["y_pool"], dpp, 1024, 1024, 1024, name="branch_tn")
    d_wba = _mm_tn(st["y_attn"], dpa, 1024, 1024, 1024, name="branch_tn")
    d_wbc = _mm_tn(st["y_conv"], dpc, 1024, 1024, 1024, name="branch_tn")

    dmix, dz_pool, d_pool_scale, d_pool_w = _pool_bwd_a(dy_pool, proj, W["pool_w"], W["pool_w_t"], small["pool_scale"])
    du = _pool_bwd_b(dmix)
    dq, dz_attn, dk, dv, d_sink = _attn_bwd(dy_attn, st["o"], proj, st["kpad"], st["vpad"], small["sink_b"])
    dcv, dz_conv, d_pw, d_ln_g, d_ln_b, d_dw_b, d_dw = _conv_bwd_a(
        dy_conv, proj, W["conv_dw"], small["conv_dw_b"], small["conv_ln_g"], small["conv_ln_b"], W["conv_pw"], W["conv_pw_t"])
    da, db = _conv_bwd_b(dcv, proj, W["conv_dw"])

    tok = put(dict(pool_w=d_pool_w, conv_dw=d_dw[:CONV_K], conv_pw=d_pw, wbp=d_wbp, wba=d_wba, wbc=d_wbc, w_out=d_w_out))
    dproj = jnp.concatenate([du, dz_pool, dq, dz_attn, da, db, dz_conv, dgp, dga, dgc,
                             dk[ATTN_BACK:].astype(BF16), dv[ATTN_BACK:].astype(BF16)], axis=1)
    d_w_in = _mm_tn(st["h"], dproj, 1024, 1536, 1024, name="win_tn", vmem=VMEM_BIG, after=tok)
    tok = put(dict(w_in=d_w_in))
    dh = _mm(dproj, W["w_in_t"], F32, 1024, 2048, 1536, name="dh_mm", vmem=VMEM_BIG, after=tok)
    dx_in, d_shift, d_a = _norm_bwd(dh, st["x"], dx, small["norm_g"], scale)
    sm = dict(d_a=d_a, d_shift=d_shift, d_gate=d_gate, pool_scale=d_pool_scale, attn_sink=d_sink[:, 0],
              conv_dw_b=d_dw_b, conv_ln_g=d_ln_g, conv_ln_b=d_ln_b)
    return dx_in, sm


def _local_step(x, target, mods, smalls, get_w, final_g, put_g, end_layer):
    stashes, Ws = [], []
    for l in range(DEPTH):
        w, more_w = get_w(l, x)
        Ws.append(w)
        x, st = _layer_fwd(x, mods[l], smalls[l], w, more_w)
        stashes.append(st)
    dx, d_final_g, loss_lanes = _final_loss(x, target, final_g)
    sms = [None] * DEPTH
    for l in reversed(range(DEPTH)):
        dx, sms[l] = _layer_bwd(dx, stashes[l], mods[l], smalls[l], Ws[l], functools.partial(put_g, l))
        end_layer(l, dx)
    return loss_lanes, dx, d_final_g, sms


BIG = ("w_in", "pool_w", "conv_pw", "wbp", "wba", "wbc", "w_out")
GRADS = BIG + ("conv_dw",)


def _to_internal(w):
    return jnp.concatenate([w[..., :3072], w[..., 3584:], w[..., 3072:3584]], axis=-1)


def _from_internal(w):
    return jnp.concatenate([w[..., :3072], w[..., COL_K:], w[..., 3072:COL_K]], axis=-1)


def _full_w_in(g):
    w_in = _to_internal(jnp.transpose(g, (1, 0, 2)).reshape(D, IN_WIDTH))
    return dict(w_in=w_in, w_in_t=w_in.T)


def _full_weights(g, conv_dw):
    t = lambda a: jnp.swapaxes(a, -1, -2)
    cols = lambda a: jnp.transpose(a, (1, 0, 2)).reshape(a.shape[1], -1)
    pool_w = jnp.transpose(g["pool_w"], (1, 0, 2, 3)).reshape(4, 256, 256)
    conv_pw = g["conv_pw"].reshape(1024, 1024)
    wbp, wba, wbc = cols(g["wbp"]), cols(g["wba"]), cols(g["wbc"])
    w_out = g["w_out"].reshape(D, D)
    conv_dw = jnp.pad(cols(conv_dw), ((0, 32 - CONV_K), (0, 0)))
    return dict(pool_w=pool_w, pool_w_t=t(pool_w), conv_pw=conv_pw, conv_pw_t=t(conv_pw),
                wbp=wbp, wba=wba, wbc=wbc, wbp_t=t(wbp), wba_t=t(wba), wbc_t=t(wbc), w_out=w_out, w_out_t=t(w_out),
                conv_dw=conv_dw)


def _pieces(name, g):
    if name == "w_in":
        return jnp.transpose(_from_internal(g).reshape(D, 8, IN_WIDTH // 8), (1, 0, 2))
    if name == "pool_w":
        return jnp.transpose(g.reshape(4, 8, 32, 256), (1, 0, 2, 3))
    if name == "conv_dw":
        return jnp.transpose(g.reshape(CONV_K, 8, 128), (1, 0, 2))
    if name == "conv_pw":
        return g.reshape(8, 128, 1024)
    if name in ("wbp", "wba", "wbc"):
        return jnp.transpose(g.reshape(1024, 8, 256), (1, 0, 2))
    return g.reshape(8, 256, D)


def _pack_small(items, rows):
    flat = jnp.concatenate([a.reshape(-1).astype(F32) for a in items])
    return jnp.pad(flat, (0, rows * LANE - flat.shape[0])).reshape(rows, LANE)


def _unpack_small(packed, shapes):
    flat, out, off = packed.reshape(-1), [], 0
    for s in shapes:
        n = 1
        for d in s:
            n *= d
        out.append(flat[off:off + n].reshape(s))
        off += n
    return out


def kernel(x, c, norm_g, w_ada, b_ada, w_in, pool_w, pool_scale, attn_sink, conv_dw, conv_dw_b, conv_ln_g, conv_ln_b, conv_pw, w_branch_pool, w_branch_attn, w_branch_conv, w_out, final_g, loss_target, m_norm_g, m_w_ada, m_b_ada, m_w_in, m_pool_w, m_pool_scale, m_attn_sink, m_conv_dw, m_conv_dw_b, m_conv_ln_g, m_conv_ln_b, m_conv_pw, m_w_branch_pool, m_w_branch_attn, m_w_branch_conv, m_w_out, m_final_g, v_norm_g, v_w_ada, v_b_ada, v_w_in, v_pool_w, v_pool_scale, v_attn_sink, v_conv_dw, v_conv_dw_b, v_conv_ln_g, v_conv_ln_b, v_conv_pw, v_w_branch_pool, v_w_branch_attn, v_w_branch_conv, v_w_out, v_final_g):
    L = DEPTH
    me = 4 * lax.axis_index("x") + 2 * lax.axis_index("y") + lax.axis_index("c")
    shards = dict(w_in=w_in, pool_w=pool_w, conv_dw=conv_dw, conv_pw=conv_pw, wbp=w_branch_pool, wba=w_branch_attn,
                  wbc=w_branch_conv, w_out=w_out)
    moms = dict(w_in=(m_w_in, v_w_in), pool_w=(m_pool_w, v_pool_w), conv_dw=(m_conv_dw, v_conv_dw),
                conv_pw=(m_conv_pw, v_conv_pw), wbp=(m_w_branch_pool, v_w_branch_pool),
                wba=(m_w_branch_attn, v_w_branch_attn), wbc=(m_w_branch_conv, v_w_branch_conv), w_out=(m_w_out, v_w_out))

    n_cd = L * CONV_K * 128
    first = _all_gather([_pack_small([c, conv_dw], 144)], "gather_c")[0].reshape(N_DEV, -1)
    c_all = first[:, :D]
    conv_dw_all = first[:, D:D + n_cd].reshape(N_DEV, L, CONV_K, 128)

    mod_part = _mod_fwd(c_all, w_ada)
    mod_all = _all_gather([mod_part.reshape(-1, LANE)], "gather_mod")[0].reshape(N_DEV, L, N_DEV, -1)
    mod = jnp.transpose(lax.dynamic_index_in_dim(mod_all, me, axis=2, keepdims=False), (1, 0, 2)).reshape(L, 3 * D)
    mod = mod + b_ada
    mods = [(mod[l:l + 1, :D], mod[l:l + 1, D:2 * D], mod[l:l + 1, 2 * D:]) for l in range(L)]

    gathers, tok = [], mod[0, 0] * 0.0
    for l in range(L):
        mine = [(shards["w_in"][l] + tok).astype(BF16)], [shards[k][l].astype(BF16) for k in BIG[1:]]
        started = [_spread_start(v, False, f"gather_start_{l}_{n}") for n, v in enumerate(mine)]
        gathers.append((mine, started))
        tok = started[0][-1][0, 0] + started[1][-1][0, 0]
    mods[0] = (mods[0][0] + tok,) + mods[0][1:]

    sink_b = jnp.broadcast_to(attn_sink[:, :, None], (L, N_HEADS, LANE))
    smalls = [dict(norm_g=norm_g[l:l + 1], pool_scale=pool_scale[l:l + 1], sink_b=sink_b[l], conv_dw_b=conv_dw_b[l:l + 1],
                   conv_ln_g=conv_ln_g[l:l + 1], conv_ln_b=conv_ln_b[l:l + 1]) for l in range(L)]

    def with_mine(landed, mine):
        return lax.dynamic_update_slice(landed, mine, (me,) + (0,) * (landed.ndim - 1))

    def get_w(l, x_in):
        mine, started = gathers[l]
        w_in = with_mine(_spread_wait(started[0], x_in, False, f"gather_wait_{l}_0")[0], mine[0][0][None])

        def more_w(proj):
            landed = _spread_wait(started[1], proj, False, f"gather_wait_{l}_1")
            g = {k: with_mine(a, b[None]) for k, a, b in zip(BIG[1:], landed, mine[1])}
            return _full_weights(g, conv_dw_all[:, l])

        return _full_w_in(w_in), more_w

    pending, parts = {l: [] for l in range(L)}, [dict() for _ in range(L)]

    def put_g(l, grads):
        names = tuple(k for k in GRADS if k in grads)
        pieces = [_pieces(k, grads[k]).astype(BF16) for k in names]
        started = _spread_start(pieces, True, f"scatter_start_{l}_{len(pending[l])}")
        pending[l].append((names, pieces, started))
        return started[-1][0:1, 0:1]

    def finish(l, after):
        for n, (names, pieces, started) in enumerate(pending.pop(l)):
            landed = _spread_wait(started, after, True, f"scatter_wait_{l}_{n}")
            for k, a, b in zip(names, landed, pieces):
                parts[l][k] = with_mine(a, lax.dynamic_slice_in_dim(b, me, 1, axis=0))

    def end_layer(l, dx):
        if l + 1 in pending:
            finish(l + 1, dx)

    loss_lanes, grad_x, d_final_g, sms = _local_step(x[0], loss_target[0], mods, smalls, get_w, final_g.reshape(1, D),
                                                     put_g, end_layer)
    finish(0, grad_x)

    stack = lambda k: jnp.concatenate([sms[l][k].reshape(1, -1) for l in range(L)], axis=0)
    scale_all = jnp.concatenate([mods[l][1] for l in range(L)], axis=0)
    d_norm_g, d_scale = _mod_bwd(stack("d_a"), norm_g, scale_all)
    dmod = jnp.concatenate([stack("d_shift"), d_scale, stack("d_gate")], axis=1)
    small_names = ("norm_g", "b_ada", "pool_scale", "attn_sink", "conv_dw_b", "conv_ln_g", "conv_ln_b", "final_g")
    small_g = (d_norm_g, dmod, stack("pool_scale"), stack("attn_sink"), stack("conv_dw_b"), stack("conv_ln_g"),
               stack("conv_ln_b"), d_final_g.reshape(D))
    small_w = (norm_g, b_ada, pool_scale, attn_sink, conv_dw_b, conv_ln_g, conv_ln_b, final_g)
    small_m = (m_norm_g, m_b_ada, m_pool_scale, m_attn_sink, m_conv_dw_b, m_conv_ln_g, m_conv_ln_b, m_final_g)
    small_v = (v_norm_g, v_b_ada, v_pool_scale, v_attn_sink, v_conv_dw_b, v_conv_ln_g, v_conv_ln_b, v_final_g)
    shapes = [a.shape for a in small_w] + [(D,)]
    n_small = sum(a.size for a in small_w) + D
    R = -(-n_small // (8 * LANE)) * 8
    zero = jnp.zeros((D,), F32)
    small_parts = _all_gather([_pack_small(small_g + (loss_lanes,), R)], "gather_small")[0]
    sg, sd, sm2, sv2 = _small_final(small_parts, _pack_small(small_w + (zero,), R), _pack_small(small_m + (zero,), R),
                                    _pack_small(small_v + (zero + 1.0,), R))
    sg, sd, sm2, sv2 = (_unpack_small(a, shapes) for a in (sg, sd, sm2, sv2))
    loss = jnp.sum(sg[-1])
    res = {n: (sg[i], sd[i], sm2[i], sv2[i]) for i, n in enumerate(small_names)}

    off = norm_g.size
    dmod_all = small_parts.reshape(N_DEV, -1)[:, off:off + L * 3 * D].reshape(N_DEV, L, 3 * D)
    dmod_mine = jnp.transpose(lax.dynamic_slice_in_dim(dmod_all, me * (3 * D // N_DEV), 3 * D // N_DEV, axis=2), (1, 0, 2))
    res["w_ada"] = _wada_bwd(c_all.T, dmod_mine, w_ada, m_w_ada, v_w_ada)

    for k in GRADS:
        shp = shards[k].shape
        to3d = lambda a: a.reshape(L, -1, shp[-1])
        out = _reduce_adamw([parts[l][k].reshape(N_DEV, -1, shp[-1]) for l in range(L)], to3d(shards[k]),
                            to3d(moms[k][0]), to3d(moms[k][1]), "adamw_" + k)
        res[k] = tuple(a.reshape(shp) for a in out)

    order = ("norm_g", "w_ada", "b_ada", "w_in", "pool_w", "pool_scale", "attn_sink", "conv_dw", "conv_dw_b", "conv_ln_g",
             "conv_ln_b", "conv_pw", "wbp", "wba", "wbc", "w_out", "final_g")
    outs = [loss, grad_x[None]]
    for j in range(4):
        outs += [res[n][j] for n in order]
    return tuple(outs)
```

```python
import functools

import jax
import jax.numpy as jnp
from jax import lax
from jax.experimental import pallas as pl
from jax.experimental.pallas import tpu as pltpu

F32, BF16 = jnp.float32, jnp.bfloat16
MESH = pl.DeviceIdType.MESH
ANY = pl.BlockSpec(memory_space=pl.ANY)

N_DEV = 8
D = 2048
DEPTH = 4
EPS = 1e-6
IN_WIDTH = 13824
POOL_WINDOWS = (2, 4, 8, 16)
POOL_GROUP = 256
POOL_HALO = 16
CONV_K = 31
CONV_HALO = 32
N_HEADS, N_KV, HEAD_DIM = 16, 4, 64
ATTN_TQ = 256
ATTN_BACK = 128
LANE = 128
VMEM_BIG = 56 * 1024 * 1024

CB_U, CB_Z, CB_Q, CB_AZ, CB_CA, CB_CB, CB_CZ, CB_GP, CB_GA, CB_GC = 0, 1, 2, 3, 4, 5, 6, 7, 9, 11
COL_K, COL_V = 13312, 13568

ADAM_LR, ADAM_B1, ADAM_B2, ADAM_EPS, ADAM_WD, ADAM_STEP = 0.001, 0.9, 0.999, 1e-08, 0.01, 10


def _cp(sem=None, vmem=None):
    return pltpu.CompilerParams(dimension_semantics=sem, vmem_limit_bytes=vmem)


def _sig(x):
    return jax.nn.sigmoid(x)


def _silu(x):
    return x * _sig(x)


def _dsilu(x):
    s = _sig(x)
    return s * (1.0 + x * (1.0 - s))


def _dot(a, b):
    return jnp.dot(a, b, preferred_element_type=F32)


def _dot_tn(a, b):
    return lax.dot_general(a, b, (((0,), (0,)), ((), ())), preferred_element_type=F32)


def _dot_nt(a, b):
    return lax.dot_general(a, b, (((1,), (1,)), ((), ())), preferred_element_type=F32)


def _full(shape):
    n = len(shape)
    return pl.BlockSpec(shape, lambda *_: (0,) * n)


def _my_pos():
    return lax.axis_index("x"), lax.axis_index("y"), lax.axis_index("c")


def _all_gather(xs, name):
    n = len(xs)

    def body(*refs):
        x_refs, o_refs = refs[:n], refs[n:2 * n]
        send_sems, recv_sems, local_sems = refs[2 * n:]
        x, y, c = _my_pos()
        sibling = (x, y, 1 - c)
        chips = [(1 - x, y), (x, 1 - y), (1 - x, 1 - y)]
        me = 4 * x + 2 * y + c

        def slot(px, py, pc):
            return 4 * px + 2 * py + pc

        def copy(t, k, block, to, src=None):
            dst = o_refs[t].at[block]
            return pltpu.make_async_remote_copy(
                src_ref=dst if src is None else src, dst_ref=dst,
                send_sem=send_sems.at[t, k], recv_sem=recv_sems.at[t, k],
                device_id=to, device_id_type=MESH)

        mine = [pltpu.make_async_copy(x_refs[t], o_refs[t].at[me], local_sems.at[t]) for t in range(n)]
        for cp in mine:
            cp.start()
        first = []
        for t in range(n):
            first.append(copy(t, 0, me, sibling, src=x_refs[t]))
            for j, chip in enumerate(chips):
                first.append(copy(t, 1 + j, me, (*chip, c), src=x_refs[t]))
        for cp in first:
            cp.start()
        passed = []
        for j, chip in enumerate(chips):
            for t in range(n):
                copy(t, 1 + j, slot(*chip, c), (x, y, c)).wait_recv()
                fwd = copy(t, 4 + j, slot(*chip, c), sibling)
                fwd.start()
                passed.append(fwd)
        for t in range(n):
            copy(t, 0, slot(x, y, 1 - c), (x, y, c)).wait_recv()
            for j, chip in enumerate(chips):
                copy(t, 4 + j, slot(*chip, 1 - c), (x, y, c)).wait_recv()
        for cp in first + passed:
            cp.wait_send()
        for cp in mine:
            cp.wait()

    return pl.pallas_call(
        body, name=name,
        out_shape=[jax.ShapeDtypeStruct((N_DEV,) + a.shape, a.dtype) for a in xs],
        in_specs=[ANY] * n, out_specs=[ANY] * n,
        scratch_shapes=[pltpu.SemaphoreType.DMA((n, 7)), pltpu.SemaphoreType.DMA((n, 7)),
                        pltpu.SemaphoreType.DMA((n,))],
    )(*xs)


N_PEER = N_DEV - 1
HBM = pl.BlockSpec(memory_space=pltpu.HBM)
SEM = pl.BlockSpec(memory_space=pltpu.SEMAPHORE)
EFFECT = pltpu.SideEffectType.DATAFLOW_SIDE_EFFECTING


def _peer(k):
    x, y, c = _my_pos()
    flip = lambda v, bit: 1 - v if bit else v
    return flip(x, (k >> 2) & 1), flip(y, (k >> 1) & 1), flip(c, k & 1)


def _spread_copies(v_ref, land_ref, send_sems, recv_sems, per_peer):
    x, y, c = _my_pos()
    me = 4 * x + 2 * y + c
    copies = []
    for k in range(1, N_DEV):
        px, py, pc = _peer(k)
        src = v_ref.at[4 * px + 2 * py + pc] if per_peer else v_ref
        copies.append(pltpu.make_async_remote_copy(
            src_ref=src, dst_ref=land_ref.at[me], send_sem=send_sems[k - 1], recv_sem=recv_sems[k - 1],
            device_id=(px, py, pc), device_id_type=MESH))
    return copies


def _spread_start(vs, per_peer, name):
    n = len(vs)
    lands = [(N_DEV,) + (v.shape[1:] if per_peer else v.shape) for v in vs]
    n_sem = 2 * N_PEER * n

    def body(*refs):
        v_refs, land_refs, outs = refs[:n], refs[n:2 * n], refs[2 * n:]
        for t in range(n):
            sems = outs[2 * N_PEER * t:2 * N_PEER * (t + 1)]
            for cp in _spread_copies(v_refs[t], land_refs[t], sems[:N_PEER], sems[N_PEER:], per_peer):
                cp.start()
        token = outs[n_sem + 2 * n]
        token[...] = jnp.zeros_like(token)

    hbm = lambda a: pltpu.with_memory_space_constraint(a, pltpu.HBM)
    return pl.pallas_call(
        body, name=name,
        out_shape=((pltpu.SemaphoreType.DMA(()),) * n_sem + tuple(pltpu.HBM(v.shape, v.dtype) for v in vs)
                   + tuple(pltpu.HBM(s, v.dtype) for s, v in zip(lands, vs)) + (jax.ShapeDtypeStruct((8, LANE), F32),)),
        in_specs=(HBM,) * (2 * n), out_specs=(SEM,) * n_sem + (HBM,) * (2 * n) + (pl.BlockSpec(memory_space=pltpu.VMEM),),
        input_output_aliases={t: n_sem + t for t in range(2 * n)},
        compiler_params=pltpu.CompilerParams(has_side_effects=EFFECT),
    )(*[hbm(v) for v in vs], *[hbm(lax.empty(s, v.dtype)) for s, v in zip(lands, vs)])


def _spread_wait(started, after, per_peer, name):
    n = (len(started) - 1) // (2 * N_PEER + 2)
    n_sem = 2 * N_PEER * n
    sems, thru = started[:n_sem], started[n_sem:n_sem + 2 * n]

    def body(*refs):
        v_refs, land_refs, rest = refs[:n], refs[n:2 * n], refs[2 * n:]
        for t in range(n):
            s = rest[2 * N_PEER * t:2 * N_PEER * (t + 1)]
            for cp in _spread_copies(v_refs[t], land_refs[t], s[:N_PEER], s[N_PEER:], per_peer):
                cp.wait_send()
                cp.wait_recv()

    return pl.pallas_call(
        body, name=name,
        out_shape=tuple(pltpu.HBM(a.shape, a.dtype) for a in thru),
        in_specs=(HBM,) * (2 * n) + (SEM,) * n_sem + (ANY,), out_specs=(HBM,) * (2 * n),
        input_output_aliases={t: t for t in range(2 * n)},
        compiler_params=pltpu.CompilerParams(has_side_effects=EFFECT),
    )(*thru, *sems, after)[n:]


def _mm(a, b, out_dtype, tm, tn, tk=None, name="mm", vmem=None, after=None, nt=False):
    M, K = a.shape
    N = b.shape[0] if nt else b.shape[1]
    tk = K if tk is None else tk
    nk = K // tk
    assert M % tm == 0 and N % tn == 0 and K % tk == 0
    dep = () if after is None else (after,)

    def body(*refs):
        a_ref, b_ref, o_ref, *acc = refs[len(dep):]
        prod = (_dot_nt if nt else _dot)(a_ref[...].astype(BF16), b_ref[...])
        if nk == 1:
            o_ref[...] = prod.astype(out_dtype)
            return
        acc_ref = acc[0] if acc else o_ref
        k = pl.program_id(2)

        @pl.when(k == 0)
        def _():
            acc_ref[...] = prod

        @pl.when(k > 0)
        def _():
            acc_ref[...] += prod

        if acc:
            @pl.when(k == nk - 1)
            def _():
                o_ref[...] = acc_ref[...].astype(out_dtype)

    scratch = [pltpu.VMEM((tm, tn), F32)] if (nk > 1 and out_dtype != F32) else []
    b_spec = pl.BlockSpec((tn, tk), lambda j, i, k: (j, k)) if nt else pl.BlockSpec((tk, tn), lambda j, i, k: (k, j))
    return pl.pallas_call(
        body, name=name, grid=(N // tn, M // tm, nk),
        in_specs=[_full((1, 1))] * len(dep) + [pl.BlockSpec((tm, tk), lambda j, i, k: (i, k)), b_spec],
        out_specs=pl.BlockSpec((tm, tn), lambda j, i, k: (i, j)),
        out_shape=jax.ShapeDtypeStruct((M, N), out_dtype), scratch_shapes=scratch,
        compiler_params=_cp(("parallel", "parallel", "arbitrary"), vmem),
    )(*dep, a, b)


def _mm_tn(a, b, tm, tn, ts, name="mm_tn", vmem=None, after=None):
    S, Ka = a.shape
    _, N = b.shape
    assert Ka % tm == 0 and N % tn == 0 and S % ts == 0
    dep = () if after is None else (after,)

    def body(*refs):
        a_ref, b_ref, o_ref = refs[len(dep):]
        prod = _dot_tn(a_ref[...].astype(BF16), b_ref[...].astype(BF16))
        k = pl.program_id(2)

        @pl.when(k == 0)
        def _():
            o_ref[...] = prod

        @pl.when(k > 0)
        def _():
            o_ref[...] += prod

    return pl.pallas_call(
        body, name=name, grid=(Ka // tm, N // tn, S // ts),
        in_specs=[_full((1, 1))] * len(dep) + [pl.BlockSpec((ts, tm), lambda i, j, k: (k, i)),
                                               pl.BlockSpec((ts, tn), lambda i, j, k: (k, j))],
        out_specs=pl.BlockSpec((tm, tn), lambda i, j, k: (i, j)),
        out_shape=jax.ShapeDtypeStruct((Ka, N), F32),
        compiler_params=_cp(("parallel", "parallel", "arbitrary"), vmem),
    )(*dep, a, b)


def _mod_fwd(c_all, w_ada):
    L, _, n = w_ada.shape

    def body(c_ref, w_ref, o_ref):
        ca = _silu(c_ref[...])
        o_ref[0] = jnp.dot(ca, w_ref[0], preferred_element_type=F32, precision=lax.Precision.HIGHEST)

    return pl.pallas_call(
        body, name="mod_fwd", grid=(L,),
        in_specs=[_full((N_DEV, D)), pl.BlockSpec((1, D, n), lambda l: (l, 0, 0))],
        out_specs=pl.BlockSpec((1, N_DEV, n), lambda l: (l, 0, 0)),
        out_shape=jax.ShapeDtypeStruct((L, N_DEV, n), F32),
        compiler_params=_cp(("parallel",)),
    )(c_all, w_ada)


def _adam_math(g, w, m, v):
    m2 = ADAM_B1 * m + (1.0 - ADAM_B1) * g
    v2 = ADAM_B2 * v + (1.0 - ADAM_B2) * (g * g)
    m_hat = m2 / (1.0 - ADAM_B1 ** ADAM_STEP)
    v_hat = v2 / (1.0 - ADAM_B2 ** ADAM_STEP)
    delta = -ADAM_LR * (m_hat / (jnp.sqrt(v_hat) + ADAM_EPS) + ADAM_WD * w)
    return delta, m2, v2


def _wada_bwd(c_all_t, dmod, w, m, v, tr=256):
    L, _, n = w.shape

    def body(c_ref, d_ref, w_ref, m_ref, v_ref, g_ref, dl_ref, m2_ref, v2_ref):
        ca = _silu(c_ref[...])
        dm = d_ref[0]
        g = ca[:, 0:1] * dm[0:1, :]
        for b in range(1, N_DEV):
            g = g + ca[:, b:b + 1] * dm[b:b + 1, :]
        delta, m2, v2 = _adam_math(g, w_ref[0], m_ref[0], v_ref[0])
        g_ref[0], dl_ref[0], m2_ref[0], v2_ref[0] = g, delta, m2, v2

    blk = pl.BlockSpec((1, tr, n), lambda l, i: (l, i, 0))
    return pl.pallas_call(
        body, name="wada_bwd", grid=(L, D // tr),
        in_specs=[pl.BlockSpec((tr, N_DEV), lambda l, i: (i, 0)), pl.BlockSpec((1, N_DEV, n), lambda l, i: (l, 0, 0)),
                  blk, blk, blk],
        out_specs=[blk] * 4, out_shape=[jax.ShapeDtypeStruct(w.shape, F32)] * 4,
        compiler_params=_cp(("parallel", "parallel")),
    )(c_all_t, dmod, w, m, v)


def _norm_mod(x, g, scale, shift, tb=256):
    S = x.shape[0]

    def body(x_ref, g_ref, sc_ref, sh_ref, h_ref):
        xv = x_ref[...]
        r = lax.rsqrt(jnp.mean(xv * xv, axis=-1, keepdims=True) + EPS)
        h_ref[...] = (xv * r * (g_ref[...] * (1.0 + sc_ref[...])) + sh_ref[...]).astype(BF16)

    row = pl.BlockSpec((tb, D), lambda i: (i, 0))
    vec = _full((1, D))
    return pl.pallas_call(
        body, name="norm_mod", grid=(S // tb,), in_specs=[row, vec, vec, vec], out_specs=row,
        out_shape=jax.ShapeDtypeStruct((S, D), BF16), compiler_params=_cp(("parallel",)),
    )(x, g, scale, shift)


def _pool_mix(u_ref, uh_ref, ubuf, i, tb):
    H = POOL_HALO
    ubuf[H:, :] = u_ref[...].astype(F32)
    ubuf[:H, :] = jnp.where(i > 0, uh_ref[...].astype(F32), 0.0)
    t = i * tb + lax.broadcasted_iota(jnp.int32, (tb, 1), 0)
    mixed = []
    for g, w in enumerate(POOL_WINDOWS):
        cs = slice(g * POOL_GROUP, (g + 1) * POOL_GROUP)
        cur = ubuf[H:, cs]
        acc = cur
        for j in range(1, w):
            acc = acc + ubuf[pl.ds(H - j, tb), cs]
        cnt = jnp.minimum(t + 1, w).astype(F32)
        mixed.append(acc / cnt - cur)
    return mixed


def _pool_specs(tb):
    H = POOL_HALO
    u = pl.BlockSpec((tb, 1024), lambda i: (i, CB_U))
    uh = pl.BlockSpec((H, 1024), lambda i: (jnp.maximum(i * (tb // H) - 1, 0), CB_U))
    z = pl.BlockSpec((tb, 1024), lambda i: (i, CB_Z))
    return u, uh, z


def _pool_fwd(proj, pool_w, pool_scale, tb=256):
    S = proj.shape[0]

    def body(u_ref, uh_ref, z_ref, w_ref, sc_ref, y_ref, ubuf):
        i = pl.program_id(0)
        mixed = _pool_mix(u_ref, uh_ref, ubuf, i, tb)
        m = jnp.concatenate([_dot(mixed[g].astype(BF16), w_ref[g]) for g in range(4)], axis=1)
        y_ref[...] = (m * sc_ref[...] * _silu(z_ref[...].astype(F32))).astype(BF16)

    u, uh, z = _pool_specs(tb)
    return pl.pallas_call(
        body, name="pool_fwd", grid=(S // tb,),
        in_specs=[u, uh, z, _full((4, 256, 256)), _full((1, 1024))],
        out_specs=pl.BlockSpec((tb, 1024), lambda i: (i, 0)),
        out_shape=jax.ShapeDtypeStruct((S, 1024), BF16),
        scratch_shapes=[pltpu.VMEM((tb + POOL_HALO, 1024), F32)],
        compiler_params=_cp(("parallel",)),
    )(proj, proj, proj, pool_w, pool_scale)


def _attn_mask(i):
    TQ, NK = ATTN_TQ, ATTN_TQ + ATTN_BACK
    qc = lax.broadcasted_iota(jnp.int32, (TQ, NK), 0) // 64
    col = lax.broadcasted_iota(jnp.int32, (TQ, NK), 1)
    kc = col // 64
    return (kc >= qc) & (kc <= qc + 2) & ((col >= ATTN_BACK) | (i > 0))


def _attn_probs(q_ref, kw, sink_ref, valid, h):
    qh = q_ref[:, h * HEAD_DIM:(h + 1) * HEAD_DIM]
    s = _dot_nt(qh, kw) * (HEAD_DIM ** -0.5)
    s = jnp.where(valid, s, -jnp.inf)
    sk = sink_ref[h:h + 1, 0:1]
    mx = jnp.maximum(jnp.max(s, axis=-1, keepdims=True), sk)
    p = jnp.exp(s - mx)
    es = jnp.exp(sk - mx)
    den = jnp.sum(p, axis=-1, keepdims=True) + es
    return p / den, es / den


def _kv_specs():
    TQ, B = ATTN_TQ, ATTN_BACK
    blk = lambda col: pl.BlockSpec((TQ, 256), lambda i: (i, col // 256))
    halo = lambda col: pl.BlockSpec((B, 256), lambda i: (jnp.maximum(i * (TQ // B) - 1, 0), col // 256))
    return [blk(COL_K), halo(COL_K), blk(COL_V), halo(COL_V)]


def _kv_window(k_ref, kh_ref, v_ref, vh_ref, kbuf, vbuf, i):
    B = ATTN_BACK
    for buf, ref, href in ((kbuf, k_ref, kh_ref), (vbuf, v_ref, vh_ref)):
        buf[:B, :] = jnp.where(i > 0, href[...], jnp.zeros_like(href))
        buf[B:, :] = ref[...]


def _attn_fwd(proj, sink_b):
    S = proj.shape[0]
    TQ, NK = ATTN_TQ, ATTN_TQ + ATTN_BACK

    def body(q_ref, z_ref, k_ref, kh_ref, v_ref, vh_ref, sink_ref, o_ref, y_ref, kbuf, vbuf):
        i = pl.program_id(0)
        _kv_window(k_ref, kh_ref, v_ref, vh_ref, kbuf, vbuf, i)
        valid = _attn_mask(i)
        for h in range(N_HEADS):
            kh = h // (N_HEADS // N_KV)
            ks = slice(kh * HEAD_DIM, (kh + 1) * HEAD_DIM)
            hs = slice(h * HEAD_DIM, (h + 1) * HEAD_DIM)
            pn, _ = _attn_probs(q_ref, kbuf[:, ks], sink_ref, valid, h)
            o = _dot(pn.astype(BF16), vbuf[:, ks])
            o_ref[:, hs] = o.astype(BF16)
            y_ref[:, hs] = (o * _silu(z_ref[:, hs].astype(F32))).astype(BF16)

    out = pl.BlockSpec((TQ, 1024), lambda i: (i, 0))
    return pl.pallas_call(
        body, name="attn_fwd", grid=(S // TQ,),
        in_specs=[pl.BlockSpec((TQ, 1024), lambda i: (i, CB_Q)), pl.BlockSpec((TQ, 1024), lambda i: (i, CB_AZ)),
                  *_kv_specs(), _full((N_HEADS, LANE))],
        out_specs=[out, out], out_shape=[jax.ShapeDtypeStruct((S, 1024), BF16)] * 2,
        scratch_shapes=[pltpu.VMEM((NK, 256), BF16)] * 2,
        compiler_params=_cp(("parallel",)),
    )(proj, proj, proj, proj, proj, proj, sink_b)


def _conv_specs(tb):
    H = CONV_HALO
    prev = lambda i: jnp.maximum(i * (tb // H) - 1, 0)
    a = pl.BlockSpec((tb, 1024), lambda i: (i, CB_CA))
    ah = pl.BlockSpec((H, 1024), lambda i: (prev(i), CB_CA))
    b = pl.BlockSpec((tb, 1024), lambda i: (i, CB_CB))
    bh = pl.BlockSpec((H, 1024), lambda i: (prev(i), CB_CB))
    z = pl.BlockSpec((tb, 1024), lambda i: (i, CB_CZ))
    return a, ah, b, bh, z


SUBLANES = 8


def _shift_copies(bufs, tb):
    n = tb + CONV_HALO - SUBLANES
    for s in range(1, SUBLANES):
        bufs[s, :n, :] = bufs[0, pl.ds(s, n), :]


def _rows_at(bufs, start, offset, cs):
    return bufs[offset % SUBLANES, pl.ds(start + offset - offset % SUBLANES, 128), cs]


def _conv_glu_dw(a_ref, ah_ref, b_ref, bh_ref, dw_ref, gbuf, ybuf, i, tb):
    H = CONV_HALO
    gbuf[0, H:, :] = a_ref[...].astype(F32) * _sig(b_ref[...].astype(F32))
    gh = ah_ref[...].astype(F32) * _sig(bh_ref[...].astype(F32))
    gbuf[0, :H, :] = jnp.where(i > 0, gh, 0.0)
    _shift_copies(gbuf, tb)
    base = H - (CONV_K - 1)
    for c in range(1024 // LANE):
        cs = slice(c * LANE, (c + 1) * LANE)
        for r in range(tb // 128):
            acc = jnp.zeros((128, LANE), F32)
            for j in range(CONV_K):
                acc = acc + dw_ref[j:j + 1, cs] * _rows_at(gbuf, r * 128, base + j, cs)
            ybuf[r * 128:(r + 1) * 128, cs] = acc


def _layer_norm_fwd(y, g, b):
    mu = jnp.mean(y, axis=-1, keepdims=True)
    yc = y - mu
    rstd = lax.rsqrt(jnp.mean(yc * yc, axis=-1, keepdims=True) + EPS)
    xh = yc * rstd
    return xh, rstd, xh * g + b


def _conv_fwd(proj, dw, dw_b, ln_g, ln_b, pw, tb=256):
    S = proj.shape[0]

    def body(a_ref, ah_ref, b_ref, bh_ref, z_ref, dw_ref, dwb_ref, lg_ref, lb_ref, pw_ref, y_ref, gbuf, ybuf):
        i = pl.program_id(0)
        _conv_glu_dw(a_ref, ah_ref, b_ref, bh_ref, dw_ref, gbuf, ybuf, i, tb)
        _, _, yn = _layer_norm_fwd(ybuf[...] + dwb_ref[...], lg_ref[...], lb_ref[...])
        out = _dot(_silu(yn).astype(BF16), pw_ref[...])
        y_ref[...] = (out * _silu(z_ref[...].astype(F32))).astype(BF16)

    vec = _full((1, 1024))
    return pl.pallas_call(
        body, name="conv_fwd", grid=(S // tb,),
        in_specs=[*_conv_specs(tb), _full((32, 1024)), vec, vec, vec, _full((1024, 1024))],
        out_specs=pl.BlockSpec((tb, 1024), lambda i: (i, 0)),
        out_shape=jax.ShapeDtypeStruct((S, 1024), BF16),
        scratch_shapes=[pltpu.VMEM((SUBLANES, tb + CONV_HALO, 1024), F32), pltpu.VMEM((tb, 1024), F32)],
        compiler_params=_cp(("parallel",)),
    )(proj, proj, proj, proj, proj, dw, dw_b, ln_g, ln_b, pw)


def _merge_fwd(yp, ya, yc, wbp, wba, wbc, proj, tm=512, tn=1024):
    S = yp.shape[0]

    def body(yp_ref, ya_ref, yc_ref, wp_ref, wa_ref, wc_ref, gp_ref, ga_ref, gc_ref, m_ref, pp_ref, pa_ref, pc_ref):
        pp = _dot(yp_ref[...], wp_ref[...])
        pa = _dot(ya_ref[...], wa_ref[...])
        pc = _dot(yc_ref[...], wc_ref[...])
        m = (_sig(gp_ref[...].astype(F32)) * pp + _sig(ga_ref[...].astype(F32)) * pa
             + _sig(gc_ref[...].astype(F32)) * pc)
        m_ref[...] = m.astype(BF16)
        pp_ref[...], pa_ref[...], pc_ref[...] = pp.astype(BF16), pa.astype(BF16), pc.astype(BF16)

    yb = pl.BlockSpec((tm, 1024), lambda j, i: (i, 0))
    wb = pl.BlockSpec((1024, tn), lambda j, i: (0, j))
    gate = lambda cb: pl.BlockSpec((tm, tn), lambda j, i: (i, cb + j))
    out = pl.BlockSpec((tm, tn), lambda j, i: (i, j))
    return pl.pallas_call(
        body, name="merge_fwd", grid=(D // tn, S // tm),
        in_specs=[yb, yb, yb, wb, wb, wb, gate(CB_GP), gate(CB_GA), gate(CB_GC)],
        out_specs=[out] * 4, out_shape=[jax.ShapeDtypeStruct((S, D), BF16)] * 4,
        compiler_params=_cp(("parallel", "parallel")),
    )(yp, ya, yc, wbp, wba, wbc, proj, proj, proj)


def _out_fwd(x, merged, w_out, gate, tm=512, tn=1024):
    S = x.shape[0]

    def body(x_ref, m_ref, w_ref, g_ref, o_ref):
        o_ref[...] = x_ref[...] + g_ref[...] * _dot(m_ref[...], w_ref[...])

    xb = pl.BlockSpec((tm, tn), lambda j, i: (i, j))
    return pl.pallas_call(
        body, name="out_fwd", grid=(D // tn, S // tm),
        in_specs=[xb, pl.BlockSpec((tm, D), lambda j, i: (i, 0)), pl.BlockSpec((D, tn), lambda j, i: (0, j)),
                  pl.BlockSpec((1, tn), lambda j, i: (0, j))],
        out_specs=xb, out_shape=jax.ShapeDtypeStruct((S, D), F32),
        compiler_params=_cp(("parallel", "parallel")),
    )(x, merged, w_out, gate)


def _final_loss(x, target, final_g, tb=256):
    S = x.shape[0]

    def body(x_ref, t_ref, g_ref, dx_ref, gg_ref, ls_ref):
        i = pl.program_id(0)
        xv, g = x_ref[...], g_ref[...]
        r = lax.rsqrt(jnp.mean(xv * xv, axis=-1, keepdims=True) + EPS)
        xh = xv * r
        e = xh * g - t_ref[...]
        dy = e * (1.0 / D)
        gy = dy * g
        dx_ref[...] = r * (gy - xh * jnp.mean(gy * xh, axis=-1, keepdims=True))
        gg = jnp.sum(dy * xh, axis=0, keepdims=True)
        ls = jnp.sum(e * e, axis=0, keepdims=True) * (0.5 / D)

        @pl.when(i == 0)
        def _():
            gg_ref[...], ls_ref[...] = gg, ls

        @pl.when(i > 0)
        def _():
            gg_ref[...] += gg
            ls_ref[...] += ls

    row = pl.BlockSpec((tb, D), lambda i: (i, 0))
    vec = _full((1, D))
    return pl.pallas_call(
        body, name="final_loss", grid=(S // tb,), in_specs=[row, row, vec], out_specs=[row, vec, vec],
        out_shape=[jax.ShapeDtypeStruct((S, D), F32), jax.ShapeDtypeStruct((1, D), F32),
                   jax.ShapeDtypeStruct((1, D), F32)],
        compiler_params=_cp(("arbitrary",)),
    )(x, target, final_g)


def _out_bwd(dx, gate, w_out, pp, pa, pc, proj, tm=512, tn=1024):
    S = dx.shape[0]

    def body(dx_ref, g_ref, w_ref, pp_ref, pa_ref, pc_ref, gp_ref, ga_ref, gc_ref,
             dpp_ref, dpa_ref, dpc_ref, dgp_ref, dga_ref, dgc_ref):
        dm = _dot_nt((dx_ref[...] * g_ref[...]).astype(BF16), w_ref[...])
        for p_ref, gl_ref, dp_ref, dg_ref in ((pp_ref, gp_ref, dpp_ref, dgp_ref), (pa_ref, ga_ref, dpa_ref, dga_ref),
                                              (pc_ref, gc_ref, dpc_ref, dgc_ref)):
            s = _sig(gl_ref[...].astype(F32))
            dp_ref[...] = (dm * s).astype(BF16)
            dg_ref[...] = (dm * p_ref[...].astype(F32) * s * (1.0 - s)).astype(BF16)

    out = pl.BlockSpec((tm, tn), lambda j, i: (i, j))
    gate_spec = lambda cb: pl.BlockSpec((tm, tn), lambda j, i: (i, cb + j))
    return pl.pallas_call(
        body, name="out_bwd", grid=(D // tn, S // tm),
        in_specs=[pl.BlockSpec((tm, D), lambda j, i: (i, 0)), _full((1, D)), pl.BlockSpec((tn, D), lambda j, i: (j, 0)),
                  out, out, out, gate_spec(CB_GP), gate_spec(CB_GA), gate_spec(CB_GC)],
        out_specs=[out] * 6, out_shape=[jax.ShapeDtypeStruct((S, D), BF16)] * 6,
        compiler_params=_cp(("parallel", "parallel")),
    )(dx, gate, w_out, pp, pa, pc, proj, proj, proj)


def _wout_post(gmat, w_out, gate, tr=256):
    def body(g_ref, w_ref, gate_ref, dw_ref, dg_ref):
        i = pl.program_id(0)
        gm = g_ref[...]
        dw_ref[...] = gm * gate_ref[...]
        part = jnp.sum(gm * w_ref[...].astype(F32), axis=0, keepdims=True)

        @pl.when(i == 0)
        def _():
            dg_ref[...] = part

        @pl.when(i > 0)
        def _():
            dg_ref[...] += part

    row = pl.BlockSpec((tr, D), lambda i: (i, 0))
    return pl.pallas_call(
        body, name="wout_post", grid=(D // tr,), in_specs=[row, row, _full((1, D))], out_specs=[row, _full((1, D))],
        out_shape=[jax.ShapeDtypeStruct((D, D), F32), jax.ShapeDtypeStruct((1, D), F32)],
        compiler_params=_cp(("arbitrary",)),
    )(gmat, w_out, gate)


def _pool_bwd_a(dy, proj, pool_w, pool_scale, tb=256):
    S = proj.shape[0]

    def body(dy_ref, u_ref, uh_ref, z_ref, w_ref, sc_ref, dmix_ref, dz_ref, dsc_ref, dw_ref, ubuf):
        i = pl.program_id(0)
        mixed = [m.astype(BF16) for m in _pool_mix(u_ref, uh_ref, ubuf, i, tb)]
        m = jnp.concatenate([_dot(mixed[g], w_ref[g]) for g in range(4)], axis=1)
        dyv, z, sc = dy_ref[...].astype(F32), z_ref[...].astype(F32), sc_ref[...]
        dyp = dyv * _silu(z)
        dz_ref[...] = (dyv * (m * sc) * _dsilu(z)).astype(BF16)
        dsc = jnp.sum(dyp * m, axis=0, keepdims=True)
        dmm = (dyp * sc).astype(BF16)
        dws = []
        for g in range(4):
            cs = slice(g * POOL_GROUP, (g + 1) * POOL_GROUP)
            dmix_ref[:, cs] = _dot_nt(dmm[:, cs], w_ref[g])
            dws.append(_dot_tn(mixed[g], dmm[:, cs]))

        @pl.when(i == 0)
        def _():
            dsc_ref[...] = dsc
            for g in range(4):
                dw_ref[g] = dws[g]

        @pl.when(i > 0)
        def _():
            dsc_ref[...] += dsc
            for g in range(4):
                dw_ref[g] += dws[g]

    u, uh, z = _pool_specs(tb)
    row = pl.BlockSpec((tb, 1024), lambda i: (i, 0))
    wfull = _full((4, 256, 256))
    return pl.pallas_call(
        body, name="pool_bwd_a", grid=(S // tb,),
        in_specs=[row, u, uh, z, wfull, _full((1, 1024))],
        out_specs=[row, row, _full((1, 1024)), wfull],
        out_shape=[jax.ShapeDtypeStruct((S, 1024), F32), jax.ShapeDtypeStruct((S, 1024), BF16),
                   jax.ShapeDtypeStruct((1, 1024), F32), jax.ShapeDtypeStruct((4, 256, 256), F32)],
        scratch_shapes=[pltpu.VMEM((tb + POOL_HALO, 1024), F32)],
        compiler_params=_cp(("arbitrary",)),
    )(dy, proj, proj, proj, pool_w, pool_scale)


def _pool_bwd_b(dmix, tb=256):
    S = dmix.shape[0]
    H = POOL_HALO
    nb = S // tb

    def body(dm_ref, dh_ref, du_ref, ebuf):
        i = pl.program_id(0)
        t = i * tb + lax.broadcasted_iota(jnp.int32, (tb, 1), 0)
        th = (i + 1) * tb + lax.broadcasted_iota(jnp.int32, (H, 1), 0)
        for g, w in enumerate(POOL_WINDOWS):
            cs = slice(g * POOL_GROUP, (g + 1) * POOL_GROUP)
            ebuf[:tb, cs] = dm_ref[:, cs] / jnp.minimum(t + 1, w).astype(F32)
            eh = dh_ref[:, cs] / jnp.minimum(th + 1, w).astype(F32)
            ebuf[tb:, cs] = jnp.where(i < nb - 1, eh, 0.0)
        for g, w in enumerate(POOL_WINDOWS):
            cs = slice(g * POOL_GROUP, (g + 1) * POOL_GROUP)
            acc = ebuf[:tb, cs]
            for j in range(1, w):
                acc = acc + ebuf[pl.ds(j, tb), cs]
            du_ref[:, cs] = (acc - dm_ref[:, cs]).astype(BF16)

    row = pl.BlockSpec((tb, 1024), lambda i: (i, 0))
    nxt = pl.BlockSpec((H, 1024), lambda i: (jnp.minimum((i + 1) * (tb // H), S // H - 1), 0))
    return pl.pallas_call(
        body, name="pool_bwd_b", grid=(nb,), in_specs=[row, nxt], out_specs=row,
        out_shape=jax.ShapeDtypeStruct((S, 1024), BF16),
        scratch_shapes=[pltpu.VMEM((tb + H, 1024), F32)],
        compiler_params=_cp(("parallel",)),
    )(dmix, dmix)


def _attn_bwd(dy, o, proj, sink_b):
    S = proj.shape[0]
    TQ, NK = ATTN_TQ, ATTN_TQ + ATTN_BACK
    nb = S // TQ
    G = N_HEADS // N_KV

    def body(dy_ref, o_ref, q_ref, z_ref, k_ref, kh_ref, v_ref, vh_ref, sink_ref, dq_ref, dz_ref, dk_hbm, dv_hbm, ds_ref,
             dk_acc, dv_acc, kbuf, vbuf):
        i = pl.program_id(0)
        _kv_window(k_ref, kh_ref, v_ref, vh_ref, kbuf, vbuf, i)

        @pl.when(i == 0)
        def _():
            dk_acc[...] = jnp.zeros_like(dk_acc)
            dv_acc[...] = jnp.zeros_like(dv_acc)
            ds_ref[...] = jnp.zeros_like(ds_ref)

        start = pl.multiple_of(i * TQ, TQ)
        valid = _attn_mask(i)
        dks, dvs = [], []
        for kh in range(N_KV):
            ks = slice(kh * HEAD_DIM, (kh + 1) * HEAD_DIM)
            kw = kbuf[:, ks]
            vw = vbuf[:, ks]
            dk_sum = jnp.zeros((NK, HEAD_DIM), F32)
            dv_sum = jnp.zeros((NK, HEAD_DIM), F32)
            for gi in range(G):
                h = kh * G + gi
                hs = slice(h * HEAD_DIM, (h + 1) * HEAD_DIM)
                pn, psink = _attn_probs(q_ref, kw, sink_ref, valid, h)
                z = z_ref[:, hs].astype(F32)
                dyv = dy_ref[:, hs].astype(F32)
                ov = o_ref[:, hs].astype(F32)
                do = dyv * _silu(z)
                dz_ref[:, hs] = (dyv * ov * _dsilu(z)).astype(BF16)
                delta = jnp.sum(do * ov, axis=-1, keepdims=True)
                dob = do.astype(BF16)
                dp = _dot_nt(dob, vw)
                ds = (pn * (dp - delta)).astype(BF16)
                dsink = -jnp.sum(psink * delta, axis=0, keepdims=True)
                ds_ref[h:h + 1, :] += jnp.broadcast_to(dsink, (1, LANE))
                dq_ref[:, hs] = (_dot(ds, kw) * (HEAD_DIM ** -0.5)).astype(BF16)
                dk_sum = dk_sum + _dot_tn(ds, q_ref[:, hs])
                dv_sum = dv_sum + _dot_tn(pn.astype(BF16), dob)
            dks.append(dk_sum * (HEAD_DIM ** -0.5))
            dvs.append(dv_sum)
        dk_acc[pl.ds(start, NK), :] += jnp.concatenate(dks, axis=1)
        dv_acc[pl.ds(start, NK), :] += jnp.concatenate(dvs, axis=1)

        @pl.when(i == nb - 1)
        def _():
            pltpu.sync_copy(dk_acc, dk_hbm)
            pltpu.sync_copy(dv_acc, dv_hbm)

    row = pl.BlockSpec((TQ, 1024), lambda i: (i, 0))
    return pl.pallas_call(
        body, name="attn_bwd", grid=(nb,),
        in_specs=[row, row, pl.BlockSpec((TQ, 1024), lambda i: (i, CB_Q)), pl.BlockSpec((TQ, 1024), lambda i: (i, CB_AZ)),
                  *_kv_specs(), _full((N_HEADS, LANE))],
        out_specs=[row, row, ANY, ANY, _full((N_HEADS, LANE))],
        out_shape=[jax.ShapeDtypeStruct((S, 1024), BF16), jax.ShapeDtypeStruct((S, 1024), BF16),
                   jax.ShapeDtypeStruct((S + ATTN_BACK, 256), F32), jax.ShapeDtypeStruct((S + ATTN_BACK, 256), F32),
                   jax.ShapeDtypeStruct((N_HEADS, LANE), F32)],
        scratch_shapes=[pltpu.VMEM((S + ATTN_BACK, 256), F32)] * 2 + [pltpu.VMEM((NK, 256), BF16)] * 2,
        compiler_params=_cp(("arbitrary",), VMEM_BIG),
    )(dy, o, proj, proj, proj, proj, proj, proj, sink_b)


def _conv_bwd_a(dy, proj, dw, dw_b, ln_g, ln_b, pw, tb=256):
    S = proj.shape[0]
    H = CONV_HALO
    base = H - (CONV_K - 1)

    def body(dy_ref, a_ref, ah_ref, b_ref, bh_ref, z_ref, dw_ref, dwb_ref, lg_ref, lb_ref, pw_ref,
             dcv_ref, dz_ref, dpw_ref, dlg_ref, dlb_ref, ddwb_ref, ddw_ref, gbuf, ybuf):
        i = pl.program_id(0)
        _conv_glu_dw(a_ref, ah_ref, b_ref, bh_ref, dw_ref, gbuf, ybuf, i, tb)
        lg = lg_ref[...]
        xh, rstd, yn = _layer_norm_fwd(ybuf[...] + dwb_ref[...], lg, lb_ref[...])
        u = _silu(yn).astype(BF16)
        out = _dot(u, pw_ref[...])
        dyv, z = dy_ref[...].astype(F32), z_ref[...].astype(F32)
        dz_ref[...] = (dyv * out * _dsilu(z)).astype(BF16)
        dout = (dyv * _silu(z)).astype(BF16)
        dpw = _dot_tn(u, dout)
        dyn = _dot_nt(dout, pw_ref[...]) * _dsilu(yn)
        dlg = jnp.sum(dyn * xh, axis=0, keepdims=True)
        dlb = jnp.sum(dyn, axis=0, keepdims=True)
        dxh = dyn * lg
        dcv = rstd * (dxh - jnp.mean(dxh, axis=-1, keepdims=True) - xh * jnp.mean(dxh * xh, axis=-1, keepdims=True))
        dcv_ref[...] = dcv
        ddwb = jnp.sum(dcv, axis=0, keepdims=True)
        ybuf[...] = dcv

        @pl.when(i == 0)
        def _():
            dpw_ref[...], dlg_ref[...], dlb_ref[...], ddwb_ref[...] = dpw, dlg, dlb, ddwb
            ddw_ref[...] = jnp.zeros_like(ddw_ref)

        @pl.when(i > 0)
        def _():
            dpw_ref[...] += dpw
            dlg_ref[...] += dlg
            dlb_ref[...] += dlb
            ddwb_ref[...] += ddwb

        for c in range(1024 // LANE):
            cs = slice(c * LANE, (c + 1) * LANE)
            for j in range(CONV_K):
                acc = jnp.zeros((1, LANE), F32)
                for r in range(tb // 128):
                    acc = acc + jnp.sum(ybuf[r * 128:(r + 1) * 128, cs] * _rows_at(gbuf, r * 128, base + j, cs),
                                        axis=0, keepdims=True)
                ddw_ref[j:j + 1, cs] += acc

    vec = _full((1, 1024))
    row = pl.BlockSpec((tb, 1024), lambda i: (i, 0))
    big = _full((1024, 1024))
    return pl.pallas_call(
        body, name="conv_bwd_a", grid=(S // tb,),
        in_specs=[row, *_conv_specs(tb), _full((32, 1024)), vec, vec, vec, big],
        out_specs=[row, row, big, vec, vec, vec, _full((32, 1024))],
        out_shape=[jax.ShapeDtypeStruct((S, 1024), F32), jax.ShapeDtypeStruct((S, 1024), BF16),
                   jax.ShapeDtypeStruct((1024, 1024), F32), jax.ShapeDtypeStruct((1, 1024), F32),
                   jax.ShapeDtypeStruct((1, 1024), F32), jax.ShapeDtypeStruct((1, 1024), F32),
                   jax.ShapeDtypeStruct((32, 1024), F32)],
        scratch_shapes=[pltpu.VMEM((SUBLANES, tb + H, 1024), F32), pltpu.VMEM((tb, 1024), F32)],
        compiler_params=_cp(("arbitrary",)),
    )(dy, proj, proj, proj, proj, proj, dw, dw_b, ln_g, ln_b, pw)


def _conv_bwd_b(dcv, proj, dw, tb=256):
    S = proj.shape[0]
    H = CONV_HALO
    nb = S // tb

    def body(d_ref, dn_ref, a_ref, b_ref, dw_ref, da_ref, db_ref, dbuf, gbuf):
        i = pl.program_id(0)
        dbuf[0, :tb, :] = d_ref[...]
        dbuf[0, tb:, :] = jnp.where(i < nb - 1, dn_ref[...], 0.0)
        _shift_copies(dbuf, tb)
        for c in range(1024 // LANE):
            cs = slice(c * LANE, (c + 1) * LANE)
            for r in range(tb // 128):
                acc = jnp.zeros((128, LANE), F32)
                for j in range(CONV_K):
                    acc = acc + dw_ref[j:j + 1, cs] * _rows_at(dbuf, r * 128, (CONV_K - 1) - j, cs)
                gbuf[r * 128:(r + 1) * 128, cs] = acc
        dg = gbuf[...]
        a, s = a_ref[...].astype(F32), _sig(b_ref[...].astype(F32))
        da_ref[...] = (dg * s).astype(BF16)
        db_ref[...] = (dg * a * s * (1.0 - s)).astype(BF16)

    row = pl.BlockSpec((tb, 1024), lambda i: (i, 0))
    nxt = pl.BlockSpec((H, 1024), lambda i: (jnp.minimum((i + 1) * (tb // H), S // H - 1), 0))
    return pl.pallas_call(
        body, name="conv_bwd_b", grid=(nb,),
        in_specs=[row, nxt, pl.BlockSpec((tb, 1024), lambda i: (i, CB_CA)), pl.BlockSpec((tb, 1024), lambda i: (i, CB_CB)),
                  _full((32, 1024))],
        out_specs=[row, row], out_shape=[jax.ShapeDtypeStruct((S, 1024), BF16)] * 2,
        scratch_shapes=[pltpu.VMEM((SUBLANES, tb + H, 1024), F32), pltpu.VMEM((tb, 1024), F32)],
        compiler_params=_cp(("parallel",)),
    )(dcv, dcv, proj, proj, dw)


def _norm_bwd(dh, x, dx_out, g, scale, tb=256):
    S = x.shape[0]

    def body(dh_ref, x_ref, dxo_ref, g_ref, sc_ref, dx_ref, dsh_ref, da_ref):
        i = pl.program_id(0)
        xv, dhv = x_ref[...], dh_ref[...]
        r = lax.rsqrt(jnp.mean(xv * xv, axis=-1, keepdims=True) + EPS)
        xh = xv * r
        gy = dhv * (g_ref[...] * (1.0 + sc_ref[...]))
        dx_ref[...] = dxo_ref[...] + r * (gy - xh * jnp.mean(gy * xh, axis=-1, keepdims=True))
        dsh = jnp.sum(dhv, axis=0, keepdims=True)
        da = jnp.sum(dhv * xh, axis=0, keepdims=True)

        @pl.when(i == 0)
        def _():
            dsh_ref[...], da_ref[...] = dsh, da

        @pl.when(i > 0)
        def _():
            dsh_ref[...] += dsh
            da_ref[...] += da

    row = pl.BlockSpec((tb, D), lambda i: (i, 0))
    vec = _full((1, D))
    return pl.pallas_call(
        body, name="norm_bwd", grid=(S // tb,), in_specs=[row, row, row, vec, vec], out_specs=[row, vec, vec],
        out_shape=[jax.ShapeDtypeStruct((S, D), F32), jax.ShapeDtypeStruct((1, D), F32), jax.ShapeDtypeStruct((1, D), F32)],
        compiler_params=_cp(("arbitrary",)),
    )(dh, x, dx_out, g, scale)


def _mod_bwd(d_a, norm_g, scale):
    def body(da_ref, g_ref, sc_ref, dg_ref, dsc_ref):
        dg_ref[...] = da_ref[...] * (1.0 + sc_ref[...])
        dsc_ref[...] = da_ref[...] * g_ref[...]

    return pl.pallas_call(body, name="mod_bwd", out_shape=[jax.ShapeDtypeStruct(d_a.shape, F32)] * 2)(d_a, norm_g, scale)


def _reduce_adamw(parts, w, m, v, name):
    L, rows, C = w.shape
    tr = rows if rows % 64 else 64

    def body(*refs):
        p_refs, (w_ref, m_ref, v_ref, g_ref, dl_ref, m2_ref, v2_ref) = refs[:L * N_DEV], refs[L * N_DEV:]
        for l in range(L):
            g = p_refs[l * N_DEV][0].astype(F32)
            for k in range(1, N_DEV):
                g = g + p_refs[l * N_DEV + k][0].astype(F32)
            g_ref[l] = g
            dl_ref[l], m2_ref[l], v2_ref[l] = _adam_math(g, w_ref[l], m_ref[l], v_ref[l])

    slot = lambda k: pl.BlockSpec((1, tr, C), lambda i: (k, i, 0))
    blk = pl.BlockSpec((L, tr, C), lambda i: (0, i, 0))
    return pl.pallas_call(
        body, name=name, grid=(rows // tr,), in_specs=[slot(k) for _ in range(L) for k in range(N_DEV)] + [blk] * 3,
        out_specs=[blk] * 4, out_shape=[jax.ShapeDtypeStruct((L, rows, C), F32)] * 4,
        compiler_params=_cp(("parallel",)),
    )(*[p for p in parts for _ in range(N_DEV)], w, m, v)


def _small_final(parts, w, m, v):
    R = w.shape[0]

    def body(p_ref, w_ref, m_ref, v_ref, g_ref, dl_ref, m2_ref, v2_ref):
        g = p_ref[0]
        for k in range(1, N_DEV):
            g = g + p_ref[k]
        delta, m2, v2 = _adam_math(g, w_ref[...], m_ref[...], v_ref[...])
        g_ref[...], dl_ref[...], m2_ref[...], v2_ref[...] = g, delta, m2, v2

    return pl.pallas_call(body, name="small_final", out_shape=[jax.ShapeDtypeStruct((R, LANE), F32)] * 4)(parts, w, m, v)


def _layer_fwd(x, mod, small, W, more_w):
    shift, scale, gate = mod
    h = _norm_mod(x, small["norm_g"], scale, shift)
    proj = _mm(h, W["w_in_t"], BF16, 512, 1536, name="proj_mm", nt=True)
    W.update(more_w(proj))
    y_pool = _pool_fwd(proj, W["pool_w"], small["pool_scale"])
    o, y_attn = _attn_fwd(proj, small["sink_b"])
    y_conv = _conv_fwd(proj, W["conv_dw"], small["conv_dw_b"], small["conv_ln_g"], small["conv_ln_b"], W["conv_pw"])
    merged, pp, pa, pc = _merge_fwd(y_pool, y_attn, y_conv, W["wbp"], W["wba"], W["wbc"], proj)
    x_new = _out_fwd(x, merged, W["w_out"], gate)
    stash = dict(x=x, h=h, proj=proj, o=o, y_pool=y_pool, y_attn=y_attn, y_conv=y_conv,
                 merged=merged, pp=pp, pa=pa, pc=pc)
    return x_new, stash


def _layer_bwd(dx, st, mod, small, W, put):
    shift, scale, gate = mod
    proj = st["proj"]
    gmat = _mm_tn(st["merged"], dx, 1024, 1024, 1024, name="wout_tn")
    d_w_out, d_gate = _wout_post(gmat, W["w_out"], gate)
    dpp, dpa, dpc, dgp, dga, dgc = _out_bwd(dx, gate, W["w_out"], st["pp"], st["pa"], st["pc"], proj)
    dy_pool = _mm(dpp, W["wbp"], BF16, 512, 1024, name="branch_bwd_mm", nt=True)
    dy_attn = _mm(dpa, W["wba"], BF16, 512, 1024, name="branch_bwd_mm", nt=True)
    dy_conv = _mm(dpc, W["wbc"], BF16, 512, 1024, name="branch_bwd_mm", nt=True)
    d_wbp = _mm_tn(st["y_pool"], dpp, 1024, 1024, 1024, name="branch_tn")
    d_wba = _mm_tn(st["y_attn"], dpa, 1024, 1024, 1024, name="branch_tn")
    d_wbc = _mm_tn(st["y_conv"], dpc, 1024, 1024, 1024, name="branch_tn")

    dmix, dz_pool, d_pool_scale, d_pool_w = _pool_bwd_a(dy_pool, proj, W["pool_w"], small["pool_scale"])
    du = _pool_bwd_b(dmix)
    dq, dz_attn, dk, dv, d_sink = _attn_bwd(dy_attn, st["o"], proj, small["sink_b"])
    dcv, dz_conv, d_pw, d_ln_g, d_ln_b, d_dw_b, d_dw = _conv_bwd_a(
        dy_conv, proj, W["conv_dw"], small["conv_dw_b"], small["conv_ln_g"], small["conv_ln_b"], W["conv_pw"])
    da, db = _conv_bwd_b(dcv, proj, W["conv_dw"])

    tok = put(dict(pool_w=d_pool_w, conv_dw=d_dw[:CONV_K], conv_pw=d_pw, wbp=d_wbp, wba=d_wba, wbc=d_wbc, w_out=d_w_out))
    dproj = jnp.concatenate([du, dz_pool, dq, dz_attn, da, db, dz_conv, dgp, dga, dgc,
                             dk[ATTN_BACK:].astype(BF16), dv[ATTN_BACK:].astype(BF16)], axis=1)
    d_w_in_t = _mm_tn(dproj, st["h"], 768, 2048, 1024, name="win_tn", vmem=VMEM_BIG, after=tok)
    tok = put(dict(w_in=d_w_in_t))
    dh = _mm(dproj, W["w_in_t"], F32, 1024, 2048, 1536, name="dh_mm", vmem=VMEM_BIG, after=tok)
    dx_in, d_shift, d_a = _norm_bwd(dh, st["x"], dx, small["norm_g"], scale)
    sm = dict(d_a=d_a, d_shift=d_shift, d_gate=d_gate, pool_scale=d_pool_scale, attn_sink=d_sink[:, 0],
              conv_dw_b=d_dw_b, conv_ln_g=d_ln_g, conv_ln_b=d_ln_b)
    return dx_in, sm


def _local_step(x, target, mods, smalls, get_w, final_g, put_g, end_layer):
    stashes, Ws = [], []
    for l in range(DEPTH):
        w, more_w = get_w(l, x)
        Ws.append(w)
        x, st = _layer_fwd(x, mods[l], smalls[l], w, more_w)
        stashes.append(st)
    dx, d_final_g, loss_lanes = _final_loss(x, target, final_g)
    sms = [None] * DEPTH
    for l in reversed(range(DEPTH)):
        dx, sms[l] = _layer_bwd(dx, stashes[l], mods[l], smalls[l], Ws[l], functools.partial(put_g, l))
        end_layer(l, dx)
    return loss_lanes, dx, d_final_g, sms


BIG = ("w_in", "pool_w", "conv_pw", "wbp", "wba", "wbc", "w_out")
GRADS = BIG + ("conv_dw",)


def _to_internal(w):
    return jnp.concatenate([w[:3072], w[3584:], w[3072:3584]], axis=0)


def _from_internal(w):
    return jnp.concatenate([w[:3072], w[COL_K:], w[3072:COL_K]], axis=0)


def _full_w_in(g):
    return dict(w_in_t=_to_internal(g.reshape(IN_WIDTH, D)))


def _full_weights(g, conv_dw):
    cols = lambda a: jnp.transpose(a, (1, 0, 2)).reshape(a.shape[1], -1)
    pool_w = jnp.transpose(g["pool_w"], (1, 0, 2, 3)).reshape(4, 256, 256)
    conv_pw = g["conv_pw"].reshape(1024, 1024)
    wbp, wba, wbc = cols(g["wbp"]), cols(g["wba"]), cols(g["wbc"])
    w_out = g["w_out"].reshape(D, D)
    conv_dw = jnp.pad(cols(conv_dw), ((0, 32 - CONV_K), (0, 0)))
    return dict(pool_w=pool_w, conv_pw=conv_pw, wbp=wbp, wba=wba, wbc=wbc, w_out=w_out, conv_dw=conv_dw)


def _pieces(name, g):
    if name == "w_in":
        return _from_internal(g).reshape(8, IN_WIDTH // 8, D)
    if name == "pool_w":
        return jnp.transpose(g.reshape(4, 8, 32, 256), (1, 0, 2, 3))
    if name == "conv_dw":
        return jnp.transpose(g.reshape(CONV_K, 8, 128), (1, 0, 2))
    if name == "conv_pw":
        return g.reshape(8, 128, 1024)
    if name in ("wbp", "wba", "wbc"):
        return jnp.transpose(g.reshape(1024, 8, 256), (1, 0, 2))
    return g.reshape(8, 256, D)


def _pack_small(items, rows):
    flat = jnp.concatenate([a.reshape(-1).astype(F32) for a in items])
    return jnp.pad(flat, (0, rows * LANE - flat.shape[0])).reshape(rows, LANE)


def _unpack_small(packed, shapes):
    flat, out, off = packed.reshape(-1), [], 0
    for s in shapes:
        n = 1
        for d in s:
            n *= d
        out.append(flat[off:off + n].reshape(s))
        off += n
    return out


def kernel(x, c, norm_g, w_ada, b_ada, w_in, pool_w, pool_scale, attn_sink, conv_dw, conv_dw_b, conv_ln_g, conv_ln_b, conv_pw, w_branch_pool, w_branch_attn, w_branch_conv, w_out, final_g, loss_target, m_norm_g, m_w_ada, m_b_ada, m_w_in, m_pool_w, m_pool_scale, m_attn_sink, m_conv_dw, m_conv_dw_b, m_conv_ln_g, m_conv_ln_b, m_conv_pw, m_w_branch_pool, m_w_branch_attn, m_w_branch_conv, m_w_out, m_final_g, v_norm_g, v_w_ada, v_b_ada, v_w_in, v_pool_w, v_pool_scale, v_attn_sink, v_conv_dw, v_conv_dw_b, v_conv_ln_g, v_conv_ln_b, v_conv_pw, v_w_branch_pool, v_w_branch_attn, v_w_branch_conv, v_w_out, v_final_g):
    L = DEPTH
    me = 4 * lax.axis_index("x") + 2 * lax.axis_index("y") + lax.axis_index("c")
    tr = lambda a: jnp.swapaxes(a, 1, 2)
    shards = dict(w_in=tr(w_in), pool_w=pool_w, conv_dw=conv_dw, conv_pw=conv_pw, wbp=w_branch_pool, wba=w_branch_attn,
                  wbc=w_branch_conv, w_out=w_out)
    moms = dict(w_in=(tr(m_w_in), tr(v_w_in)), pool_w=(m_pool_w, v_pool_w), conv_dw=(m_conv_dw, v_conv_dw),
                conv_pw=(m_conv_pw, v_conv_pw), wbp=(m_w_branch_pool, v_w_branch_pool),
                wba=(m_w_branch_attn, v_w_branch_attn), wbc=(m_w_branch_conv, v_w_branch_conv), w_out=(m_w_out, v_w_out))

    n_cd = L * CONV_K * 128
    first = _all_gather([_pack_small([c, conv_dw], 144)], "gather_c")[0].reshape(N_DEV, -1)
    c_all = first[:, :D]
    conv_dw_all = first[:, D:D + n_cd].reshape(N_DEV, L, CONV_K, 128)

    mod_part = _mod_fwd(c_all, w_ada)
    mod_all = _all_gather([mod_part.reshape(-1, LANE)], "gather_mod")[0].reshape(N_DEV, L, N_DEV, -1)
    mod = jnp.transpose(lax.dynamic_index_in_dim(mod_all, me, axis=2, keepdims=False), (1, 0, 2)).reshape(L, 3 * D)
    mod = mod + b_ada
    mods = [(mod[l:l + 1, :D], mod[l:l + 1, D:2 * D], mod[l:l + 1, 2 * D:]) for l in range(L)]

    gathers, tok = [], mod[0, 0] * 0.0
    for l in range(L):
        mine = [(shards["w_in"][l] + tok).astype(BF16)], [shards[k][l].astype(BF16) for k in BIG[1:]]
        started = [_spread_start(v, False, f"gather_start_{l}_{n}") for n, v in enumerate(mine)]
        gathers.append((mine, started))
        tok = started[0][-1][0, 0] + started[1][-1][0, 0]
    mods[0] = (mods[0][0] + tok,) + mods[0][1:]

    sink_b = jnp.broadcast_to(attn_sink[:, :, None], (L, N_HEADS, LANE))
    smalls = [dict(norm_g=norm_g[l:l + 1], pool_scale=pool_scale[l:l + 1], sink_b=sink_b[l], conv_dw_b=conv_dw_b[l:l + 1],
                   conv_ln_g=conv_ln_g[l:l + 1], conv_ln_b=conv_ln_b[l:l + 1]) for l in range(L)]

    def with_mine(landed, mine):
        return lax.dynamic_update_slice(landed, mine, (me,) + (0,) * (landed.ndim - 1))

    def get_w(l, x_in):
        mine, started = gathers[l]
        w_in = with_mine(_spread_wait(started[0], x_in, False, f"gather_wait_{l}_0")[0], mine[0][0][None])

        def more_w(proj):
            landed = _spread_wait(started[1], proj, False, f"gather_wait_{l}_1")
            g = {k: with_mine(a, b[None]) for k, a, b in zip(BIG[1:], landed, mine[1])}
            return _full_weights(g, conv_dw_all[:, l])

        return _full_w_in(w_in), more_w

    pending, parts = {l: [] for l in range(L)}, [dict() for _ in range(L)]

    def put_g(l, grads):
        names = tuple(k for k in GRADS if k in grads)
        pieces = [_pieces(k, grads[k]).astype(BF16) for k in names]
        started = _spread_start(pieces, True, f"scatter_start_{l}_{len(pending[l])}")
        pending[l].append((names, pieces, started))
        return started[-1][0:1, 0:1]

    def finish(l, after):
        for n, (names, pieces, started) in enumerate(pending.pop(l)):
            landed = _spread_wait(started, after, True, f"scatter_wait_{l}_{n}")
            for k, a, b in zip(names, landed, pieces):
                parts[l][k] = with_mine(a, lax.dynamic_slice_in_dim(b, me, 1, axis=0))

    def end_layer(l, dx):
        if l + 1 in pending:
            finish(l + 1, dx)

    loss_lanes, grad_x, d_final_g, sms = _local_step(x[0], loss_target[0], mods, smalls, get_w, final_g.reshape(1, D),
                                                     put_g, end_layer)
    finish(0, grad_x)

    stack = lambda k: jnp.concatenate([sms[l][k].reshape(1, -1) for l in range(L)], axis=0)
    scale_all = jnp.concatenate([mods[l][1] for l in range(L)], axis=0)
    d_norm_g, d_scale = _mod_bwd(stack("d_a"), norm_g, scale_all)
    dmod = jnp.concatenate([stack("d_shift"), d_scale, stack("d_gate")], axis=1)
    small_names = ("norm_g", "b_ada", "pool_scale", "attn_sink", "conv_dw_b", "conv_ln_g", "conv_ln_b", "final_g")
    small_g = (d_norm_g, dmod, stack("pool_scale"), stack("attn_sink"), stack("conv_dw_b"), stack("conv_ln_g"),
               stack("conv_ln_b"), d_final_g.reshape(D))
    small_w = (norm_g, b_ada, pool_scale, attn_sink, conv_dw_b, conv_ln_g, conv_ln_b, final_g)
    small_m = (m_norm_g, m_b_ada, m_pool_scale, m_attn_sink, m_conv_dw_b, m_conv_ln_g, m_conv_ln_b, m_final_g)
    small_v = (v_norm_g, v_b_ada, v_pool_scale, v_attn_sink, v_conv_dw_b, v_conv_ln_g, v_conv_ln_b, v_final_g)
    shapes = [a.shape for a in small_w] + [(D,)]
    n_small = sum(a.size for a in small_w) + D
    R = -(-n_small // (8 * LANE)) * 8
    zero = jnp.zeros((D,), F32)
    small_parts = _all_gather([_pack_small(small_g + (loss_lanes,), R)], "gather_small")[0]
    sg, sd, sm2, sv2 = _small_final(small_parts, _pack_small(small_w + (zero,), R), _pack_small(small_m + (zero,), R),
                                    _pack_small(small_v + (zero + 1.0,), R))
    sg, sd, sm2, sv2 = (_unpack_small(a, shapes) for a in (sg, sd, sm2, sv2))
    loss = jnp.sum(sg[-1])
    res = {n: (sg[i], sd[i], sm2[i], sv2[i]) for i, n in enumerate(small_names)}

    off = norm_g.size
    dmod_all = small_parts.reshape(N_DEV, -1)[:, off:off + L * 3 * D].reshape(N_DEV, L, 3 * D)
    dmod_mine = jnp.transpose(lax.dynamic_slice_in_dim(dmod_all, me * (3 * D // N_DEV), 3 * D // N_DEV, axis=2), (1, 0, 2))
    res["w_ada"] = _wada_bwd(c_all.T, dmod_mine, w_ada, m_w_ada, v_w_ada)

    for k in GRADS:
        shp = shards[k].shape
        to3d = lambda a: a.reshape(L, -1, shp[-1])
        out = _reduce_adamw([parts[l][k].reshape(N_DEV, -1, shp[-1]) for l in range(L)], to3d(shards[k]),
                            to3d(moms[k][0]), to3d(moms[k][1]), "adamw_" + k)
        res[k] = tuple(a.reshape(shp) for a in out)
    res["w_in"] = tuple(tr(a) for a in res["w_in"])

    order = ("norm_g", "w_ada", "b_ada", "w_in", "pool_w", "pool_scale", "attn_sink", "conv_dw", "conv_dw_b", "conv_ln_g",
             "conv_ln_b", "conv_pw", "wbp", "wba", "wbc", "w_out", "final_g")
    outs = [loss, grad_x[None]]
    for j in range(4):
        outs += [res[n][j] for n in order]
    return tuple(outs)
```

```python
import functools

import jax
import jax.numpy as jnp
from jax import lax
from jax.experimental import pallas as pl
from jax.experimental.pallas import tpu as pltpu

F32, BF16 = jnp.float32, jnp.bfloat16
MESH = pl.DeviceIdType.MESH
ANY = pl.BlockSpec(memory_space=pl.ANY)

N_DEV = 8
D = 2048
DEPTH = 4
EPS = 1e-6
IN_WIDTH = 13824
POOL_WINDOWS = (2, 4, 8, 16)
POOL_GROUP = 256
POOL_HALO = 16
CONV_K = 31
CONV_HALO = 32
N_HEADS, N_KV, HEAD_DIM = 16, 4, 64
ATTN_TQ = 256
ATTN_BACK = 128
LANE = 128
VMEM_BIG = 56 * 1024 * 1024

OFF_U, OFF_Z, OFF_Q, OFF_K, OFF_V, OFF_AZ, OFF_CA, OFF_CB, OFF_CZ = 0, 1024, 2048, 3072, 3328, 3584, 4608, 5632, 6656
OFF_GP, OFF_GA, OFF_GC = 7680, 9728, 11776


def _seg(rows, width, off, first_row=None):
    start = (lambda i: i * rows) if first_row is None else first_row
    return pl.BlockSpec((pl.Element(rows), pl.Element(width)), lambda i: (pl.multiple_of(start(i), rows), off))


def _seg2(tm, tn, off):
    return pl.BlockSpec((pl.Element(tm), pl.Element(tn)), lambda j, i: (i * tm, pl.multiple_of(off + j * tn, LANE)))

ADAM_LR, ADAM_B1, ADAM_B2, ADAM_EPS, ADAM_WD, ADAM_STEP = 0.001, 0.9, 0.999, 1e-08, 0.01, 10


def _cp(sem=None, vmem=None):
    return pltpu.CompilerParams(dimension_semantics=sem, vmem_limit_bytes=vmem)


def _sig(x):
    return jax.nn.sigmoid(x)


def _silu(x):
    return x * _sig(x)


def _dsilu(x):
    s = _sig(x)
    return s * (1.0 + x * (1.0 - s))


def _dot(a, b):
    return jnp.dot(a, b, preferred_element_type=F32)


def _dot_tn(a, b):
    return lax.dot_general(a, b, (((0,), (0,)), ((), ())), preferred_element_type=F32)


def _dot_nt(a, b):
    return lax.dot_general(a, b, (((1,), (1,)), ((), ())), preferred_element_type=F32)


def _full(shape):
    n = len(shape)
    return pl.BlockSpec(shape, lambda *_: (0,) * n)


def _my_pos():
    return lax.axis_index("x"), lax.axis_index("y"), lax.axis_index("c")


def _all_gather(xs, name):
    n = len(xs)

    def body(*refs):
        x_refs, o_refs = refs[:n], refs[n:2 * n]
        send_sems, recv_sems, local_sems = refs[2 * n:]
        x, y, c = _my_pos()
        sibling = (x, y, 1 - c)
        chips = [(1 - x, y), (x, 1 - y), (1 - x, 1 - y)]
        me = 4 * x + 2 * y + c

        def slot(px, py, pc):
            return 4 * px + 2 * py + pc

        def copy(t, k, block, to, src=None):
            dst = o_refs[t].at[block]
            return pltpu.make_async_remote_copy(
                src_ref=dst if src is None else src, dst_ref=dst,
                send_sem=send_sems.at[t, k], recv_sem=recv_sems.at[t, k],
                device_id=to, device_id_type=MESH)

        mine = [pltpu.make_async_copy(x_refs[t], o_refs[t].at[me], local_sems.at[t]) for t in range(n)]
        for cp in mine:
            cp.start()
        first = []
        for t in range(n):
            first.append(copy(t, 0, me, sibling, src=x_refs[t]))
            for j, chip in enumerate(chips):
                first.append(copy(t, 1 + j, me, (*chip, c), src=x_refs[t]))
        for cp in first:
            cp.start()
        passed = []
        for j, chip in enumerate(chips):
            for t in range(n):
                copy(t, 1 + j, slot(*chip, c), (x, y, c)).wait_recv()
                fwd = copy(t, 4 + j, slot(*chip, c), sibling)
                fwd.start()
                passed.append(fwd)
        for t in range(n):
            copy(t, 0, slot(x, y, 1 - c), (x, y, c)).wait_recv()
            for j, chip in enumerate(chips):
                copy(t, 4 + j, slot(*chip, 1 - c), (x, y, c)).wait_recv()
        for cp in first + passed:
            cp.wait_send()
        for cp in mine:
            cp.wait()

    return pl.pallas_call(
        body, name=name,
        out_shape=[jax.ShapeDtypeStruct((N_DEV,) + a.shape, a.dtype) for a in xs],
        in_specs=[ANY] * n, out_specs=[ANY] * n,
        scratch_shapes=[pltpu.SemaphoreType.DMA((n, 7)), pltpu.SemaphoreType.DMA((n, 7)),
                        pltpu.SemaphoreType.DMA((n,))],
    )(*xs)


N_PEER = N_DEV - 1
HBM = pl.BlockSpec(memory_space=pltpu.HBM)
SEM = pl.BlockSpec(memory_space=pltpu.SEMAPHORE)
EFFECT = pltpu.SideEffectType.DATAFLOW_SIDE_EFFECTING


def _peer(k):
    x, y, c = _my_pos()
    flip = lambda v, bit: 1 - v if bit else v
    return flip(x, (k >> 2) & 1), flip(y, (k >> 1) & 1), flip(c, k & 1)


def _spread_copies(v_ref, land_ref, send_sems, recv_sems, per_peer):
    x, y, c = _my_pos()
    me = 4 * x + 2 * y + c
    copies = []
    for k in range(1, N_DEV):
        px, py, pc = _peer(k)
        src = v_ref.at[4 * px + 2 * py + pc] if per_peer else v_ref
        copies.append(pltpu.make_async_remote_copy(
            src_ref=src, dst_ref=land_ref.at[me], send_sem=send_sems[k - 1], recv_sem=recv_sems[k - 1],
            device_id=(px, py, pc), device_id_type=MESH))
    return copies


def _spread_start(vs, per_peer, name):
    n = len(vs)
    lands = [(N_DEV,) + (v.shape[1:] if per_peer else v.shape) for v in vs]
    n_sem = 2 * N_PEER * n

    def body(*refs):
        v_refs, land_refs, outs = refs[:n], refs[n:2 * n], refs[2 * n:]
        for t in range(n):
            sems = outs[2 * N_PEER * t:2 * N_PEER * (t + 1)]
            for cp in _spread_copies(v_refs[t], land_refs[t], sems[:N_PEER], sems[N_PEER:], per_peer):
                cp.start()
        token = outs[n_sem + 2 * n]
        token[...] = jnp.zeros_like(token)

    hbm = lambda a: pltpu.with_memory_space_constraint(a, pltpu.HBM)
    return pl.pallas_call(
        body, name=name,
        out_shape=((pltpu.SemaphoreType.DMA(()),) * n_sem + tuple(pltpu.HBM(v.shape, v.dtype) for v in vs)
                   + tuple(pltpu.HBM(s, v.dtype) for s, v in zip(lands, vs)) + (jax.ShapeDtypeStruct((8, LANE), F32),)),
        in_specs=(HBM,) * (2 * n), out_specs=(SEM,) * n_sem + (HBM,) * (2 * n) + (pl.BlockSpec(memory_space=pltpu.VMEM),),
        input_output_aliases={t: n_sem + t for t in range(2 * n)},
        compiler_params=pltpu.CompilerParams(has_side_effects=EFFECT),
    )(*[hbm(v) for v in vs], *[hbm(lax.empty(s, v.dtype)) for s, v in zip(lands, vs)])


def _spread_wait(started, after, per_peer, name):
    n = (len(started) - 1) // (2 * N_PEER + 2)
    n_sem = 2 * N_PEER * n
    sems, thru = started[:n_sem], started[n_sem:n_sem + 2 * n]

    def body(*refs):
        v_refs, land_refs, rest = refs[:n], refs[n:2 * n], refs[2 * n:]
        for t in range(n):
            s = rest[2 * N_PEER * t:2 * N_PEER * (t + 1)]
            for cp in _spread_copies(v_refs[t], land_refs[t], s[:N_PEER], s[N_PEER:], per_peer):
                cp.wait_send()
                cp.wait_recv()

    return pl.pallas_call(
        body, name=name,
        out_shape=tuple(pltpu.HBM(a.shape, a.dtype) for a in thru),
        in_specs=(HBM,) * (2 * n) + (SEM,) * n_sem + (ANY,), out_specs=(HBM,) * (2 * n),
        input_output_aliases={t: t for t in range(2 * n)},
        compiler_params=pltpu.CompilerParams(has_side_effects=EFFECT),
    )(*thru, *sems, after)[n:]


def _mm(a, b, out_dtype, tm, tn, tk=None, name="mm", vmem=None, after=None, nt=False):
    M, K = a.shape
    N = b.shape[0] if nt else b.shape[1]
    tk = K if tk is None else tk
    nk = K // tk
    assert M % tm == 0 and N % tn == 0 and K % tk == 0
    dep = () if after is None else (after,)

    def body(*refs):
        a_ref, b_ref, o_ref, *acc = refs[len(dep):]
        prod = (_dot_nt if nt else _dot)(a_ref[...].astype(BF16), b_ref[...])
        if nk == 1:
            o_ref[...] = prod.astype(out_dtype)
            return
        acc_ref = acc[0] if acc else o_ref
        k = pl.program_id(2)

        @pl.when(k == 0)
        def _():
            acc_ref[...] = prod

        @pl.when(k > 0)
        def _():
            acc_ref[...] += prod

        if acc:
            @pl.when(k == nk - 1)
            def _():
                o_ref[...] = acc_ref[...].astype(out_dtype)

    scratch = [pltpu.VMEM((tm, tn), F32)] if (nk > 1 and out_dtype != F32) else []
    b_spec = pl.BlockSpec((tn, tk), lambda j, i, k: (j, k)) if nt else pl.BlockSpec((tk, tn), lambda j, i, k: (k, j))
    return pl.pallas_call(
        body, name=name, grid=(N // tn, M // tm, nk),
        in_specs=[_full((1, 1))] * len(dep) + [pl.BlockSpec((tm, tk), lambda j, i, k: (i, k)), b_spec],
        out_specs=pl.BlockSpec((tm, tn), lambda j, i, k: (i, j)),
        out_shape=jax.ShapeDtypeStruct((M, N), out_dtype), scratch_shapes=scratch,
        compiler_params=_cp(("parallel", "parallel", "arbitrary"), vmem),
    )(*dep, a, b)


def _mm_tn(a, b, tm, tn, ts, name="mm_tn", vmem=None, after=None):
    S, Ka = a.shape
    _, N = b.shape
    assert Ka % tm == 0 and N % tn == 0 and S % ts == 0
    dep = () if after is None else (after,)

    def body(*refs):
        a_ref, b_ref, o_ref = refs[len(dep):]
        prod = _dot_tn(a_ref[...].astype(BF16), b_ref[...].astype(BF16))
        k = pl.program_id(2)

        @pl.when(k == 0)
        def _():
            o_ref[...] = prod

        @pl.when(k > 0)
        def _():
            o_ref[...] += prod

    return pl.pallas_call(
        body, name=name, grid=(Ka // tm, N // tn, S // ts),
        in_specs=[_full((1, 1))] * len(dep) + [pl.BlockSpec((ts, tm), lambda i, j, k: (k, i)),
                                               pl.BlockSpec((ts, tn), lambda i, j, k: (k, j))],
        out_specs=pl.BlockSpec((tm, tn), lambda i, j, k: (i, j)),
        out_shape=jax.ShapeDtypeStruct((Ka, N), F32),
        compiler_params=_cp(("parallel", "parallel", "arbitrary"), vmem),
    )(*dep, a, b)


def _mod_fwd(c_all, w_ada):
    L, _, n = w_ada.shape

    def body(c_ref, w_ref, o_ref):
        ca = _silu(c_ref[...])
        o_ref[0] = jnp.dot(ca, w_ref[0], preferred_element_type=F32, precision=lax.Precision.HIGHEST)

    return pl.pallas_call(
        body, name="mod_fwd", grid=(L,),
        in_specs=[_full((N_DEV, D)), pl.BlockSpec((1, D, n), lambda l: (l, 0, 0))],
        out_specs=pl.BlockSpec((1, N_DEV, n), lambda l: (l, 0, 0)),
        out_shape=jax.ShapeDtypeStruct((L, N_DEV, n), F32),
        compiler_params=_cp(("parallel",)),
    )(c_all, w_ada)


def _adam_math(g, w, m, v):
    m2 = ADAM_B1 * m + (1.0 - ADAM_B1) * g
    v2 = ADAM_B2 * v + (1.0 - ADAM_B2) * (g * g)
    m_hat = m2 / (1.0 - ADAM_B1 ** ADAM_STEP)
    v_hat = v2 / (1.0 - ADAM_B2 ** ADAM_STEP)
    delta = -ADAM_LR * (m_hat / (jnp.sqrt(v_hat) + ADAM_EPS) + ADAM_WD * w)
    return delta, m2, v2


def _wada_bwd(c_all_t, dmod, w, m, v, tr=256):
    L, _, n = w.shape

    def body(c_ref, d_ref, w_ref, m_ref, v_ref, g_ref, dl_ref, m2_ref, v2_ref):
        ca = _silu(c_ref[...])
        dm = d_ref[0]
        g = ca[:, 0:1] * dm[0:1, :]
        for b in range(1, N_DEV):
            g = g + ca[:, b:b + 1] * dm[b:b + 1, :]
        delta, m2, v2 = _adam_math(g, w_ref[0], m_ref[0], v_ref[0])
        g_ref[0], dl_ref[0], m2_ref[0], v2_ref[0] = g, delta, m2, v2

    blk = pl.BlockSpec((1, tr, n), lambda l, i: (l, i, 0))
    return pl.pallas_call(
        body, name="wada_bwd", grid=(L, D // tr),
        in_specs=[pl.BlockSpec((tr, N_DEV), lambda l, i: (i, 0)), pl.BlockSpec((1, N_DEV, n), lambda l, i: (l, 0, 0)),
                  blk, blk, blk],
        out_specs=[blk] * 4, out_shape=[jax.ShapeDtypeStruct(w.shape, F32)] * 4,
        compiler_params=_cp(("parallel", "parallel")),
    )(c_all_t, dmod, w, m, v)


def _norm_mod(x, g, scale, shift, tb=256):
    S = x.shape[0]

    def body(x_ref, g_ref, sc_ref, sh_ref, h_ref):
        xv = x_ref[...]
        r = lax.rsqrt(jnp.mean(xv * xv, axis=-1, keepdims=True) + EPS)
        h_ref[...] = (xv * r * (g_ref[...] * (1.0 + sc_ref[...])) + sh_ref[...]).astype(BF16)

    row = pl.BlockSpec((tb, D), lambda i: (i, 0))
    vec = _full((1, D))
    return pl.pallas_call(
        body, name="norm_mod", grid=(S // tb,), in_specs=[row, vec, vec, vec], out_specs=row,
        out_shape=jax.ShapeDtypeStruct((S, D), BF16), compiler_params=_cp(("parallel",)),
    )(x, g, scale, shift)


def _pool_mix(u_ref, uh_ref, ubuf, i, tb):
    H = POOL_HALO
    ubuf[H:, :] = u_ref[...].astype(F32)
    ubuf[:H, :] = jnp.where(i > 0, uh_ref[...].astype(F32), 0.0)
    t = i * tb + lax.broadcasted_iota(jnp.int32, (tb, 1), 0)
    mixed = []
    for g, w in enumerate(POOL_WINDOWS):
        cs = slice(g * POOL_GROUP, (g + 1) * POOL_GROUP)
        cur = ubuf[H:, cs]
        acc = cur
        for j in range(1, w):
            acc = acc + ubuf[pl.ds(H - j, tb), cs]
        cnt = jnp.minimum(t + 1, w).astype(F32)
        mixed.append(acc / cnt - cur)
    return mixed


def _pool_specs(tb):
    H = POOL_HALO
    u = _seg(tb, 1024, OFF_U)
    uh = _seg(H, 1024, OFF_U, lambda i: jnp.maximum(i * tb - H, 0))
    z = _seg(tb, 1024, OFF_Z)
    return u, uh, z


def _pool_fwd(proj, pool_w, pool_scale, tb=256):
    S = proj.shape[0]

    def body(u_ref, uh_ref, z_ref, w_ref, sc_ref, y_ref, ubuf):
        i = pl.program_id(0)
        mixed = _pool_mix(u_ref, uh_ref, ubuf, i, tb)
        m = jnp.concatenate([_dot(mixed[g].astype(BF16), w_ref[g]) for g in range(4)], axis=1)
        y_ref[...] = (m * sc_ref[...] * _silu(z_ref[...].astype(F32))).astype(BF16)

    u, uh, z = _pool_specs(tb)
    return pl.pallas_call(
        body, name="pool_fwd", grid=(S // tb,),
        in_specs=[u, uh, z, _full((4, 256, 256)), _full((1, 1024))],
        out_specs=pl.BlockSpec((tb, 1024), lambda i: (i, 0)),
        out_shape=jax.ShapeDtypeStruct((S, 1024), BF16),
        scratch_shapes=[pltpu.VMEM((tb + POOL_HALO, 1024), F32)],
        compiler_params=_cp(("parallel",)),
    )(proj, proj, proj, pool_w, pool_scale)


def _attn_mask(i):
    TQ, NK = ATTN_TQ, ATTN_TQ + ATTN_BACK
    qc = lax.broadcasted_iota(jnp.int32, (TQ, NK), 0) // 64
    col = lax.broadcasted_iota(jnp.int32, (TQ, NK), 1)
    kc = col // 64
    return (kc >= qc) & (kc <= qc + 2) & ((col >= ATTN_BACK) | (i > 0))


def _attn_probs(q_ref, kw, sink_ref, valid, h):
    qh = q_ref[:, h * HEAD_DIM:(h + 1) * HEAD_DIM]
    s = _dot_nt(qh, kw) * (HEAD_DIM ** -0.5)
    s = jnp.where(valid, s, -jnp.inf)
    sk = sink_ref[h:h + 1, 0:1]
    mx = jnp.maximum(jnp.max(s, axis=-1, keepdims=True), sk)
    p = jnp.exp(s - mx)
    es = jnp.exp(sk - mx)
    den = jnp.sum(p, axis=-1, keepdims=True) + es
    return p / den, es / den


def _kv_specs():
    TQ, B = ATTN_TQ, ATTN_BACK
    blk = lambda off: _seg(TQ, 256, off)
    halo = lambda off: _seg(B, 256, off, lambda i: jnp.maximum(i * TQ - B, 0))
    return [blk(OFF_K), halo(OFF_K), blk(OFF_V), halo(OFF_V)]


def _kv_window(k_ref, kh_ref, v_ref, vh_ref, kbuf, vbuf, i):
    B = ATTN_BACK
    for buf, ref, href in ((kbuf, k_ref, kh_ref), (vbuf, v_ref, vh_ref)):
        buf[:B, :] = jnp.where(i > 0, href[...], jnp.zeros_like(href))
        buf[B:, :] = ref[...]


def _attn_fwd(proj, sink_b):
    S = proj.shape[0]
    TQ, NK = ATTN_TQ, ATTN_TQ + ATTN_BACK

    def body(q_ref, z_ref, k_ref, kh_ref, v_ref, vh_ref, sink_ref, o_ref, y_ref, kbuf, vbuf):
        i = pl.program_id(0)
        _kv_window(k_ref, kh_ref, v_ref, vh_ref, kbuf, vbuf, i)
        valid = _attn_mask(i)
        for hp in range(N_HEADS // 2):
            kh = 2 * hp // (N_HEADS // N_KV)
            ks = slice(kh * HEAD_DIM, (kh + 1) * HEAD_DIM)
            ps = slice(hp * LANE, (hp + 1) * LANE)
            os = []
            for h in (2 * hp, 2 * hp + 1):
                pn, _ = _attn_probs(q_ref, kbuf[:, ks], sink_ref, valid, h)
                os.append(_dot(pn.astype(BF16), vbuf[:, ks]))
            o = jnp.concatenate(os, axis=1)
            o_ref[:, ps] = o.astype(BF16)
            y_ref[:, ps] = (o * _silu(z_ref[:, ps].astype(F32))).astype(BF16)

    out = pl.BlockSpec((TQ, 1024), lambda i: (i, 0))
    return pl.pallas_call(
        body, name="attn_fwd", grid=(S // TQ,),
        in_specs=[_seg(TQ, 1024, OFF_Q), _seg(TQ, 1024, OFF_AZ), *_kv_specs(), _full((N_HEADS, LANE))],
        out_specs=[out, out], out_shape=[jax.ShapeDtypeStruct((S, 1024), BF16)] * 2,
        scratch_shapes=[pltpu.VMEM((NK, 256), BF16)] * 2,
        compiler_params=_cp(("parallel",)),
    )(proj, proj, proj, proj, proj, proj, sink_b)


def _conv_specs(tb):
    H = CONV_HALO
    prev = lambda i: jnp.maximum(i * tb - H, 0)
    a = _seg(tb, 1024, OFF_CA)
    ah = _seg(H, 1024, OFF_CA, prev)
    b = _seg(tb, 1024, OFF_CB)
    bh = _seg(H, 1024, OFF_CB, prev)
    z = _seg(tb, 1024, OFF_CZ)
    return a, ah, b, bh, z


SUBLANES = 8


def _shift_copies(bufs, tb):
    n = tb + CONV_HALO - SUBLANES
    for s in range(1, SUBLANES):
        bufs[s, :n, :] = bufs[0, pl.ds(s, n), :]


def _rows_at(bufs, offset, n, cs):
    return bufs[offset % SUBLANES, pl.ds(offset - offset % SUBLANES, n), cs]


def _conv_glu_dw(a_ref, ah_ref, b_ref, bh_ref, dw_ref, gbuf, ybuf, i, tb):
    H = CONV_HALO
    gbuf[0, H:, :] = a_ref[...].astype(F32) * _sig(b_ref[...].astype(F32))
    gh = ah_ref[...].astype(F32) * _sig(bh_ref[...].astype(F32))
    gbuf[0, :H, :] = jnp.where(i > 0, gh, 0.0)
    _shift_copies(gbuf, tb)
    base = H - (CONV_K - 1)
    for c in range(1024 // LANE):
        cs = slice(c * LANE, (c + 1) * LANE)
        acc = dw_ref[0:1, cs] * _rows_at(gbuf, base, tb, cs)
        for j in range(1, CONV_K):
            acc = acc + dw_ref[j:j + 1, cs] * _rows_at(gbuf, base + j, tb, cs)
        ybuf[:, cs] = acc


def _layer_norm_fwd(y, g, b):
    mu = jnp.mean(y, axis=-1, keepdims=True)
    yc = y - mu
    rstd = lax.rsqrt(jnp.mean(yc * yc, axis=-1, keepdims=True) + EPS)
    xh = yc * rstd
    return xh, rstd, xh * g + b


def _conv_fwd(proj, dw, dw_b, ln_g, ln_b, pw, tb=256):
    S = proj.shape[0]

    def body(a_ref, ah_ref, b_ref, bh_ref, z_ref, dw_ref, dwb_ref, lg_ref, lb_ref, pw_ref, y_ref, gbuf, ybuf):
        i = pl.program_id(0)
        _conv_glu_dw(a_ref, ah_ref, b_ref, bh_ref, dw_ref, gbuf, ybuf, i, tb)
        _, _, yn = _layer_norm_fwd(ybuf[...] + dwb_ref[...], lg_ref[...], lb_ref[...])
        out = _dot(_silu(yn).astype(BF16), pw_ref[...])
        y_ref[...] = (out * _silu(z_ref[...].astype(F32))).astype(BF16)

    vec = _full((1, 1024))
    return pl.pallas_call(
        body, name="conv_fwd", grid=(S // tb,),
        in_specs=[*_conv_specs(tb), _full((32, 1024)), vec, vec, vec, _full((1024, 1024))],
        out_specs=pl.BlockSpec((tb, 1024), lambda i: (i, 0)),
        out_shape=jax.ShapeDtypeStruct((S, 1024), BF16),
        scratch_shapes=[pltpu.VMEM((SUBLANES, tb + CONV_HALO, 1024), F32), pltpu.VMEM((tb, 1024), F32)],
        compiler_params=_cp(("parallel",)),
    )(proj, proj, proj, proj, proj, dw, dw_b, ln_g, ln_b, pw)


def _merge_fwd(yp, ya, yc, wbp, wba, wbc, proj, tm=512, tn=1024):
    S = yp.shape[0]

    def body(yp_ref, ya_ref, yc_ref, wp_ref, wa_ref, wc_ref, gp_ref, ga_ref, gc_ref, m_ref, pp_ref, pa_ref, pc_ref):
        pp = _dot(yp_ref[...], wp_ref[...])
        pa = _dot(ya_ref[...], wa_ref[...])
        pc = _dot(yc_ref[...], wc_ref[...])
        m = (_sig(gp_ref[...].astype(F32)) * pp + _sig(ga_ref[...].astype(F32)) * pa
             + _sig(gc_ref[...].astype(F32)) * pc)
        m_ref[...] = m.astype(BF16)
        pp_ref[...], pa_ref[...], pc_ref[...] = pp.astype(BF16), pa.astype(BF16), pc.astype(BF16)

    yb = pl.BlockSpec((tm, 1024), lambda j, i: (i, 0))
    wb = pl.BlockSpec((1024, tn), lambda j, i: (0, j))
    out = pl.BlockSpec((tm, tn), lambda j, i: (i, j))
    return pl.pallas_call(
        body, name="merge_fwd", grid=(D // tn, S // tm),
        in_specs=[yb, yb, yb, wb, wb, wb, _seg2(tm, tn, OFF_GP), _seg2(tm, tn, OFF_GA), _seg2(tm, tn, OFF_GC)],
        out_specs=[out] * 4, out_shape=[jax.ShapeDtypeStruct((S, D), BF16)] * 4,
        compiler_params=_cp(("parallel", "parallel")),
    )(yp, ya, yc, wbp, wba, wbc, proj, proj, proj)


def _out_fwd(x, merged, w_out, gate, tm=512, tn=1024):
    S = x.shape[0]

    def body(x_ref, m_ref, w_ref, g_ref, o_ref):
        o_ref[...] = x_ref[...] + g_ref[...] * _dot(m_ref[...], w_ref[...])

    xb = pl.BlockSpec((tm, tn), lambda j, i: (i, j))
    return pl.pallas_call(
        body, name="out_fwd", grid=(D // tn, S // tm),
        in_specs=[xb, pl.BlockSpec((tm, D), lambda j, i: (i, 0)), pl.BlockSpec((D, tn), lambda j, i: (0, j)),
                  pl.BlockSpec((1, tn), lambda j, i: (0, j))],
        out_specs=xb, out_shape=jax.ShapeDtypeStruct((S, D), F32),
        compiler_params=_cp(("parallel", "parallel")),
    )(x, merged, w_out, gate)


def _final_loss(x, target, final_g, tb=256):
    S = x.shape[0]

    def body(x_ref, t_ref, g_ref, dx_ref, gg_ref, ls_ref):
        i = pl.program_id(0)
        xv, g = x_ref[...], g_ref[...]
        r = lax.rsqrt(jnp.mean(xv * xv, axis=-1, keepdims=True) + EPS)
        xh = xv * r
        e = xh * g - t_ref[...]
        dy = e * (1.0 / D)
        gy = dy * g
        dx_ref[...] = r * (gy - xh * jnp.mean(gy * xh, axis=-1, keepdims=True))
        gg = jnp.sum(dy * xh, axis=0, keepdims=True)
        ls = jnp.sum(e * e, axis=0, keepdims=True) * (0.5 / D)

        @pl.when(i == 0)
        def _():
            gg_ref[...], ls_ref[...] = gg, ls

        @pl.when(i > 0)
        def _():
            gg_ref[...] += gg
            ls_ref[...] += ls

    row = pl.BlockSpec((tb, D), lambda i: (i, 0))
    vec = _full((1, D))
    return pl.pallas_call(
        body, name="final_loss", grid=(S // tb,), in_specs=[row, row, vec], out_specs=[row, vec, vec],
        out_shape=[jax.ShapeDtypeStruct((S, D), F32), jax.ShapeDtypeStruct((1, D), F32),
                   jax.ShapeDtypeStruct((1, D), F32)],
        compiler_params=_cp(("arbitrary",)),
    )(x, target, final_g)


def _out_bwd(dx, gate, w_out, pp, pa, pc, proj, tm=512, tn=1024):
    S = dx.shape[0]

    def body(dx_ref, g_ref, w_ref, pp_ref, pa_ref, pc_ref, gp_ref, ga_ref, gc_ref,
             dpp_ref, dpa_ref, dpc_ref, dgp_ref, dga_ref, dgc_ref):
        dm = _dot_nt((dx_ref[...] * g_ref[...]).astype(BF16), w_ref[...])
        for p_ref, gl_ref, dp_ref, dg_ref in ((pp_ref, gp_ref, dpp_ref, dgp_ref), (pa_ref, ga_ref, dpa_ref, dga_ref),
                                              (pc_ref, gc_ref, dpc_ref, dgc_ref)):
            s = _sig(gl_ref[...].astype(F32))
            dp_ref[...] = (dm * s).astype(BF16)
            dg_ref[...] = (dm * p_ref[...].astype(F32) * s * (1.0 - s)).astype(BF16)

    out = pl.BlockSpec((tm, tn), lambda j, i: (i, j))
    return pl.pallas_call(
        body, name="out_bwd", grid=(D // tn, S // tm),
        in_specs=[pl.BlockSpec((tm, D), lambda j, i: (i, 0)), _full((1, D)), pl.BlockSpec((tn, D), lambda j, i: (j, 0)),
                  out, out, out, _seg2(tm, tn, OFF_GP), _seg2(tm, tn, OFF_GA), _seg2(tm, tn, OFF_GC)],
        out_specs=[out] * 6, out_shape=[jax.ShapeDtypeStruct((S, D), BF16)] * 6,
        compiler_params=_cp(("parallel", "parallel")),
    )(dx, gate, w_out, pp, pa, pc, proj, proj, proj)


def _wout_post(gmat, w_out, gate, tr=256):
    def body(g_ref, w_ref, gate_ref, dw_ref, dg_ref):
        i = pl.program_id(0)
        gm = g_ref[...]
        dw_ref[...] = gm * gate_ref[...]
        part = jnp.sum(gm * w_ref[...].astype(F32), axis=0, keepdims=True)

        @pl.when(i == 0)
        def _():
            dg_ref[...] = part

        @pl.when(i > 0)
        def _():
            dg_ref[...] += part

    row = pl.BlockSpec((tr, D), lambda i: (i, 0))
    return pl.pallas_call(
        body, name="wout_post", grid=(D // tr,), in_specs=[row, row, _full((1, D))], out_specs=[row, _full((1, D))],
        out_shape=[jax.ShapeDtypeStruct((D, D), F32), jax.ShapeDtypeStruct((1, D), F32)],
        compiler_params=_cp(("arbitrary",)),
    )(gmat, w_out, gate)


def _pool_bwd_a(dy, proj, pool_w, pool_scale, tb=256):
    S = proj.shape[0]

    def body(dy_ref, u_ref, uh_ref, z_ref, w_ref, sc_ref, dmix_ref, dz_ref, dsc_ref, dw_ref, ubuf):
        i = pl.program_id(0)
        mixed = [m.astype(BF16) for m in _pool_mix(u_ref, uh_ref, ubuf, i, tb)]
        m = jnp.concatenate([_dot(mixed[g], w_ref[g]) for g in range(4)], axis=1)
        dyv, z, sc = dy_ref[...].astype(F32), z_ref[...].astype(F32), sc_ref[...]
        dyp = dyv * _silu(z)
        dz_ref[...] = (dyv * (m * sc) * _dsilu(z)).astype(BF16)
        dsc = jnp.sum(dyp * m, axis=0, keepdims=True)
        dmm = (dyp * sc).astype(BF16)
        dws = []
        for g in range(4):
            cs = slice(g * POOL_GROUP, (g + 1) * POOL_GROUP)
            dmix_ref[:, cs] = _dot_nt(dmm[:, cs], w_ref[g])
            dws.append(_dot_tn(mixed[g], dmm[:, cs]))

        @pl.when(i == 0)
        def _():
            dsc_ref[...] = dsc
            for g in range(4):
                dw_ref[g] = dws[g]

        @pl.when(i > 0)
        def _():
            dsc_ref[...] += dsc
            for g in range(4):
                dw_ref[g] += dws[g]

    u, uh, z = _pool_specs(tb)
    row = pl.BlockSpec((tb, 1024), lambda i: (i, 0))
    wfull = _full((4, 256, 256))
    return pl.pallas_call(
        body, name="pool_bwd_a", grid=(S // tb,),
        in_specs=[row, u, uh, z, wfull, _full((1, 1024))],
        out_specs=[row, row, _full((1, 1024)), wfull],
        out_shape=[jax.ShapeDtypeStruct((S, 1024), F32), jax.ShapeDtypeStruct((S, 1024), BF16),
                   jax.ShapeDtypeStruct((1, 1024), F32), jax.ShapeDtypeStruct((4, 256, 256), F32)],
        scratch_shapes=[pltpu.VMEM((tb + POOL_HALO, 1024), F32)],
        compiler_params=_cp(("arbitrary",)),
    )(dy, proj, proj, proj, pool_w, pool_scale)


def _pool_bwd_b(dmix, tb=256):
    S = dmix.shape[0]
    H = POOL_HALO
    nb = S // tb

    def body(dm_ref, dh_ref, du_ref, ebuf):
        i = pl.program_id(0)
        t = i * tb + lax.broadcasted_iota(jnp.int32, (tb, 1), 0)
        th = (i + 1) * tb + lax.broadcasted_iota(jnp.int32, (H, 1), 0)
        for g, w in enumerate(POOL_WINDOWS):
            cs = slice(g * POOL_GROUP, (g + 1) * POOL_GROUP)
            ebuf[:tb, cs] = dm_ref[:, cs] / jnp.minimum(t + 1, w).astype(F32)
            eh = dh_ref[:, cs] / jnp.minimum(th + 1, w).astype(F32)
            ebuf[tb:, cs] = jnp.where(i < nb - 1, eh, 0.0)
        for g, w in enumerate(POOL_WINDOWS):
            cs = slice(g * POOL_GROUP, (g + 1) * POOL_GROUP)
            acc = ebuf[:tb, cs]
            for j in range(1, w):
                acc = acc + ebuf[pl.ds(j, tb), cs]
            du_ref[:, cs] = (acc - dm_ref[:, cs]).astype(BF16)

    row = pl.BlockSpec((tb, 1024), lambda i: (i, 0))
    nxt = pl.BlockSpec((H, 1024), lambda i: (jnp.minimum((i + 1) * (tb // H), S // H - 1), 0))
    return pl.pallas_call(
        body, name="pool_bwd_b", grid=(nb,), in_specs=[row, nxt], out_specs=row,
        out_shape=jax.ShapeDtypeStruct((S, 1024), BF16),
        scratch_shapes=[pltpu.VMEM((tb + H, 1024), F32)],
        compiler_params=_cp(("parallel",)),
    )(dmix, dmix)


def _attn_bwd(dy, o, proj, sink_b):
    S = proj.shape[0]
    TQ, NK = ATTN_TQ, ATTN_TQ + ATTN_BACK
    nb = S // TQ
    G = N_HEADS // N_KV

    def body(dy_ref, o_ref, q_ref, z_ref, k_ref, kh_ref, v_ref, vh_ref, sink_ref, dq_ref, dz_ref, dk_hbm, dv_hbm, ds_ref,
             dk_acc, dv_acc, kbuf, vbuf):
        i = pl.program_id(0)
        _kv_window(k_ref, kh_ref, v_ref, vh_ref, kbuf, vbuf, i)

        @pl.when(i == 0)
        def _():
            dk_acc[...] = jnp.zeros_like(dk_acc)
            dv_acc[...] = jnp.zeros_like(dv_acc)
            ds_ref[...] = jnp.zeros_like(ds_ref)

        start = pl.multiple_of(i * TQ, TQ)
        valid = _attn_mask(i)
        dks, dvs = [], []
        for kh in range(N_KV):
            ks = slice(kh * HEAD_DIM, (kh + 1) * HEAD_DIM)
            kw = kbuf[:, ks]
            vw = vbuf[:, ks]
            dk_sum = jnp.zeros((NK, HEAD_DIM), F32)
            dv_sum = jnp.zeros((NK, HEAD_DIM), F32)
            dqs = []
            for gi in range(G):
                h = kh * G + gi
                hs = slice(h * HEAD_DIM, (h + 1) * HEAD_DIM)
                pn, psink = _attn_probs(q_ref, kw, sink_ref, valid, h)
                ov = o_ref[:, hs].astype(F32)
                do = dy_ref[:, hs].astype(F32) * _silu(z_ref[:, hs].astype(F32))
                delta = jnp.sum(do * ov, axis=-1, keepdims=True)
                dob = do.astype(BF16)
                dp = _dot_nt(dob, vw)
                ds = (pn * (dp - delta)).astype(BF16)
                dsink = -jnp.sum(psink * delta, axis=0, keepdims=True)
                ds_ref[h:h + 1, :] += jnp.broadcast_to(dsink, (1, LANE))
                dqs.append(_dot(ds, kw) * (HEAD_DIM ** -0.5))
                dk_sum = dk_sum + _dot_tn(ds, q_ref[:, hs])
                dv_sum = dv_sum + _dot_tn(pn.astype(BF16), dob)
            gs = slice(kh * G * HEAD_DIM, (kh + 1) * G * HEAD_DIM)
            dq_ref[:, gs] = jnp.concatenate(dqs, axis=1).astype(BF16)
            z = z_ref[:, gs].astype(F32)
            dz_ref[:, gs] = (dy_ref[:, gs].astype(F32) * o_ref[:, gs].astype(F32) * _dsilu(z)).astype(BF16)
            dks.append(dk_sum * (HEAD_DIM ** -0.5))
            dvs.append(dv_sum)
        dk_acc[pl.ds(start, NK), :] += jnp.concatenate(dks, axis=1)
        dv_acc[pl.ds(start, NK), :] += jnp.concatenate(dvs, axis=1)

        @pl.when(i == nb - 1)
        def _():
            pltpu.sync_copy(dk_acc, dk_hbm)
            pltpu.sync_copy(dv_acc, dv_hbm)

    row = pl.BlockSpec((TQ, 1024), lambda i: (i, 0))
    return pl.pallas_call(
        body, name="attn_bwd", grid=(nb,),
        in_specs=[row, row, _seg(TQ, 1024, OFF_Q), _seg(TQ, 1024, OFF_AZ), *_kv_specs(), _full((N_HEADS, LANE))],
        out_specs=[row, row, ANY, ANY, _full((N_HEADS, LANE))],
        out_shape=[jax.ShapeDtypeStruct((S, 1024), BF16), jax.ShapeDtypeStruct((S, 1024), BF16),
                   jax.ShapeDtypeStruct((S + ATTN_BACK, 256), F32), jax.ShapeDtypeStruct((S + ATTN_BACK, 256), F32),
                   jax.ShapeDtypeStruct((N_HEADS, LANE), F32)],
        scratch_shapes=[pltpu.VMEM((S + ATTN_BACK, 256), F32)] * 2 + [pltpu.VMEM((NK, 256), BF16)] * 2,
        compiler_params=_cp(("arbitrary",), VMEM_BIG),
    )(dy, o, proj, proj, proj, proj, proj, proj, sink_b)


def _conv_bwd_a(dy, proj, dw, dw_b, ln_g, ln_b, pw, tb=256):
    S = proj.shape[0]
    H = CONV_HALO
    base = H - (CONV_K - 1)

    def body(dy_ref, a_ref, ah_ref, b_ref, bh_ref, z_ref, dw_ref, dwb_ref, lg_ref, lb_ref, pw_ref,
             dcv_ref, dz_ref, dpw_ref, dlg_ref, dlb_ref, ddwb_ref, ddw_ref, gbuf, ybuf):
        i = pl.program_id(0)
        _conv_glu_dw(a_ref, ah_ref, b_ref, bh_ref, dw_ref, gbuf, ybuf, i, tb)
        lg = lg_ref[...]
        xh, rstd, yn = _layer_norm_fwd(ybuf[...] + dwb_ref[...], lg, lb_ref[...])
        u = _silu(yn).astype(BF16)
        out = _dot(u, pw_ref[...])
        dyv, z = dy_ref[...].astype(F32), z_ref[...].astype(F32)
        dz_ref[...] = (dyv * out * _dsilu(z)).astype(BF16)
        dout = (dyv * _silu(z)).astype(BF16)
        dpw = _dot_tn(u, dout)
        dyn = _dot_nt(dout, pw_ref[...]) * _dsilu(yn)
        dlg = jnp.sum(dyn * xh, axis=0, keepdims=True)
        dlb = jnp.sum(dyn, axis=0, keepdims=True)
        dxh = dyn * lg
        dcv = rstd * (dxh - jnp.mean(dxh, axis=-1, keepdims=True) - xh * jnp.mean(dxh * xh, axis=-1, keepdims=True))
        dcv_ref[...] = dcv
        ddwb = jnp.sum(dcv, axis=0, keepdims=True)
        ybuf[...] = dcv

        @pl.when(i == 0)
        def _():
            dpw_ref[...], dlg_ref[...], dlb_ref[...], ddwb_ref[...] = dpw, dlg, dlb, ddwb
            ddw_ref[...] = jnp.zeros_like(ddw_ref)

        @pl.when(i > 0)
        def _():
            dpw_ref[...] += dpw
            dlg_ref[...] += dlg
            dlb_ref[...] += dlb
            ddwb_ref[...] += ddwb

        for c in range(1024 // LANE):
            cs = slice(c * LANE, (c + 1) * LANE)
            for j in range(CONV_K):
                ddw_ref[j:j + 1, cs] += jnp.sum(ybuf[:, cs] * _rows_at(gbuf, base + j, tb, cs), axis=0, keepdims=True)

    vec = _full((1, 1024))
    row = pl.BlockSpec((tb, 1024), lambda i: (i, 0))
    big = _full((1024, 1024))
    return pl.pallas_call(
        body, name="conv_bwd_a", grid=(S // tb,),
        in_specs=[row, *_conv_specs(tb), _full((32, 1024)), vec, vec, vec, big],
        out_specs=[row, row, big, vec, vec, vec, _full((32, 1024))],
        out_shape=[jax.ShapeDtypeStruct((S, 1024), F32), jax.ShapeDtypeStruct((S, 1024), BF16),
                   jax.ShapeDtypeStruct((1024, 1024), F32), jax.ShapeDtypeStruct((1, 1024), F32),
                   jax.ShapeDtypeStruct((1, 1024), F32), jax.ShapeDtypeStruct((1, 1024), F32),
                   jax.ShapeDtypeStruct((32, 1024), F32)],
        scratch_shapes=[pltpu.VMEM((SUBLANES, tb + H, 1024), F32), pltpu.VMEM((tb, 1024), F32)],
        compiler_params=_cp(("arbitrary",)),
    )(dy, proj, proj, proj, proj, proj, dw, dw_b, ln_g, ln_b, pw)


def _conv_bwd_b(dcv, proj, dw, tb=256):
    S = proj.shape[0]
    H = CONV_HALO
    nb = S // tb

    def body(d_ref, dn_ref, a_ref, b_ref, dw_ref, da_ref, db_ref, dbuf, gbuf):
        i = pl.program_id(0)
        dbuf[0, :tb, :] = d_ref[...]
        dbuf[0, tb:, :] = jnp.where(i < nb - 1, dn_ref[...], 0.0)
        _shift_copies(dbuf, tb)
        for c in range(1024 // LANE):
            cs = slice(c * LANE, (c + 1) * LANE)
            acc = dw_ref[0:1, cs] * _rows_at(dbuf, CONV_K - 1, tb, cs)
            for j in range(1, CONV_K):
                acc = acc + dw_ref[j:j + 1, cs] * _rows_at(dbuf, (CONV_K - 1) - j, tb, cs)
            gbuf[:, cs] = acc
        dg = gbuf[...]
        a, s = a_ref[...].astype(F32), _sig(b_ref[...].astype(F32))
        da_ref[...] = (dg * s).astype(BF16)
        db_ref[...] = (dg * a * s * (1.0 - s)).astype(BF16)

    row = pl.BlockSpec((tb, 1024), lambda i: (i, 0))
    nxt = pl.BlockSpec((H, 1024), lambda i: (jnp.minimum((i + 1) * (tb // H), S // H - 1), 0))
    return pl.pallas_call(
        body, name="conv_bwd_b", grid=(nb,),
        in_specs=[row, nxt, _seg(tb, 1024, OFF_CA), _seg(tb, 1024, OFF_CB), _full((32, 1024))],
        out_specs=[row, row], out_shape=[jax.ShapeDtypeStruct((S, 1024), BF16)] * 2,
        scratch_shapes=[pltpu.VMEM((SUBLANES, tb + H, 1024), F32), pltpu.VMEM((tb, 1024), F32)],
        compiler_params=_cp(("parallel",)),
    )(dcv, dcv, proj, proj, dw)


def _norm_bwd(dh, x, dx_out, g, scale, tb=256):
    S = x.shape[0]

    def body(dh_ref, x_ref, dxo_ref, g_ref, sc_ref, dx_ref, dsh_ref, da_ref):
        i = pl.program_id(0)
        xv, dhv = x_ref[...], dh_ref[...]
        r = lax.rsqrt(jnp.mean(xv * xv, axis=-1, keepdims=True) + EPS)
        xh = xv * r
        gy = dhv * (g_ref[...] * (1.0 + sc_ref[...]))
        dx_ref[...] = dxo_ref[...] + r * (gy - xh * jnp.mean(gy * xh, axis=-1, keepdims=True))
        dsh = jnp.sum(dhv, axis=0, keepdims=True)
        da = jnp.sum(dhv * xh, axis=0, keepdims=True)

        @pl.when(i == 0)
        def _():
            dsh_ref[...], da_ref[...] = dsh, da

        @pl.when(i > 0)
        def _():
            dsh_ref[...] += dsh
            da_ref[...] += da

    row = pl.BlockSpec((tb, D), lambda i: (i, 0))
    vec = _full((1, D))
    return pl.pallas_call(
        body, name="norm_bwd", grid=(S // tb,), in_specs=[row, row, row, vec, vec], out_specs=[row, vec, vec],
        out_shape=[jax.ShapeDtypeStruct((S, D), F32), jax.ShapeDtypeStruct((1, D), F32), jax.ShapeDtypeStruct((1, D), F32)],
        compiler_params=_cp(("arbitrary",)),
    )(dh, x, dx_out, g, scale)


def _mod_bwd(d_a, norm_g, scale):
    def body(da_ref, g_ref, sc_ref, dg_ref, dsc_ref):
        dg_ref[...] = da_ref[...] * (1.0 + sc_ref[...])
        dsc_ref[...] = da_ref[...] * g_ref[...]

    return pl.pallas_call(body, name="mod_bwd", out_shape=[jax.ShapeDtypeStruct(d_a.shape, F32)] * 2)(d_a, norm_g, scale)


def _reduce_adamw(parts, w, m, v, name):
    L, rows, C = w.shape
    tr = rows if rows % 64 else 64

    def body(*refs):
        p_refs, (w_ref, m_ref, v_ref, g_ref, dl_ref, m2_ref, v2_ref) = refs[:L * N_DEV], refs[L * N_DEV:]
        for l in range(L):
            g = p_refs[l * N_DEV][0].astype(F32)
            for k in range(1, N_DEV):
                g = g + p_refs[l * N_DEV + k][0].astype(F32)
            g_ref[l] = g
            dl_ref[l], m2_ref[l], v2_ref[l] = _adam_math(g, w_ref[l], m_ref[l], v_ref[l])

    slot = lambda k: pl.BlockSpec((1, tr, C), lambda i: (k, i, 0))
    blk = pl.BlockSpec((L, tr, C), lambda i: (0, i, 0))
    return pl.pallas_call(
        body, name=name, grid=(rows // tr,), in_specs=[slot(k) for _ in range(L) for k in range(N_DEV)] + [blk] * 3,
        out_specs=[blk] * 4, out_shape=[jax.ShapeDtypeStruct((L, rows, C), F32)] * 4,
        compiler_params=_cp(("parallel",)),
    )(*[p for p in parts for _ in range(N_DEV)], w, m, v)


def _small_final(parts, w, m, v):
    R = w.shape[0]

    def body(p_ref, w_ref, m_ref, v_ref, g_ref, dl_ref, m2_ref, v2_ref):
        g = p_ref[0]
        for k in range(1, N_DEV):
            g = g + p_ref[k]
        delta, m2, v2 = _adam_math(g, w_ref[...], m_ref[...], v_ref[...])
        g_ref[...], dl_ref[...], m2_ref[...], v2_ref[...] = g, delta, m2, v2

    return pl.pallas_call(body, name="small_final", out_shape=[jax.ShapeDtypeStruct((R, LANE), F32)] * 4)(parts, w, m, v)


def _layer_fwd(x, mod, small, W, more_w):
    shift, scale, gate = mod
    h = _norm_mod(x, small["norm_g"], scale, shift)
    proj = _mm(h, W["w_in_t"], BF16, 512, 1536, name="proj_mm", nt=True)
    W.update(more_w(proj))
    y_pool = _pool_fwd(proj, W["pool_w"], small["pool_scale"])
    o, y_attn = _attn_fwd(proj, small["sink_b"])
    y_conv = _conv_fwd(proj, W["conv_dw"], small["conv_dw_b"], small["conv_ln_g"], small["conv_ln_b"], W["conv_pw"])
    merged, pp, pa, pc = _merge_fwd(y_pool, y_attn, y_conv, W["wbp"], W["wba"], W["wbc"], proj)
    x_new = _out_fwd(x, merged, W["w_out"], gate)
    stash = dict(x=x, h=h, proj=proj, o=o, y_pool=y_pool, y_attn=y_attn, y_conv=y_conv,
                 merged=merged, pp=pp, pa=pa, pc=pc)
    return x_new, stash


def _layer_bwd(dx, st, mod, small, W, put):
    shift, scale, gate = mod
    proj = st["proj"]
    gmat = _mm_tn(st["merged"], dx, 1024, 1024, 1024, name="wout_tn")
    d_w_out, d_gate = _wout_post(gmat, W["w_out"], gate)
    dpp, dpa, dpc, dgp, dga, dgc = _out_bwd(dx, gate, W["w_out"], st["pp"], st["pa"], st["pc"], proj)
    dy_pool = _mm(dpp, W["wbp"], BF16, 512, 1024, name="branch_bwd_mm", nt=True)
    dy_attn = _mm(dpa, W["wba"], BF16, 512, 1024, name="branch_bwd_mm", nt=True)
    dy_conv = _mm(dpc, W["wbc"], BF16, 512, 1024, name="branch_bwd_mm", nt=True)
    d_wbp = _mm_tn(st["y_pool"], dpp, 1024, 1024, 1024, name="branch_tn")
    d_wba = _mm_tn(st["y_attn"], dpa, 1024, 1024, 1024, name="branch_tn")
    d_wbc = _mm_tn(st["y_conv"], dpc, 1024, 1024, 1024, name="branch_tn")

    dmix, dz_pool, d_pool_scale, d_pool_w = _pool_bwd_a(dy_pool, proj, W["pool_w"], small["pool_scale"])
    du = _pool_bwd_b(dmix)
    dq, dz_attn, dk, dv, d_sink = _attn_bwd(dy_attn, st["o"], proj, small["sink_b"])
    dcv, dz_conv, d_pw, d_ln_g, d_ln_b, d_dw_b, d_dw = _conv_bwd_a(
        dy_conv, proj, W["conv_dw"], small["conv_dw_b"], small["conv_ln_g"], small["conv_ln_b"], W["conv_pw"])
    da, db = _conv_bwd_b(dcv, proj, W["conv_dw"])

    tok = put(dict(pool_w=d_pool_w, conv_dw=d_dw[:CONV_K], conv_pw=d_pw, wbp=d_wbp, wba=d_wba, wbc=d_wbc, w_out=d_w_out))
    dproj = jnp.concatenate([du, dz_pool, dq, dk[ATTN_BACK:].astype(BF16), dv[ATTN_BACK:].astype(BF16), dz_attn, da, db,
                             dz_conv, dgp, dga, dgc], axis=1)
    d_w_in_t = _mm_tn(dproj, st["h"], 768, 2048, 1024, name="win_tn", vmem=VMEM_BIG, after=tok)
    tok = put(dict(w_in=d_w_in_t))
    dh = _mm(dproj, W["w_in_t"], F32, 1024, 2048, 1536, name="dh_mm", vmem=VMEM_BIG, after=tok)
    dx_in, d_shift, d_a = _norm_bwd(dh, st["x"], dx, small["norm_g"], scale)
    sm = dict(d_a=d_a, d_shift=d_shift, d_gate=d_gate, pool_scale=d_pool_scale, attn_sink=d_sink[:, 0],
              conv_dw_b=d_dw_b, conv_ln_g=d_ln_g, conv_ln_b=d_ln_b)
    return dx_in, sm


def _local_step(x, target, mods, smalls, get_w, final_g, put_g, end_layer):
    stashes, Ws = [], []
    for l in range(DEPTH):
        w, more_w = get_w(l, x)
        Ws.append(w)
        x, st = _layer_fwd(x, mods[l], smalls[l], w, more_w)
        stashes.append(st)
    dx, d_final_g, loss_lanes = _final_loss(x, target, final_g)
    sms = [None] * DEPTH
    for l in reversed(range(DEPTH)):
        dx, sms[l] = _layer_bwd(dx, stashes[l], mods[l], smalls[l], Ws[l], functools.partial(put_g, l))
        end_layer(l, dx)
    return loss_lanes, dx, d_final_g, sms


BIG = ("w_in", "pool_w", "conv_pw", "wbp", "wba", "wbc", "w_out")
GRADS = BIG + ("conv_dw",)


def _full_w_in(g):
    return dict(w_in_t=g.reshape(IN_WIDTH, D))


def _full_weights(g, conv_dw):
    cols = lambda a: jnp.transpose(a, (1, 0, 2)).reshape(a.shape[1], -1)
    pool_w = jnp.transpose(g["pool_w"], (1, 0, 2, 3)).reshape(4, 256, 256)
    conv_pw = g["conv_pw"].reshape(1024, 1024)
    wbp, wba, wbc = cols(g["wbp"]), cols(g["wba"]), cols(g["wbc"])
    w_out = g["w_out"].reshape(D, D)
    conv_dw = jnp.pad(cols(conv_dw), ((0, 32 - CONV_K), (0, 0)))
    return dict(pool_w=pool_w, conv_pw=conv_pw, wbp=wbp, wba=wba, wbc=wbc, w_out=w_out, conv_dw=conv_dw)


def _pieces(name, g):
    if name == "w_in":
        return g.reshape(8, IN_WIDTH // 8, D)
    if name == "pool_w":
        return jnp.transpose(g.reshape(4, 8, 32, 256), (1, 0, 2, 3))
    if name == "conv_dw":
        return jnp.transpose(g.reshape(CONV_K, 8, 128), (1, 0, 2))
    if name == "conv_pw":
        return g.reshape(8, 128, 1024)
    if name in ("wbp", "wba", "wbc"):
        return jnp.transpose(g.reshape(1024, 8, 256), (1, 0, 2))
    return g.reshape(8, 256, D)


def _pack_small(items, rows):
    flat = jnp.concatenate([a.reshape(-1).astype(F32) for a in items])
    return jnp.pad(flat, (0, rows * LANE - flat.shape[0])).reshape(rows, LANE)


def _unpack_small(packed, shapes):
    flat, out, off = packed.reshape(-1), [], 0
    for s in shapes:
        n = 1
        for d in s:
            n *= d
        out.append(flat[off:off + n].reshape(s))
        off += n
    return out


def kernel(x, c, norm_g, w_ada, b_ada, w_in, pool_w, pool_scale, attn_sink, conv_dw, conv_dw_b, conv_ln_g, conv_ln_b, conv_pw, w_branch_pool, w_branch_attn, w_branch_conv, w_out, final_g, loss_target, m_norm_g, m_w_ada, m_b_ada, m_w_in, m_pool_w, m_pool_scale, m_attn_sink, m_conv_dw, m_conv_dw_b, m_conv_ln_g, m_conv_ln_b, m_conv_pw, m_w_branch_pool, m_w_branch_attn, m_w_branch_conv, m_w_out, m_final_g, v_norm_g, v_w_ada, v_b_ada, v_w_in, v_pool_w, v_pool_scale, v_attn_sink, v_conv_dw, v_conv_dw_b, v_conv_ln_g, v_conv_ln_b, v_conv_pw, v_w_branch_pool, v_w_branch_attn, v_w_branch_conv, v_w_out, v_final_g):
    L = DEPTH
    me = 4 * lax.axis_index("x") + 2 * lax.axis_index("y") + lax.axis_index("c")
    tr = lambda a: jnp.swapaxes(a, 1, 2)
    shards = dict(w_in=tr(w_in), pool_w=pool_w, conv_dw=conv_dw, conv_pw=conv_pw, wbp=w_branch_pool, wba=w_branch_attn,
                  wbc=w_branch_conv, w_out=w_out)
    moms = dict(w_in=(tr(m_w_in), tr(v_w_in)), pool_w=(m_pool_w, v_pool_w), conv_dw=(m_conv_dw, v_conv_dw),
                conv_pw=(m_conv_pw, v_conv_pw), wbp=(m_w_branch_pool, v_w_branch_pool),
                wba=(m_w_branch_attn, v_w_branch_attn), wbc=(m_w_branch_conv, v_w_branch_conv), w_out=(m_w_out, v_w_out))

    n_cd = L * CONV_K * 128
    first = _all_gather([_pack_small([c, conv_dw], 144)], "gather_c")[0].reshape(N_DEV, -1)
    c_all = first[:, :D]
    conv_dw_all = first[:, D:D + n_cd].reshape(N_DEV, L, CONV_K, 128)

    mod_part = _mod_fwd(c_all, w_ada)
    mod_all = _all_gather([mod_part.reshape(-1, LANE)], "gather_mod")[0].reshape(N_DEV, L, N_DEV, -1)
    mod = jnp.transpose(lax.dynamic_index_in_dim(mod_all, me, axis=2, keepdims=False), (1, 0, 2)).reshape(L, 3 * D)
    mod = mod + b_ada
    mods = [(mod[l:l + 1, :D], mod[l:l + 1, D:2 * D], mod[l:l + 1, 2 * D:]) for l in range(L)]

    w_in_0 = _all_gather([(shards["w_in"][0] + mod[0, 0] * 0.0).astype(BF16)], "gather_w_in_0")[0]
    gathers, tok = [], w_in_0[0, 0, 0].astype(F32) * 0.0
    for l in range(L):
        first = lambda a: (a + tok).astype(BF16)
        mine = [first(shards["w_in"][l])], [first(shards[k][l]) if k == BIG[1] else shards[k][l].astype(BF16) for k in BIG[1:]]
        started = [None if (l, n) == (0, 0) else _spread_start(v, False, f"gather_start_{l}_{n}") for n, v in enumerate(mine)]
        gathers.append((mine, started))
        tok = sum(st[-1][0, 0] for st in started if st is not None)
    mods[0] = (mods[0][0] + tok,) + mods[0][1:]

    sink_b = jnp.broadcast_to(attn_sink[:, :, None], (L, N_HEADS, LANE))
    smalls = [dict(norm_g=norm_g[l:l + 1], pool_scale=pool_scale[l:l + 1], sink_b=sink_b[l], conv_dw_b=conv_dw_b[l:l + 1],
                   conv_ln_g=conv_ln_g[l:l + 1], conv_ln_b=conv_ln_b[l:l + 1]) for l in range(L)]

    def with_mine(landed, mine):
        return lax.dynamic_update_slice(landed, mine, (me,) + (0,) * (landed.ndim - 1))

    def get_w(l, x_in):
        mine, started = gathers[l]
        if l == 0:
            w_in = w_in_0
        else:
            w_in = with_mine(_spread_wait(started[0], x_in, False, f"gather_wait_{l}_0")[0], mine[0][0][None])

        def more_w(proj):
            landed = _spread_wait(started[1], proj, False, f"gather_wait_{l}_1")
            g = {k: with_mine(a, b[None]) for k, a, b in zip(BIG[1:], landed, mine[1])}
            return _full_weights(g, conv_dw_all[:, l])

        return _full_w_in(w_in), more_w

    pending, parts = {l: [] for l in range(L)}, [dict() for _ in range(L)]

    def put_g(l, grads):
        names = tuple(k for k in GRADS if k in grads)
        pieces = [_pieces(k, grads[k]).astype(BF16) for k in names]
        started = _spread_start(pieces, True, f"scatter_start_{l}_{len(pending[l])}")
        pending[l].append((names, pieces, started))
        return started[-1][0:1, 0:1]

    def finish(l, after):
        for n, (names, pieces, started) in enumerate(pending.pop(l)):
            landed = _spread_wait(started, after, True, f"scatter_wait_{l}_{n}")
            for k, a, b in zip(names, landed, pieces):
                parts[l][k] = with_mine(a, lax.dynamic_slice_in_dim(b, me, 1, axis=0))

    def end_layer(l, dx):
        if l + 1 in pending:
            finish(l + 1, dx)

    loss_lanes, grad_x, d_final_g, sms = _local_step(x[0], loss_target[0], mods, smalls, get_w, final_g.reshape(1, D),
                                                     put_g, end_layer)
    finish(0, grad_x)

    stack = lambda k: jnp.concatenate([sms[l][k].reshape(1, -1) for l in range(L)], axis=0)
    scale_all = jnp.concatenate([mods[l][1] for l in range(L)], axis=0)
    d_norm_g, d_scale = _mod_bwd(stack("d_a"), norm_g, scale_all)
    dmod = jnp.concatenate([stack("d_shift"), d_scale, stack("d_gate")], axis=1)
    small_names = ("norm_g", "b_ada", "pool_scale", "attn_sink", "conv_dw_b", "conv_ln_g", "conv_ln_b", "final_g")
    small_g = (d_norm_g, dmod, stack("pool_scale"), stack("attn_sink"), stack("conv_dw_b"), stack("conv_ln_g"),
               stack("conv_ln_b"), d_final_g.reshape(D))
    small_w = (norm_g, b_ada, pool_scale, attn_sink, conv_dw_b, conv_ln_g, conv_ln_b, final_g)
    small_m = (m_norm_g, m_b_ada, m_pool_scale, m_attn_sink, m_conv_dw_b, m_conv_ln_g, m_conv_ln_b, m_final_g)
    small_v = (v_norm_g, v_b_ada, v_pool_scale, v_attn_sink, v_conv_dw_b, v_conv_ln_g, v_conv_ln_b, v_final_g)
    shapes = [a.shape for a in small_w] + [(D,)]
    n_small = sum(a.size for a in small_w) + D
    R = -(-n_small // (8 * LANE)) * 8
    zero = jnp.zeros((D,), F32)
    small_parts = _all_gather([_pack_small(small_g + (loss_lanes,), R)], "gather_small")[0]
    sg, sd, sm2, sv2 = _small_final(small_parts, _pack_small(small_w + (zero,), R), _pack_small(small_m + (zero,), R),
                                    _pack_small(small_v + (zero + 1.0,), R))
    sg, sd, sm2, sv2 = (_unpack_small(a, shapes) for a in (sg, sd, sm2, sv2))
    loss = jnp.sum(sg[-1])
    res = {n: (sg[i], sd[i], sm2[i], sv2[i]) for i, n in enumerate(small_names)}

    off = norm_g.size
    dmod_all = small_parts.reshape(N_DEV, -1)[:, off:off + L * 3 * D].reshape(N_DEV, L, 3 * D)
    dmod_mine = jnp.transpose(lax.dynamic_slice_in_dim(dmod_all, me * (3 * D // N_DEV), 3 * D // N_DEV, axis=2), (1, 0, 2))
    res["w_ada"] = _wada_bwd(c_all.T, dmod_mine, w_ada, m_w_ada, v_w_ada)

    for k in GRADS:
        shp = shards[k].shape
        to3d = lambda a: a.reshape(L, -1, shp[-1])
        out = _reduce_adamw([parts[l][k].reshape(N_DEV, -1, shp[-1]) for l in range(L)], to3d(shards[k]),
                            to3d(moms[k][0]), to3d(moms[k][1]), "adamw_" + k)
        res[k] = tuple(a.reshape(shp) for a in out)
    res["w_in"] = tuple(tr(a) for a in res["w_in"])

    order = ("norm_g", "w_ada", "b_ada", "w_in", "pool_w", "pool_scale", "attn_sink", "conv_dw", "conv_dw_b", "conv_ln_g",
             "conv_ln_b", "conv_pw", "wbp", "wba", "wbc", "w_out", "final_g")
    outs = [loss, grad_x[None]]
    for j in range(4):
        outs += [res[n][j] for n in order]
    return tuple(outs)
```

```python
import functools

import jax
import jax.numpy as jnp
from jax import lax
from jax.experimental import pallas as pl
from jax.experimental.pallas import tpu as pltpu

F32, BF16 = jnp.float32, jnp.bfloat16
MESH = pl.DeviceIdType.MESH
ANY = pl.BlockSpec(memory_space=pl.ANY)

N_DEV = 8
D = 2048
DEPTH = 4
EPS = 1e-6
IN_WIDTH = 13824
POOL_WINDOWS = (2, 4, 8, 16)
POOL_GROUP = 256
POOL_HALO = 16
CONV_K = 31
CONV_HALO = 32
N_HEADS, N_KV, HEAD_DIM = 16, 4, 64
ATTN_TQ = 256
ATTN_BACK = 128
LANE = 128
VMEM_BIG = 56 * 1024 * 1024

OFF_U, OFF_Z, OFF_Q, OFF_K, OFF_V, OFF_AZ, OFF_CA, OFF_CB, OFF_CZ = 0, 1024, 2048, 3072, 3328, 3584, 4608, 5632, 6656
OFF_GP, OFF_GA, OFF_GC = 7680, 9728, 11776


def _seg(rows, width, off, first_row=None):
    start = (lambda i: i * rows) if first_row is None else first_row
    return pl.BlockSpec((pl.Element(rows), pl.Element(width)), lambda i: (pl.multiple_of(start(i), rows), off))


def _seg2(tm, tn, off):
    return pl.BlockSpec((pl.Element(tm), pl.Element(tn)), lambda j, i: (i * tm, pl.multiple_of(off + j * tn, LANE)))

ADAM_LR, ADAM_B1, ADAM_B2, ADAM_EPS, ADAM_WD, ADAM_STEP = 0.001, 0.9, 0.999, 1e-08, 0.01, 10


def _cp(sem=None, vmem=None):
    return pltpu.CompilerParams(dimension_semantics=sem, vmem_limit_bytes=vmem)


def _sig(x):
    return jax.nn.sigmoid(x)


def _silu(x):
    return x * _sig(x)


def _dsilu(x):
    s = _sig(x)
    return s * (1.0 + x * (1.0 - s))


def _dot(a, b):
    return jnp.dot(a, b, preferred_element_type=F32)


def _dot_tn(a, b):
    return lax.dot_general(a, b, (((0,), (0,)), ((), ())), preferred_element_type=F32)


def _dot_nt(a, b):
    return lax.dot_general(a, b, (((1,), (1,)), ((), ())), preferred_element_type=F32)


def _full(shape):
    n = len(shape)
    return pl.BlockSpec(shape, lambda *_: (0,) * n)


def _my_pos():
    return lax.axis_index("x"), lax.axis_index("y"), lax.axis_index("c")


def _all_gather(xs, name):
    n = len(xs)

    def body(*refs):
        x_refs, o_refs = refs[:n], refs[n:2 * n]
        send_sems, recv_sems, local_sems = refs[2 * n:]
        x, y, c = _my_pos()
        sibling = (x, y, 1 - c)
        chips = [(1 - x, y), (x, 1 - y), (1 - x, 1 - y)]
        me = 4 * x + 2 * y + c

        def slot(px, py, pc):
            return 4 * px + 2 * py + pc

        def copy(t, k, block, to, src=None):
            dst = o_refs[t].at[block]
            return pltpu.make_async_remote_copy(
                src_ref=dst if src is None else src, dst_ref=dst,
                send_sem=send_sems.at[t, k], recv_sem=recv_sems.at[t, k],
                device_id=to, device_id_type=MESH)

        mine = [pltpu.make_async_copy(x_refs[t], o_refs[t].at[me], local_sems.at[t]) for t in range(n)]
        for cp in mine:
            cp.start()
        first = []
        for t in range(n):
            first.append(copy(t, 0, me, sibling, src=x_refs[t]))
            for j, chip in enumerate(chips):
                first.append(copy(t, 1 + j, me, (*chip, c), src=x_refs[t]))
        for cp in first:
            cp.start()
        passed = []
        for j, chip in enumerate(chips):
            for t in range(n):
                copy(t, 1 + j, slot(*chip, c), (x, y, c)).wait_recv()
                fwd = copy(t, 4 + j, slot(*chip, c), sibling)
                fwd.start()
                passed.append(fwd)
        for t in range(n):
            copy(t, 0, slot(x, y, 1 - c), (x, y, c)).wait_recv()
            for j, chip in enumerate(chips):
                copy(t, 4 + j, slot(*chip, 1 - c), (x, y, c)).wait_recv()
        for cp in first + passed:
            cp.wait_send()
        for cp in mine:
            cp.wait()

    return pl.pallas_call(
        body, name=name,
        out_shape=[jax.ShapeDtypeStruct((N_DEV,) + a.shape, a.dtype) for a in xs],
        in_specs=[ANY] * n, out_specs=[ANY] * n,
        scratch_shapes=[pltpu.SemaphoreType.DMA((n, 7)), pltpu.SemaphoreType.DMA((n, 7)),
                        pltpu.SemaphoreType.DMA((n,))],
    )(*xs)


N_PEER = N_DEV - 1
HBM = pl.BlockSpec(memory_space=pltpu.HBM)
SEM = pl.BlockSpec(memory_space=pltpu.SEMAPHORE)
EFFECT = pltpu.SideEffectType.DATAFLOW_SIDE_EFFECTING


def _peer(k):
    x, y, c = _my_pos()
    flip = lambda v, bit: 1 - v if bit else v
    return flip(x, (k >> 2) & 1), flip(y, (k >> 1) & 1), flip(c, k & 1)


def _spread_copies(v_ref, land_ref, send_sems, recv_sems, per_peer):
    x, y, c = _my_pos()
    me = 4 * x + 2 * y + c
    copies = []
    for k in range(1, N_DEV):
        px, py, pc = _peer(k)
        src = v_ref.at[4 * px + 2 * py + pc] if per_peer else v_ref
        copies.append(pltpu.make_async_remote_copy(
            src_ref=src, dst_ref=land_ref.at[me], send_sem=send_sems[k - 1], recv_sem=recv_sems[k - 1],
            device_id=(px, py, pc), device_id_type=MESH))
    return copies


def _spread_start(vs, per_peer, name):
    n = len(vs)
    lands = [(N_DEV,) + (v.shape[1:] if per_peer else v.shape) for v in vs]
    n_sem = 2 * N_PEER * n

    def body(*refs):
        v_refs, land_refs, outs = refs[:n], refs[n:2 * n], refs[2 * n:]
        for t in range(n):
            sems = outs[2 * N_PEER * t:2 * N_PEER * (t + 1)]
            for cp in _spread_copies(v_refs[t], land_refs[t], sems[:N_PEER], sems[N_PEER:], per_peer):
                cp.start()
        token = outs[n_sem + 2 * n]
        token[...] = jnp.zeros_like(token)

    hbm = lambda a: pltpu.with_memory_space_constraint(a, pltpu.HBM)
    return pl.pallas_call(
        body, name=name,
        out_shape=((pltpu.SemaphoreType.DMA(()),) * n_sem + tuple(pltpu.HBM(v.shape, v.dtype) for v in vs)
                   + tuple(pltpu.HBM(s, v.dtype) for s, v in zip(lands, vs)) + (jax.ShapeDtypeStruct((8, LANE), F32),)),
        in_specs=(HBM,) * (2 * n), out_specs=(SEM,) * n_sem + (HBM,) * (2 * n) + (pl.BlockSpec(memory_space=pltpu.VMEM),),
        input_output_aliases={t: n_sem + t for t in range(2 * n)},
        compiler_params=pltpu.CompilerParams(has_side_effects=EFFECT),
    )(*[hbm(v) for v in vs], *[hbm(lax.empty(s, v.dtype)) for s, v in zip(lands, vs)])


def _spread_wait(started, after, per_peer, name):
    n = (len(started) - 1) // (2 * N_PEER + 2)
    n_sem = 2 * N_PEER * n
    sems, thru = started[:n_sem], started[n_sem:n_sem + 2 * n]

    def body(*refs):
        v_refs, land_refs, rest = refs[:n], refs[n:2 * n], refs[2 * n:]
        for t in range(n):
            s = rest[2 * N_PEER * t:2 * N_PEER * (t + 1)]
            for cp in _spread_copies(v_refs[t], land_refs[t], s[:N_PEER], s[N_PEER:], per_peer):
                cp.wait_send()
                cp.wait_recv()

    return pl.pallas_call(
        body, name=name,
        out_shape=tuple(pltpu.HBM(a.shape, a.dtype) for a in thru),
        in_specs=(HBM,) * (2 * n) + (SEM,) * n_sem + (ANY,), out_specs=(HBM,) * (2 * n),
        input_output_aliases={t: t for t in range(2 * n)},
        compiler_params=pltpu.CompilerParams(has_side_effects=EFFECT),
    )(*thru, *sems, after)[n:]


def _mm(a, b, out_dtype, tm, tn, tk=None, name="mm", vmem=None, after=None, nt=False):
    M, K = a.shape
    N = b.shape[0] if nt else b.shape[1]
    tk = K if tk is None else tk
    nk = K // tk
    assert M % tm == 0 and N % tn == 0 and K % tk == 0
    dep = () if after is None else (after,)

    def body(*refs):
        a_ref, b_ref, o_ref, *acc = refs[len(dep):]
        prod = (_dot_nt if nt else _dot)(a_ref[...].astype(BF16), b_ref[...])
        if nk == 1:
            o_ref[...] = prod.astype(out_dtype)
            return
        acc_ref = acc[0] if acc else o_ref
        k = pl.program_id(2)

        @pl.when(k == 0)
        def _():
            acc_ref[...] = prod

        @pl.when(k > 0)
        def _():
            acc_ref[...] += prod

        if acc:
            @pl.when(k == nk - 1)
            def _():
                o_ref[...] = acc_ref[...].astype(out_dtype)

    scratch = [pltpu.VMEM((tm, tn), F32)] if (nk > 1 and out_dtype != F32) else []
    b_spec = pl.BlockSpec((tn, tk), lambda j, i, k: (j, k)) if nt else pl.BlockSpec((tk, tn), lambda j, i, k: (k, j))
    return pl.pallas_call(
        body, name=name, grid=(N // tn, M // tm, nk),
        in_specs=[_full((1, 1))] * len(dep) + [pl.BlockSpec((tm, tk), lambda j, i, k: (i, k)), b_spec],
        out_specs=pl.BlockSpec((tm, tn), lambda j, i, k: (i, j)),
        out_shape=jax.ShapeDtypeStruct((M, N), out_dtype), scratch_shapes=scratch,
        compiler_params=_cp(("parallel", "parallel", "arbitrary"), vmem),
    )(*dep, a, b)


def _mm_tn(a, b, tm, tn, ts, name="mm_tn", vmem=None, after=None):
    S, Ka = a.shape
    _, N = b.shape
    assert Ka % tm == 0 and N % tn == 0 and S % ts == 0
    dep = () if after is None else (after,)

    def body(*refs):
        a_ref, b_ref, o_ref = refs[len(dep):]
        prod = _dot_tn(a_ref[...].astype(BF16), b_ref[...].astype(BF16))
        k = pl.program_id(2)

        @pl.when(k == 0)
        def _():
            o_ref[...] = prod

        @pl.when(k > 0)
        def _():
            o_ref[...] += prod

    return pl.pallas_call(
        body, name=name, grid=(Ka // tm, N // tn, S // ts),
        in_specs=[_full((1, 1))] * len(dep) + [pl.BlockSpec((ts, tm), lambda i, j, k: (k, i)),
                                               pl.BlockSpec((ts, tn), lambda i, j, k: (k, j))],
        out_specs=pl.BlockSpec((tm, tn), lambda i, j, k: (i, j)),
        out_shape=jax.ShapeDtypeStruct((Ka, N), F32),
        compiler_params=_cp(("parallel", "parallel", "arbitrary"), vmem),
    )(*dep, a, b)


def _mod_fwd(c_all, w_ada):
    L, _, n = w_ada.shape

    def body(c_ref, w_ref, o_ref):
        ca = _silu(c_ref[...])
        o_ref[0] = jnp.dot(ca, w_ref[0], preferred_element_type=F32, precision=lax.Precision.HIGHEST)

    return pl.pallas_call(
        body, name="mod_fwd", grid=(L,),
        in_specs=[_full((N_DEV, D)), pl.BlockSpec((1, D, n), lambda l: (l, 0, 0))],
        out_specs=pl.BlockSpec((1, N_DEV, n), lambda l: (l, 0, 0)),
        out_shape=jax.ShapeDtypeStruct((L, N_DEV, n), F32),
        compiler_params=_cp(("parallel",)),
    )(c_all, w_ada)


def _adam_math(g, w, m, v):
    m2 = ADAM_B1 * m + (1.0 - ADAM_B1) * g
    v2 = ADAM_B2 * v + (1.0 - ADAM_B2) * (g * g)
    m_hat = m2 / (1.0 - ADAM_B1 ** ADAM_STEP)
    v_hat = v2 / (1.0 - ADAM_B2 ** ADAM_STEP)
    delta = -ADAM_LR * (m_hat / (jnp.sqrt(v_hat) + ADAM_EPS) + ADAM_WD * w)
    return delta, m2, v2


def _wada_bwd(c_all_t, dmod, w, m, v, tr=256):
    L, _, n = w.shape

    def body(c_ref, d_ref, w_ref, m_ref, v_ref, g_ref, dl_ref, m2_ref, v2_ref):
        ca = _silu(c_ref[...])
        dm = d_ref[0]
        g = ca[:, 0:1] * dm[0:1, :]
        for b in range(1, N_DEV):
            g = g + ca[:, b:b + 1] * dm[b:b + 1, :]
        delta, m2, v2 = _adam_math(g, w_ref[0], m_ref[0], v_ref[0])
        g_ref[0], dl_ref[0], m2_ref[0], v2_ref[0] = g, delta, m2, v2

    blk = pl.BlockSpec((1, tr, n), lambda l, i: (l, i, 0))
    return pl.pallas_call(
        body, name="wada_bwd", grid=(L, D // tr),
        in_specs=[pl.BlockSpec((tr, N_DEV), lambda l, i: (i, 0)), pl.BlockSpec((1, N_DEV, n), lambda l, i: (l, 0, 0)),
                  blk, blk, blk],
        out_specs=[blk] * 4, out_shape=[jax.ShapeDtypeStruct(w.shape, F32)] * 4,
        compiler_params=_cp(("parallel", "parallel")),
    )(c_all_t, dmod, w, m, v)


def _norm_mod(x, g, scale, shift, tb=256):
    S = x.shape[0]

    def body(x_ref, g_ref, sc_ref, sh_ref, h_ref):
        xv = x_ref[...]
        r = lax.rsqrt(jnp.mean(xv * xv, axis=-1, keepdims=True) + EPS)
        h_ref[...] = (xv * r * (g_ref[...] * (1.0 + sc_ref[...])) + sh_ref[...]).astype(BF16)

    row = pl.BlockSpec((tb, D), lambda i: (i, 0))
    vec = _full((1, D))
    return pl.pallas_call(
        body, name="norm_mod", grid=(S // tb,), in_specs=[row, vec, vec, vec], out_specs=row,
        out_shape=jax.ShapeDtypeStruct((S, D), BF16), compiler_params=_cp(("parallel",)),
    )(x, g, scale, shift)


def _pool_mix(u_ref, uh_ref, ubuf, i, tb):
    H = POOL_HALO
    ubuf[H:, :] = u_ref[...].astype(F32)
    ubuf[:H, :] = jnp.where(i > 0, uh_ref[...].astype(F32), 0.0)
    t = i * tb + lax.broadcasted_iota(jnp.int32, (tb, 1), 0)
    mixed = []
    for g, w in enumerate(POOL_WINDOWS):
        cs = slice(g * POOL_GROUP, (g + 1) * POOL_GROUP)
        cur = ubuf[H:, cs]
        acc = cur
        for j in range(1, w):
            acc = acc + ubuf[pl.ds(H - j, tb), cs]
        cnt = jnp.minimum(t + 1, w).astype(F32)
        mixed.append(acc / cnt - cur)
    return mixed


def _pool_specs(tb):
    H = POOL_HALO
    u = _seg(tb, 1024, OFF_U)
    uh = _seg(H, 1024, OFF_U, lambda i: jnp.maximum(i * tb - H, 0))
    z = _seg(tb, 1024, OFF_Z)
    return u, uh, z


def _pool_fwd(proj, pool_w, pool_scale, tb=256):
    S = proj.shape[0]

    def body(u_ref, uh_ref, z_ref, w_ref, sc_ref, y_ref, ubuf):
        i = pl.program_id(0)
        mixed = _pool_mix(u_ref, uh_ref, ubuf, i, tb)
        m = jnp.concatenate([_dot(mixed[g].astype(BF16), w_ref[g]) for g in range(4)], axis=1)
        y_ref[...] = (m * sc_ref[...] * _silu(z_ref[...].astype(F32))).astype(BF16)

    u, uh, z = _pool_specs(tb)
    return pl.pallas_call(
        body, name="pool_fwd", grid=(S // tb,),
        in_specs=[u, uh, z, _full((4, 256, 256)), _full((1, 1024))],
        out_specs=pl.BlockSpec((tb, 1024), lambda i: (i, 0)),
        out_shape=jax.ShapeDtypeStruct((S, 1024), BF16),
        scratch_shapes=[pltpu.VMEM((tb + POOL_HALO, 1024), F32)],
        compiler_params=_cp(("parallel",)),
    )(proj, proj, proj, pool_w, pool_scale)


def _attn_mask(i):
    TQ, NK = ATTN_TQ, ATTN_TQ + ATTN_BACK
    qc = lax.broadcasted_iota(jnp.int32, (TQ, NK), 0) // 64
    col = lax.broadcasted_iota(jnp.int32, (TQ, NK), 1)
    kc = col // 64
    return (kc >= qc) & (kc <= qc + 2) & ((col >= ATTN_BACK) | (i > 0))


def _attn_probs(q_ref, kw, sink_ref, valid, h):
    qh = q_ref[:, h * HEAD_DIM:(h + 1) * HEAD_DIM]
    s = _dot_nt(qh, kw) * (HEAD_DIM ** -0.5)
    s = jnp.where(valid, s, -jnp.inf)
    sk = sink_ref[h:h + 1, 0:1]
    mx = jnp.maximum(jnp.max(s, axis=-1, keepdims=True), sk)
    p = jnp.exp(s - mx)
    es = jnp.exp(sk - mx)
    den = jnp.sum(p, axis=-1, keepdims=True) + es
    return p / den, es / den


def _kv_specs():
    TQ, B = ATTN_TQ, ATTN_BACK
    blk = lambda off: _seg(TQ, 256, off)
    halo = lambda off: _seg(B, 256, off, lambda i: jnp.maximum(i * TQ - B, 0))
    return [blk(OFF_K), halo(OFF_K), blk(OFF_V), halo(OFF_V)]


def _kv_window(k_ref, kh_ref, v_ref, vh_ref, kbuf, vbuf, i):
    B = ATTN_BACK
    for buf, ref, href in ((kbuf, k_ref, kh_ref), (vbuf, v_ref, vh_ref)):
        buf[:B, :] = jnp.where(i > 0, href[...], jnp.zeros_like(href))
        buf[B:, :] = ref[...]


def _attn_fwd(proj, sink_b):
    S = proj.shape[0]
    TQ, NK = ATTN_TQ, ATTN_TQ + ATTN_BACK

    def body(q_ref, z_ref, k_ref, kh_ref, v_ref, vh_ref, sink_ref, o_ref, y_ref, kbuf, vbuf):
        i = pl.program_id(0)
        _kv_window(k_ref, kh_ref, v_ref, vh_ref, kbuf, vbuf, i)
        valid = _attn_mask(i)
        for hp in range(N_HEADS // 2):
            kh = 2 * hp // (N_HEADS // N_KV)
            ks = slice(kh * HEAD_DIM, (kh + 1) * HEAD_DIM)
            ps = slice(hp * LANE, (hp + 1) * LANE)
            os = []
            for h in (2 * hp, 2 * hp + 1):
                pn, _ = _attn_probs(q_ref, kbuf[:, ks], sink_ref, valid, h)
                os.append(_dot(pn.astype(BF16), vbuf[:, ks]))
            o = jnp.concatenate(os, axis=1)
            o_ref[:, ps] = o.astype(BF16)
            y_ref[:, ps] = (o * _silu(z_ref[:, ps].astype(F32))).astype(BF16)

    out = pl.BlockSpec((TQ, 1024), lambda i: (i, 0))
    return pl.pallas_call(
        body, name="attn_fwd", grid=(S // TQ,),
        in_specs=[_seg(TQ, 1024, OFF_Q), _seg(TQ, 1024, OFF_AZ), *_kv_specs(), _full((N_HEADS, LANE))],
        out_specs=[out, out], out_shape=[jax.ShapeDtypeStruct((S, 1024), BF16)] * 2,
        scratch_shapes=[pltpu.VMEM((NK, 256), BF16)] * 2,
        compiler_params=_cp(("parallel",)),
    )(proj, proj, proj, proj, proj, proj, sink_b)


def _conv_specs(tb):
    H = CONV_HALO
    prev = lambda i: jnp.maximum(i * tb - H, 0)
    a = _seg(tb, 1024, OFF_CA)
    ah = _seg(H, 1024, OFF_CA, prev)
    b = _seg(tb, 1024, OFF_CB)
    bh = _seg(H, 1024, OFF_CB, prev)
    z = _seg(tb, 1024, OFF_CZ)
    return a, ah, b, bh, z


SUBLANES = 8
CONV_TAPS = tuple((j, CONV_HALO - (CONV_K - 1) + j) for j in range(CONV_K))
CONV_TAPS_T = tuple((j, (CONV_K - 1) - j) for j in range(CONV_K))


def _lane_chunks(fn):
    def body(c, carry):
        fn(pl.ds(pl.multiple_of(c * LANE, LANE), LANE))
        return carry

    lax.fori_loop(0, 1024 // LANE, body, 0)


def _shifted_tiles(src_ref, cs, s, n):
    row = lax.broadcasted_iota(jnp.int32, (SUBLANES, LANE), 0)
    prev = None
    for t in range(n + (1 if s else 0)):
        v = src_ref[pl.ds(SUBLANES * t, SUBLANES), cs]
        if s == 0:
            yield t, v
            continue
        x = pltpu.roll(v, SUBLANES - s, 0)
        if prev is not None:
            yield t - 1, jnp.where(row < SUBLANES - s, prev, x)
        prev = x


def _by_shift(taps):
    groups = {}
    for j, o in taps:
        groups.setdefault(o % SUBLANES, []).append((j, o // SUBLANES))
    return sorted(groups.items())


def _taps_apply(src_ref, w_ref, dst_ref, tb, taps, cs):
    nu = tb // SUBLANES
    acc = [None] * nu
    for s, group in _by_shift(taps):
        w = {j: w_ref[j:j + 1, cs] for j, _ in group}
        for t, g in _shifted_tiles(src_ref, cs, s, nu + max(a for _, a in group)):
            for j, a in group:
                if 0 <= t - a < nu:
                    term = w[j] * g
                    acc[t - a] = term if acc[t - a] is None else acc[t - a] + term
    dst_ref[:, cs] = jnp.concatenate(acc, axis=0)


def _taps_reduce(src_ref, d_ref, out_ref, tb, taps, cs):
    nu = tb // SUBLANES
    d = [d_ref[pl.ds(SUBLANES * u, SUBLANES), cs] for u in range(nu)]
    for s, group in _by_shift(taps):
        part = {j: None for j, _ in group}
        for t, g in _shifted_tiles(src_ref, cs, s, nu + max(a for _, a in group)):
            for j, a in group:
                if 0 <= t - a < nu:
                    term = d[t - a] * g
                    part[j] = term if part[j] is None else part[j] + term
        for j, _ in group:
            out_ref[j:j + 1, cs] += jnp.sum(part[j], axis=0, keepdims=True)


def _conv_glu_dw(a_ref, ah_ref, b_ref, bh_ref, dw_ref, gbuf, ybuf, i, tb):
    H = CONV_HALO
    gbuf[H:, :] = a_ref[...].astype(F32) * _sig(b_ref[...].astype(F32))
    gh = ah_ref[...].astype(F32) * _sig(bh_ref[...].astype(F32))
    gbuf[:H, :] = jnp.where(i > 0, gh, 0.0)
    _lane_chunks(lambda cs: _taps_apply(gbuf, dw_ref, ybuf, tb, CONV_TAPS, cs))


def _layer_norm_fwd(y, g, b):
    mu = jnp.mean(y, axis=-1, keepdims=True)
    yc = y - mu
    rstd = lax.rsqrt(jnp.mean(yc * yc, axis=-1, keepdims=True) + EPS)
    xh = yc * rstd
    return xh, rstd, xh * g + b


def _conv_fwd(proj, dw, dw_b, ln_g, ln_b, pw, tb=256):
    S = proj.shape[0]

    def body(a_ref, ah_ref, b_ref, bh_ref, z_ref, dw_ref, dwb_ref, lg_ref, lb_ref, pw_ref, y_ref, gbuf, ybuf):
        i = pl.program_id(0)
        _conv_glu_dw(a_ref, ah_ref, b_ref, bh_ref, dw_ref, gbuf, ybuf, i, tb)
        _, _, yn = _layer_norm_fwd(ybuf[...] + dwb_ref[...], lg_ref[...], lb_ref[...])
        out = _dot(_silu(yn).astype(BF16), pw_ref[...])
        y_ref[...] = (out * _silu(z_ref[...].astype(F32))).astype(BF16)

    vec = _full((1, 1024))
    return pl.pallas_call(
        body, name="conv_fwd", grid=(S // tb,),
        in_specs=[*_conv_specs(tb), _full((32, 1024)), vec, vec, vec, _full((1024, 1024))],
        out_specs=pl.BlockSpec((tb, 1024), lambda i: (i, 0)),
        out_shape=jax.ShapeDtypeStruct((S, 1024), BF16),
        scratch_shapes=[pltpu.VMEM((tb + CONV_HALO, 1024), F32), pltpu.VMEM((tb, 1024), F32)],
        compiler_params=_cp(("parallel",)),
    )(proj, proj, proj, proj, proj, dw, dw_b, ln_g, ln_b, pw)


def _merge_fwd(yp, ya, yc, wbp, wba, wbc, proj, tm=512, tn=1024):
    S = yp.shape[0]

    def body(yp_ref, ya_ref, yc_ref, wp_ref, wa_ref, wc_ref, gp_ref, ga_ref, gc_ref, m_ref, pp_ref, pa_ref, pc_ref):
        pp = _dot(yp_ref[...], wp_ref[...])
        pa = _dot(ya_ref[...], wa_ref[...])
        pc = _dot(yc_ref[...], wc_ref[...])
        m = (_sig(gp_ref[...].astype(F32)) * pp + _sig(ga_ref[...].astype(F32)) * pa
             + _sig(gc_ref[...].astype(F32)) * pc)
        m_ref[...] = m.astype(BF16)
        pp_ref[...], pa_ref[...], pc_ref[...] = pp.astype(BF16), pa.astype(BF16), pc.astype(BF16)

    yb = pl.BlockSpec((tm, 1024), lambda j, i: (i, 0))
    wb = pl.BlockSpec((1024, tn), lambda j, i: (0, j))
    out = pl.BlockSpec((tm, tn), lambda j, i: (i, j))
    return pl.pallas_call(
        body, name="merge_fwd", grid=(D // tn, S // tm),
        in_specs=[yb, yb, yb, wb, wb, wb, _seg2(tm, tn, OFF_GP), _seg2(tm, tn, OFF_GA), _seg2(tm, tn, OFF_GC)],
        out_specs=[out] * 4, out_shape=[jax.ShapeDtypeStruct((S, D), BF16)] * 4,
        compiler_params=_cp(("parallel", "parallel")),
    )(yp, ya, yc, wbp, wba, wbc, proj, proj, proj)


def _out_fwd(x, merged, w_out, gate, tm=512, tn=1024):
    S = x.shape[0]

    def body(x_ref, m_ref, w_ref, g_ref, o_ref):
        o_ref[...] = x_ref[...] + g_ref[...] * _dot(m_ref[...], w_ref[...])

    xb = pl.BlockSpec((tm, tn), lambda j, i: (i, j))
    return pl.pallas_call(
        body, name="out_fwd", grid=(D // tn, S // tm),
        in_specs=[xb, pl.BlockSpec((tm, D), lambda j, i: (i, 0)), pl.BlockSpec((D, tn), lambda j, i: (0, j)),
                  pl.BlockSpec((1, tn), lambda j, i: (0, j))],
        out_specs=xb, out_shape=jax.ShapeDtypeStruct((S, D), F32),
        compiler_params=_cp(("parallel", "parallel")),
    )(x, merged, w_out, gate)


def _final_loss(x, target, final_g, tb=256):
    S = x.shape[0]

    def body(x_ref, t_ref, g_ref, dx_ref, gg_ref, ls_ref):
        i = pl.program_id(0)
        xv, g = x_ref[...], g_ref[...]
        r = lax.rsqrt(jnp.mean(xv * xv, axis=-1, keepdims=True) + EPS)
        xh = xv * r
        e = xh * g - t_ref[...]
        dy = e * (1.0 / D)
        gy = dy * g
        dx_ref[...] = r * (gy - xh * jnp.mean(gy * xh, axis=-1, keepdims=True))
        gg = jnp.sum(dy * xh, axis=0, keepdims=True)
        ls = jnp.sum(e * e, axis=0, keepdims=True) * (0.5 / D)

        @pl.when(i == 0)
        def _():
            gg_ref[...], ls_ref[...] = gg, ls

        @pl.when(i > 0)
        def _():
            gg_ref[...] += gg
            ls_ref[...] += ls

    row = pl.BlockSpec((tb, D), lambda i: (i, 0))
    vec = _full((1, D))
    return pl.pallas_call(
        body, name="final_loss", grid=(S // tb,), in_specs=[row, row, vec], out_specs=[row, vec, vec],
        out_shape=[jax.ShapeDtypeStruct((S, D), F32), jax.ShapeDtypeStruct((1, D), F32),
                   jax.ShapeDtypeStruct((1, D), F32)],
        compiler_params=_cp(("arbitrary",)),
    )(x, target, final_g)


def _out_bwd(dx, gate, w_out, pp, pa, pc, proj, tm=512, tn=1024):
    S = dx.shape[0]

    def body(dx_ref, g_ref, w_ref, pp_ref, pa_ref, pc_ref, gp_ref, ga_ref, gc_ref,
             dpp_ref, dpa_ref, dpc_ref, dgp_ref, dga_ref, dgc_ref):
        dm = _dot_nt((dx_ref[...] * g_ref[...]).astype(BF16), w_ref[...])
        for p_ref, gl_ref, dp_ref, dg_ref in ((pp_ref, gp_ref, dpp_ref, dgp_ref), (pa_ref, ga_ref, dpa_ref, dga_ref),
                                              (pc_ref, gc_ref, dpc_ref, dgc_ref)):
            s = _sig(gl_ref[...].astype(F32))
            dp_ref[...] = (dm * s).astype(BF16)
            dg_ref[...] = (dm * p_ref[...].astype(F32) * s * (1.0 - s)).astype(BF16)

    out = pl.BlockSpec((tm, tn), lambda j, i: (i, j))
    return pl.pallas_call(
        body, name="out_bwd", grid=(D // tn, S // tm),
        in_specs=[pl.BlockSpec((tm, D), lambda j, i: (i, 0)), _full((1, D)), pl.BlockSpec((tn, D), lambda j, i: (j, 0)),
                  out, out, out, _seg2(tm, tn, OFF_GP), _seg2(tm, tn, OFF_GA), _seg2(tm, tn, OFF_GC)],
        out_specs=[out] * 6, out_shape=[jax.ShapeDtypeStruct((S, D), BF16)] * 6,
        compiler_params=_cp(("parallel", "parallel")),
    )(dx, gate, w_out, pp, pa, pc, proj, proj, proj)


def _wout_post(gmat, w_out, gate, tr=256):
    def body(g_ref, w_ref, gate_ref, dw_ref, dg_ref):
        i = pl.program_id(0)
        gm = g_ref[...]
        dw_ref[...] = gm * gate_ref[...]
        part = jnp.sum(gm * w_ref[...].astype(F32), axis=0, keepdims=True)

        @pl.when(i == 0)
        def _():
            dg_ref[...] = part

        @pl.when(i > 0)
        def _():
            dg_ref[...] += part

    row = pl.BlockSpec((tr, D), lambda i: (i, 0))
    return pl.pallas_call(
        body, name="wout_post", grid=(D // tr,), in_specs=[row, row, _full((1, D))], out_specs=[row, _full((1, D))],
        out_shape=[jax.ShapeDtypeStruct((D, D), F32), jax.ShapeDtypeStruct((1, D), F32)],
        compiler_params=_cp(("arbitrary",)),
    )(gmat, w_out, gate)


def _pool_bwd_a(dy, proj, pool_w, pool_scale, tb=256):
    S = proj.shape[0]

    def body(dy_ref, u_ref, uh_ref, z_ref, w_ref, sc_ref, dmix_ref, dz_ref, dsc_ref, dw_ref, ubuf):
        i = pl.program_id(0)
        mixed = [m.astype(BF16) for m in _pool_mix(u_ref, uh_ref, ubuf, i, tb)]
        m = jnp.concatenate([_dot(mixed[g], w_ref[g]) for g in range(4)], axis=1)
        dyv, z, sc = dy_ref[...].astype(F32), z_ref[...].astype(F32), sc_ref[...]
        dyp = dyv * _silu(z)
        dz_ref[...] = (dyv * (m * sc) * _dsilu(z)).astype(BF16)
        dsc = jnp.sum(dyp * m, axis=0, keepdims=True)
        dmm = (dyp * sc).astype(BF16)
        dws = []
        for g in range(4):
            cs = slice(g * POOL_GROUP, (g + 1) * POOL_GROUP)
            dmix_ref[:, cs] = _dot_nt(dmm[:, cs], w_ref[g])
            dws.append(_dot_tn(mixed[g], dmm[:, cs]))

        @pl.when(i == 0)
        def _():
            dsc_ref[...] = dsc
            for g in range(4):
                dw_ref[g] = dws[g]

        @pl.when(i > 0)
        def _():
            dsc_ref[...] += dsc
            for g in range(4):
                dw_ref[g] += dws[g]

    u, uh, z = _pool_specs(tb)
    row = pl.BlockSpec((tb, 1024), lambda i: (i, 0))
    wfull = _full((4, 256, 256))
    return pl.pallas_call(
        body, name="pool_bwd_a", grid=(S // tb,),
        in_specs=[row, u, uh, z, wfull, _full((1, 1024))],
        out_specs=[row, row, _full((1, 1024)), wfull],
        out_shape=[jax.ShapeDtypeStruct((S, 1024), F32), jax.ShapeDtypeStruct((S, 1024), BF16),
                   jax.ShapeDtypeStruct((1, 1024), F32), jax.ShapeDtypeStruct((4, 256, 256), F32)],
        scratch_shapes=[pltpu.VMEM((tb + POOL_HALO, 1024), F32)],
        compiler_params=_cp(("arbitrary",)),
    )(dy, proj, proj, proj, pool_w, pool_scale)


def _pool_bwd_b(dmix, tb=256):
    S = dmix.shape[0]
    H = POOL_HALO
    nb = S // tb

    def body(dm_ref, dh_ref, du_ref, ebuf):
        i = pl.program_id(0)
        t = i * tb + lax.broadcasted_iota(jnp.int32, (tb, 1), 0)
        th = (i + 1) * tb + lax.broadcasted_iota(jnp.int32, (H, 1), 0)
        for g, w in enumerate(POOL_WINDOWS):
            cs = slice(g * POOL_GROUP, (g + 1) * POOL_GROUP)
            ebuf[:tb, cs] = dm_ref[:, cs] / jnp.minimum(t + 1, w).astype(F32)
            eh = dh_ref[:, cs] / jnp.minimum(th + 1, w).astype(F32)
            ebuf[tb:, cs] = jnp.where(i < nb - 1, eh, 0.0)
        for g, w in enumerate(POOL_WINDOWS):
            cs = slice(g * POOL_GROUP, (g + 1) * POOL_GROUP)
            acc = ebuf[:tb, cs]
            for j in range(1, w):
                acc = acc + ebuf[pl.ds(j, tb), cs]
            du_ref[:, cs] = (acc - dm_ref[:, cs]).astype(BF16)

    row = pl.BlockSpec((tb, 1024), lambda i: (i, 0))
    nxt = pl.BlockSpec((H, 1024), lambda i: (jnp.minimum((i + 1) * (tb // H), S // H - 1), 0))
    return pl.pallas_call(
        body, name="pool_bwd_b", grid=(nb,), in_specs=[row, nxt], out_specs=row,
        out_shape=jax.ShapeDtypeStruct((S, 1024), BF16),
        scratch_shapes=[pltpu.VMEM((tb + H, 1024), F32)],
        compiler_params=_cp(("parallel",)),
    )(dmix, dmix)


def _attn_bwd(dy, o, proj, sink_b):
    S = proj.shape[0]
    TQ, NK = ATTN_TQ, ATTN_TQ + ATTN_BACK
    nb = S // TQ
    G = N_HEADS // N_KV

    def body(dy_ref, o_ref, q_ref, z_ref, k_ref, kh_ref, v_ref, vh_ref, sink_ref, dq_ref, dz_ref, dk_hbm, dv_hbm, ds_ref,
             dk_acc, dv_acc, kbuf, vbuf):
        i = pl.program_id(0)
        _kv_window(k_ref, kh_ref, v_ref, vh_ref, kbuf, vbuf, i)

        @pl.when(i == 0)
        def _():
            dk_acc[...] = jnp.zeros_like(dk_acc)
            dv_acc[...] = jnp.zeros_like(dv_acc)
            ds_ref[...] = jnp.zeros_like(ds_ref)

        start = pl.multiple_of(i * TQ, TQ)
        valid = _attn_mask(i)
        dks, dvs = [], []
        for kh in range(N_KV):
            ks = slice(kh * HEAD_DIM, (kh + 1) * HEAD_DIM)
            kw = kbuf[:, ks]
            vw = vbuf[:, ks]
            dk_sum = jnp.zeros((NK, HEAD_DIM), F32)
            dv_sum = jnp.zeros((NK, HEAD_DIM), F32)
            dqs = []
            for gi in range(G):
                h = kh * G + gi
                hs = slice(h * HEAD_DIM, (h + 1) * HEAD_DIM)
                pn, psink = _attn_probs(q_ref, kw, sink_ref, valid, h)
                ov = o_ref[:, hs].astype(F32)
                do = dy_ref[:, hs].astype(F32) * _silu(z_ref[:, hs].astype(F32))
                delta = jnp.sum(do * ov, axis=-1, keepdims=True)
                dob = do.astype(BF16)
                dp = _dot_nt(dob, vw)
                ds = (pn * (dp - delta)).astype(BF16)
                dsink = -jnp.sum(psink * delta, axis=0, keepdims=True)
                ds_ref[h:h + 1, :] += jnp.broadcast_to(dsink, (1, LANE))
                dqs.append(_dot(ds, kw) * (HEAD_DIM ** -0.5))
                dk_sum = dk_sum + _dot_tn(ds, q_ref[:, hs])
                dv_sum = dv_sum + _dot_tn(pn.astype(BF16), dob)
            gs = slice(kh * G * HEAD_DIM, (kh + 1) * G * HEAD_DIM)
            dq_ref[:, gs] = jnp.concatenate(dqs, axis=1).astype(BF16)
            z = z_ref[:, gs].astype(F32)
            dz_ref[:, gs] = (dy_ref[:, gs].astype(F32) * o_ref[:, gs].astype(F32) * _dsilu(z)).astype(BF16)
            dks.append(dk_sum * (HEAD_DIM ** -0.5))
            dvs.append(dv_sum)
        dk_acc[pl.ds(start, NK), :] += jnp.concatenate(dks, axis=1)
        dv_acc[pl.ds(start, NK), :] += jnp.concatenate(dvs, axis=1)

        @pl.when(i == nb - 1)
        def _():
            pltpu.sync_copy(dk_acc, dk_hbm)
            pltpu.sync_copy(dv_acc, dv_hbm)

    row = pl.BlockSpec((TQ, 1024), lambda i: (i, 0))
    return pl.pallas_call(
        body, name="attn_bwd", grid=(nb,),
        in_specs=[row, row, _seg(TQ, 1024, OFF_Q), _seg(TQ, 1024, OFF_AZ), *_kv_specs(), _full((N_HEADS, LANE))],
        out_specs=[row, row, ANY, ANY, _full((N_HEADS, LANE))],
        out_shape=[jax.ShapeDtypeStruct((S, 1024), BF16), jax.ShapeDtypeStruct((S, 1024), BF16),
                   jax.ShapeDtypeStruct((S + ATTN_BACK, 256), F32), jax.ShapeDtypeStruct((S + ATTN_BACK, 256), F32),
                   jax.ShapeDtypeStruct((N_HEADS, LANE), F32)],
        scratch_shapes=[pltpu.VMEM((S + ATTN_BACK, 256), F32)] * 2 + [pltpu.VMEM((NK, 256), BF16)] * 2,
        compiler_params=_cp(("arbitrary",), VMEM_BIG),
    )(dy, o, proj, proj, proj, proj, proj, proj, sink_b)


def _conv_bwd_a(dy, proj, dw, dw_b, ln_g, ln_b, pw, tb=256):
    S = proj.shape[0]
    H = CONV_HALO

    def body(dy_ref, a_ref, ah_ref, b_ref, bh_ref, z_ref, dw_ref, dwb_ref, lg_ref, lb_ref, pw_ref,
             dcv_ref, dz_ref, dpw_ref, dlg_ref, dlb_ref, ddwb_ref, ddw_ref, gbuf, ybuf):
        i = pl.program_id(0)
        _conv_glu_dw(a_ref, ah_ref, b_ref, bh_ref, dw_ref, gbuf, ybuf, i, tb)
        lg = lg_ref[...]
        xh, rstd, yn = _layer_norm_fwd(ybuf[...] + dwb_ref[...], lg, lb_ref[...])
        u = _silu(yn).astype(BF16)
        out = _dot(u, pw_ref[...])
        dyv, z = dy_ref[...].astype(F32), z_ref[...].astype(F32)
        dz_ref[...] = (dyv * out * _dsilu(z)).astype(BF16)
        dout = (dyv * _silu(z)).astype(BF16)
        dpw = _dot_tn(u, dout)
        dyn = _dot_nt(dout, pw_ref[...]) * _dsilu(yn)
        dlg = jnp.sum(dyn * xh, axis=0, keepdims=True)
        dlb = jnp.sum(dyn, axis=0, keepdims=True)
        dxh = dyn * lg
        dcv = rstd * (dxh - jnp.mean(dxh, axis=-1, keepdims=True) - xh * jnp.mean(dxh * xh, axis=-1, keepdims=True))
        dcv_ref[...] = dcv
        ddwb = jnp.sum(dcv, axis=0, keepdims=True)
        ybuf[...] = dcv

        @pl.when(i == 0)
        def _():
            dpw_ref[...], dlg_ref[...], dlb_ref[...], ddwb_ref[...] = dpw, dlg, dlb, ddwb
            ddw_ref[...] = jnp.zeros_like(ddw_ref)

        @pl.when(i > 0)
        def _():
            dpw_ref[...] += dpw
            dlg_ref[...] += dlg
            dlb_ref[...] += dlb
            ddwb_ref[...] += ddwb

        _lane_chunks(lambda cs: _taps_reduce(gbuf, ybuf, ddw_ref, tb, CONV_TAPS, cs))

    vec = _full((1, 1024))
    row = pl.BlockSpec((tb, 1024), lambda i: (i, 0))
    big = _full((1024, 1024))
    return pl.pallas_call(
        body, name="conv_bwd_a", grid=(S // tb,),
        in_specs=[row, *_conv_specs(tb), _full((32, 1024)), vec, vec, vec, big],
        out_specs=[row, row, big, vec, vec, vec, _full((32, 1024))],
        out_shape=[jax.ShapeDtypeStruct((S, 1024), F32), jax.ShapeDtypeStruct((S, 1024), BF16),
                   jax.ShapeDtypeStruct((1024, 1024), F32), jax.ShapeDtypeStruct((1, 1024), F32),
                   jax.ShapeDtypeStruct((1, 1024), F32), jax.ShapeDtypeStruct((1, 1024), F32),
                   jax.ShapeDtypeStruct((32, 1024), F32)],
        scratch_shapes=[pltpu.VMEM((tb + H, 1024), F32), pltpu.VMEM((tb, 1024), F32)],
        compiler_params=_cp(("arbitrary",)),
    )(dy, proj, proj, proj, proj, proj, dw, dw_b, ln_g, ln_b, pw)


def _conv_bwd_b(dcv, proj, dw, tb=256):
    S = proj.shape[0]
    H = CONV_HALO
    nb = S // tb

    def body(d_ref, dn_ref, a_ref, b_ref, dw_ref, da_ref, db_ref, dbuf, gbuf):
        i = pl.program_id(0)
        dbuf[:tb, :] = d_ref[...]
        dbuf[tb:, :] = jnp.where(i < nb - 1, dn_ref[...], 0.0)
        _lane_chunks(lambda cs: _taps_apply(dbuf, dw_ref, gbuf, tb, CONV_TAPS_T, cs))
        dg = gbuf[...]
        a, s = a_ref[...].astype(F32), _sig(b_ref[...].astype(F32))
        da_ref[...] = (dg * s).astype(BF16)
        db_ref[...] = (dg * a * s * (1.0 - s)).astype(BF16)

    row = pl.BlockSpec((tb, 1024), lambda i: (i, 0))
    nxt = pl.BlockSpec((H, 1024), lambda i: (jnp.minimum((i + 1) * (tb // H), S // H - 1), 0))
    return pl.pallas_call(
        body, name="conv_bwd_b", grid=(nb,),
        in_specs=[row, nxt, _seg(tb, 1024, OFF_CA), _seg(tb, 1024, OFF_CB), _full((32, 1024))],
        out_specs=[row, row], out_shape=[jax.ShapeDtypeStruct((S, 1024), BF16)] * 2,
        scratch_shapes=[pltpu.VMEM((tb + H, 1024), F32), pltpu.VMEM((tb, 1024), F32)],
        compiler_params=_cp(("parallel",)),
    )(dcv, dcv, proj, proj, dw)


def _norm_bwd(dh, x, dx_out, g, scale, tb=256):
    S = x.shape[0]

    def body(dh_ref, x_ref, dxo_ref, g_ref, sc_ref, dx_ref, dsh_ref, da_ref):
        i = pl.program_id(0)
        xv, dhv = x_ref[...], dh_ref[...]
        r = lax.rsqrt(jnp.mean(xv * xv, axis=-1, keepdims=True) + EPS)
        xh = xv * r
        gy = dhv * (g_ref[...] * (1.0 + sc_ref[...]))
        dx_ref[...] = dxo_ref[...] + r * (gy - xh * jnp.mean(gy * xh, axis=-1, keepdims=True))
        dsh = jnp.sum(dhv, axis=0, keepdims=True)
        da = jnp.sum(dhv * xh, axis=0, keepdims=True)

        @pl.when(i == 0)
        def _():
            dsh_ref[...], da_ref[...] = dsh, da

        @pl.when(i > 0)
        def _():
            dsh_ref[...] += dsh
            da_ref[...] += da

    row = pl.BlockSpec((tb, D), lambda i: (i, 0))
    vec = _full((1, D))
    return pl.pallas_call(
        body, name="norm_bwd", grid=(S // tb,), in_specs=[row, row, row, vec, vec], out_specs=[row, vec, vec],
        out_shape=[jax.ShapeDtypeStruct((S, D), F32), jax.ShapeDtypeStruct((1, D), F32), jax.ShapeDtypeStruct((1, D), F32)],
        compiler_params=_cp(("arbitrary",)),
    )(dh, x, dx_out, g, scale)


def _mod_bwd(d_a, norm_g, scale):
    def body(da_ref, g_ref, sc_ref, dg_ref, dsc_ref):
        dg_ref[...] = da_ref[...] * (1.0 + sc_ref[...])
        dsc_ref[...] = da_ref[...] * g_ref[...]

    return pl.pallas_call(body, name="mod_bwd", out_shape=[jax.ShapeDtypeStruct(d_a.shape, F32)] * 2)(d_a, norm_g, scale)


def _reduce_adamw(parts, w, m, v, name):
    L, rows, C = w.shape
    tr = rows if rows % 64 else 64

    def body(*refs):
        p_refs, (w_ref, m_ref, v_ref, g_ref, dl_ref, m2_ref, v2_ref) = refs[:L * N_DEV], refs[L * N_DEV:]
        for l in range(L):
            g = p_refs[l * N_DEV][0].astype(F32)
            for k in range(1, N_DEV):
                g = g + p_refs[l * N_DEV + k][0].astype(F32)
            g_ref[l] = g
            dl_ref[l], m2_ref[l], v2_ref[l] = _adam_math(g, w_ref[l], m_ref[l], v_ref[l])

    slot = lambda k: pl.BlockSpec((1, tr, C), lambda i: (k, i, 0))
    blk = pl.BlockSpec((L, tr, C), lambda i: (0, i, 0))
    return pl.pallas_call(
        body, name=name, grid=(rows // tr,), in_specs=[slot(k) for _ in range(L) for k in range(N_DEV)] + [blk] * 3,
        out_specs=[blk] * 4, out_shape=[jax.ShapeDtypeStruct((L, rows, C), F32)] * 4,
        compiler_params=_cp(("parallel",)),
    )(*[p for p in parts for _ in range(N_DEV)], w, m, v)


def _small_final(parts, w, m, v):
    R = w.shape[0]

    def body(p_ref, w_ref, m_ref, v_ref, g_ref, dl_ref, m2_ref, v2_ref):
        g = p_ref[0]
        for k in range(1, N_DEV):
            g = g + p_ref[k]
        delta, m2, v2 = _adam_math(g, w_ref[...], m_ref[...], v_ref[...])
        g_ref[...], dl_ref[...], m2_ref[...], v2_ref[...] = g, delta, m2, v2

    return pl.pallas_call(body, name="small_final", out_shape=[jax.ShapeDtypeStruct((R, LANE), F32)] * 4)(parts, w, m, v)


def _layer_fwd(x, mod, small, W, more_w):
    shift, scale, gate = mod
    h = _norm_mod(x, small["norm_g"], scale, shift)
    proj = _mm(h, W["w_in_t"], BF16, 512, 1536, name="proj_mm", nt=True)
    W.update(more_w(proj))
    y_pool = _pool_fwd(proj, W["pool_w"], small["pool_scale"])
    o, y_attn = _attn_fwd(proj, small["sink_b"])
    y_conv = _conv_fwd(proj, W["conv_dw"], small["conv_dw_b"], small["conv_ln_g"], small["conv_ln_b"], W["conv_pw"])
    merged, pp, pa, pc = _merge_fwd(y_pool, y_attn, y_conv, W["wbp"], W["wba"], W["wbc"], proj)
    x_new = _out_fwd(x, merged, W["w_out"], gate)
    stash = dict(x=x, h=h, proj=proj, o=o, y_pool=y_pool, y_attn=y_attn, y_conv=y_conv,
                 merged=merged, pp=pp, pa=pa, pc=pc)
    return x_new, stash


def _layer_bwd(dx, st, mod, small, W, put):
    shift, scale, gate = mod
    proj = st["proj"]
    gmat = _mm_tn(st["merged"], dx, 1024, 1024, 1024, name="wout_tn")
    d_w_out, d_gate = _wout_post(gmat, W["w_out"], gate)
    dpp, dpa, dpc, dgp, dga, dgc = _out_bwd(dx, gate, W["w_out"], st["pp"], st["pa"], st["pc"], proj)
    dy_pool = _mm(dpp, W["wbp"], BF16, 512, 1024, name="branch_bwd_mm", nt=True)
    dy_attn = _mm(dpa, W["wba"], BF16, 512, 1024, name="branch_bwd_mm", nt=True)
    dy_conv = _mm(dpc, W["wbc"], BF16, 512, 1024, name="branch_bwd_mm", nt=True)
    d_wbp = _mm_tn(st["y_pool"], dpp, 1024, 1024, 1024, name="branch_tn")
    d_wba = _mm_tn(st["y_attn"], dpa, 1024, 1024, 1024, name="branch_tn")
    d_wbc = _mm_tn(st["y_conv"], dpc, 1024, 1024, 1024, name="branch_tn")

    dmix, dz_pool, d_pool_scale, d_pool_w = _pool_bwd_a(dy_pool, proj, W["pool_w"], small["pool_scale"])
    du = _pool_bwd_b(dmix)
    dq, dz_attn, dk, dv, d_sink = _attn_bwd(dy_attn, st["o"], proj, small["sink_b"])
    dcv, dz_conv, d_pw, d_ln_g, d_ln_b, d_dw_b, d_dw = _conv_bwd_a(
        dy_conv, proj, W["conv_dw"], small["conv_dw_b"], small["conv_ln_g"], small["conv_ln_b"], W["conv_pw"])
    da, db = _conv_bwd_b(dcv, proj, W["conv_dw"])

    tok = put(dict(pool_w=d_pool_w, conv_dw=d_dw[:CONV_K], conv_pw=d_pw, wbp=d_wbp, wba=d_wba, wbc=d_wbc, w_out=d_w_out))
    dproj = jnp.concatenate([du, dz_pool, dq, dk[ATTN_BACK:].astype(BF16), dv[ATTN_BACK:].astype(BF16), dz_attn, da, db,
                             dz_conv, dgp, dga, dgc], axis=1)
    d_w_in_t = _mm_tn(dproj, st["h"], 768, 2048, 1024, name="win_tn", vmem=VMEM_BIG, after=tok)
    tok = put(dict(w_in=d_w_in_t))
    dh = _mm(dproj, W["w_in_t"], F32, 1024, 2048, 1536, name="dh_mm", vmem=VMEM_BIG, after=tok)
    dx_in, d_shift, d_a = _norm_bwd(dh, st["x"], dx, small["norm_g"], scale)
    sm = dict(d_a=d_a, d_shift=d_shift, d_gate=d_gate, pool_scale=d_pool_scale, attn_sink=d_sink[:, 0],
              conv_dw_b=d_dw_b, conv_ln_g=d_ln_g, conv_ln_b=d_ln_b)
    return dx_in, sm


def _local_step(x, target, mods, smalls, get_w, final_g, put_g, end_layer):
    stashes, Ws = [], []
    for l in range(DEPTH):
        w, more_w = get_w(l, x)
        Ws.append(w)
        x, st = _layer_fwd(x, mods[l], smalls[l], w, more_w)
        stashes.append(st)
    dx, d_final_g, loss_lanes = _final_loss(x, target, final_g)
    sms = [None] * DEPTH
    for l in reversed(range(DEPTH)):
        dx, sms[l] = _layer_bwd(dx, stashes[l], mods[l], smalls[l], Ws[l], functools.partial(put_g, l))
        end_layer(l, dx)
    return loss_lanes, dx, d_final_g, sms


BIG = ("w_in", "pool_w", "conv_pw", "wbp", "wba", "wbc", "w_out")
GRADS = BIG + ("conv_dw",)


def _full_w_in(g):
    return dict(w_in_t=g.reshape(IN_WIDTH, D))


def _full_weights(g, conv_dw):
    cols = lambda a: jnp.transpose(a, (1, 0, 2)).reshape(a.shape[1], -1)
    pool_w = jnp.transpose(g["pool_w"], (1, 0, 2, 3)).reshape(4, 256, 256)
    conv_pw = g["conv_pw"].reshape(1024, 1024)
    wbp, wba, wbc = cols(g["wbp"]), cols(g["wba"]), cols(g["wbc"])
    w_out = g["w_out"].reshape(D, D)
    conv_dw = jnp.pad(cols(conv_dw), ((0, 32 - CONV_K), (0, 0)))
    return dict(pool_w=pool_w, conv_pw=conv_pw, wbp=wbp, wba=wba, wbc=wbc, w_out=w_out, conv_dw=conv_dw)


def _pieces(name, g):
    if name == "w_in":
        return g.reshape(8, IN_WIDTH // 8, D)
    if name == "pool_w":
        return jnp.transpose(g.reshape(4, 8, 32, 256), (1, 0, 2, 3))
    if name == "conv_dw":
        return jnp.transpose(g.reshape(CONV_K, 8, 128), (1, 0, 2))
    if name == "conv_pw":
        return g.reshape(8, 128, 1024)
    if name in ("wbp", "wba", "wbc"):
        return jnp.transpose(g.reshape(1024, 8, 256), (1, 0, 2))
    return g.reshape(8, 256, D)


def _pack_small(items, rows):
    flat = jnp.concatenate([a.reshape(-1).astype(F32) for a in items])
    return jnp.pad(flat, (0, rows * LANE - flat.shape[0])).reshape(rows, LANE)


def _unpack_small(packed, shapes):
    flat, out, off = packed.reshape(-1), [], 0
    for s in shapes:
        n = 1
        for d in s:
            n *= d
        out.append(flat[off:off + n].reshape(s))
        off += n
    return out


def kernel(x, c, norm_g, w_ada, b_ada, w_in, pool_w, pool_scale, attn_sink, conv_dw, conv_dw_b, conv_ln_g, conv_ln_b, conv_pw, w_branch_pool, w_branch_attn, w_branch_conv, w_out, final_g, loss_target, m_norm_g, m_w_ada, m_b_ada, m_w_in, m_pool_w, m_pool_scale, m_attn_sink, m_conv_dw, m_conv_dw_b, m_conv_ln_g, m_conv_ln_b, m_conv_pw, m_w_branch_pool, m_w_branch_attn, m_w_branch_conv, m_w_out, m_final_g, v_norm_g, v_w_ada, v_b_ada, v_w_in, v_pool_w, v_pool_scale, v_attn_sink, v_conv_dw, v_conv_dw_b, v_conv_ln_g, v_conv_ln_b, v_conv_pw, v_w_branch_pool, v_w_branch_attn, v_w_branch_conv, v_w_out, v_final_g):
    L = DEPTH
    me = 4 * lax.axis_index("x") + 2 * lax.axis_index("y") + lax.axis_index("c")
    tr = lambda a: jnp.swapaxes(a, 1, 2)
    shards = dict(w_in=tr(w_in), pool_w=pool_w, conv_dw=conv_dw, conv_pw=conv_pw, wbp=w_branch_pool, wba=w_branch_attn,
                  wbc=w_branch_conv, w_out=w_out)
    moms = dict(w_in=(tr(m_w_in), tr(v_w_in)), pool_w=(m_pool_w, v_pool_w), conv_dw=(m_conv_dw, v_conv_dw),
                conv_pw=(m_conv_pw, v_conv_pw), wbp=(m_w_branch_pool, v_w_branch_pool),
                wba=(m_w_branch_attn, v_w_branch_attn), wbc=(m_w_branch_conv, v_w_branch_conv), w_out=(m_w_out, v_w_out))

    n_cd = L * CONV_K * 128
    first = _all_gather([_pack_small([c, conv_dw], 144)], "gather_c")[0].reshape(N_DEV, -1)
    c_all = first[:, :D]
    conv_dw_all = first[:, D:D + n_cd].reshape(N_DEV, L, CONV_K, 128)

    mod_part = _mod_fwd(c_all, w_ada)
    mod_all = _all_gather([mod_part.reshape(-1, LANE)], "gather_mod")[0].reshape(N_DEV, L, N_DEV, -1)
    mod = jnp.transpose(lax.dynamic_index_in_dim(mod_all, me, axis=2, keepdims=False), (1, 0, 2)).reshape(L, 3 * D)
    mod = mod + b_ada
    mods = [(mod[l:l + 1, :D], mod[l:l + 1, D:2 * D], mod[l:l + 1, 2 * D:]) for l in range(L)]

    w_in_0 = _all_gather([(shards["w_in"][0] + mod[0, 0] * 0.0).astype(BF16)], "gather_w_in_0")[0]
    gathers, tok = [], w_in_0[0, 0, 0].astype(F32) * 0.0
    for l in range(L):
        first = lambda a: (a + tok).astype(BF16)
        mine = [first(shards["w_in"][l])], [first(shards[k][l]) if k == BIG[1] else shards[k][l].astype(BF16) for k in BIG[1:]]
        started = [None if (l, n) == (0, 0) else _spread_start(v, False, f"gather_start_{l}_{n}") for n, v in enumerate(mine)]
        gathers.append((mine, started))
        tok = sum(st[-1][0, 0] for st in started if st is not None)
    mods[0] = (mods[0][0] + tok,) + mods[0][1:]

    sink_b = jnp.broadcast_to(attn_sink[:, :, None], (L, N_HEADS, LANE))
    smalls = [dict(norm_g=norm_g[l:l + 1], pool_scale=pool_scale[l:l + 1], sink_b=sink_b[l], conv_dw_b=conv_dw_b[l:l + 1],
                   conv_ln_g=conv_ln_g[l:l + 1], conv_ln_b=conv_ln_b[l:l + 1]) for l in range(L)]

    def with_mine(landed, mine):
        return lax.dynamic_update_slice(landed, mine, (me,) + (0,) * (landed.ndim - 1))

    def get_w(l, x_in):
        mine, started = gathers[l]
        if l == 0:
            w_in = w_in_0
        else:
            w_in = with_mine(_spread_wait(started[0], x_in, False, f"gather_wait_{l}_0")[0], mine[0][0][None])

        def more_w(proj):
            landed = _spread_wait(started[1], proj, False, f"gather_wait_{l}_1")
            g = {k: with_mine(a, b[None]) for k, a, b in zip(BIG[1:], landed, mine[1])}
            return _full_weights(g, conv_dw_all[:, l])

        return _full_w_in(w_in), more_w

    pending, parts = {l: [] for l in range(L)}, [dict() for _ in range(L)]

    def put_g(l, grads):
        names = tuple(k for k in GRADS if k in grads)
        pieces = [_pieces(k, grads[k]).astype(BF16) for k in names]
        started = _spread_start(pieces, True, f"scatter_start_{l}_{len(pending[l])}")
        pending[l].append((names, pieces, started))
        return started[-1][0:1, 0:1]

    def finish(l, after):
        for n, (names, pieces, started) in enumerate(pending.pop(l)):
            landed = _spread_wait(started, after, True, f"scatter_wait_{l}_{n}")
            for k, a, b in zip(names, landed, pieces):
                parts[l][k] = with_mine(a, lax.dynamic_slice_in_dim(b, me, 1, axis=0))

    def end_layer(l, dx):
        if l + 1 in pending:
            finish(l + 1, dx)

    loss_lanes, grad_x, d_final_g, sms = _local_step(x[0], loss_target[0], mods, smalls, get_w, final_g.reshape(1, D),
                                                     put_g, end_layer)
    finish(0, grad_x)

    stack = lambda k: jnp.concatenate([sms[l][k].reshape(1, -1) for l in range(L)], axis=0)
    scale_all = jnp.concatenate([mods[l][1] for l in range(L)], axis=0)
    d_norm_g, d_scale = _mod_bwd(stack("d_a"), norm_g, scale_all)
    dmod = jnp.concatenate([stack("d_shift"), d_scale, stack("d_gate")], axis=1)
    small_names = ("norm_g", "b_ada", "pool_scale", "attn_sink", "conv_dw_b", "conv_ln_g", "conv_ln_b", "final_g")
    small_g = (d_norm_g, dmod, stack("pool_scale"), stack("attn_sink"), stack("conv_dw_b"), stack("conv_ln_g"),
               stack("conv_ln_b"), d_final_g.reshape(D))
    small_w = (norm_g, b_ada, pool_scale, attn_sink, conv_dw_b, conv_ln_g, conv_ln_b, final_g)
    small_m = (m_norm_g, m_b_ada, m_pool_scale, m_attn_sink, m_conv_dw_b, m_conv_ln_g, m_conv_ln_b, m_final_g)
    small_v = (v_norm_g, v_b_ada, v_pool_scale, v_attn_sink, v_conv_dw_b, v_conv_ln_g, v_conv_ln_b, v_final_g)
    shapes = [a.shape for a in small_w] + [(D,)]
    n_small = sum(a.size for a in small_w) + D
    R = -(-n_small // (8 * LANE)) * 8
    zero = jnp.zeros((D,), F32)
    small_parts = _all_gather([_pack_small(small_g + (loss_lanes,), R)], "gather_small")[0]
    sg, sd, sm2, sv2 = _small_final(small_parts, _pack_small(small_w + (zero,), R), _pack_small(small_m + (zero,), R),
                                    _pack_small(small_v + (zero + 1.0,), R))
    sg, sd, sm2, sv2 = (_unpack_small(a, shapes) for a in (sg, sd, sm2, sv2))
    loss = jnp.sum(sg[-1])
    res = {n: (sg[i], sd[i], sm2[i], sv2[i]) for i, n in enumerate(small_names)}

    off = norm_g.size
    dmod_all = small_parts.reshape(N_DEV, -1)[:, off:off + L * 3 * D].reshape(N_DEV, L, 3 * D)
    dmod_mine = jnp.transpose(lax.dynamic_slice_in_dim(dmod_all, me * (3 * D // N_DEV), 3 * D // N_DEV, axis=2), (1, 0, 2))
    res["w_ada"] = _wada_bwd(c_all.T, dmod_mine, w_ada, m_w_ada, v_w_ada)

    for k in GRADS:
        shp = shards[k].shape
        to3d = lambda a: a.reshape(L, -1, shp[-1])
        out = _reduce_adamw([parts[l][k].reshape(N_DEV, -1, shp[-1]) for l in range(L)], to3d(shards[k]),
                            to3d(moms[k][0]), to3d(moms[k][1]), "adamw_" + k)
        res[k] = tuple(a.reshape(shp) for a in out)
    res["w_in"] = tuple(tr(a) for a in res["w_in"])

    order = ("norm_g", "w_ada", "b_ada", "w_in", "pool_w", "pool_scale", "attn_sink", "conv_dw", "conv_dw_b", "conv_ln_g",
             "conv_ln_b", "conv_pw", "wbp", "wba", "wbc", "w_out", "final_g")
    outs = [loss, grad_x[None]]
    for j in range(4):
        outs += [res[n][j] for n in order]
    return tuple(outs)
```

```python
import functools

import jax
import jax.numpy as jnp
from jax import lax
from jax.experimental import pallas as pl
from jax.experimental.pallas import tpu as pltpu

F32, BF16 = jnp.float32, jnp.bfloat16
MESH = pl.DeviceIdType.MESH
ANY = pl.BlockSpec(memory_space=pl.ANY)

N_DEV = 8
D = 2048
DEPTH = 4
EPS = 1e-6
IN_WIDTH = 13824
POOL_WINDOWS = (2, 4, 8, 16)
POOL_GROUP = 256
POOL_HALO = 16
CONV_K = 31
CONV_HALO = 32
N_HEADS, N_KV, HEAD_DIM = 16, 4, 64
ATTN_TQ = 256
ATTN_BACK = 128
LANE = 128
VMEM_BIG = 56 * 1024 * 1024

OFF_U, OFF_Z, OFF_Q, OFF_K, OFF_V, OFF_AZ, OFF_CA, OFF_CB, OFF_CZ = 0, 1024, 2048, 3072, 3328, 3584, 4608, 5632, 6656
OFF_GP, OFF_GA, OFF_GC = 7680, 9728, 11776


def _seg(rows, width, off, first_row=None):
    start = (lambda i: i * rows) if first_row is None else first_row
    return pl.BlockSpec((pl.Element(rows), pl.Element(width)), lambda i: (pl.multiple_of(start(i), rows), off))


def _seg2(tm, tn, off):
    return pl.BlockSpec((pl.Element(tm), pl.Element(tn)), lambda j, i: (i * tm, pl.multiple_of(off + j * tn, LANE)))

ADAM_LR, ADAM_B1, ADAM_B2, ADAM_EPS, ADAM_WD, ADAM_STEP = 0.001, 0.9, 0.999, 1e-08, 0.01, 10


def _cp(sem=None, vmem=None):
    return pltpu.CompilerParams(dimension_semantics=sem, vmem_limit_bytes=vmem)


def _sig(x):
    return jax.nn.sigmoid(x)


def _silu(x):
    return x * _sig(x)


def _dsilu(x):
    s = _sig(x)
    return s * (1.0 + x * (1.0 - s))


def _dot(a, b):
    return jnp.dot(a, b, preferred_element_type=F32)


def _dot_tn(a, b):
    return lax.dot_general(a, b, (((0,), (0,)), ((), ())), preferred_element_type=F32)


def _dot_nt(a, b):
    return lax.dot_general(a, b, (((1,), (1,)), ((), ())), preferred_element_type=F32)


def _full(shape):
    n = len(shape)
    return pl.BlockSpec(shape, lambda *_: (0,) * n)


def _my_pos():
    return lax.axis_index("x"), lax.axis_index("y"), lax.axis_index("c")


def _all_gather(xs, name):
    n = len(xs)

    def body(*refs):
        x_refs, o_refs = refs[:n], refs[n:2 * n]
        send_sems, recv_sems, local_sems = refs[2 * n:]
        x, y, c = _my_pos()
        sibling = (x, y, 1 - c)
        chips = [(1 - x, y), (x, 1 - y), (1 - x, 1 - y)]
        me = 4 * x + 2 * y + c

        def slot(px, py, pc):
            return 4 * px + 2 * py + pc

        def copy(t, k, block, to, src=None):
            dst = o_refs[t].at[block]
            return pltpu.make_async_remote_copy(
                src_ref=dst if src is None else src, dst_ref=dst,
                send_sem=send_sems.at[t, k], recv_sem=recv_sems.at[t, k],
                device_id=to, device_id_type=MESH)

        mine = [pltpu.make_async_copy(x_refs[t], o_refs[t].at[me], local_sems.at[t]) for t in range(n)]
        for cp in mine:
            cp.start()
        first = []
        for t in range(n):
            first.append(copy(t, 0, me, sibling, src=x_refs[t]))
            for j, chip in enumerate(chips):
                first.append(copy(t, 1 + j, me, (*chip, c), src=x_refs[t]))
        for cp in first:
            cp.start()
        passed = []
        for j, chip in enumerate(chips):
            for t in range(n):
                copy(t, 1 + j, slot(*chip, c), (x, y, c)).wait_recv()
                fwd = copy(t, 4 + j, slot(*chip, c), sibling)
                fwd.start()
                passed.append(fwd)
        for t in range(n):
            copy(t, 0, slot(x, y, 1 - c), (x, y, c)).wait_recv()
            for j, chip in enumerate(chips):
                copy(t, 4 + j, slot(*chip, 1 - c), (x, y, c)).wait_recv()
        for cp in first + passed:
            cp.wait_send()
        for cp in mine:
            cp.wait()

    return pl.pallas_call(
        body, name=name,
        out_shape=[jax.ShapeDtypeStruct((N_DEV,) + a.shape, a.dtype) for a in xs],
        in_specs=[ANY] * n, out_specs=[ANY] * n,
        scratch_shapes=[pltpu.SemaphoreType.DMA((n, 7)), pltpu.SemaphoreType.DMA((n, 7)),
                        pltpu.SemaphoreType.DMA((n,))],
    )(*xs)


N_PEER = N_DEV - 1
HBM = pl.BlockSpec(memory_space=pltpu.HBM)
SEM = pl.BlockSpec(memory_space=pltpu.SEMAPHORE)
EFFECT = pltpu.SideEffectType.DATAFLOW_SIDE_EFFECTING


def _peer(k):
    x, y, c = _my_pos()
    flip = lambda v, bit: 1 - v if bit else v
    return flip(x, (k >> 2) & 1), flip(y, (k >> 1) & 1), flip(c, k & 1)


def _spread_copies(v_ref, land_ref, send_sems, recv_sems, per_peer):
    x, y, c = _my_pos()
    me = 4 * x + 2 * y + c
    copies = []
    for k in range(1, N_DEV):
        px, py, pc = _peer(k)
        src = v_ref.at[4 * px + 2 * py + pc] if per_peer else v_ref
        copies.append(pltpu.make_async_remote_copy(
            src_ref=src, dst_ref=land_ref.at[me], send_sem=send_sems[k - 1], recv_sem=recv_sems[k - 1],
            device_id=(px, py, pc), device_id_type=MESH))
    return copies


def _spread_start(vs, per_peer, name):
    n = len(vs)
    lands = [(N_DEV,) + (v.shape[1:] if per_peer else v.shape) for v in vs]
    n_sem = 2 * N_PEER * n

    def body(*refs):
        v_refs, land_refs, outs = refs[:n], refs[n:2 * n], refs[2 * n:]
        for t in range(n):
            sems = outs[2 * N_PEER * t:2 * N_PEER * (t + 1)]
            for cp in _spread_copies(v_refs[t], land_refs[t], sems[:N_PEER], sems[N_PEER:], per_peer):
                cp.start()
        token = outs[n_sem + 2 * n]
        token[...] = jnp.zeros_like(token)

    hbm = lambda a: pltpu.with_memory_space_constraint(a, pltpu.HBM)
    return pl.pallas_call(
        body, name=name,
        out_shape=((pltpu.SemaphoreType.DMA(()),) * n_sem + tuple(pltpu.HBM(v.shape, v.dtype) for v in vs)
                   + tuple(pltpu.HBM(s, v.dtype) for s, v in zip(lands, vs)) + (jax.ShapeDtypeStruct((8, LANE), F32),)),
        in_specs=(HBM,) * (2 * n), out_specs=(SEM,) * n_sem + (HBM,) * (2 * n) + (pl.BlockSpec(memory_space=pltpu.VMEM),),
        input_output_aliases={t: n_sem + t for t in range(2 * n)},
        compiler_params=pltpu.CompilerParams(has_side_effects=EFFECT),
    )(*[hbm(v) for v in vs], *[hbm(lax.empty(s, v.dtype)) for s, v in zip(lands, vs)])


def _spread_wait(started, after, per_peer, name):
    n = (len(started) - 1) // (2 * N_PEER + 2)
    n_sem = 2 * N_PEER * n
    sems, thru = started[:n_sem], started[n_sem:n_sem + 2 * n]

    def body(*refs):
        v_refs, land_refs, rest = refs[:n], refs[n:2 * n], refs[2 * n:]
        for t in range(n):
            s = rest[2 * N_PEER * t:2 * N_PEER * (t + 1)]
            for cp in _spread_copies(v_refs[t], land_refs[t], s[:N_PEER], s[N_PEER:], per_peer):
                cp.wait_send()
                cp.wait_recv()

    return pl.pallas_call(
        body, name=name,
        out_shape=tuple(pltpu.HBM(a.shape, a.dtype) for a in thru),
        in_specs=(HBM,) * (2 * n) + (SEM,) * n_sem + (ANY,), out_specs=(HBM,) * (2 * n),
        input_output_aliases={t: t for t in range(2 * n)},
        compiler_params=pltpu.CompilerParams(has_side_effects=EFFECT),
    )(*thru, *sems, after)[n:]


def _mm(a, b, out_dtype, tm, tn, tk=None, name="mm", vmem=None, after=None, nt=False):
    M, K = a.shape
    N = b.shape[0] if nt else b.shape[1]
    tk = K if tk is None else tk
    nk = K // tk
    assert M % tm == 0 and N % tn == 0 and K % tk == 0
    dep = () if after is None else (after,)

    def body(*refs):
        a_ref, b_ref, o_ref, *acc = refs[len(dep):]
        prod = (_dot_nt if nt else _dot)(a_ref[...].astype(BF16), b_ref[...])
        if nk == 1:
            o_ref[...] = prod.astype(out_dtype)
            return
        acc_ref = acc[0] if acc else o_ref
        k = pl.program_id(2)

        @pl.when(k == 0)
        def _():
            acc_ref[...] = prod

        @pl.when(k > 0)
        def _():
            acc_ref[...] += prod

        if acc:
            @pl.when(k == nk - 1)
            def _():
                o_ref[...] = acc_ref[...].astype(out_dtype)

    scratch = [pltpu.VMEM((tm, tn), F32)] if (nk > 1 and out_dtype != F32) else []
    b_spec = pl.BlockSpec((tn, tk), lambda j, i, k: (j, k)) if nt else pl.BlockSpec((tk, tn), lambda j, i, k: (k, j))
    return pl.pallas_call(
        body, name=name, grid=(N // tn, M // tm, nk),
        in_specs=[_full((1, 1))] * len(dep) + [pl.BlockSpec((tm, tk), lambda j, i, k: (i, k)), b_spec],
        out_specs=pl.BlockSpec((tm, tn), lambda j, i, k: (i, j)),
        out_shape=jax.ShapeDtypeStruct((M, N), out_dtype), scratch_shapes=scratch,
        compiler_params=_cp(("parallel", "parallel", "arbitrary"), vmem),
    )(*dep, a, b)


def _mm_tn(a, b, tm, tn, ts, name="mm_tn", vmem=None, after=None, out_dtype=F32):
    S, Ka = a.shape
    _, N = b.shape
    assert Ka % tm == 0 and N % tn == 0 and S % ts == 0
    dep = () if after is None else (after,)

    nk = S // ts

    def body(*refs):
        a_ref, b_ref, o_ref, *acc = refs[len(dep):]
        acc_ref = acc[0] if acc else o_ref
        prod = _dot_tn(a_ref[...].astype(BF16), b_ref[...].astype(BF16))
        k = pl.program_id(2)

        @pl.when(k == 0)
        def _():
            acc_ref[...] = prod

        @pl.when(k > 0)
        def _():
            acc_ref[...] += prod

        if acc:
            @pl.when(k == nk - 1)
            def _():
                o_ref[...] = acc_ref[...].astype(out_dtype)

    return pl.pallas_call(
        body, name=name, grid=(Ka // tm, N // tn, S // ts),
        in_specs=[_full((1, 1))] * len(dep) + [pl.BlockSpec((ts, tm), lambda i, j, k: (k, i)),
                                               pl.BlockSpec((ts, tn), lambda i, j, k: (k, j))],
        out_specs=pl.BlockSpec((tm, tn), lambda i, j, k: (i, j)),
        out_shape=jax.ShapeDtypeStruct((Ka, N), out_dtype),
        scratch_shapes=[] if out_dtype == F32 else [pltpu.VMEM((tm, tn), F32)],
        compiler_params=_cp(("parallel", "parallel", "arbitrary"), vmem),
    )(*dep, a, b)


def _mod_fwd(c_all, w_ada):
    L, _, n = w_ada.shape

    def body(c_ref, w_ref, o_ref):
        ca = _silu(c_ref[...])
        o_ref[0] = jnp.dot(ca, w_ref[0], preferred_element_type=F32, precision=lax.Precision.HIGHEST)

    return pl.pallas_call(
        body, name="mod_fwd", grid=(L,),
        in_specs=[_full((N_DEV, D)), pl.BlockSpec((1, D, n), lambda l: (l, 0, 0))],
        out_specs=pl.BlockSpec((1, N_DEV, n), lambda l: (l, 0, 0)),
        out_shape=jax.ShapeDtypeStruct((L, N_DEV, n), F32),
        compiler_params=_cp(("parallel",)),
    )(c_all, w_ada)


def _adam_math(g, w, m, v):
    m2 = ADAM_B1 * m + (1.0 - ADAM_B1) * g
    v2 = ADAM_B2 * v + (1.0 - ADAM_B2) * (g * g)
    m_hat = m2 / (1.0 - ADAM_B1 ** ADAM_STEP)
    v_hat = v2 / (1.0 - ADAM_B2 ** ADAM_STEP)
    delta = -ADAM_LR * (m_hat / (jnp.sqrt(v_hat) + ADAM_EPS) + ADAM_WD * w)
    return delta, m2, v2


def _wada_bwd(c_all_t, dmod, w, m, v, tr=256):
    L, _, n = w.shape

    def body(c_ref, d_ref, w_ref, m_ref, v_ref, g_ref, dl_ref, m2_ref, v2_ref):
        ca = _silu(c_ref[...])
        dm = d_ref[0]
        g = ca[:, 0:1] * dm[0:1, :]
        for b in range(1, N_DEV):
            g = g + ca[:, b:b + 1] * dm[b:b + 1, :]
        delta, m2, v2 = _adam_math(g, w_ref[0], m_ref[0], v_ref[0])
        g_ref[0], dl_ref[0], m2_ref[0], v2_ref[0] = g, delta, m2, v2

    blk = pl.BlockSpec((1, tr, n), lambda l, i: (l, i, 0))
    return pl.pallas_call(
        body, name="wada_bwd", grid=(L, D // tr),
        in_specs=[pl.BlockSpec((tr, N_DEV), lambda l, i: (i, 0)), pl.BlockSpec((1, N_DEV, n), lambda l, i: (l, 0, 0)),
                  blk, blk, blk],
        out_specs=[blk] * 4, out_shape=[jax.ShapeDtypeStruct(w.shape, F32)] * 4,
        compiler_params=_cp(("parallel", "parallel")),
    )(c_all_t, dmod, w, m, v)


def _norm_mod(x, g, scale, shift, tb=256):
    S = x.shape[0]

    def body(x_ref, g_ref, sc_ref, sh_ref, h_ref):
        xv = x_ref[...]
        r = lax.rsqrt(jnp.mean(xv * xv, axis=-1, keepdims=True) + EPS)
        h_ref[...] = (xv * r * (g_ref[...] * (1.0 + sc_ref[...])) + sh_ref[...]).astype(BF16)

    row = pl.BlockSpec((tb, D), lambda i: (i, 0))
    vec = _full((1, D))
    return pl.pallas_call(
        body, name="norm_mod", grid=(S // tb,), in_specs=[row, vec, vec, vec], out_specs=row,
        out_shape=jax.ShapeDtypeStruct((S, D), BF16), compiler_params=_cp(("parallel",)),
    )(x, g, scale, shift)


def _pool_mix(u_ref, uh_ref, ubuf, i, tb):
    H = POOL_HALO
    ubuf[H:, :] = u_ref[...].astype(F32)
    ubuf[:H, :] = jnp.where(i > 0, uh_ref[...].astype(F32), 0.0)
    t = i * tb + lax.broadcasted_iota(jnp.int32, (tb, 1), 0)
    mixed = []
    for g, w in enumerate(POOL_WINDOWS):
        cs = slice(g * POOL_GROUP, (g + 1) * POOL_GROUP)
        cur = ubuf[H:, cs]
        acc = cur
        for j in range(1, w):
            acc = acc + ubuf[pl.ds(H - j, tb), cs]
        cnt = jnp.minimum(t + 1, w).astype(F32)
        mixed.append(acc / cnt - cur)
    return mixed


def _pool_specs(tb):
    H = POOL_HALO
    u = _seg(tb, 1024, OFF_U)
    uh = _seg(H, 1024, OFF_U, lambda i: jnp.maximum(i * tb - H, 0))
    z = _seg(tb, 1024, OFF_Z)
    return u, uh, z


def _pool_fwd(proj, pool_w, pool_scale, tb=256):
    S = proj.shape[0]

    def body(u_ref, uh_ref, z_ref, w_ref, sc_ref, y_ref, ubuf):
        i = pl.program_id(0)
        mixed = _pool_mix(u_ref, uh_ref, ubuf, i, tb)
        m = jnp.concatenate([_dot(mixed[g].astype(BF16), w_ref[g]) for g in range(4)], axis=1)
        y_ref[...] = (m * sc_ref[...] * _silu(z_ref[...].astype(F32))).astype(BF16)

    u, uh, z = _pool_specs(tb)
    return pl.pallas_call(
        body, name="pool_fwd", grid=(S // tb,),
        in_specs=[u, uh, z, _full((4, 256, 256)), _full((1, 1024))],
        out_specs=pl.BlockSpec((tb, 1024), lambda i: (i, 0)),
        out_shape=jax.ShapeDtypeStruct((S, 1024), BF16),
        scratch_shapes=[pltpu.VMEM((tb + POOL_HALO, 1024), F32)],
        compiler_params=_cp(("parallel",)),
    )(proj, proj, proj, pool_w, pool_scale)


def _attn_mask(i):
    TQ, NK = ATTN_TQ, ATTN_TQ + ATTN_BACK
    qc = lax.broadcasted_iota(jnp.int32, (TQ, NK), 0) // 64
    col = lax.broadcasted_iota(jnp.int32, (TQ, NK), 1)
    kc = col // 64
    return (kc >= qc) & (kc <= qc + 2) & ((col >= ATTN_BACK) | (i > 0))


def _block_diag(kbuf, vbuf, kbd, vbd, kh):
    NK = ATTN_TQ + ATTN_BACK
    ks = slice(kh * HEAD_DIM, (kh + 1) * HEAD_DIM)
    for g in range(N_HEADS // N_KV):
        kbd[g * NK:(g + 1) * NK, g * HEAD_DIM:(g + 1) * HEAD_DIM] = kbuf[:, ks]
        vbd[g * NK:(g + 1) * NK, g * HEAD_DIM:(g + 1) * HEAD_DIM] = vbuf[:, ks]


def _attn_probs(q_ref, kw, sink_ref, valid, h):
    qh = q_ref[:, h * HEAD_DIM:(h + 1) * HEAD_DIM]
    return _softmax_sink(_dot_nt(qh, kw) * (HEAD_DIM ** -0.5), sink_ref, valid, h)


def _softmax_sink(s, sink_ref, valid, h):
    s = jnp.where(valid, s, -jnp.inf)
    sk = sink_ref[h:h + 1, 0:1]
    mx = jnp.maximum(jnp.max(s, axis=-1, keepdims=True), sk)
    p = jnp.exp(s - mx)
    es = jnp.exp(sk - mx)
    den = jnp.sum(p, axis=-1, keepdims=True) + es
    return p / den, es / den


def _kv_specs():
    TQ, B = ATTN_TQ, ATTN_BACK
    blk = lambda off: _seg(TQ, 256, off)
    halo = lambda off: _seg(B, 256, off, lambda i: jnp.maximum(i * TQ - B, 0))
    return [blk(OFF_K), halo(OFF_K), blk(OFF_V), halo(OFF_V)]


def _kv_window(k_ref, kh_ref, v_ref, vh_ref, kbuf, vbuf, i):
    B = ATTN_BACK
    for buf, ref, href in ((kbuf, k_ref, kh_ref), (vbuf, v_ref, vh_ref)):
        buf[:B, :] = jnp.where(i > 0, href[...], jnp.zeros_like(href))
        buf[B:, :] = ref[...]


def _attn_fwd(proj, sink_b):
    S = proj.shape[0]
    TQ, NK = ATTN_TQ, ATTN_TQ + ATTN_BACK

    G = N_HEADS // N_KV

    def body(q_ref, z_ref, k_ref, kh_ref, v_ref, vh_ref, sink_ref, o_ref, y_ref, kbuf, vbuf, kbd, vbd):
        i = pl.program_id(0)
        _kv_window(k_ref, kh_ref, v_ref, vh_ref, kbuf, vbuf, i)
        valid = _attn_mask(i)
        kbd[...] = jnp.zeros_like(kbd)
        vbd[...] = jnp.zeros_like(vbd)
        for kh in range(N_KV):
            _block_diag(kbuf, vbuf, kbd, vbd, kh)
            gs = slice(kh * G * HEAD_DIM, (kh + 1) * G * HEAD_DIM)
            s_all = _dot_nt(q_ref[:, gs], kbd[...]) * (HEAD_DIM ** -0.5)
            pn = [_softmax_sink(s_all[:, g * NK:(g + 1) * NK], sink_ref, valid, kh * G + g)[0].astype(BF16) for g in range(G)]
            o = _dot(jnp.concatenate(pn, axis=1), vbd[...])
            o_ref[:, gs] = o.astype(BF16)
            y_ref[:, gs] = (o * _silu(z_ref[:, gs].astype(F32))).astype(BF16)

    out = pl.BlockSpec((TQ, 1024), lambda i: (i, 0))
    return pl.pallas_call(
        body, name="attn_fwd", grid=(S // TQ,),
        in_specs=[_seg(TQ, 1024, OFF_Q), _seg(TQ, 1024, OFF_AZ), *_kv_specs(), _full((N_HEADS, LANE))],
        out_specs=[out, out], out_shape=[jax.ShapeDtypeStruct((S, 1024), BF16)] * 2,
        scratch_shapes=[pltpu.VMEM((NK, 256), BF16)] * 2 + [pltpu.VMEM((G * NK, G * HEAD_DIM), BF16)] * 2,
        compiler_params=_cp(("parallel",)),
    )(proj, proj, proj, proj, proj, proj, sink_b)


def _conv_specs(tb):
    H = CONV_HALO
    prev = lambda i: jnp.maximum(i * tb - H, 0)
    a = _seg(tb, 1024, OFF_CA)
    ah = _seg(H, 1024, OFF_CA, prev)
    b = _seg(tb, 1024, OFF_CB)
    bh = _seg(H, 1024, OFF_CB, prev)
    z = _seg(tb, 1024, OFF_CZ)
    return a, ah, b, bh, z


SUBLANES = 8
CONV_TAPS = tuple((j, CONV_HALO - (CONV_K - 1) + j) for j in range(CONV_K))
CONV_TAPS_T = tuple((j, (CONV_K - 1) - j) for j in range(CONV_K))


def _lane_chunks(fn):
    def body(c, carry):
        fn(pl.ds(pl.multiple_of(c * LANE, LANE), LANE))
        return carry

    lax.fori_loop(0, 1024 // LANE, body, 0)


def _shifted_tiles(src_ref, cs, s, n):
    row = lax.broadcasted_iota(jnp.int32, (SUBLANES, LANE), 0)
    prev = None
    for t in range(n + (1 if s else 0)):
        v = src_ref[pl.ds(SUBLANES * t, SUBLANES), cs]
        if s == 0:
            yield t, v
            continue
        x = pltpu.roll(v, SUBLANES - s, 0)
        if prev is not None:
            yield t - 1, jnp.where(row < SUBLANES - s, prev, x)
        prev = x


def _by_shift(taps):
    groups = {}
    for j, o in taps:
        groups.setdefault(o % SUBLANES, []).append((j, o // SUBLANES))
    return sorted(groups.items())


def _taps_apply(src_ref, w_ref, dst_ref, tb, taps, cs):
    nu = tb // SUBLANES
    acc = [None] * nu
    for s, group in _by_shift(taps):
        w = {j: w_ref[j:j + 1, cs] for j, _ in group}
        for t, g in _shifted_tiles(src_ref, cs, s, nu + max(a for _, a in group)):
            for j, a in group:
                if 0 <= t - a < nu:
                    term = w[j] * g
                    acc[t - a] = term if acc[t - a] is None else acc[t - a] + term
    dst_ref[:, cs] = jnp.concatenate(acc, axis=0)


def _taps_reduce(src_ref, d_ref, out_ref, tb, taps, cs):
    nu = tb // SUBLANES
    d = [d_ref[pl.ds(SUBLANES * u, SUBLANES), cs] for u in range(nu)]
    for s, group in _by_shift(taps):
        part = {j: None for j, _ in group}
        for t, g in _shifted_tiles(src_ref, cs, s, nu + max(a for _, a in group)):
            for j, a in group:
                if 0 <= t - a < nu:
                    term = d[t - a] * g
                    part[j] = term if part[j] is None else part[j] + term
        for j, _ in group:
            out_ref[j:j + 1, cs] += jnp.sum(part[j], axis=0, keepdims=True)


def _conv_glu_dw(a_ref, ah_ref, b_ref, bh_ref, dw_ref, gbuf, ybuf, i, tb):
    H = CONV_HALO
    gbuf[H:, :] = a_ref[...].astype(F32) * _sig(b_ref[...].astype(F32))
    gh = ah_ref[...].astype(F32) * _sig(bh_ref[...].astype(F32))
    gbuf[:H, :] = jnp.where(i > 0, gh, 0.0)
    _lane_chunks(lambda cs: _taps_apply(gbuf, dw_ref, ybuf, tb, CONV_TAPS, cs))


def _layer_norm_fwd(y, g, b):
    mu = jnp.mean(y, axis=-1, keepdims=True)
    yc = y - mu
    rstd = lax.rsqrt(jnp.mean(yc * yc, axis=-1, keepdims=True) + EPS)
    xh = yc * rstd
    return xh, rstd, xh * g + b


def _conv_fwd(proj, dw, dw_b, ln_g, ln_b, pw, tb=256):
    S = proj.shape[0]

    def body(a_ref, ah_ref, b_ref, bh_ref, z_ref, dw_ref, dwb_ref, lg_ref, lb_ref, pw_ref, y_ref, gbuf, ybuf):
        i = pl.program_id(0)
        _conv_glu_dw(a_ref, ah_ref, b_ref, bh_ref, dw_ref, gbuf, ybuf, i, tb)
        _, _, yn = _layer_norm_fwd(ybuf[...] + dwb_ref[...], lg_ref[...], lb_ref[...])
        out = _dot(_silu(yn).astype(BF16), pw_ref[...])
        y_ref[...] = (out * _silu(z_ref[...].astype(F32))).astype(BF16)

    vec = _full((1, 1024))
    return pl.pallas_call(
        body, name="conv_fwd", grid=(S // tb,),
        in_specs=[*_conv_specs(tb), _full((32, 1024)), vec, vec, vec, _full((1024, 1024))],
        out_specs=pl.BlockSpec((tb, 1024), lambda i: (i, 0)),
        out_shape=jax.ShapeDtypeStruct((S, 1024), BF16),
        scratch_shapes=[pltpu.VMEM((tb + CONV_HALO, 1024), F32), pltpu.VMEM((tb, 1024), F32)],
        compiler_params=_cp(("parallel",)),
    )(proj, proj, proj, proj, proj, dw, dw_b, ln_g, ln_b, pw)


def _merge_fwd(yp, ya, yc, wbp, wba, wbc, proj, tm=512, tn=1024):
    S = yp.shape[0]

    def body(yp_ref, ya_ref, yc_ref, wp_ref, wa_ref, wc_ref, gp_ref, ga_ref, gc_ref, m_ref, pp_ref, pa_ref, pc_ref):
        pp = _dot(yp_ref[...], wp_ref[...])
        pa = _dot(ya_ref[...], wa_ref[...])
        pc = _dot(yc_ref[...], wc_ref[...])
        m = (_sig(gp_ref[...].astype(F32)) * pp + _sig(ga_ref[...].astype(F32)) * pa
             + _sig(gc_ref[...].astype(F32)) * pc)
        m_ref[...] = m.astype(BF16)
        pp_ref[...], pa_ref[...], pc_ref[...] = pp.astype(BF16), pa.astype(BF16), pc.astype(BF16)

    yb = pl.BlockSpec((tm, 1024), lambda j, i: (i, 0))
    wb = pl.BlockSpec((1024, tn), lambda j, i: (0, j))
    out = pl.BlockSpec((tm, tn), lambda j, i: (i, j))
    return pl.pallas_call(
        body, name="merge_fwd", grid=(D // tn, S // tm),
        in_specs=[yb, yb, yb, wb, wb, wb, _seg2(tm, tn, OFF_GP), _seg2(tm, tn, OFF_GA), _seg2(tm, tn, OFF_GC)],
        out_specs=[out] * 4, out_shape=[jax.ShapeDtypeStruct((S, D), BF16)] * 4,
        compiler_params=_cp(("parallel", "parallel")),
    )(yp, ya, yc, wbp, wba, wbc, proj, proj, proj)


def _out_fwd(x, merged, w_out, gate, tm=512, tn=1024):
    S = x.shape[0]

    def body(x_ref, m_ref, w_ref, g_ref, o_ref):
        o_ref[...] = x_ref[...] + g_ref[...] * _dot(m_ref[...], w_ref[...])

    xb = pl.BlockSpec((tm, tn), lambda j, i: (i, j))
    return pl.pallas_call(
        body, name="out_fwd", grid=(D // tn, S // tm),
        in_specs=[xb, pl.BlockSpec((tm, D), lambda j, i: (i, 0)), pl.BlockSpec((D, tn), lambda j, i: (0, j)),
                  pl.BlockSpec((1, tn), lambda j, i: (0, j))],
        out_specs=xb, out_shape=jax.ShapeDtypeStruct((S, D), F32),
        compiler_params=_cp(("parallel", "parallel")),
    )(x, merged, w_out, gate)


def _final_loss(x, target, final_g, tb=256):
    S = x.shape[0]

    def body(x_ref, t_ref, g_ref, dx_ref, gg_ref, ls_ref):
        i = pl.program_id(0)
        xv, g = x_ref[...], g_ref[...]
        r = lax.rsqrt(jnp.mean(xv * xv, axis=-1, keepdims=True) + EPS)
        xh = xv * r
        e = xh * g - t_ref[...]
        dy = e * (1.0 / D)
        gy = dy * g
        dx_ref[...] = r * (gy - xh * jnp.mean(gy * xh, axis=-1, keepdims=True))
        gg = jnp.sum(dy * xh, axis=0, keepdims=True)
        ls = jnp.sum(e * e, axis=0, keepdims=True) * (0.5 / D)

        @pl.when(i == 0)
        def _():
            gg_ref[...], ls_ref[...] = gg, ls

        @pl.when(i > 0)
        def _():
            gg_ref[...] += gg
            ls_ref[...] += ls

    row = pl.BlockSpec((tb, D), lambda i: (i, 0))
    vec = _full((1, D))
    return pl.pallas_call(
        body, name="final_loss", grid=(S // tb,), in_specs=[row, row, vec], out_specs=[row, vec, vec],
        out_shape=[jax.ShapeDtypeStruct((S, D), F32), jax.ShapeDtypeStruct((1, D), F32),
                   jax.ShapeDtypeStruct((1, D), F32)],
        compiler_params=_cp(("arbitrary",)),
    )(x, target, final_g)


def _out_bwd(dx, gate, w_out, pp, pa, pc, proj, tm=512, tn=1024):
    S = dx.shape[0]

    def body(dx_ref, g_ref, w_ref, pp_ref, pa_ref, pc_ref, gp_ref, ga_ref, gc_ref,
             dpp_ref, dpa_ref, dpc_ref, dgp_ref, dga_ref, dgc_ref):
        dm = _dot_nt((dx_ref[...] * g_ref[...]).astype(BF16), w_ref[...])
        for p_ref, gl_ref, dp_ref, dg_ref in ((pp_ref, gp_ref, dpp_ref, dgp_ref), (pa_ref, ga_ref, dpa_ref, dga_ref),
                                              (pc_ref, gc_ref, dpc_ref, dgc_ref)):
            s = _sig(gl_ref[...].astype(F32))
            dp_ref[...] = (dm * s).astype(BF16)
            dg_ref[...] = (dm * p_ref[...].astype(F32) * s * (1.0 - s)).astype(BF16)

    out = pl.BlockSpec((tm, tn), lambda j, i: (i, j))
    return pl.pallas_call(
        body, name="out_bwd", grid=(D // tn, S // tm),
        in_specs=[pl.BlockSpec((tm, D), lambda j, i: (i, 0)), _full((1, D)), pl.BlockSpec((tn, D), lambda j, i: (j, 0)),
                  out, out, out, _seg2(tm, tn, OFF_GP), _seg2(tm, tn, OFF_GA), _seg2(tm, tn, OFF_GC)],
        out_specs=[out] * 6, out_shape=[jax.ShapeDtypeStruct((S, D), BF16)] * 6,
        compiler_params=_cp(("parallel", "parallel")),
    )(dx, gate, w_out, pp, pa, pc, proj, proj, proj)


def _wout_post(gmat, w_out, gate, tr=256):
    def body(g_ref, w_ref, gate_ref, dw_ref, dg_ref):
        i = pl.program_id(0)
        gm = g_ref[...]
        dw_ref[...] = gm * gate_ref[...]
        part = jnp.sum(gm * w_ref[...].astype(F32), axis=0, keepdims=True)

        @pl.when(i == 0)
        def _():
            dg_ref[...] = part

        @pl.when(i > 0)
        def _():
            dg_ref[...] += part

    row = pl.BlockSpec((tr, D), lambda i: (i, 0))
    return pl.pallas_call(
        body, name="wout_post", grid=(D // tr,), in_specs=[row, row, _full((1, D))], out_specs=[row, _full((1, D))],
        out_shape=[jax.ShapeDtypeStruct((D, D), F32), jax.ShapeDtypeStruct((1, D), F32)],
        compiler_params=_cp(("arbitrary",)),
    )(gmat, w_out, gate)


def _pool_bwd_a(dy, proj, pool_w, pool_scale, tb=256):
    S = proj.shape[0]

    def body(dy_ref, u_ref, uh_ref, z_ref, w_ref, sc_ref, dmix_ref, dz_ref, dsc_ref, dw_ref, ubuf):
        i = pl.program_id(0)
        mixed = [m.astype(BF16) for m in _pool_mix(u_ref, uh_ref, ubuf, i, tb)]
        m = jnp.concatenate([_dot(mixed[g], w_ref[g]) for g in range(4)], axis=1)
        dyv, z, sc = dy_ref[...].astype(F32), z_ref[...].astype(F32), sc_ref[...]
        dyp = dyv * _silu(z)
        dz_ref[...] = (dyv * (m * sc) * _dsilu(z)).astype(BF16)
        dsc = jnp.sum(dyp * m, axis=0, keepdims=True)
        dmm = (dyp * sc).astype(BF16)
        dws = []
        for g in range(4):
            cs = slice(g * POOL_GROUP, (g + 1) * POOL_GROUP)
            dmix_ref[:, cs] = _dot_nt(dmm[:, cs], w_ref[g])
            dws.append(_dot_tn(mixed[g], dmm[:, cs]))

        @pl.when(i == 0)
        def _():
            dsc_ref[...] = dsc
            for g in range(4):
                dw_ref[g] = dws[g]

        @pl.when(i > 0)
        def _():
            dsc_ref[...] += dsc
            for g in range(4):
                dw_ref[g] += dws[g]

    u, uh, z = _pool_specs(tb)
    row = pl.BlockSpec((tb, 1024), lambda i: (i, 0))
    wfull = _full((4, 256, 256))
    return pl.pallas_call(
        body, name="pool_bwd_a", grid=(S // tb,),
        in_specs=[row, u, uh, z, wfull, _full((1, 1024))],
        out_specs=[row, row, _full((1, 1024)), wfull],
        out_shape=[jax.ShapeDtypeStruct((S, 1024), F32), jax.ShapeDtypeStruct((S, 1024), BF16),
                   jax.ShapeDtypeStruct((1, 1024), F32), jax.ShapeDtypeStruct((4, 256, 256), F32)],
        scratch_shapes=[pltpu.VMEM((tb + POOL_HALO, 1024), F32)],
        compiler_params=_cp(("arbitrary",)),
    )(dy, proj, proj, proj, pool_w, pool_scale)


def _pool_bwd_b(dmix, tb=256):
    S = dmix.shape[0]
    H = POOL_HALO
    nb = S // tb

    def body(dm_ref, dh_ref, du_ref, ebuf):
        i = pl.program_id(0)
        t = i * tb + lax.broadcasted_iota(jnp.int32, (tb, 1), 0)
        th = (i + 1) * tb + lax.broadcasted_iota(jnp.int32, (H, 1), 0)
        for g, w in enumerate(POOL_WINDOWS):
            cs = slice(g * POOL_GROUP, (g + 1) * POOL_GROUP)
            ebuf[:tb, cs] = dm_ref[:, cs] / jnp.minimum(t + 1, w).astype(F32)
            eh = dh_ref[:, cs] / jnp.minimum(th + 1, w).astype(F32)
            ebuf[tb:, cs] = jnp.where(i < nb - 1, eh, 0.0)
        for g, w in enumerate(POOL_WINDOWS):
            cs = slice(g * POOL_GROUP, (g + 1) * POOL_GROUP)
            acc = ebuf[:tb, cs]
            for j in range(1, w):
                acc = acc + ebuf[pl.ds(j, tb), cs]
            du_ref[:, cs] = (acc - dm_ref[:, cs]).astype(BF16)

    row = pl.BlockSpec((tb, 1024), lambda i: (i, 0))
    nxt = pl.BlockSpec((H, 1024), lambda i: (jnp.minimum((i + 1) * (tb // H), S // H - 1), 0))
    return pl.pallas_call(
        body, name="pool_bwd_b", grid=(nb,), in_specs=[row, nxt], out_specs=row,
        out_shape=jax.ShapeDtypeStruct((S, 1024), BF16),
        scratch_shapes=[pltpu.VMEM((tb + H, 1024), F32)],
        compiler_params=_cp(("parallel",)),
    )(dmix, dmix)


def _attn_bwd(dy, o, proj, sink_b):
    S = proj.shape[0]
    TQ, NK = ATTN_TQ, ATTN_TQ + ATTN_BACK
    nb = S // TQ
    G = N_HEADS // N_KV

    def body(dy_ref, o_ref, q_ref, z_ref, k_ref, kh_ref, v_ref, vh_ref, sink_ref, dq_ref, dz_ref, dk_hbm, dv_hbm, ds_ref,
             dk_acc, dv_acc, kbuf, vbuf):
        i = pl.program_id(0)
        _kv_window(k_ref, kh_ref, v_ref, vh_ref, kbuf, vbuf, i)

        @pl.when(i == 0)
        def _():
            dk_acc[...] = jnp.zeros_like(dk_acc)
            dv_acc[...] = jnp.zeros_like(dv_acc)
            ds_ref[...] = jnp.zeros_like(ds_ref)

        start = pl.multiple_of(i * TQ, TQ)
        valid = _attn_mask(i)
        dks, dvs = [], []
        for kh in range(N_KV):
            ks = slice(kh * HEAD_DIM, (kh + 1) * HEAD_DIM)
            kw = kbuf[:, ks]
            vw = vbuf[:, ks]
            dk_sum = jnp.zeros((NK, HEAD_DIM), F32)
            dv_sum = jnp.zeros((NK, HEAD_DIM), F32)
            dqs = []
            for gi in range(G):
                h = kh * G + gi
                hs = slice(h * HEAD_DIM, (h + 1) * HEAD_DIM)
                pn, psink = _attn_probs(q_ref, kw, sink_ref, valid, h)
                ov = o_ref[:, hs].astype(F32)
                do = dy_ref[:, hs].astype(F32) * _silu(z_ref[:, hs].astype(F32))
                delta = jnp.sum(do * ov, axis=-1, keepdims=True)
                dob = do.astype(BF16)
                dp = _dot_nt(dob, vw)
                ds = (pn * (dp - delta)).astype(BF16)
                dsink = -jnp.sum(psink * delta, axis=0, keepdims=True)
                ds_ref[h:h + 1, :] += jnp.broadcast_to(dsink, (1, LANE))
                dqs.append(_dot(ds, kw) * (HEAD_DIM ** -0.5))
                dk_sum = dk_sum + _dot_tn(ds, q_ref[:, hs])
                dv_sum = dv_sum + _dot_tn(pn.astype(BF16), dob)
            gs = slice(kh * G * HEAD_DIM, (kh + 1) * G * HEAD_DIM)
            dq_ref[:, gs] = jnp.concatenate(dqs, axis=1).astype(BF16)
            z = z_ref[:, gs].astype(F32)
            dz_ref[:, gs] = (dy_ref[:, gs].astype(F32) * o_ref[:, gs].astype(F32) * _dsilu(z)).astype(BF16)
            dks.append(dk_sum * (HEAD_DIM ** -0.5))
            dvs.append(dv_sum)
        dk_acc[pl.ds(start, NK), :] += jnp.concatenate(dks, axis=1)
        dv_acc[pl.ds(start, NK), :] += jnp.concatenate(dvs, axis=1)

        @pl.when(i == nb - 1)
        def _():
            pltpu.sync_copy(dk_acc, dk_hbm)
            pltpu.sync_copy(dv_acc, dv_hbm)

    row = pl.BlockSpec((TQ, 1024), lambda i: (i, 0))
    return pl.pallas_call(
        body, name="attn_bwd", grid=(nb,),
        in_specs=[row, row, _seg(TQ, 1024, OFF_Q), _seg(TQ, 1024, OFF_AZ), *_kv_specs(), _full((N_HEADS, LANE))],
        out_specs=[row, row, ANY, ANY, _full((N_HEADS, LANE))],
        out_shape=[jax.ShapeDtypeStruct((S, 1024), BF16), jax.ShapeDtypeStruct((S, 1024), BF16),
                   jax.ShapeDtypeStruct((S + ATTN_BACK, 256), F32), jax.ShapeDtypeStruct((S + ATTN_BACK, 256), F32),
                   jax.ShapeDtypeStruct((N_HEADS, LANE), F32)],
        scratch_shapes=[pltpu.VMEM((S + ATTN_BACK, 256), F32)] * 2 + [pltpu.VMEM((NK, 256), BF16)] * 2,
        compiler_params=_cp(("arbitrary",), VMEM_BIG),
    )(dy, o, proj, proj, proj, proj, proj, proj, sink_b)


def _conv_bwd_a(dy, proj, dw, dw_b, ln_g, ln_b, pw, tb=256):
    S = proj.shape[0]
    H = CONV_HALO

    def body(dy_ref, a_ref, ah_ref, b_ref, bh_ref, z_ref, dw_ref, dwb_ref, lg_ref, lb_ref, pw_ref,
             dcv_ref, dz_ref, dpw_ref, dlg_ref, dlb_ref, ddwb_ref, ddw_ref, gbuf, ybuf):
        i = pl.program_id(0)
        _conv_glu_dw(a_ref, ah_ref, b_ref, bh_ref, dw_ref, gbuf, ybuf, i, tb)
        lg = lg_ref[...]
        xh, rstd, yn = _layer_norm_fwd(ybuf[...] + dwb_ref[...], lg, lb_ref[...])
        u = _silu(yn).astype(BF16)
        out = _dot(u, pw_ref[...])
        dyv, z = dy_ref[...].astype(F32), z_ref[...].astype(F32)
        dz_ref[...] = (dyv * out * _dsilu(z)).astype(BF16)
        dout = (dyv * _silu(z)).astype(BF16)
        dpw = _dot_tn(u, dout)
        dyn = _dot_nt(dout, pw_ref[...]) * _dsilu(yn)
        dlg = jnp.sum(dyn * xh, axis=0, keepdims=True)
        dlb = jnp.sum(dyn, axis=0, keepdims=True)
        dxh = dyn * lg
        dcv = rstd * (dxh - jnp.mean(dxh, axis=-1, keepdims=True) - xh * jnp.mean(dxh * xh, axis=-1, keepdims=True))
        dcv_ref[...] = dcv
        ddwb = jnp.sum(dcv, axis=0, keepdims=True)
        ybuf[...] = dcv

        @pl.when(i == 0)
        def _():
            dpw_ref[...], dlg_ref[...], dlb_ref[...], ddwb_ref[...] = dpw, dlg, dlb, ddwb
            ddw_ref[...] = jnp.zeros_like(ddw_ref)

        @pl.when(i > 0)
        def _():
            dpw_ref[...] += dpw
            dlg_ref[...] += dlg
            dlb_ref[...] += dlb
            ddwb_ref[...] += ddwb

        _lane_chunks(lambda cs: _taps_reduce(gbuf, ybuf, ddw_ref, tb, CONV_TAPS, cs))

    vec = _full((1, 1024))
    row = pl.BlockSpec((tb, 1024), lambda i: (i, 0))
    big = _full((1024, 1024))
    return pl.pallas_call(
        body, name="conv_bwd_a", grid=(S // tb,),
        in_specs=[row, *_conv_specs(tb), _full((32, 1024)), vec, vec, vec, big],
        out_specs=[row, row, big, vec, vec, vec, _full((32, 1024))],
        out_shape=[jax.ShapeDtypeStruct((S, 1024), F32), jax.ShapeDtypeStruct((S, 1024), BF16),
                   jax.ShapeDtypeStruct((1024, 1024), F32), jax.ShapeDtypeStruct((1, 1024), F32),
                   jax.ShapeDtypeStruct((1, 1024), F32), jax.ShapeDtypeStruct((1, 1024), F32),
                   jax.ShapeDtypeStruct((32, 1024), F32)],
        scratch_shapes=[pltpu.VMEM((tb + H, 1024), F32), pltpu.VMEM((tb, 1024), F32)],
        compiler_params=_cp(("arbitrary",)),
    )(dy, proj, proj, proj, proj, proj, dw, dw_b, ln_g, ln_b, pw)


def _conv_bwd_b(dcv, proj, dw, tb=256):
    S = proj.shape[0]
    H = CONV_HALO
    nb = S // tb

    def body(d_ref, dn_ref, a_ref, b_ref, dw_ref, da_ref, db_ref, dbuf, gbuf):
        i = pl.program_id(0)
        dbuf[:tb, :] = d_ref[...]
        dbuf[tb:, :] = jnp.where(i < nb - 1, dn_ref[...], 0.0)
        _lane_chunks(lambda cs: _taps_apply(dbuf, dw_ref, gbuf, tb, CONV_TAPS_T, cs))
        dg = gbuf[...]
        a, s = a_ref[...].astype(F32), _sig(b_ref[...].astype(F32))
        da_ref[...] = (dg * s).astype(BF16)
        db_ref[...] = (dg * a * s * (1.0 - s)).astype(BF16)

    row = pl.BlockSpec((tb, 1024), lambda i: (i, 0))
    nxt = pl.BlockSpec((H, 1024), lambda i: (jnp.minimum((i + 1) * (tb // H), S // H - 1), 0))
    return pl.pallas_call(
        body, name="conv_bwd_b", grid=(nb,),
        in_specs=[row, nxt, _seg(tb, 1024, OFF_CA), _seg(tb, 1024, OFF_CB), _full((32, 1024))],
        out_specs=[row, row], out_shape=[jax.ShapeDtypeStruct((S, 1024), BF16)] * 2,
        scratch_shapes=[pltpu.VMEM((tb + H, 1024), F32), pltpu.VMEM((tb, 1024), F32)],
        compiler_params=_cp(("parallel",)),
    )(dcv, dcv, proj, proj, dw)


def _norm_bwd(dh, x, dx_out, g, scale, tb=256):
    S = x.shape[0]

    def body(dh_ref, x_ref, dxo_ref, g_ref, sc_ref, dx_ref, dsh_ref, da_ref):
        i = pl.program_id(0)
        xv, dhv = x_ref[...], dh_ref[...]
        r = lax.rsqrt(jnp.mean(xv * xv, axis=-1, keepdims=True) + EPS)
        xh = xv * r
        gy = dhv * (g_ref[...] * (1.0 + sc_ref[...]))
        dx_ref[...] = dxo_ref[...] + r * (gy - xh * jnp.mean(gy * xh, axis=-1, keepdims=True))
        dsh = jnp.sum(dhv, axis=0, keepdims=True)
        da = jnp.sum(dhv * xh, axis=0, keepdims=True)

        @pl.when(i == 0)
        def _():
            dsh_ref[...], da_ref[...] = dsh, da

        @pl.when(i > 0)
        def _():
            dsh_ref[...] += dsh
            da_ref[...] += da

    row = pl.BlockSpec((tb, D), lambda i: (i, 0))
    vec = _full((1, D))
    return pl.pallas_call(
        body, name="norm_bwd", grid=(S // tb,), in_specs=[row, row, row, vec, vec], out_specs=[row, vec, vec],
        out_shape=[jax.ShapeDtypeStruct((S, D), F32), jax.ShapeDtypeStruct((1, D), F32), jax.ShapeDtypeStruct((1, D), F32)],
        compiler_params=_cp(("arbitrary",)),
    )(dh, x, dx_out, g, scale)


def _mod_bwd(d_a, norm_g, scale):
    def body(da_ref, g_ref, sc_ref, dg_ref, dsc_ref):
        dg_ref[...] = da_ref[...] * (1.0 + sc_ref[...])
        dsc_ref[...] = da_ref[...] * g_ref[...]

    return pl.pallas_call(body, name="mod_bwd", out_shape=[jax.ShapeDtypeStruct(d_a.shape, F32)] * 2)(d_a, norm_g, scale)


def _reduce_adamw(parts, w, m, v, name):
    L, rows, C = w.shape
    tr = rows if rows % 64 else 64

    def body(*refs):
        p_refs, (w_ref, m_ref, v_ref, g_ref, dl_ref, m2_ref, v2_ref) = refs[:L * N_DEV], refs[L * N_DEV:]
        for l in range(L):
            g = p_refs[l * N_DEV][0].astype(F32)
            for k in range(1, N_DEV):
                g = g + p_refs[l * N_DEV + k][0].astype(F32)
            g_ref[l] = g
            dl_ref[l], m2_ref[l], v2_ref[l] = _adam_math(g, w_ref[l], m_ref[l], v_ref[l])

    slot = lambda k: pl.BlockSpec((1, tr, C), lambda i: (k, i, 0))
    blk = pl.BlockSpec((L, tr, C), lambda i: (0, i, 0))
    return pl.pallas_call(
        body, name=name, grid=(rows // tr,), in_specs=[slot(k) for _ in range(L) for k in range(N_DEV)] + [blk] * 3,
        out_specs=[blk] * 4, out_shape=[jax.ShapeDtypeStruct((L, rows, C), F32)] * 4,
        compiler_params=_cp(("parallel",)),
    )(*[p for p in parts for _ in range(N_DEV)], w, m, v)


def _small_final(parts, w, m, v):
    R = w.shape[0]

    def body(p_ref, w_ref, m_ref, v_ref, g_ref, dl_ref, m2_ref, v2_ref):
        g = p_ref[0]
        for k in range(1, N_DEV):
            g = g + p_ref[k]
        delta, m2, v2 = _adam_math(g, w_ref[...], m_ref[...], v_ref[...])
        g_ref[...], dl_ref[...], m2_ref[...], v2_ref[...] = g, delta, m2, v2

    return pl.pallas_call(body, name="small_final", out_shape=[jax.ShapeDtypeStruct((R, LANE), F32)] * 4)(parts, w, m, v)


def _layer_fwd(x, mod, small, W, more_w):
    shift, scale, gate = mod
    h = _norm_mod(x, small["norm_g"], scale, shift)
    proj = _mm(h, W["w_in_t"], BF16, 512, 1536, name="proj_mm", nt=True)
    W.update(more_w(proj))
    y_pool = _pool_fwd(proj, W["pool_w"], small["pool_scale"])
    o, y_attn = _attn_fwd(proj, small["sink_b"])
    y_conv = _conv_fwd(proj, W["conv_dw"], small["conv_dw_b"], small["conv_ln_g"], small["conv_ln_b"], W["conv_pw"])
    merged, pp, pa, pc = _merge_fwd(y_pool, y_attn, y_conv, W["wbp"], W["wba"], W["wbc"], proj)
    x_new = _out_fwd(x, merged, W["w_out"], gate)
    stash = dict(x=x, h=h, proj=proj, o=o, y_pool=y_pool, y_attn=y_attn, y_conv=y_conv,
                 merged=merged, pp=pp, pa=pa, pc=pc)
    return x_new, stash


def _layer_bwd(dx, st, mod, small, W, put):
    shift, scale, gate = mod
    proj = st["proj"]
    gmat = _mm_tn(st["merged"], dx, 1024, 1024, 1024, name="wout_tn")
    d_w_out, d_gate = _wout_post(gmat, W["w_out"], gate)
    dpp, dpa, dpc, dgp, dga, dgc = _out_bwd(dx, gate, W["w_out"], st["pp"], st["pa"], st["pc"], proj)
    dy_pool = _mm(dpp, W["wbp"], BF16, 512, 1024, name="branch_bwd_mm", nt=True)
    dy_attn = _mm(dpa, W["wba"], BF16, 512, 1024, name="branch_bwd_mm", nt=True)
    dy_conv = _mm(dpc, W["wbc"], BF16, 512, 1024, name="branch_bwd_mm", nt=True)
    d_wbp = _mm_tn(st["y_pool"], dpp, 1024, 1024, 1024, name="branch_tn")
    d_wba = _mm_tn(st["y_attn"], dpa, 1024, 1024, 1024, name="branch_tn")
    d_wbc = _mm_tn(st["y_conv"], dpc, 1024, 1024, 1024, name="branch_tn")

    dmix, dz_pool, d_pool_scale, d_pool_w = _pool_bwd_a(dy_pool, proj, W["pool_w"], small["pool_scale"])
    du = _pool_bwd_b(dmix)
    dq, dz_attn, dk, dv, d_sink = _attn_bwd(dy_attn, st["o"], proj, small["sink_b"])
    dcv, dz_conv, d_pw, d_ln_g, d_ln_b, d_dw_b, d_dw = _conv_bwd_a(
        dy_conv, proj, W["conv_dw"], small["conv_dw_b"], small["conv_ln_g"], small["conv_ln_b"], W["conv_pw"])
    da, db = _conv_bwd_b(dcv, proj, W["conv_dw"])

    tok = put(dict(pool_w=d_pool_w, conv_dw=d_dw[:CONV_K], conv_pw=d_pw, wbp=d_wbp, wba=d_wba, wbc=d_wbc, w_out=d_w_out))
    dproj = jnp.concatenate([du, dz_pool, dq, dk[ATTN_BACK:].astype(BF16), dv[ATTN_BACK:].astype(BF16), dz_attn, da, db,
                             dz_conv, dgp, dga, dgc], axis=1)
    d_w_in_t = _mm_tn(dproj, st["h"], 768, 2048, 1024, name="win_tn", vmem=VMEM_BIG, after=tok, out_dtype=BF16)
    tok = put(dict(w_in=d_w_in_t))
    dh = _mm(dproj, W["w_in_t"], F32, 1024, 2048, 1536, name="dh_mm", vmem=VMEM_BIG, after=tok)
    dx_in, d_shift, d_a = _norm_bwd(dh, st["x"], dx, small["norm_g"], scale)
    sm = dict(d_a=d_a, d_shift=d_shift, d_gate=d_gate, pool_scale=d_pool_scale, attn_sink=d_sink[:, 0],
              conv_dw_b=d_dw_b, conv_ln_g=d_ln_g, conv_ln_b=d_ln_b)
    return dx_in, sm


def _local_step(x, target, mods, smalls, get_w, final_g, put_g, end_layer):
    stashes, Ws = [], []
    for l in range(DEPTH):
        w, more_w = get_w(l, x)
        Ws.append(w)
        x, st = _layer_fwd(x, mods[l], smalls[l], w, more_w)
        stashes.append(st)
    dx, d_final_g, loss_lanes = _final_loss(x, target, final_g)
    sms = [None] * DEPTH
    for l in reversed(range(DEPTH)):
        dx, sms[l] = _layer_bwd(dx, stashes[l], mods[l], smalls[l], Ws[l], functools.partial(put_g, l))
        end_layer(l, dx)
    return loss_lanes, dx, d_final_g, sms


BIG = ("w_in", "pool_w", "conv_pw", "wbp", "wba", "wbc", "w_out")
GRADS = BIG + ("conv_dw",)


def _full_w_in(g):
    return dict(w_in_t=g.reshape(IN_WIDTH, D))


def _full_weights(g, conv_dw):
    cols = lambda a: jnp.transpose(a, (1, 0, 2)).reshape(a.shape[1], -1)
    pool_w = jnp.transpose(g["pool_w"], (1, 0, 2, 3)).reshape(4, 256, 256)
    conv_pw = g["conv_pw"].reshape(1024, 1024)
    wbp, wba, wbc = cols(g["wbp"]), cols(g["wba"]), cols(g["wbc"])
    w_out = g["w_out"].reshape(D, D)
    conv_dw = jnp.pad(cols(conv_dw), ((0, 32 - CONV_K), (0, 0)))
    return dict(pool_w=pool_w, conv_pw=conv_pw, wbp=wbp, wba=wba, wbc=wbc, w_out=w_out, conv_dw=conv_dw)


def _pieces(name, g):
    if name == "w_in":
        return g.reshape(8, IN_WIDTH // 8, D)
    if name == "pool_w":
        return jnp.transpose(g.reshape(4, 8, 32, 256), (1, 0, 2, 3))
    if name == "conv_dw":
        return jnp.transpose(g.reshape(CONV_K, 8, 128), (1, 0, 2))
    if name == "conv_pw":
        return g.reshape(8, 128, 1024)
    if name in ("wbp", "wba", "wbc"):
        return jnp.transpose(g.reshape(1024, 8, 256), (1, 0, 2))
    return g.reshape(8, 256, D)


def _pack_small(items, rows):
    flat = jnp.concatenate([a.reshape(-1).astype(F32) for a in items])
    return jnp.pad(flat, (0, rows * LANE - flat.shape[0])).reshape(rows, LANE)


def _unpack_small(packed, shapes):
    flat, out, off = packed.reshape(-1), [], 0
    for s in shapes:
        n = 1
        for d in s:
            n *= d
        out.append(flat[off:off + n].reshape(s))
        off += n
    return out


def kernel(x, c, norm_g, w_ada, b_ada, w_in, pool_w, pool_scale, attn_sink, conv_dw, conv_dw_b, conv_ln_g, conv_ln_b, conv_pw, w_branch_pool, w_branch_attn, w_branch_conv, w_out, final_g, loss_target, m_norm_g, m_w_ada, m_b_ada, m_w_in, m_pool_w, m_pool_scale, m_attn_sink, m_conv_dw, m_conv_dw_b, m_conv_ln_g, m_conv_ln_b, m_conv_pw, m_w_branch_pool, m_w_branch_attn, m_w_branch_conv, m_w_out, m_final_g, v_norm_g, v_w_ada, v_b_ada, v_w_in, v_pool_w, v_pool_scale, v_attn_sink, v_conv_dw, v_conv_dw_b, v_conv_ln_g, v_conv_ln_b, v_conv_pw, v_w_branch_pool, v_w_branch_attn, v_w_branch_conv, v_w_out, v_final_g):
    L = DEPTH
    me = 4 * lax.axis_index("x") + 2 * lax.axis_index("y") + lax.axis_index("c")
    tr = lambda a: jnp.swapaxes(a, 1, 2)
    shards = dict(w_in=tr(w_in), pool_w=pool_w, conv_dw=conv_dw, conv_pw=conv_pw, wbp=w_branch_pool, wba=w_branch_attn,
                  wbc=w_branch_conv, w_out=w_out)
    moms = dict(w_in=(tr(m_w_in), tr(v_w_in)), pool_w=(m_pool_w, v_pool_w), conv_dw=(m_conv_dw, v_conv_dw),
                conv_pw=(m_conv_pw, v_conv_pw), wbp=(m_w_branch_pool, v_w_branch_pool),
                wba=(m_w_branch_attn, v_w_branch_attn), wbc=(m_w_branch_conv, v_w_branch_conv), w_out=(m_w_out, v_w_out))

    n_cd = L * CONV_K * 128
    first = _all_gather([_pack_small([c, conv_dw], 144)], "gather_c")[0].reshape(N_DEV, -1)
    c_all = first[:, :D]
    conv_dw_all = first[:, D:D + n_cd].reshape(N_DEV, L, CONV_K, 128)

    mod_part = _mod_fwd(c_all, w_ada)
    mod_all = _all_gather([mod_part.reshape(-1, LANE)], "gather_mod")[0].reshape(N_DEV, L, N_DEV, -1)
    mod = jnp.transpose(lax.dynamic_index_in_dim(mod_all, me, axis=2, keepdims=False), (1, 0, 2)).reshape(L, 3 * D)
    mod = mod + b_ada
    mods = [(mod[l:l + 1, :D], mod[l:l + 1, D:2 * D], mod[l:l + 1, 2 * D:]) for l in range(L)]

    w_in_0 = _all_gather([(shards["w_in"][0] + mod[0, 0] * 0.0).astype(BF16)], "gather_w_in_0")[0]
    gathers, tok = [], w_in_0[0, 0, 0].astype(F32) * 0.0
    for l in range(L):
        first = lambda a: (a + tok).astype(BF16)
        mine = [first(shards["w_in"][l])], [first(shards[k][l]) if k == BIG[1] else shards[k][l].astype(BF16) for k in BIG[1:]]
        started = [None if (l, n) == (0, 0) else _spread_start(v, False, f"gather_start_{l}_{n}") for n, v in enumerate(mine)]
        gathers.append((mine, started))
        tok = sum(st[-1][0, 0] for st in started if st is not None)
    mods[0] = (mods[0][0] + tok,) + mods[0][1:]

    sink_b = jnp.broadcast_to(attn_sink[:, :, None], (L, N_HEADS, LANE))
    smalls = [dict(norm_g=norm_g[l:l + 1], pool_scale=pool_scale[l:l + 1], sink_b=sink_b[l], conv_dw_b=conv_dw_b[l:l + 1],
                   conv_ln_g=conv_ln_g[l:l + 1], conv_ln_b=conv_ln_b[l:l + 1]) for l in range(L)]

    def with_mine(landed, mine):
        return lax.dynamic_update_slice(landed, mine, (me,) + (0,) * (landed.ndim - 1))

    def get_w(l, x_in):
        mine, started = gathers[l]
        if l == 0:
            w_in = w_in_0
        else:
            w_in = with_mine(_spread_wait(started[0], x_in, False, f"gather_wait_{l}_0")[0], mine[0][0][None])

        def more_w(proj):
            landed = _spread_wait(started[1], proj, False, f"gather_wait_{l}_1")
            g = {k: with_mine(a, b[None]) for k, a, b in zip(BIG[1:], landed, mine[1])}
            return _full_weights(g, conv_dw_all[:, l])

        return _full_w_in(w_in), more_w

    pending, parts = {l: [] for l in range(L)}, [dict() for _ in range(L)]

    def put_g(l, grads):
        names = tuple(k for k in GRADS if k in grads)
        pieces = [_pieces(k, grads[k]).astype(BF16) for k in names]
        started = _spread_start(pieces, True, f"scatter_start_{l}_{len(pending[l])}")
        pending[l].append((names, pieces, started))
        return started[-1][0:1, 0:1]

    def finish(l, after):
        for n, (names, pieces, started) in enumerate(pending.pop(l)):
            landed = _spread_wait(started, after, True, f"scatter_wait_{l}_{n}")
            for k, a, b in zip(names, landed, pieces):
                parts[l][k] = with_mine(a, lax.dynamic_slice_in_dim(b, me, 1, axis=0))

    def end_layer(l, dx):
        if l + 1 in pending:
            finish(l + 1, dx)

    loss_lanes, grad_x, d_final_g, sms = _local_step(x[0], loss_target[0], mods, smalls, get_w, final_g.reshape(1, D),
                                                     put_g, end_layer)
    finish(0, grad_x)

    stack = lambda k: jnp.concatenate([sms[l][k].reshape(1, -1) for l in range(L)], axis=0)
    scale_all = jnp.concatenate([mods[l][1] for l in range(L)], axis=0)
    d_norm_g, d_scale = _mod_bwd(stack("d_a"), norm_g, scale_all)
    dmod = jnp.concatenate([stack("d_shift"), d_scale, stack("d_gate")], axis=1)
    small_names = ("norm_g", "b_ada", "pool_scale", "attn_sink", "conv_dw_b", "conv_ln_g", "conv_ln_b", "final_g")
    small_g = (d_norm_g, dmod, stack("pool_scale"), stack("attn_sink"), stack("conv_dw_b"), stack("conv_ln_g"),
               stack("conv_ln_b"), d_final_g.reshape(D))
    small_w = (norm_g, b_ada, pool_scale, attn_sink, conv_dw_b, conv_ln_g, conv_ln_b, final_g)
    small_m = (m_norm_g, m_b_ada, m_pool_scale, m_attn_sink, m_conv_dw_b, m_conv_ln_g, m_conv_ln_b, m_final_g)
    small_v = (v_norm_g, v_b_ada, v_pool_scale, v_attn_sink, v_conv_dw_b, v_conv_ln_g, v_conv_ln_b, v_final_g)
    shapes = [a.shape for a in small_w] + [(D,)]
    n_small = sum(a.size for a in small_w) + D
    R = -(-n_small // (8 * LANE)) * 8
    zero = jnp.zeros((D,), F32)
    small_parts = _all_gather([_pack_small(small_g + (loss_lanes,), R)], "gather_small")[0]
    sg, sd, sm2, sv2 = _small_final(small_parts, _pack_small(small_w + (zero,), R), _pack_small(small_m + (zero,), R),
                                    _pack_small(small_v + (zero + 1.0,), R))
    sg, sd, sm2, sv2 = (_unpack_small(a, shapes) for a in (sg, sd, sm2, sv2))
    loss = jnp.sum(sg[-1])
    res = {n: (sg[i], sd[i], sm2[i], sv2[i]) for i, n in enumerate(small_names)}

    off = norm_g.size
    dmod_all = small_parts.reshape(N_DEV, -1)[:, off:off + L * 3 * D].reshape(N_DEV, L, 3 * D)
    dmod_mine = jnp.transpose(lax.dynamic_slice_in_dim(dmod_all, me * (3 * D // N_DEV), 3 * D // N_DEV, axis=2), (1, 0, 2))
    res["w_ada"] = _wada_bwd(c_all.T, dmod_mine, w_ada, m_w_ada, v_w_ada)

    for k in GRADS:
        shp = shards[k].shape
        to3d = lambda a: a.reshape(L, -1, shp[-1])
        out = _reduce_adamw([parts[l][k].reshape(N_DEV, -1, shp[-1]) for l in range(L)], to3d(shards[k]),
                            to3d(moms[k][0]), to3d(moms[k][1]), "adamw_" + k)
        res[k] = tuple(a.reshape(shp) for a in out)
    res["w_in"] = tuple(tr(a) for a in res["w_in"])

    order = ("norm_g", "w_ada", "b_ada", "w_in", "pool_w", "pool_scale", "attn_sink", "conv_dw", "conv_dw_b", "conv_ln_g",
             "conv_ln_b", "conv_pw", "wbp", "wba", "wbc", "w_out", "final_g")
    outs = [loss, grad_x[None]]
    for j in range(4):
        outs += [res[n][j] for n in order]
    return tuple(outs)
```

```python
import functools

import jax
import jax.numpy as jnp
from jax import lax
from jax.experimental import pallas as pl
from jax.experimental.pallas import tpu as pltpu

F32, BF16 = jnp.float32, jnp.bfloat16
MESH = pl.DeviceIdType.MESH
ANY = pl.BlockSpec(memory_space=pl.ANY)

N_DEV = 8
D = 2048
DEPTH = 4
EPS = 1e-6
IN_WIDTH = 13824
POOL_WINDOWS = (2, 4, 8, 16)
POOL_GROUP = 256
POOL_HALO = 16
CONV_K = 31
CONV_HALO = 32
N_HEADS, N_KV, HEAD_DIM = 16, 4, 64
ATTN_TQ = 256
ATTN_BACK = 128
LANE = 128
VMEM_BIG = 56 * 1024 * 1024

OFF_U, OFF_Z, OFF_Q, OFF_K, OFF_V, OFF_AZ, OFF_CA, OFF_CB, OFF_CZ = 0, 1024, 2048, 3072, 3328, 3584, 4608, 5632, 6656
OFF_GP, OFF_GA, OFF_GC = 7680, 9728, 11776


def _seg(rows, width, off, first_row=None):
    start = (lambda i: i * rows) if first_row is None else first_row
    return pl.BlockSpec((pl.Element(rows), pl.Element(width)), lambda i: (pl.multiple_of(start(i), rows), off))


def _seg2(tm, tn, off):
    return pl.BlockSpec((pl.Element(tm), pl.Element(tn)), lambda j, i: (i * tm, pl.multiple_of(off + j * tn, LANE)))

ADAM_LR, ADAM_B1, ADAM_B2, ADAM_EPS, ADAM_WD, ADAM_STEP = 0.001, 0.9, 0.999, 1e-08, 0.01, 10


def _cp(sem=None, vmem=None):
    return pltpu.CompilerParams(dimension_semantics=sem, vmem_limit_bytes=vmem)


def _sig(x):
    return jax.nn.sigmoid(x)


def _silu(x):
    return x * _sig(x)


def _dsilu(x):
    s = _sig(x)
    return s * (1.0 + x * (1.0 - s))


def _dot(a, b):
    return jnp.dot(a, b, preferred_element_type=F32)


def _dot_tn(a, b):
    return lax.dot_general(a, b, (((0,), (0,)), ((), ())), preferred_element_type=F32)


def _dot_nt(a, b):
    return lax.dot_general(a, b, (((1,), (1,)), ((), ())), preferred_element_type=F32)


def _full(shape):
    n = len(shape)
    return pl.BlockSpec(shape, lambda *_: (0,) * n)


def _my_pos():
    return lax.axis_index("x"), lax.axis_index("y"), lax.axis_index("c")


def _all_gather(xs, name):
    n = len(xs)

    def body(*refs):
        x_refs, o_refs = refs[:n], refs[n:2 * n]
        send_sems, recv_sems, local_sems = refs[2 * n:]
        x, y, c = _my_pos()
        sibling = (x, y, 1 - c)
        chips = [(1 - x, y), (x, 1 - y), (1 - x, 1 - y)]
        me = 4 * x + 2 * y + c

        def slot(px, py, pc):
            return 4 * px + 2 * py + pc

        def copy(t, k, block, to, src=None):
            dst = o_refs[t].at[block]
            return pltpu.make_async_remote_copy(
                src_ref=dst if src is None else src, dst_ref=dst,
                send_sem=send_sems.at[t, k], recv_sem=recv_sems.at[t, k],
                device_id=to, device_id_type=MESH)

        mine = [pltpu.make_async_copy(x_refs[t], o_refs[t].at[me], local_sems.at[t]) for t in range(n)]
        for cp in mine:
            cp.start()
        first = []
        for t in range(n):
            first.append(copy(t, 0, me, sibling, src=x_refs[t]))
            for j, chip in enumerate(chips):
                first.append(copy(t, 1 + j, me, (*chip, c), src=x_refs[t]))
        for cp in first:
            cp.start()
        passed = []
        for j, chip in enumerate(chips):
            for t in range(n):
                copy(t, 1 + j, slot(*chip, c), (x, y, c)).wait_recv()
                fwd = copy(t, 4 + j, slot(*chip, c), sibling)
                fwd.start()
                passed.append(fwd)
        for t in range(n):
            copy(t, 0, slot(x, y, 1 - c), (x, y, c)).wait_recv()
            for j, chip in enumerate(chips):
                copy(t, 4 + j, slot(*chip, 1 - c), (x, y, c)).wait_recv()
        for cp in first + passed:
            cp.wait_send()
        for cp in mine:
            cp.wait()

    return pl.pallas_call(
        body, name=name,
        out_shape=[jax.ShapeDtypeStruct((N_DEV,) + a.shape, a.dtype) for a in xs],
        in_specs=[ANY] * n, out_specs=[ANY] * n,
        scratch_shapes=[pltpu.SemaphoreType.DMA((n, 7)), pltpu.SemaphoreType.DMA((n, 7)),
                        pltpu.SemaphoreType.DMA((n,))],
    )(*xs)


N_PEER = N_DEV - 1
HBM = pl.BlockSpec(memory_space=pltpu.HBM)
SEM = pl.BlockSpec(memory_space=pltpu.SEMAPHORE)
EFFECT = pltpu.SideEffectType.DATAFLOW_SIDE_EFFECTING


def _peer(k):
    x, y, c = _my_pos()
    flip = lambda v, bit: 1 - v if bit else v
    return flip(x, (k >> 2) & 1), flip(y, (k >> 1) & 1), flip(c, k & 1)


def _spread_copies(v_ref, land_ref, send_sems, recv_sems, per_peer):
    x, y, c = _my_pos()
    me = 4 * x + 2 * y + c
    copies = []
    for k in range(1, N_DEV):
        px, py, pc = _peer(k)
        src = v_ref.at[4 * px + 2 * py + pc] if per_peer else v_ref
        copies.append(pltpu.make_async_remote_copy(
            src_ref=src, dst_ref=land_ref.at[me], send_sem=send_sems[k - 1], recv_sem=recv_sems[k - 1],
            device_id=(px, py, pc), device_id_type=MESH))
    return copies


def _spread_start(vs, per_peer, name):
    n = len(vs)
    lands = [(N_DEV,) + (v.shape[1:] if per_peer else v.shape) for v in vs]
    n_sem = 2 * N_PEER * n

    def body(*refs):
        v_refs, land_refs, outs = refs[:n], refs[n:2 * n], refs[2 * n:]
        for t in range(n):
            sems = outs[2 * N_PEER * t:2 * N_PEER * (t + 1)]
            for cp in _spread_copies(v_refs[t], land_refs[t], sems[:N_PEER], sems[N_PEER:], per_peer):
                cp.start()
        token = outs[n_sem + 2 * n]
        token[...] = jnp.zeros_like(token)

    hbm = lambda a: pltpu.with_memory_space_constraint(a, pltpu.HBM)
    return pl.pallas_call(
        body, name=name,
        out_shape=((pltpu.SemaphoreType.DMA(()),) * n_sem + tuple(pltpu.HBM(v.shape, v.dtype) for v in vs)
                   + tuple(pltpu.HBM(s, v.dtype) for s, v in zip(lands, vs)) + (jax.ShapeDtypeStruct((8, LANE), F32),)),
        in_specs=(HBM,) * (2 * n), out_specs=(SEM,) * n_sem + (HBM,) * (2 * n) + (pl.BlockSpec(memory_space=pltpu.VMEM),),
        input_output_aliases={t: n_sem + t for t in range(2 * n)},
        compiler_params=pltpu.CompilerParams(has_side_effects=EFFECT),
    )(*[hbm(v) for v in vs], *[hbm(lax.empty(s, v.dtype)) for s, v in zip(lands, vs)])


def _spread_wait(started, after, per_peer, name):
    n = (len(started) - 1) // (2 * N_PEER + 2)
    n_sem = 2 * N_PEER * n
    sems, thru = started[:n_sem], started[n_sem:n_sem + 2 * n]

    def body(*refs):
        v_refs, land_refs, rest = refs[:n], refs[n:2 * n], refs[2 * n:]
        for t in range(n):
            s = rest[2 * N_PEER * t:2 * N_PEER * (t + 1)]
            for cp in _spread_copies(v_refs[t], land_refs[t], s[:N_PEER], s[N_PEER:], per_peer):
                cp.wait_send()
                cp.wait_recv()

    return pl.pallas_call(
        body, name=name,
        out_shape=tuple(pltpu.HBM(a.shape, a.dtype) for a in thru),
        in_specs=(HBM,) * (2 * n) + (SEM,) * n_sem + (ANY,), out_specs=(HBM,) * (2 * n),
        input_output_aliases={t: t for t in range(2 * n)},
        compiler_params=pltpu.CompilerParams(has_side_effects=EFFECT),
    )(*thru, *sems, after)[n:]


def _mm(a, b, out_dtype, tm, tn, tk=None, name="mm", vmem=None, after=None, nt=False):
    M, K = a.shape
    N = b.shape[0] if nt else b.shape[1]
    tk = K if tk is None else tk
    nk = K // tk
    assert M % tm == 0 and N % tn == 0 and K % tk == 0
    dep = () if after is None else (after,)

    def body(*refs):
        a_ref, b_ref, o_ref, *acc = refs[len(dep):]
        prod = (_dot_nt if nt else _dot)(a_ref[...].astype(BF16), b_ref[...])
        if nk == 1:
            o_ref[...] = prod.astype(out_dtype)
            return
        acc_ref = acc[0] if acc else o_ref
        k = pl.program_id(2)

        @pl.when(k == 0)
        def _():
            acc_ref[...] = prod

        @pl.when(k > 0)
        def _():
            acc_ref[...] += prod

        if acc:
            @pl.when(k == nk - 1)
            def _():
                o_ref[...] = acc_ref[...].astype(out_dtype)

    scratch = [pltpu.VMEM((tm, tn), F32)] if (nk > 1 and out_dtype != F32) else []
    b_spec = pl.BlockSpec((tn, tk), lambda j, i, k: (j, k)) if nt else pl.BlockSpec((tk, tn), lambda j, i, k: (k, j))
    return pl.pallas_call(
        body, name=name, grid=(N // tn, M // tm, nk),
        in_specs=[_full((1, 1))] * len(dep) + [pl.BlockSpec((tm, tk), lambda j, i, k: (i, k)), b_spec],
        out_specs=pl.BlockSpec((tm, tn), lambda j, i, k: (i, j)),
        out_shape=jax.ShapeDtypeStruct((M, N), out_dtype), scratch_shapes=scratch,
        compiler_params=_cp(("parallel", "parallel", "arbitrary"), vmem),
    )(*dep, a, b)


def _mm_tn(a, b, tm, tn, ts, name="mm_tn", vmem=None, after=None, out_dtype=F32):
    S, Ka = a.shape
    _, N = b.shape
    assert Ka % tm == 0 and N % tn == 0 and S % ts == 0
    dep = () if after is None else (after,)

    nk = S // ts

    def body(*refs):
        a_ref, b_ref, o_ref, *acc = refs[len(dep):]
        acc_ref = acc[0] if acc else o_ref
        prod = _dot_tn(a_ref[...].astype(BF16), b_ref[...].astype(BF16))
        k = pl.program_id(2)

        @pl.when(k == 0)
        def _():
            acc_ref[...] = prod

        @pl.when(k > 0)
        def _():
            acc_ref[...] += prod

        if acc:
            @pl.when(k == nk - 1)
            def _():
                o_ref[...] = acc_ref[...].astype(out_dtype)

    return pl.pallas_call(
        body, name=name, grid=(Ka // tm, N // tn, S // ts),
        in_specs=[_full((1, 1))] * len(dep) + [pl.BlockSpec((ts, tm), lambda i, j, k: (k, i)),
                                               pl.BlockSpec((ts, tn), lambda i, j, k: (k, j))],
        out_specs=pl.BlockSpec((tm, tn), lambda i, j, k: (i, j)),
        out_shape=jax.ShapeDtypeStruct((Ka, N), out_dtype),
        scratch_shapes=[] if out_dtype == F32 else [pltpu.VMEM((tm, tn), F32)],
        compiler_params=_cp(("parallel", "parallel", "arbitrary"), vmem),
    )(*dep, a, b)


def _mod_fwd(c_all, w_ada):
    L, _, n = w_ada.shape

    def body(c_ref, w_ref, o_ref):
        ca = _silu(c_ref[...])
        o_ref[0] = jnp.dot(ca, w_ref[0], preferred_element_type=F32, precision=lax.Precision.HIGHEST)

    return pl.pallas_call(
        body, name="mod_fwd", grid=(L,),
        in_specs=[_full((N_DEV, D)), pl.BlockSpec((1, D, n), lambda l: (l, 0, 0))],
        out_specs=pl.BlockSpec((1, N_DEV, n), lambda l: (l, 0, 0)),
        out_shape=jax.ShapeDtypeStruct((L, N_DEV, n), F32),
        compiler_params=_cp(("parallel",)),
    )(c_all, w_ada)


def _adam_math(g, w, m, v):
    m2 = ADAM_B1 * m + (1.0 - ADAM_B1) * g
    v2 = ADAM_B2 * v + (1.0 - ADAM_B2) * (g * g)
    m_hat = m2 / (1.0 - ADAM_B1 ** ADAM_STEP)
    v_hat = v2 / (1.0 - ADAM_B2 ** ADAM_STEP)
    delta = -ADAM_LR * (m_hat / (jnp.sqrt(v_hat) + ADAM_EPS) + ADAM_WD * w)
    return delta, m2, v2


def _wada_bwd(c_all_t, dmod, w, m, v, tr=256):
    L, _, n = w.shape

    def body(c_ref, d_ref, w_ref, m_ref, v_ref, g_ref, dl_ref, m2_ref, v2_ref):
        ca = _silu(c_ref[...])
        dm = d_ref[0]
        g = ca[:, 0:1] * dm[0:1, :]
        for b in range(1, N_DEV):
            g = g + ca[:, b:b + 1] * dm[b:b + 1, :]
        delta, m2, v2 = _adam_math(g, w_ref[0], m_ref[0], v_ref[0])
        g_ref[0], dl_ref[0], m2_ref[0], v2_ref[0] = g, delta, m2, v2

    blk = pl.BlockSpec((1, tr, n), lambda l, i: (l, i, 0))
    return pl.pallas_call(
        body, name="wada_bwd", grid=(L, D // tr),
        in_specs=[pl.BlockSpec((tr, N_DEV), lambda l, i: (i, 0)), pl.BlockSpec((1, N_DEV, n), lambda l, i: (l, 0, 0)),
                  blk, blk, blk],
        out_specs=[blk] * 4, out_shape=[jax.ShapeDtypeStruct(w.shape, F32)] * 4,
        compiler_params=_cp(("parallel", "parallel")),
    )(c_all_t, dmod, w, m, v)


def _norm_mod(x, g, scale, shift, tb=256):
    S = x.shape[0]

    def body(x_ref, g_ref, sc_ref, sh_ref, h_ref):
        xv = x_ref[...]
        r = lax.rsqrt(jnp.mean(xv * xv, axis=-1, keepdims=True) + EPS)
        h_ref[...] = (xv * r * (g_ref[...] * (1.0 + sc_ref[...])) + sh_ref[...]).astype(BF16)

    row = pl.BlockSpec((tb, D), lambda i: (i, 0))
    vec = _full((1, D))
    return pl.pallas_call(
        body, name="norm_mod", grid=(S // tb,), in_specs=[row, vec, vec, vec], out_specs=row,
        out_shape=jax.ShapeDtypeStruct((S, D), BF16), compiler_params=_cp(("parallel",)),
    )(x, g, scale, shift)


def _pool_mix(u_ref, uh_ref, ubuf, i, tb):
    H = POOL_HALO
    ubuf[H:, :] = u_ref[...].astype(F32)
    ubuf[:H, :] = jnp.where(i > 0, uh_ref[...].astype(F32), 0.0)
    t = i * tb + lax.broadcasted_iota(jnp.int32, (tb, 1), 0)
    mixed = []
    for g, w in enumerate(POOL_WINDOWS):
        cs = slice(g * POOL_GROUP, (g + 1) * POOL_GROUP)
        cur = ubuf[H:, cs]
        acc = cur
        for j in range(1, w):
            acc = acc + ubuf[pl.ds(H - j, tb), cs]
        cnt = jnp.minimum(t + 1, w).astype(F32)
        mixed.append(acc / cnt - cur)
    return mixed


def _pool_specs(tb):
    H = POOL_HALO
    u = _seg(tb, 1024, OFF_U)
    uh = _seg(H, 1024, OFF_U, lambda i: jnp.maximum(i * tb - H, 0))
    z = _seg(tb, 1024, OFF_Z)
    return u, uh, z


def _pool_fwd(proj, pool_w, pool_scale, tb=256):
    S = proj.shape[0]

    def body(u_ref, uh_ref, z_ref, w_ref, sc_ref, y_ref, ubuf):
        i = pl.program_id(0)
        mixed = _pool_mix(u_ref, uh_ref, ubuf, i, tb)
        m = jnp.concatenate([_dot(mixed[g].astype(BF16), w_ref[g]) for g in range(4)], axis=1)
        y_ref[...] = (m * sc_ref[...] * _silu(z_ref[...].astype(F32))).astype(BF16)

    u, uh, z = _pool_specs(tb)
    return pl.pallas_call(
        body, name="pool_fwd", grid=(S // tb,),
        in_specs=[u, uh, z, _full((4, 256, 256)), _full((1, 1024))],
        out_specs=pl.BlockSpec((tb, 1024), lambda i: (i, 0)),
        out_shape=jax.ShapeDtypeStruct((S, 1024), BF16),
        scratch_shapes=[pltpu.VMEM((tb + POOL_HALO, 1024), F32)],
        compiler_params=_cp(("parallel",)),
    )(proj, proj, proj, pool_w, pool_scale)


def _attn_mask(i):
    TQ, NK = ATTN_TQ, ATTN_TQ + ATTN_BACK
    qc = lax.broadcasted_iota(jnp.int32, (TQ, NK), 0) // 64
    col = lax.broadcasted_iota(jnp.int32, (TQ, NK), 1)
    kc = col // 64
    return (kc >= qc) & (kc <= qc + 2) & ((col >= ATTN_BACK) | (i > 0))


def _block_diag(kbuf, vbuf, kbd, vbd, kh):
    NK = ATTN_TQ + ATTN_BACK
    ks = slice(kh * HEAD_DIM, (kh + 1) * HEAD_DIM)
    for g in range(N_HEADS // N_KV):
        kbd[g * NK:(g + 1) * NK, g * HEAD_DIM:(g + 1) * HEAD_DIM] = kbuf[:, ks]
        vbd[g * NK:(g + 1) * NK, g * HEAD_DIM:(g + 1) * HEAD_DIM] = vbuf[:, ks]


def _attn_probs(q_ref, kw, sink_ref, valid, h):
    qh = q_ref[:, h * HEAD_DIM:(h + 1) * HEAD_DIM]
    return _softmax_sink(_dot_nt(qh, kw) * (HEAD_DIM ** -0.5), sink_ref, valid, h)


def _softmax_sink(s, sink_ref, valid, h):
    s = jnp.where(valid, s, -jnp.inf)
    sk = sink_ref[h:h + 1, 0:1]
    mx = jnp.maximum(jnp.max(s, axis=-1, keepdims=True), sk)
    p = jnp.exp(s - mx)
    es = jnp.exp(sk - mx)
    den = jnp.sum(p, axis=-1, keepdims=True) + es
    return p / den, es / den


def _kv_specs():
    TQ, B = ATTN_TQ, ATTN_BACK
    blk = lambda off: _seg(TQ, 256, off)
    halo = lambda off: _seg(B, 256, off, lambda i: jnp.maximum(i * TQ - B, 0))
    return [blk(OFF_K), halo(OFF_K), blk(OFF_V), halo(OFF_V)]


def _kv_window(k_ref, kh_ref, v_ref, vh_ref, kbuf, vbuf, i):
    B = ATTN_BACK
    for buf, ref, href in ((kbuf, k_ref, kh_ref), (vbuf, v_ref, vh_ref)):
        buf[:B, :] = jnp.where(i > 0, href[...], jnp.zeros_like(href))
        buf[B:, :] = ref[...]


def _attn_fwd(proj, sink_b):
    S = proj.shape[0]
    TQ, NK = ATTN_TQ, ATTN_TQ + ATTN_BACK

    G = N_HEADS // N_KV

    def body(q_ref, z_ref, k_ref, kh_ref, v_ref, vh_ref, sink_ref, o_ref, y_ref, kbuf, vbuf, kbd, vbd):
        i = pl.program_id(0)
        _kv_window(k_ref, kh_ref, v_ref, vh_ref, kbuf, vbuf, i)
        valid = _attn_mask(i)
        kbd[...] = jnp.zeros_like(kbd)
        vbd[...] = jnp.zeros_like(vbd)
        for kh in range(N_KV):
            _block_diag(kbuf, vbuf, kbd, vbd, kh)
            gs = slice(kh * G * HEAD_DIM, (kh + 1) * G * HEAD_DIM)
            s_all = _dot_nt(q_ref[:, gs], kbd[...]) * (HEAD_DIM ** -0.5)
            pn = [_softmax_sink(s_all[:, g * NK:(g + 1) * NK], sink_ref, valid, kh * G + g)[0].astype(BF16) for g in range(G)]
            o = _dot(jnp.concatenate(pn, axis=1), vbd[...])
            o_ref[:, gs] = o.astype(BF16)
            y_ref[:, gs] = (o * _silu(z_ref[:, gs].astype(F32))).astype(BF16)

    out = pl.BlockSpec((TQ, 1024), lambda i: (i, 0))
    return pl.pallas_call(
        body, name="attn_fwd", grid=(S // TQ,),
        in_specs=[_seg(TQ, 1024, OFF_Q), _seg(TQ, 1024, OFF_AZ), *_kv_specs(), _full((N_HEADS, LANE))],
        out_specs=[out, out], out_shape=[jax.ShapeDtypeStruct((S, 1024), BF16)] * 2,
        scratch_shapes=[pltpu.VMEM((NK, 256), BF16)] * 2 + [pltpu.VMEM((G * NK, G * HEAD_DIM), BF16)] * 2,
        compiler_params=_cp(("parallel",)),
    )(proj, proj, proj, proj, proj, proj, sink_b)


def _conv_specs(tb):
    H = CONV_HALO
    prev = lambda i: jnp.maximum(i * tb - H, 0)
    a = _seg(tb, 1024, OFF_CA)
    ah = _seg(H, 1024, OFF_CA, prev)
    b = _seg(tb, 1024, OFF_CB)
    bh = _seg(H, 1024, OFF_CB, prev)
    z = _seg(tb, 1024, OFF_CZ)
    return a, ah, b, bh, z


SUBLANES = 8
CONV_TAPS = tuple((j, CONV_HALO - (CONV_K - 1) + j) for j in range(CONV_K))
CONV_TAPS_T = tuple((j, (CONV_K - 1) - j) for j in range(CONV_K))


def _lane_chunks(fn):
    def body(c, carry):
        fn(pl.ds(pl.multiple_of(c * LANE, LANE), LANE))
        return carry

    lax.fori_loop(0, 1024 // LANE, body, 0)


def _shifted_tiles(src_ref, cs, s, n):
    row = lax.broadcasted_iota(jnp.int32, (SUBLANES, LANE), 0)
    prev = None
    for t in range(n + (1 if s else 0)):
        v = src_ref[pl.ds(SUBLANES * t, SUBLANES), cs]
        if s == 0:
            yield t, v
            continue
        x = pltpu.roll(v, SUBLANES - s, 0)
        if prev is not None:
            yield t - 1, jnp.where(row < SUBLANES - s, prev, x)
        prev = x


def _by_shift(taps):
    groups = {}
    for j, o in taps:
        groups.setdefault(o % SUBLANES, []).append((j, o // SUBLANES))
    return sorted(groups.items())


def _taps_apply(src_ref, w_ref, dst_ref, tb, taps, cs):
    nu = tb // SUBLANES
    acc = [None] * nu
    for s, group in _by_shift(taps):
        w = {j: w_ref[j:j + 1, cs] for j, _ in group}
        for t, g in _shifted_tiles(src_ref, cs, s, nu + max(a for _, a in group)):
            for j, a in group:
                if 0 <= t - a < nu:
                    term = w[j] * g
                    acc[t - a] = term if acc[t - a] is None else acc[t - a] + term
    dst_ref[:, cs] = jnp.concatenate(acc, axis=0)


def _taps_reduce(src_ref, d_ref, out_ref, tb, taps, cs):
    nu = tb // SUBLANES
    d = [d_ref[pl.ds(SUBLANES * u, SUBLANES), cs] for u in range(nu)]
    for s, group in _by_shift(taps):
        part = {j: None for j, _ in group}
        for t, g in _shifted_tiles(src_ref, cs, s, nu + max(a for _, a in group)):
            for j, a in group:
                if 0 <= t - a < nu:
                    term = d[t - a] * g
                    part[j] = term if part[j] is None else part[j] + term
        for j, _ in group:
            out_ref[j:j + 1, cs] += jnp.sum(part[j], axis=0, keepdims=True)


def _conv_glu_dw(a_ref, ah_ref, b_ref, bh_ref, dw_ref, gbuf, ybuf, i, tb):
    H = CONV_HALO
    gbuf[H:, :] = a_ref[...].astype(F32) * _sig(b_ref[...].astype(F32))
    gh = ah_ref[...].astype(F32) * _sig(bh_ref[...].astype(F32))
    gbuf[:H, :] = jnp.where(i > 0, gh, 0.0)
    _lane_chunks(lambda cs: _taps_apply(gbuf, dw_ref, ybuf, tb, CONV_TAPS, cs))


def _layer_norm_fwd(y, g, b):
    mu = jnp.mean(y, axis=-1, keepdims=True)
    yc = y - mu
    rstd = lax.rsqrt(jnp.mean(yc * yc, axis=-1, keepdims=True) + EPS)
    xh = yc * rstd
    return xh, rstd, xh * g + b


def _conv_fwd(proj, dw, dw_b, ln_g, ln_b, pw, tb=256):
    S = proj.shape[0]

    def body(a_ref, ah_ref, b_ref, bh_ref, z_ref, dw_ref, dwb_ref, lg_ref, lb_ref, pw_ref, y_ref, gbuf, ybuf):
        i = pl.program_id(0)
        _conv_glu_dw(a_ref, ah_ref, b_ref, bh_ref, dw_ref, gbuf, ybuf, i, tb)
        _, _, yn = _layer_norm_fwd(ybuf[...] + dwb_ref[...], lg_ref[...], lb_ref[...])
        out = _dot(_silu(yn).astype(BF16), pw_ref[...])
        y_ref[...] = (out * _silu(z_ref[...].astype(F32))).astype(BF16)

    vec = _full((1, 1024))
    return pl.pallas_call(
        body, name="conv_fwd", grid=(S // tb,),
        in_specs=[*_conv_specs(tb), _full((32, 1024)), vec, vec, vec, _full((1024, 1024))],
        out_specs=pl.BlockSpec((tb, 1024), lambda i: (i, 0)),
        out_shape=jax.ShapeDtypeStruct((S, 1024), BF16),
        scratch_shapes=[pltpu.VMEM((tb + CONV_HALO, 1024), F32), pltpu.VMEM((tb, 1024), F32)],
        compiler_params=_cp(("parallel",)),
    )(proj, proj, proj, proj, proj, dw, dw_b, ln_g, ln_b, pw)


def _merge_fwd(yp, ya, yc, wbp, wba, wbc, proj, tm=512, tn=1024):
    S = yp.shape[0]

    def body(yp_ref, ya_ref, yc_ref, wp_ref, wa_ref, wc_ref, gp_ref, ga_ref, gc_ref, m_ref, pp_ref, pa_ref, pc_ref):
        pp = _dot(yp_ref[...], wp_ref[...])
        pa = _dot(ya_ref[...], wa_ref[...])
        pc = _dot(yc_ref[...], wc_ref[...])
        m = (_sig(gp_ref[...].astype(F32)) * pp + _sig(ga_ref[...].astype(F32)) * pa
             + _sig(gc_ref[...].astype(F32)) * pc)
        m_ref[...] = m.astype(BF16)
        pp_ref[...], pa_ref[...], pc_ref[...] = pp.astype(BF16), pa.astype(BF16), pc.astype(BF16)

    yb = pl.BlockSpec((tm, 1024), lambda j, i: (i, 0))
    wb = pl.BlockSpec((1024, tn), lambda j, i: (0, j))
    out = pl.BlockSpec((tm, tn), lambda j, i: (i, j))
    return pl.pallas_call(
        body, name="merge_fwd", grid=(D // tn, S // tm),
        in_specs=[yb, yb, yb, wb, wb, wb, _seg2(tm, tn, OFF_GP), _seg2(tm, tn, OFF_GA), _seg2(tm, tn, OFF_GC)],
        out_specs=[out] * 4, out_shape=[jax.ShapeDtypeStruct((S, D), BF16)] * 4,
        compiler_params=_cp(("parallel", "parallel")),
    )(yp, ya, yc, wbp, wba, wbc, proj, proj, proj)


def _out_fwd(x, merged, w_out, gate, tm=512, tn=1024):
    S = x.shape[0]

    def body(x_ref, m_ref, w_ref, g_ref, o_ref):
        o_ref[...] = x_ref[...] + g_ref[...] * _dot(m_ref[...], w_ref[...])

    xb = pl.BlockSpec((tm, tn), lambda j, i: (i, j))
    return pl.pallas_call(
        body, name="out_fwd", grid=(D // tn, S // tm),
        in_specs=[xb, pl.BlockSpec((tm, D), lambda j, i: (i, 0)), pl.BlockSpec((D, tn), lambda j, i: (0, j)),
                  pl.BlockSpec((1, tn), lambda j, i: (0, j))],
        out_specs=xb, out_shape=jax.ShapeDtypeStruct((S, D), F32),
        compiler_params=_cp(("parallel", "parallel")),
    )(x, merged, w_out, gate)


def _final_loss(x, target, final_g, tb=256):
    S = x.shape[0]

    def body(x_ref, t_ref, g_ref, dx_ref, gg_ref, ls_ref):
        i = pl.program_id(0)
        xv, g = x_ref[...], g_ref[...]
        r = lax.rsqrt(jnp.mean(xv * xv, axis=-1, keepdims=True) + EPS)
        xh = xv * r
        e = xh * g - t_ref[...]
        dy = e * (1.0 / D)
        gy = dy * g
        dx_ref[...] = r * (gy - xh * jnp.mean(gy * xh, axis=-1, keepdims=True))
        gg = jnp.sum(dy * xh, axis=0, keepdims=True)
        ls = jnp.sum(e * e, axis=0, keepdims=True) * (0.5 / D)

        @pl.when(i == 0)
        def _():
            gg_ref[...], ls_ref[...] = gg, ls

        @pl.when(i > 0)
        def _():
            gg_ref[...] += gg
            ls_ref[...] += ls

    row = pl.BlockSpec((tb, D), lambda i: (i, 0))
    vec = _full((1, D))
    return pl.pallas_call(
        body, name="final_loss", grid=(S // tb,), in_specs=[row, row, vec], out_specs=[row, vec, vec],
        out_shape=[jax.ShapeDtypeStruct((S, D), F32), jax.ShapeDtypeStruct((1, D), F32),
                   jax.ShapeDtypeStruct((1, D), F32)],
        compiler_params=_cp(("arbitrary",)),
    )(x, target, final_g)


def _out_bwd(dx, gate, w_out, pp, pa, pc, proj, tm=512, tn=1024):
    S = dx.shape[0]

    def body(dx_ref, g_ref, w_ref, pp_ref, pa_ref, pc_ref, gp_ref, ga_ref, gc_ref,
             dpp_ref, dpa_ref, dpc_ref, dgp_ref, dga_ref, dgc_ref):
        dm = _dot_nt((dx_ref[...] * g_ref[...]).astype(BF16), w_ref[...])
        for p_ref, gl_ref, dp_ref, dg_ref in ((pp_ref, gp_ref, dpp_ref, dgp_ref), (pa_ref, ga_ref, dpa_ref, dga_ref),
                                              (pc_ref, gc_ref, dpc_ref, dgc_ref)):
            s = _sig(gl_ref[...].astype(F32))
            dp_ref[...] = (dm * s).astype(BF16)
            dg_ref[...] = (dm * p_ref[...].astype(F32) * s * (1.0 - s)).astype(BF16)

    out = pl.BlockSpec((tm, tn), lambda j, i: (i, j))
    return pl.pallas_call(
        body, name="out_bwd", grid=(D // tn, S // tm),
        in_specs=[pl.BlockSpec((tm, D), lambda j, i: (i, 0)), _full((1, D)), pl.BlockSpec((tn, D), lambda j, i: (j, 0)),
                  out, out, out, _seg2(tm, tn, OFF_GP), _seg2(tm, tn, OFF_GA), _seg2(tm, tn, OFF_GC)],
        out_specs=[out] * 6, out_shape=[jax.ShapeDtypeStruct((S, D), BF16)] * 6,
        compiler_params=_cp(("parallel", "parallel")),
    )(dx, gate, w_out, pp, pa, pc, proj, proj, proj)


def _wout_post(gmat, w_out, gate, tr=256):
    def body(g_ref, w_ref, gate_ref, dw_ref, dg_ref):
        i = pl.program_id(0)
        gm = g_ref[...]
        dw_ref[...] = gm * gate_ref[...]
        part = jnp.sum(gm * w_ref[...].astype(F32), axis=0, keepdims=True)

        @pl.when(i == 0)
        def _():
            dg_ref[...] = part

        @pl.when(i > 0)
        def _():
            dg_ref[...] += part

    row = pl.BlockSpec((tr, D), lambda i: (i, 0))
    return pl.pallas_call(
        body, name="wout_post", grid=(D // tr,), in_specs=[row, row, _full((1, D))], out_specs=[row, _full((1, D))],
        out_shape=[jax.ShapeDtypeStruct((D, D), F32), jax.ShapeDtypeStruct((1, D), F32)],
        compiler_params=_cp(("arbitrary",)),
    )(gmat, w_out, gate)


def _pool_bwd_a(dy, proj, pool_w, pool_scale, dproj, tb=256):
    S = proj.shape[0]

    def body(dy_ref, u_ref, uh_ref, z_ref, w_ref, sc_ref, _, dmix_ref, dz_ref, dsc_ref, dw_ref, ubuf):
        i = pl.program_id(0)
        mixed = [m.astype(BF16) for m in _pool_mix(u_ref, uh_ref, ubuf, i, tb)]
        m = jnp.concatenate([_dot(mixed[g], w_ref[g]) for g in range(4)], axis=1)
        dyv, z, sc = dy_ref[...].astype(F32), z_ref[...].astype(F32), sc_ref[...]
        dyp = dyv * _silu(z)
        dz_ref[...] = (dyv * (m * sc) * _dsilu(z)).astype(BF16)
        dsc = jnp.sum(dyp * m, axis=0, keepdims=True)
        dmm = (dyp * sc).astype(BF16)
        dws = []
        for g in range(4):
            cs = slice(g * POOL_GROUP, (g + 1) * POOL_GROUP)
            dmix_ref[:, cs] = _dot_nt(dmm[:, cs], w_ref[g])
            dws.append(_dot_tn(mixed[g], dmm[:, cs]))

        @pl.when(i == 0)
        def _():
            dsc_ref[...] = dsc
            for g in range(4):
                dw_ref[g] = dws[g]

        @pl.when(i > 0)
        def _():
            dsc_ref[...] += dsc
            for g in range(4):
                dw_ref[g] += dws[g]

    u, uh, z = _pool_specs(tb)
    row = pl.BlockSpec((tb, 1024), lambda i: (i, 0))
    wfull = _full((4, 256, 256))
    return pl.pallas_call(
        body, name="pool_bwd_a", grid=(S // tb,),
        in_specs=[row, u, uh, z, wfull, _full((1, 1024)), ANY],
        out_specs=[row, _seg(tb, 1024, OFF_Z), _full((1, 1024)), wfull],
        out_shape=[jax.ShapeDtypeStruct((S, 1024), F32), jax.ShapeDtypeStruct(dproj.shape, BF16),
                   jax.ShapeDtypeStruct((1, 1024), F32), jax.ShapeDtypeStruct((4, 256, 256), F32)],
        scratch_shapes=[pltpu.VMEM((tb + POOL_HALO, 1024), F32)], input_output_aliases={6: 1},
        compiler_params=_cp(("arbitrary",)),
    )(dy, proj, proj, proj, pool_w, pool_scale, dproj)


def _pool_bwd_b(dmix, dproj, tb=256):
    S = dmix.shape[0]
    H = POOL_HALO
    nb = S // tb

    def body(dm_ref, dh_ref, _, du_ref, ebuf):
        i = pl.program_id(0)
        t = i * tb + lax.broadcasted_iota(jnp.int32, (tb, 1), 0)
        th = (i + 1) * tb + lax.broadcasted_iota(jnp.int32, (H, 1), 0)
        for g, w in enumerate(POOL_WINDOWS):
            cs = slice(g * POOL_GROUP, (g + 1) * POOL_GROUP)
            ebuf[:tb, cs] = dm_ref[:, cs] / jnp.minimum(t + 1, w).astype(F32)
            eh = dh_ref[:, cs] / jnp.minimum(th + 1, w).astype(F32)
            ebuf[tb:, cs] = jnp.where(i < nb - 1, eh, 0.0)
        for g, w in enumerate(POOL_WINDOWS):
            cs = slice(g * POOL_GROUP, (g + 1) * POOL_GROUP)
            acc = ebuf[:tb, cs]
            for j in range(1, w):
                acc = acc + ebuf[pl.ds(j, tb), cs]
            du_ref[:, cs] = (acc - dm_ref[:, cs]).astype(BF16)

    row = pl.BlockSpec((tb, 1024), lambda i: (i, 0))
    nxt = pl.BlockSpec((H, 1024), lambda i: (jnp.minimum((i + 1) * (tb // H), S // H - 1), 0))
    return pl.pallas_call(
        body, name="pool_bwd_b", grid=(nb,), in_specs=[row, nxt, ANY], out_specs=_seg(tb, 1024, OFF_U),
        out_shape=jax.ShapeDtypeStruct(dproj.shape, BF16),
        scratch_shapes=[pltpu.VMEM((tb + H, 1024), F32)], input_output_aliases={2: 0},
        compiler_params=_cp(("parallel",)),
    )(dmix, dmix, dproj)


def _attn_bwd(dy, o, proj, sink_b, dproj):
    S = proj.shape[0]
    TQ, NK = ATTN_TQ, ATTN_TQ + ATTN_BACK
    nb = S // TQ
    G = N_HEADS // N_KV

    QZ = OFF_AZ + 1024 - OFF_Q

    def body(dy_ref, o_ref, q_ref, z_ref, k_ref, kh_ref, v_ref, vh_ref, sink_ref, _, dqz_ref, dk_hbm, dv_hbm, ds_ref,
             dk_acc, dv_acc, kbuf, vbuf):
        i = pl.program_id(0)
        dq_ref, dz_ref = dqz_ref.at[:, :1024], dqz_ref.at[:, OFF_AZ - OFF_Q:]
        dqz_ref[:, 1024:OFF_AZ - OFF_Q] = jnp.zeros((TQ, OFF_AZ - OFF_Q - 1024), BF16)
        _kv_window(k_ref, kh_ref, v_ref, vh_ref, kbuf, vbuf, i)

        @pl.when(i == 0)
        def _():
            dk_acc[...] = jnp.zeros_like(dk_acc)
            dv_acc[...] = jnp.zeros_like(dv_acc)
            ds_ref[...] = jnp.zeros_like(ds_ref)

        start = pl.multiple_of(i * TQ, TQ)
        valid = _attn_mask(i)
        dks, dvs = [], []
        for kh in range(N_KV):
            ks = slice(kh * HEAD_DIM, (kh + 1) * HEAD_DIM)
            kw = kbuf[:, ks]
            vw = vbuf[:, ks]
            dk_sum = jnp.zeros((NK, HEAD_DIM), F32)
            dv_sum = jnp.zeros((NK, HEAD_DIM), F32)
            dqs = []
            for gi in range(G):
                h = kh * G + gi
                hs = slice(h * HEAD_DIM, (h + 1) * HEAD_DIM)
                pn, psink = _attn_probs(q_ref, kw, sink_ref, valid, h)
                ov = o_ref[:, hs].astype(F32)
                do = dy_ref[:, hs].astype(F32) * _silu(z_ref[:, hs].astype(F32))
                delta = jnp.sum(do * ov, axis=-1, keepdims=True)
                dob = do.astype(BF16)
                dp = _dot_nt(dob, vw)
                ds = (pn * (dp - delta)).astype(BF16)
                dsink = -jnp.sum(psink * delta, axis=0, keepdims=True)
                ds_ref[h:h + 1, :] += jnp.broadcast_to(dsink, (1, LANE))
                dqs.append(_dot(ds, kw) * (HEAD_DIM ** -0.5))
                dk_sum = dk_sum + _dot_tn(ds, q_ref[:, hs])
                dv_sum = dv_sum + _dot_tn(pn.astype(BF16), dob)
            gs = slice(kh * G * HEAD_DIM, (kh + 1) * G * HEAD_DIM)
            dq_ref[:, gs] = jnp.concatenate(dqs, axis=1).astype(BF16)
            z = z_ref[:, gs].astype(F32)
            dz_ref[:, gs] = (dy_ref[:, gs].astype(F32) * o_ref[:, gs].astype(F32) * _dsilu(z)).astype(BF16)
            dks.append(dk_sum * (HEAD_DIM ** -0.5))
            dvs.append(dv_sum)
        dk_acc[pl.ds(start, NK), :] += jnp.concatenate(dks, axis=1)
        dv_acc[pl.ds(start, NK), :] += jnp.concatenate(dvs, axis=1)

        @pl.when(i == nb - 1)
        def _():
            pltpu.sync_copy(dk_acc, dk_hbm)
            pltpu.sync_copy(dv_acc, dv_hbm)

    row = pl.BlockSpec((TQ, 1024), lambda i: (i, 0))
    return pl.pallas_call(
        body, name="attn_bwd", grid=(nb,),
        in_specs=[row, row, _seg(TQ, 1024, OFF_Q), _seg(TQ, 1024, OFF_AZ), *_kv_specs(), _full((N_HEADS, LANE)), ANY],
        out_specs=[_seg(TQ, QZ, OFF_Q), ANY, ANY, _full((N_HEADS, LANE))],
        out_shape=[jax.ShapeDtypeStruct(dproj.shape, BF16),
                   jax.ShapeDtypeStruct((S + ATTN_BACK, 256), F32), jax.ShapeDtypeStruct((S + ATTN_BACK, 256), F32),
                   jax.ShapeDtypeStruct((N_HEADS, LANE), F32)],
        scratch_shapes=[pltpu.VMEM((S + ATTN_BACK, 256), F32)] * 2 + [pltpu.VMEM((NK, 256), BF16)] * 2,
        input_output_aliases={9: 0}, compiler_params=_cp(("arbitrary",), VMEM_BIG),
    )(dy, o, proj, proj, proj, proj, proj, proj, sink_b, dproj)


def _conv_bwd_a(dy, proj, dw, dw_b, ln_g, ln_b, pw, dproj, tb=256):
    S = proj.shape[0]
    H = CONV_HALO

    def body(dy_ref, a_ref, ah_ref, b_ref, bh_ref, z_ref, dw_ref, dwb_ref, lg_ref, lb_ref, pw_ref, _,
             dcv_ref, dz_ref, dpw_ref, dlg_ref, dlb_ref, ddwb_ref, ddw_ref, gbuf, ybuf):
        i = pl.program_id(0)
        _conv_glu_dw(a_ref, ah_ref, b_ref, bh_ref, dw_ref, gbuf, ybuf, i, tb)
        lg = lg_ref[...]
        xh, rstd, yn = _layer_norm_fwd(ybuf[...] + dwb_ref[...], lg, lb_ref[...])
        u = _silu(yn).astype(BF16)
        out = _dot(u, pw_ref[...])
        dyv, z = dy_ref[...].astype(F32), z_ref[...].astype(F32)
        dz_ref[...] = (dyv * out * _dsilu(z)).astype(BF16)
        dout = (dyv * _silu(z)).astype(BF16)
        dpw = _dot_tn(u, dout)
        dyn = _dot_nt(dout, pw_ref[...]) * _dsilu(yn)
        dlg = jnp.sum(dyn * xh, axis=0, keepdims=True)
        dlb = jnp.sum(dyn, axis=0, keepdims=True)
        dxh = dyn * lg
        dcv = rstd * (dxh - jnp.mean(dxh, axis=-1, keepdims=True) - xh * jnp.mean(dxh * xh, axis=-1, keepdims=True))
        dcv_ref[...] = dcv
        ddwb = jnp.sum(dcv, axis=0, keepdims=True)
        ybuf[...] = dcv

        @pl.when(i == 0)
        def _():
            dpw_ref[...], dlg_ref[...], dlb_ref[...], ddwb_ref[...] = dpw, dlg, dlb, ddwb
            ddw_ref[...] = jnp.zeros_like(ddw_ref)

        @pl.when(i > 0)
        def _():
            dpw_ref[...] += dpw
            dlg_ref[...] += dlg
            dlb_ref[...] += dlb
            ddwb_ref[...] += ddwb

        _lane_chunks(lambda cs: _taps_reduce(gbuf, ybuf, ddw_ref, tb, CONV_TAPS, cs))

    vec = _full((1, 1024))
    row = pl.BlockSpec((tb, 1024), lambda i: (i, 0))
    big = _full((1024, 1024))
    return pl.pallas_call(
        body, name="conv_bwd_a", grid=(S // tb,),
        in_specs=[row, *_conv_specs(tb), _full((32, 1024)), vec, vec, vec, big, ANY],
        out_specs=[row, _seg(tb, 1024, OFF_CZ), big, vec, vec, vec, _full((32, 1024))],
        out_shape=[jax.ShapeDtypeStruct((S, 1024), F32), jax.ShapeDtypeStruct(dproj.shape, BF16),
                   jax.ShapeDtypeStruct((1024, 1024), F32), jax.ShapeDtypeStruct((1, 1024), F32),
                   jax.ShapeDtypeStruct((1, 1024), F32), jax.ShapeDtypeStruct((1, 1024), F32),
                   jax.ShapeDtypeStruct((32, 1024), F32)],
        scratch_shapes=[pltpu.VMEM((tb + H, 1024), F32), pltpu.VMEM((tb, 1024), F32)],
        input_output_aliases={11: 1}, compiler_params=_cp(("arbitrary",)),
    )(dy, proj, proj, proj, proj, proj, dw, dw_b, ln_g, ln_b, pw, dproj)


def _conv_bwd_b(dcv, proj, dw, dproj, tb=256):
    S = proj.shape[0]
    H = CONV_HALO
    nb = S // tb

    def body(d_ref, dn_ref, a_ref, b_ref, dw_ref, _, dab_ref, dbuf, gbuf):
        i = pl.program_id(0)
        dbuf[:tb, :] = d_ref[...]
        dbuf[tb:, :] = jnp.where(i < nb - 1, dn_ref[...], 0.0)
        _lane_chunks(lambda cs: _taps_apply(dbuf, dw_ref, gbuf, tb, CONV_TAPS_T, cs))
        dg = gbuf[...]
        a, s = a_ref[...].astype(F32), _sig(b_ref[...].astype(F32))
        dab_ref[:, :1024] = (dg * s).astype(BF16)
        dab_ref[:, 1024:] = (dg * a * s * (1.0 - s)).astype(BF16)

    row = pl.BlockSpec((tb, 1024), lambda i: (i, 0))
    nxt = pl.BlockSpec((H, 1024), lambda i: (jnp.minimum((i + 1) * (tb // H), S // H - 1), 0))
    return pl.pallas_call(
        body, name="conv_bwd_b", grid=(nb,),
        in_specs=[row, nxt, _seg(tb, 1024, OFF_CA), _seg(tb, 1024, OFF_CB), _full((32, 1024)), ANY],
        out_specs=_seg(tb, 2048, OFF_CA), out_shape=jax.ShapeDtypeStruct(dproj.shape, BF16),
        scratch_shapes=[pltpu.VMEM((tb + H, 1024), F32), pltpu.VMEM((tb, 1024), F32)],
        input_output_aliases={5: 0}, compiler_params=_cp(("parallel",)),
    )(dcv, dcv, proj, proj, dw, dproj)


def _norm_bwd(dh, x, dx_out, g, scale, tb=256):
    S = x.shape[0]

    def body(dh_ref, x_ref, dxo_ref, g_ref, sc_ref, dx_ref, dsh_ref, da_ref):
        i = pl.program_id(0)
        xv, dhv = x_ref[...], dh_ref[...]
        r = lax.rsqrt(jnp.mean(xv * xv, axis=-1, keepdims=True) + EPS)
        xh = xv * r
        gy = dhv * (g_ref[...] * (1.0 + sc_ref[...]))
        dx_ref[...] = dxo_ref[...] + r * (gy - xh * jnp.mean(gy * xh, axis=-1, keepdims=True))
        dsh = jnp.sum(dhv, axis=0, keepdims=True)
        da = jnp.sum(dhv * xh, axis=0, keepdims=True)

        @pl.when(i == 0)
        def _():
            dsh_ref[...], da_ref[...] = dsh, da

        @pl.when(i > 0)
        def _():
            dsh_ref[...] += dsh
            da_ref[...] += da

    row = pl.BlockSpec((tb, D), lambda i: (i, 0))
    vec = _full((1, D))
    return pl.pallas_call(
        body, name="norm_bwd", grid=(S // tb,), in_specs=[row, row, row, vec, vec], out_specs=[row, vec, vec],
        out_shape=[jax.ShapeDtypeStruct((S, D), F32), jax.ShapeDtypeStruct((1, D), F32), jax.ShapeDtypeStruct((1, D), F32)],
        compiler_params=_cp(("arbitrary",)),
    )(dh, x, dx_out, g, scale)


def _mod_bwd(d_a, norm_g, scale):
    def body(da_ref, g_ref, sc_ref, dg_ref, dsc_ref):
        dg_ref[...] = da_ref[...] * (1.0 + sc_ref[...])
        dsc_ref[...] = da_ref[...] * g_ref[...]

    return pl.pallas_call(body, name="mod_bwd", out_shape=[jax.ShapeDtypeStruct(d_a.shape, F32)] * 2)(d_a, norm_g, scale)


def _reduce_adamw(parts, w, m, v, name):
    L, rows, C = w.shape
    tr = rows if rows % 64 else 64

    def body(*refs):
        p_refs, (w_ref, m_ref, v_ref, g_ref, dl_ref, m2_ref, v2_ref) = refs[:L * N_DEV], refs[L * N_DEV:]
        for l in range(L):
            g = p_refs[l * N_DEV][0].astype(F32)
            for k in range(1, N_DEV):
                g = g + p_refs[l * N_DEV + k][0].astype(F32)
            g_ref[l] = g
            dl_ref[l], m2_ref[l], v2_ref[l] = _adam_math(g, w_ref[l], m_ref[l], v_ref[l])

    slot = lambda k: pl.BlockSpec((1, tr, C), lambda i: (k, i, 0))
    blk = pl.BlockSpec((L, tr, C), lambda i: (0, i, 0))
    return pl.pallas_call(
        body, name=name, grid=(rows // tr,), in_specs=[slot(k) for _ in range(L) for k in range(N_DEV)] + [blk] * 3,
        out_specs=[blk] * 4, out_shape=[jax.ShapeDtypeStruct((L, rows, C), F32)] * 4,
        compiler_params=_cp(("parallel",)),
    )(*[p for p in parts for _ in range(N_DEV)], w, m, v)


def _small_final(parts, w, m, v):
    R = w.shape[0]

    def body(p_ref, w_ref, m_ref, v_ref, g_ref, dl_ref, m2_ref, v2_ref):
        g = p_ref[0]
        for k in range(1, N_DEV):
            g = g + p_ref[k]
        delta, m2, v2 = _adam_math(g, w_ref[...], m_ref[...], v_ref[...])
        g_ref[...], dl_ref[...], m2_ref[...], v2_ref[...] = g, delta, m2, v2

    return pl.pallas_call(body, name="small_final", out_shape=[jax.ShapeDtypeStruct((R, LANE), F32)] * 4)(parts, w, m, v)


def _layer_fwd(x, mod, small, W, more_w):
    shift, scale, gate = mod
    h = _norm_mod(x, small["norm_g"], scale, shift)
    proj = _mm(h, W["w_in_t"], BF16, 512, 1536, name="proj_mm", nt=True)
    W.update(more_w(proj))
    y_pool = _pool_fwd(proj, W["pool_w"], small["pool_scale"])
    o, y_attn = _attn_fwd(proj, small["sink_b"])
    y_conv = _conv_fwd(proj, W["conv_dw"], small["conv_dw_b"], small["conv_ln_g"], small["conv_ln_b"], W["conv_pw"])
    merged, pp, pa, pc = _merge_fwd(y_pool, y_attn, y_conv, W["wbp"], W["wba"], W["wbc"], proj)
    x_new = _out_fwd(x, merged, W["w_out"], gate)
    stash = dict(x=x, h=h, proj=proj, o=o, y_pool=y_pool, y_attn=y_attn, y_conv=y_conv,
                 merged=merged, pp=pp, pa=pa, pc=pc)
    return x_new, stash


def _layer_bwd(dx, st, mod, small, W, put):
    shift, scale, gate = mod
    proj = st["proj"]
    gmat = _mm_tn(st["merged"], dx, 1024, 1024, 1024, name="wout_tn")
    d_w_out, d_gate = _wout_post(gmat, W["w_out"], gate)
    dpp, dpa, dpc, dgp, dga, dgc = _out_bwd(dx, gate, W["w_out"], st["pp"], st["pa"], st["pc"], proj)
    dy_pool = _mm(dpp, W["wbp"], BF16, 512, 1024, name="branch_bwd_mm", nt=True)
    dy_attn = _mm(dpa, W["wba"], BF16, 512, 1024, name="branch_bwd_mm", nt=True)
    dy_conv = _mm(dpc, W["wbc"], BF16, 512, 1024, name="branch_bwd_mm", nt=True)
    d_wbp = _mm_tn(st["y_pool"], dpp, 1024, 1024, 1024, name="branch_tn")
    d_wba = _mm_tn(st["y_attn"], dpa, 1024, 1024, 1024, name="branch_tn")
    d_wbc = _mm_tn(st["y_conv"], dpc, 1024, 1024, 1024, name="branch_tn")

    dproj = lax.empty(proj.shape, BF16)
    dmix, dproj, d_pool_scale, d_pool_w = _pool_bwd_a(dy_pool, proj, W["pool_w"], small["pool_scale"], dproj)
    dproj = _pool_bwd_b(dmix, dproj)
    dproj, dk, dv, d_sink = _attn_bwd(dy_attn, st["o"], proj, small["sink_b"], dproj)
    dcv, dproj, d_pw, d_ln_g, d_ln_b, d_dw_b, d_dw = _conv_bwd_a(
        dy_conv, proj, W["conv_dw"], small["conv_dw_b"], small["conv_ln_g"], small["conv_ln_b"], W["conv_pw"], dproj)
    dproj = _conv_bwd_b(dcv, proj, W["conv_dw"], dproj)
    for piece, off in ((dk[ATTN_BACK:], OFF_K), (dv[ATTN_BACK:], OFF_V), (dgp, OFF_GP), (dga, OFF_GA), (dgc, OFF_GC)):
        dproj = lax.dynamic_update_slice(dproj, piece.astype(BF16), (0, off))

    tok = put(dict(pool_w=d_pool_w, conv_dw=d_dw[:CONV_K], conv_pw=d_pw, wbp=d_wbp, wba=d_wba, wbc=d_wbc, w_out=d_w_out))
    d_w_in_t = _mm_tn(dproj, st["h"], 768, 2048, 1024, name="win_tn", vmem=VMEM_BIG, after=tok, out_dtype=BF16)
    tok = put(dict(w_in=d_w_in_t))
    dh = _mm(dproj, W["w_in_t"], F32, 1024, 2048, 1536, name="dh_mm", vmem=VMEM_BIG, after=tok)
    dx_in, d_shift, d_a = _norm_bwd(dh, st["x"], dx, small["norm_g"], scale)
    sm = dict(d_a=d_a, d_shift=d_shift, d_gate=d_gate, pool_scale=d_pool_scale, attn_sink=d_sink[:, 0],
              conv_dw_b=d_dw_b, conv_ln_g=d_ln_g, conv_ln_b=d_ln_b)
    return dx_in, sm


def _local_step(x, target, mods, smalls, get_w, final_g, put_g, end_layer):
    stashes, Ws = [], []
    for l in range(DEPTH):
        w, more_w = get_w(l, x)
        Ws.append(w)
        x, st = _layer_fwd(x, mods[l], smalls[l], w, more_w)
        stashes.append(st)
    dx, d_final_g, loss_lanes = _final_loss(x, target, final_g)
    sms = [None] * DEPTH
    for l in reversed(range(DEPTH)):
        dx, sms[l] = _layer_bwd(dx, stashes[l], mods[l], smalls[l], Ws[l], functools.partial(put_g, l))
        end_layer(l, dx)
    return loss_lanes, dx, d_final_g, sms


BIG = ("w_in", "pool_w", "conv_pw", "wbp", "wba", "wbc", "w_out")
GRADS = BIG + ("conv_dw",)


def _full_w_in(g):
    return dict(w_in_t=g.reshape(IN_WIDTH, D))


def _full_weights(g, conv_dw):
    cols = lambda a: jnp.transpose(a, (1, 0, 2)).reshape(a.shape[1], -1)
    pool_w = jnp.transpose(g["pool_w"], (1, 0, 2, 3)).reshape(4, 256, 256)
    conv_pw = g["conv_pw"].reshape(1024, 1024)
    wbp, wba, wbc = cols(g["wbp"]), cols(g["wba"]), cols(g["wbc"])
    w_out = g["w_out"].reshape(D, D)
    conv_dw = jnp.pad(cols(conv_dw), ((0, 32 - CONV_K), (0, 0)))
    return dict(pool_w=pool_w, conv_pw=conv_pw, wbp=wbp, wba=wba, wbc=wbc, w_out=w_out, conv_dw=conv_dw)


def _pieces(name, g):
    if name == "w_in":
        return g.reshape(8, IN_WIDTH // 8, D)
    if name == "pool_w":
        return jnp.transpose(g.reshape(4, 8, 32, 256), (1, 0, 2, 3))
    if name == "conv_dw":
        return jnp.transpose(g.reshape(CONV_K, 8, 128), (1, 0, 2))
    if name == "conv_pw":
        return g.reshape(8, 128, 1024)
    if name in ("wbp", "wba", "wbc"):
        return jnp.transpose(g.reshape(1024, 8, 256), (1, 0, 2))
    return g.reshape(8, 256, D)


def _pack_small(items, rows):
    flat = jnp.concatenate([a.reshape(-1).astype(F32) for a in items])
    return jnp.pad(flat, (0, rows * LANE - flat.shape[0])).reshape(rows, LANE)


def _unpack_small(packed, shapes):
    flat, out, off = packed.reshape(-1), [], 0
    for s in shapes:
        n = 1
        for d in s:
            n *= d
        out.append(flat[off:off + n].reshape(s))
        off += n
    return out


def kernel(x, c, norm_g, w_ada, b_ada, w_in, pool_w, pool_scale, attn_sink, conv_dw, conv_dw_b, conv_ln_g, conv_ln_b, conv_pw, w_branch_pool, w_branch_attn, w_branch_conv, w_out, final_g, loss_target, m_norm_g, m_w_ada, m_b_ada, m_w_in, m_pool_w, m_pool_scale, m_attn_sink, m_conv_dw, m_conv_dw_b, m_conv_ln_g, m_conv_ln_b, m_conv_pw, m_w_branch_pool, m_w_branch_attn, m_w_branch_conv, m_w_out, m_final_g, v_norm_g, v_w_ada, v_b_ada, v_w_in, v_pool_w, v_pool_scale, v_attn_sink, v_conv_dw, v_conv_dw_b, v_conv_ln_g, v_conv_ln_b, v_conv_pw, v_w_branch_pool, v_w_branch_attn, v_w_branch_conv, v_w_out, v_final_g):
    L = DEPTH
    me = 4 * lax.axis_index("x") + 2 * lax.axis_index("y") + lax.axis_index("c")
    tr = lambda a: jnp.swapaxes(a, 1, 2)
    shards = dict(w_in=tr(w_in), pool_w=pool_w, conv_dw=conv_dw, conv_pw=conv_pw, wbp=w_branch_pool, wba=w_branch_attn,
                  wbc=w_branch_conv, w_out=w_out)
    moms = dict(w_in=(tr(m_w_in), tr(v_w_in)), pool_w=(m_pool_w, v_pool_w), conv_dw=(m_conv_dw, v_conv_dw),
                conv_pw=(m_conv_pw, v_conv_pw), wbp=(m_w_branch_pool, v_w_branch_pool),
                wba=(m_w_branch_attn, v_w_branch_attn), wbc=(m_w_branch_conv, v_w_branch_conv), w_out=(m_w_out, v_w_out))

    n_cd = L * CONV_K * 128
    first = _all_gather([_pack_small([c, conv_dw], 144)], "gather_c")[0].reshape(N_DEV, -1)
    c_all = first[:, :D]
    conv_dw_all = first[:, D:D + n_cd].reshape(N_DEV, L, CONV_K, 128)

    mod_part = _mod_fwd(c_all, w_ada)
    mod_all = _all_gather([mod_part.reshape(-1, LANE)], "gather_mod")[0].reshape(N_DEV, L, N_DEV, -1)
    mod = jnp.transpose(lax.dynamic_index_in_dim(mod_all, me, axis=2, keepdims=False), (1, 0, 2)).reshape(L, 3 * D)
    mod = mod + b_ada
    mods = [(mod[l:l + 1, :D], mod[l:l + 1, D:2 * D], mod[l:l + 1, 2 * D:]) for l in range(L)]

    w_in_0 = _all_gather([(shards["w_in"][0] + mod[0, 0] * 0.0).astype(BF16)], "gather_w_in_0")[0]
    gathers, tok = [], w_in_0[0, 0, 0].astype(F32) * 0.0
    for l in range(L):
        first = lambda a: (a + tok).astype(BF16)
        mine = [first(shards["w_in"][l])], [first(shards[k][l]) if k == BIG[1] else shards[k][l].astype(BF16) for k in BIG[1:]]
        started = [None if (l, n) == (0, 0) else _spread_start(v, False, f"gather_start_{l}_{n}") for n, v in enumerate(mine)]
        gathers.append((mine, started))
        tok = sum(st[-1][0, 0] for st in started if st is not None)
    mods[0] = (mods[0][0] + tok,) + mods[0][1:]

    sink_b = jnp.broadcast_to(attn_sink[:, :, None], (L, N_HEADS, LANE))
    smalls = [dict(norm_g=norm_g[l:l + 1], pool_scale=pool_scale[l:l + 1], sink_b=sink_b[l], conv_dw_b=conv_dw_b[l:l + 1],
                   conv_ln_g=conv_ln_g[l:l + 1], conv_ln_b=conv_ln_b[l:l + 1]) for l in range(L)]

    def with_mine(landed, mine):
        return lax.dynamic_update_slice(landed, mine, (me,) + (0,) * (landed.ndim - 1))

    def get_w(l, x_in):
        mine, started = gathers[l]
        if l == 0:
            w_in = w_in_0
        else:
            w_in = with_mine(_spread_wait(started[0], x_in, False, f"gather_wait_{l}_0")[0], mine[0][0][None])

        def more_w(proj):
            landed = _spread_wait(started[1], proj, False, f"gather_wait_{l}_1")
            g = {k: with_mine(a, b[None]) for k, a, b in zip(BIG[1:], landed, mine[1])}
            return _full_weights(g, conv_dw_all[:, l])

        return _full_w_in(w_in), more_w

    pending, parts = {l: [] for l in range(L)}, [dict() for _ in range(L)]

    def put_g(l, grads):
        names = tuple(k for k in GRADS if k in grads)
        pieces = [_pieces(k, grads[k]).astype(BF16) for k in names]
        started = _spread_start(pieces, True, f"scatter_start_{l}_{len(pending[l])}")
        pending[l].append((names, pieces, started))
        return started[-1][0:1, 0:1]

    def finish(l, after):
        for n, (names, pieces, started) in enumerate(pending.pop(l)):
            landed = _spread_wait(started, after, True, f"scatter_wait_{l}_{n}")
            for k, a, b in zip(names, landed, pieces):
                parts[l][k] = with_mine(a, lax.dynamic_slice_in_dim(b, me, 1, axis=0))

    def end_layer(l, dx):
        if l + 1 in pending:
            finish(l + 1, dx)

    loss_lanes, grad_x, d_final_g, sms = _local_step(x[0], loss_target[0], mods, smalls, get_w, final_g.reshape(1, D),
                                                     put_g, end_layer)
    finish(0, grad_x)

    stack = lambda k: jnp.concatenate([sms[l][k].reshape(1, -1) for l in range(L)], axis=0)
    scale_all = jnp.concatenate([mods[l][1] for l in range(L)], axis=0)
    d_norm_g, d_scale = _mod_bwd(stack("d_a"), norm_g, scale_all)
    dmod = jnp.concatenate([stack("d_shift"), d_scale, stack("d_gate")], axis=1)
    small_names = ("norm_g", "b_ada", "pool_scale", "attn_sink", "conv_dw_b", "conv_ln_g", "conv_ln_b", "final_g")
    small_g = (d_norm_g, dmod, stack("pool_scale"), stack("attn_sink"), stack("conv_dw_b"), stack("conv_ln_g"),
               stack("conv_ln_b"), d_final_g.reshape(D))
    small_w = (norm_g, b_ada, pool_scale, attn_sink, conv_dw_b, conv_ln_g, conv_ln_b, final_g)
    small_m = (m_norm_g, m_b_ada, m_pool_scale, m_attn_sink, m_conv_dw_b, m_conv_ln_g, m_conv_ln_b, m_final_g)
    small_v = (v_norm_g, v_b_ada, v_pool_scale, v_attn_sink, v_conv_dw_b, v_conv_ln_g, v_conv_ln_b, v_final_g)
    shapes = [a.shape for a in small_w] + [(D,)]
    n_small = sum(a.size for a in small_w) + D
    R = -(-n_small // (8 * LANE)) * 8
    zero = jnp.zeros((D,), F32)
    small_parts = _all_gather([_pack_small(small_g + (loss_lanes,), R)], "gather_small")[0]
    sg, sd, sm2, sv2 = _small_final(small_parts, _pack_small(small_w + (zero,), R), _pack_small(small_m + (zero,), R),
                                    _pack_small(small_v + (zero + 1.0,), R))
    sg, sd, sm2, sv2 = (_unpack_small(a, shapes) for a in (sg, sd, sm2, sv2))
    loss = jnp.sum(sg[-1])
    res = {n: (sg[i], sd[i], sm2[i], sv2[i]) for i, n in enumerate(small_names)}

    off = norm_g.size
    dmod_all = small_parts.reshape(N_DEV, -1)[:, off:off + L * 3 * D].reshape(N_DEV, L, 3 * D)
    dmod_mine = jnp.transpose(lax.dynamic_slice_in_dim(dmod_all, me * (3 * D // N_DEV), 3 * D // N_DEV, axis=2), (1, 0, 2))
    res["w_ada"] = _wada_bwd(c_all.T, dmod_mine, w_ada, m_w_ada, v_w_ada)

    for k in GRADS:
        shp = shards[k].shape
        to3d = lambda a: a.reshape(L, -1, shp[-1])
        out = _reduce_adamw([parts[l][k].reshape(N_DEV, -1, shp[-1]) for l in range(L)], to3d(shards[k]),
                            to3d(moms[k][0]), to3d(moms[k][1]), "adamw_" + k)
        res[k] = tuple(a.reshape(shp) for a in out)
    res["w_in"] = tuple(tr(a) for a in res["w_in"])

    order = ("norm_g", "w_ada", "b_ada", "w_in", "pool_w", "pool_scale", "attn_sink", "conv_dw", "conv_dw_b", "conv_ln_g",
             "conv_ln_b", "conv_pw", "wbp", "wba", "wbc", "w_out", "final_g")
    outs = [loss, grad_x[None]]
    for j in range(4):
        outs += [res[n][j] for n in order]
    return tuple(outs)
```

```python
import functools

import jax
import jax.numpy as jnp
from jax import lax
from jax.experimental import pallas as pl
from jax.experimental.pallas import tpu as pltpu

F32, BF16 = jnp.float32, jnp.bfloat16
MESH = pl.DeviceIdType.MESH
ANY = pl.BlockSpec(memory_space=pl.ANY)

N_DEV = 8
D = 2048
DEPTH = 4
EPS = 1e-6
IN_WIDTH = 13824
POOL_WINDOWS = (2, 4, 8, 16)
POOL_GROUP = 256
POOL_HALO = 16
CONV_K = 31
CONV_HALO = 32
N_HEADS, N_KV, HEAD_DIM = 16, 4, 64
ATTN_TQ = 256
ATTN_BACK = 128
LANE = 128
VMEM_BIG = 56 * 1024 * 1024

OFF_U, OFF_Z, OFF_Q, OFF_K, OFF_V, OFF_AZ, OFF_CA, OFF_CB, OFF_CZ = 0, 1024, 2048, 3072, 3328, 3584, 4608, 5632, 6656
OFF_GP, OFF_GA, OFF_GC = 7680, 9728, 11776


def _seg(rows, width, off, first_row=None):
    start = (lambda i: i * rows) if first_row is None else first_row
    return pl.BlockSpec((pl.Element(rows), pl.Element(width)), lambda i: (pl.multiple_of(start(i), rows), off))


def _seg2(tm, tn, off):
    return pl.BlockSpec((pl.Element(tm), pl.Element(tn)), lambda j, i: (i * tm, pl.multiple_of(off + j * tn, LANE)))

ADAM_LR, ADAM_B1, ADAM_B2, ADAM_EPS, ADAM_WD, ADAM_STEP = 0.001, 0.9, 0.999, 1e-08, 0.01, 10


def _cp(sem=None, vmem=None):
    return pltpu.CompilerParams(dimension_semantics=sem, vmem_limit_bytes=vmem)


def _sig(x):
    return jax.nn.sigmoid(x)


def _silu(x):
    return x * _sig(x)


def _dsilu(x):
    s = _sig(x)
    return s * (1.0 + x * (1.0 - s))


def _dot(a, b):
    return jnp.dot(a, b, preferred_element_type=F32)


def _dot_tn(a, b):
    return lax.dot_general(a, b, (((0,), (0,)), ((), ())), preferred_element_type=F32)


def _dot_nt(a, b):
    return lax.dot_general(a, b, (((1,), (1,)), ((), ())), preferred_element_type=F32)


def _full(shape):
    n = len(shape)
    return pl.BlockSpec(shape, lambda *_: (0,) * n)


def _my_pos():
    return lax.axis_index("x"), lax.axis_index("y"), lax.axis_index("c")


def _all_gather(xs, name):
    n = len(xs)

    def body(*refs):
        x_refs, o_refs = refs[:n], refs[n:2 * n]
        send_sems, recv_sems, local_sems = refs[2 * n:]
        x, y, c = _my_pos()
        sibling = (x, y, 1 - c)
        chips = [(1 - x, y), (x, 1 - y), (1 - x, 1 - y)]
        me = 4 * x + 2 * y + c

        def slot(px, py, pc):
            return 4 * px + 2 * py + pc

        def copy(t, k, block, to, src=None):
            dst = o_refs[t].at[block]
            return pltpu.make_async_remote_copy(
                src_ref=dst if src is None else src, dst_ref=dst,
                send_sem=send_sems.at[t, k], recv_sem=recv_sems.at[t, k],
                device_id=to, device_id_type=MESH)

        mine = [pltpu.make_async_copy(x_refs[t], o_refs[t].at[me], local_sems.at[t]) for t in range(n)]
        for cp in mine:
            cp.start()
        first = []
        for t in range(n):
            first.append(copy(t, 0, me, sibling, src=x_refs[t]))
            for j, chip in enumerate(chips):
                first.append(copy(t, 1 + j, me, (*chip, c), src=x_refs[t]))
        for cp in first:
            cp.start()
        passed = []
        for j, chip in enumerate(chips):
            for t in range(n):
                copy(t, 1 + j, slot(*chip, c), (x, y, c)).wait_recv()
                fwd = copy(t, 4 + j, slot(*chip, c), sibling)
                fwd.start()
                passed.append(fwd)
        for t in range(n):
            copy(t, 0, slot(x, y, 1 - c), (x, y, c)).wait_recv()
            for j, chip in enumerate(chips):
                copy(t, 4 + j, slot(*chip, 1 - c), (x, y, c)).wait_recv()
        for cp in first + passed:
            cp.wait_send()
        for cp in mine:
            cp.wait()

    return pl.pallas_call(
        body, name=name,
        out_shape=[jax.ShapeDtypeStruct((N_DEV,) + a.shape, a.dtype) for a in xs],
        in_specs=[ANY] * n, out_specs=[ANY] * n,
        scratch_shapes=[pltpu.SemaphoreType.DMA((n, 7)), pltpu.SemaphoreType.DMA((n, 7)),
                        pltpu.SemaphoreType.DMA((n,))],
    )(*xs)


N_PEER = N_DEV - 1
HBM = pl.BlockSpec(memory_space=pltpu.HBM)
SEM = pl.BlockSpec(memory_space=pltpu.SEMAPHORE)
EFFECT = pltpu.SideEffectType.DATAFLOW_SIDE_EFFECTING


def _peer(k):
    x, y, c = _my_pos()
    flip = lambda v, bit: 1 - v if bit else v
    return flip(x, (k >> 2) & 1), flip(y, (k >> 1) & 1), flip(c, k & 1)


def _spread_copies(v_ref, land_ref, send_sems, recv_sems, per_peer):
    x, y, c = _my_pos()
    me = 4 * x + 2 * y + c
    copies = []
    for k in range(1, N_DEV):
        px, py, pc = _peer(k)
        src = v_ref.at[4 * px + 2 * py + pc] if per_peer else v_ref
        copies.append(pltpu.make_async_remote_copy(
            src_ref=src, dst_ref=land_ref.at[me], send_sem=send_sems[k - 1], recv_sem=recv_sems[k - 1],
            device_id=(px, py, pc), device_id_type=MESH))
    return copies


def _spread_start(vs, per_peer, name):
    n = len(vs)
    lands = [(N_DEV,) + (v.shape[1:] if per_peer else v.shape) for v in vs]
    n_sem = 2 * N_PEER * n

    def body(*refs):
        v_refs, land_refs, outs = refs[:n], refs[n:2 * n], refs[2 * n:]
        for t in range(n):
            sems = outs[2 * N_PEER * t:2 * N_PEER * (t + 1)]
            for cp in _spread_copies(v_refs[t], land_refs[t], sems[:N_PEER], sems[N_PEER:], per_peer):
                cp.start()
        token = outs[n_sem + 2 * n]
        token[...] = jnp.zeros_like(token)

    hbm = lambda a: pltpu.with_memory_space_constraint(a, pltpu.HBM)
    return pl.pallas_call(
        body, name=name,
        out_shape=((pltpu.SemaphoreType.DMA(()),) * n_sem + tuple(pltpu.HBM(v.shape, v.dtype) for v in vs)
                   + tuple(pltpu.HBM(s, v.dtype) for s, v in zip(lands, vs)) + (jax.ShapeDtypeStruct((8, LANE), F32),)),
        in_specs=(HBM,) * (2 * n), out_specs=(SEM,) * n_sem + (HBM,) * (2 * n) + (pl.BlockSpec(memory_space=pltpu.VMEM),),
        input_output_aliases={t: n_sem + t for t in range(2 * n)},
        compiler_params=pltpu.CompilerParams(has_side_effects=EFFECT),
    )(*[hbm(v) for v in vs], *[hbm(lax.empty(s, v.dtype)) for s, v in zip(lands, vs)])


def _spread_wait(started, after, per_peer, name):
    n = (len(started) - 1) // (2 * N_PEER + 2)
    n_sem = 2 * N_PEER * n
    sems, thru = started[:n_sem], started[n_sem:n_sem + 2 * n]

    def body(*refs):
        v_refs, land_refs, rest = refs[:n], refs[n:2 * n], refs[2 * n:]
        for t in range(n):
            s = rest[2 * N_PEER * t:2 * N_PEER * (t + 1)]
            for cp in _spread_copies(v_refs[t], land_refs[t], s[:N_PEER], s[N_PEER:], per_peer):
                cp.wait_send()
                cp.wait_recv()

    out = pl.pallas_call(
        body, name=name,
        out_shape=tuple(pltpu.HBM(a.shape, a.dtype) for a in thru),
        in_specs=(HBM,) * (2 * n) + (SEM,) * n_sem + (ANY,), out_specs=(HBM,) * (2 * n),
        input_output_aliases={t: t for t in range(2 * n)},
        compiler_params=pltpu.CompilerParams(has_side_effects=EFFECT),
    )(*thru, *sems, after)
    return out[:n], out[n:]


def _mm(a, b, out_dtype, tm, tn, tk=None, name="mm", vmem=None, after=None, nt=False):
    M, K = a.shape
    N = b.shape[0] if nt else b.shape[1]
    tk = K if tk is None else tk
    nk = K // tk
    assert M % tm == 0 and N % tn == 0 and K % tk == 0
    dep = () if after is None else (after,)

    def body(*refs):
        a_ref, b_ref, o_ref, *acc = refs[len(dep):]
        prod = (_dot_nt if nt else _dot)(a_ref[...].astype(BF16), b_ref[...])
        if nk == 1:
            o_ref[...] = prod.astype(out_dtype)
            return
        acc_ref = acc[0] if acc else o_ref
        k = pl.program_id(2)

        @pl.when(k == 0)
        def _():
            acc_ref[...] = prod

        @pl.when(k > 0)
        def _():
            acc_ref[...] += prod

        if acc:
            @pl.when(k == nk - 1)
            def _():
                o_ref[...] = acc_ref[...].astype(out_dtype)

    scratch = [pltpu.VMEM((tm, tn), F32)] if (nk > 1 and out_dtype != F32) else []
    b_spec = pl.BlockSpec((tn, tk), lambda j, i, k: (j, k)) if nt else pl.BlockSpec((tk, tn), lambda j, i, k: (k, j))
    return pl.pallas_call(
        body, name=name, grid=(N // tn, M // tm, nk),
        in_specs=[_full((1, 1))] * len(dep) + [pl.BlockSpec((tm, tk), lambda j, i, k: (i, k)), b_spec],
        out_specs=pl.BlockSpec((tm, tn), lambda j, i, k: (i, j)),
        out_shape=jax.ShapeDtypeStruct((M, N), out_dtype), scratch_shapes=scratch,
        compiler_params=_cp(("parallel", "parallel", "arbitrary"), vmem),
    )(*dep, a, b)


def _mm_tn(a, b, tm, tn, ts, name="mm_tn", vmem=None, after=None, out_dtype=F32):
    S, Ka = a.shape
    _, N = b.shape
    assert Ka % tm == 0 and N % tn == 0 and S % ts == 0
    dep = () if after is None else (after,)

    nk = S // ts

    def body(*refs):
        a_ref, b_ref, o_ref, *acc = refs[len(dep):]
        acc_ref = acc[0] if acc else o_ref
        prod = _dot_tn(a_ref[...].astype(BF16), b_ref[...].astype(BF16))
        k = pl.program_id(2)

        @pl.when(k == 0)
        def _():
            acc_ref[...] = prod

        @pl.when(k > 0)
        def _():
            acc_ref[...] += prod

        if acc:
            @pl.when(k == nk - 1)
            def _():
                o_ref[...] = acc_ref[...].astype(out_dtype)

    return pl.pallas_call(
        body, name=name, grid=(Ka // tm, N // tn, S // ts),
        in_specs=[_full((1, 1))] * len(dep) + [pl.BlockSpec((ts, tm), lambda i, j, k: (k, i)),
                                               pl.BlockSpec((ts, tn), lambda i, j, k: (k, j))],
        out_specs=pl.BlockSpec((tm, tn), lambda i, j, k: (i, j)),
        out_shape=jax.ShapeDtypeStruct((Ka, N), out_dtype),
        scratch_shapes=[] if out_dtype == F32 else [pltpu.VMEM((tm, tn), F32)],
        compiler_params=_cp(("parallel", "parallel", "arbitrary"), vmem),
    )(*dep, a, b)


def _mod_fwd(c_all, w_ada):
    L, _, n = w_ada.shape

    def body(c_ref, w_ref, o_ref):
        ca = _silu(c_ref[...])
        o_ref[0] = jnp.dot(ca, w_ref[0], preferred_element_type=F32, precision=lax.Precision.HIGHEST)

    return pl.pallas_call(
        body, name="mod_fwd", grid=(L,),
        in_specs=[_full((N_DEV, D)), pl.BlockSpec((1, D, n), lambda l: (l, 0, 0))],
        out_specs=pl.BlockSpec((1, N_DEV, n), lambda l: (l, 0, 0)),
        out_shape=jax.ShapeDtypeStruct((L, N_DEV, n), F32),
        compiler_params=_cp(("parallel",)),
    )(c_all, w_ada)


def _adam_math(g, w, m, v):
    m2 = ADAM_B1 * m + (1.0 - ADAM_B1) * g
    v2 = ADAM_B2 * v + (1.0 - ADAM_B2) * (g * g)
    m_hat = m2 / (1.0 - ADAM_B1 ** ADAM_STEP)
    v_hat = v2 / (1.0 - ADAM_B2 ** ADAM_STEP)
    delta = -ADAM_LR * (m_hat / (jnp.sqrt(v_hat) + ADAM_EPS) + ADAM_WD * w)
    return delta, m2, v2


def _wada_bwd(c_all_t, dmod, w, m, v, tr=256):
    L, _, n = w.shape

    def body(c_ref, d_ref, w_ref, m_ref, v_ref, g_ref, dl_ref, m2_ref, v2_ref):
        ca = _silu(c_ref[...])
        dm = d_ref[0]
        g = ca[:, 0:1] * dm[0:1, :]
        for b in range(1, N_DEV):
            g = g + ca[:, b:b + 1] * dm[b:b + 1, :]
        delta, m2, v2 = _adam_math(g, w_ref[0], m_ref[0], v_ref[0])
        g_ref[0], dl_ref[0], m2_ref[0], v2_ref[0] = g, delta, m2, v2

    blk = pl.BlockSpec((1, tr, n), lambda l, i: (l, i, 0))
    return pl.pallas_call(
        body, name="wada_bwd", grid=(L, D // tr),
        in_specs=[pl.BlockSpec((tr, N_DEV), lambda l, i: (i, 0)), pl.BlockSpec((1, N_DEV, n), lambda l, i: (l, 0, 0)),
                  blk, blk, blk],
        out_specs=[blk] * 4, out_shape=[jax.ShapeDtypeStruct(w.shape, F32)] * 4,
        compiler_params=_cp(("parallel", "parallel")),
    )(c_all_t, dmod, w, m, v)


def _norm_mod(x, g, scale, shift, tb=256):
    S = x.shape[0]

    def body(x_ref, g_ref, sc_ref, sh_ref, h_ref):
        xv = x_ref[...]
        r = lax.rsqrt(jnp.mean(xv * xv, axis=-1, keepdims=True) + EPS)
        h_ref[...] = (xv * r * (g_ref[...] * (1.0 + sc_ref[...])) + sh_ref[...]).astype(BF16)

    row = pl.BlockSpec((tb, D), lambda i: (i, 0))
    vec = _full((1, D))
    return pl.pallas_call(
        body, name="norm_mod", grid=(S // tb,), in_specs=[row, vec, vec, vec], out_specs=row,
        out_shape=jax.ShapeDtypeStruct((S, D), BF16), compiler_params=_cp(("parallel",)),
    )(x, g, scale, shift)


def _pool_mix(u_ref, uh_ref, ubuf, i, tb):
    H = POOL_HALO
    ubuf[H:, :] = u_ref[...].astype(F32)
    ubuf[:H, :] = jnp.where(i > 0, uh_ref[...].astype(F32), 0.0)
    t = i * tb + lax.broadcasted_iota(jnp.int32, (tb, 1), 0)
    mixed = []
    for g, w in enumerate(POOL_WINDOWS):
        cs = slice(g * POOL_GROUP, (g + 1) * POOL_GROUP)
        cur = ubuf[H:, cs]
        acc = cur
        for j in range(1, w):
            acc = acc + ubuf[pl.ds(H - j, tb), cs]
        cnt = jnp.minimum(t + 1, w).astype(F32)
        mixed.append(acc / cnt - cur)
    return mixed


def _pool_specs(tb):
    H = POOL_HALO
    u = _seg(tb, 1024, OFF_U)
    uh = _seg(H, 1024, OFF_U, lambda i: jnp.maximum(i * tb - H, 0))
    z = _seg(tb, 1024, OFF_Z)
    return u, uh, z


def _pool_fwd(proj, pool_w, pool_scale, tb=256):
    S = proj.shape[0]

    def body(u_ref, uh_ref, z_ref, w_ref, sc_ref, y_ref, ubuf):
        i = pl.program_id(0)
        mixed = _pool_mix(u_ref, uh_ref, ubuf, i, tb)
        m = jnp.concatenate([_dot(mixed[g].astype(BF16), w_ref[g]) for g in range(4)], axis=1)
        y_ref[...] = (m * sc_ref[...] * _silu(z_ref[...].astype(F32))).astype(BF16)

    u, uh, z = _pool_specs(tb)
    return pl.pallas_call(
        body, name="pool_fwd", grid=(S // tb,),
        in_specs=[u, uh, z, _full((4, 256, 256)), _full((1, 1024))],
        out_specs=pl.BlockSpec((tb, 1024), lambda i: (i, 0)),
        out_shape=jax.ShapeDtypeStruct((S, 1024), BF16),
        scratch_shapes=[pltpu.VMEM((tb + POOL_HALO, 1024), F32)],
        compiler_params=_cp(("parallel",)),
    )(proj, proj, proj, pool_w, pool_scale)


def _attn_mask(i):
    TQ, NK = ATTN_TQ, ATTN_TQ + ATTN_BACK
    qc = lax.broadcasted_iota(jnp.int32, (TQ, NK), 0) // 64
    col = lax.broadcasted_iota(jnp.int32, (TQ, NK), 1)
    kc = col // 64
    return (kc >= qc) & (kc <= qc + 2) & ((col >= ATTN_BACK) | (i > 0))


def _block_diag(kbuf, vbuf, kbd, vbd, kh):
    NK = ATTN_TQ + ATTN_BACK
    ks = slice(kh * HEAD_DIM, (kh + 1) * HEAD_DIM)
    for g in range(N_HEADS // N_KV):
        kbd[g * NK:(g + 1) * NK, g * HEAD_DIM:(g + 1) * HEAD_DIM] = kbuf[:, ks]
        vbd[g * NK:(g + 1) * NK, g * HEAD_DIM:(g + 1) * HEAD_DIM] = vbuf[:, ks]


def _attn_probs(q_ref, kw, sink_ref, valid, h):
    qh = q_ref[:, h * HEAD_DIM:(h + 1) * HEAD_DIM]
    return _softmax_sink(_dot_nt(qh, kw) * (HEAD_DIM ** -0.5), sink_ref, valid, h)


def _softmax_sink(s, sink_ref, valid, h):
    s = jnp.where(valid, s, -jnp.inf)
    sk = sink_ref[h:h + 1, 0:1]
    mx = jnp.maximum(jnp.max(s, axis=-1, keepdims=True), sk)
    p = jnp.exp(s - mx)
    es = jnp.exp(sk - mx)
    den = jnp.sum(p, axis=-1, keepdims=True) + es
    return p / den, es / den


def _kv_specs():
    TQ, B = ATTN_TQ, ATTN_BACK
    blk = lambda off: _seg(TQ, 256, off)
    halo = lambda off: _seg(B, 256, off, lambda i: jnp.maximum(i * TQ - B, 0))
    return [blk(OFF_K), halo(OFF_K), blk(OFF_V), halo(OFF_V)]


def _kv_window(k_ref, kh_ref, v_ref, vh_ref, kbuf, vbuf, i):
    B = ATTN_BACK
    for buf, ref, href in ((kbuf, k_ref, kh_ref), (vbuf, v_ref, vh_ref)):
        buf[:B, :] = jnp.where(i > 0, href[...], jnp.zeros_like(href))
        buf[B:, :] = ref[...]


def _attn_fwd(proj, sink_b):
    S = proj.shape[0]
    TQ, NK = ATTN_TQ, ATTN_TQ + ATTN_BACK

    G = N_HEADS // N_KV

    def body(q_ref, z_ref, k_ref, kh_ref, v_ref, vh_ref, sink_ref, o_ref, y_ref, kbuf, vbuf, kbd, vbd):
        i = pl.program_id(0)
        _kv_window(k_ref, kh_ref, v_ref, vh_ref, kbuf, vbuf, i)
        valid = _attn_mask(i)
        kbd[...] = jnp.zeros_like(kbd)
        vbd[...] = jnp.zeros_like(vbd)
        for kh in range(N_KV):
            _block_diag(kbuf, vbuf, kbd, vbd, kh)
            gs = slice(kh * G * HEAD_DIM, (kh + 1) * G * HEAD_DIM)
            s_all = _dot_nt(q_ref[:, gs], kbd[...]) * (HEAD_DIM ** -0.5)
            pn = [_softmax_sink(s_all[:, g * NK:(g + 1) * NK], sink_ref, valid, kh * G + g)[0].astype(BF16) for g in range(G)]
            o = _dot(jnp.concatenate(pn, axis=1), vbd[...])
            o_ref[:, gs] = o.astype(BF16)
            y_ref[:, gs] = (o * _silu(z_ref[:, gs].astype(F32))).astype(BF16)

    out = pl.BlockSpec((TQ, 1024), lambda i: (i, 0))
    return pl.pallas_call(
        body, name="attn_fwd", grid=(S // TQ,),
        in_specs=[_seg(TQ, 1024, OFF_Q), _seg(TQ, 1024, OFF_AZ), *_kv_specs(), _full((N_HEADS, LANE))],
        out_specs=[out, out], out_shape=[jax.ShapeDtypeStruct((S, 1024), BF16)] * 2,
        scratch_shapes=[pltpu.VMEM((NK, 256), BF16)] * 2 + [pltpu.VMEM((G * NK, G * HEAD_DIM), BF16)] * 2,
        compiler_params=_cp(("parallel",)),
    )(proj, proj, proj, proj, proj, proj, sink_b)


def _conv_specs(tb):
    H = CONV_HALO
    prev = lambda i: jnp.maximum(i * tb - H, 0)
    a = _seg(tb, 1024, OFF_CA)
    ah = _seg(H, 1024, OFF_CA, prev)
    b = _seg(tb, 1024, OFF_CB)
    bh = _seg(H, 1024, OFF_CB, prev)
    z = _seg(tb, 1024, OFF_CZ)
    return a, ah, b, bh, z


SUBLANES = 8
CONV_TAPS = tuple((j, CONV_HALO - (CONV_K - 1) + j) for j in range(CONV_K))
CONV_TAPS_T = tuple((j, (CONV_K - 1) - j) for j in range(CONV_K))


def _lane_chunks(fn):
    def body(c, carry):
        fn(pl.ds(pl.multiple_of(c * LANE, LANE), LANE))
        return carry

    lax.fori_loop(0, 1024 // LANE, body, 0)


def _shifted_tiles(src_ref, cs, s, n):
    row = lax.broadcasted_iota(jnp.int32, (SUBLANES, LANE), 0)
    prev = None
    for t in range(n + (1 if s else 0)):
        v = src_ref[pl.ds(SUBLANES * t, SUBLANES), cs]
        if s == 0:
            yield t, v
            continue
        x = pltpu.roll(v, SUBLANES - s, 0)
        if prev is not None:
            yield t - 1, jnp.where(row < SUBLANES - s, prev, x)
        prev = x


def _by_shift(taps):
    groups = {}
    for j, o in taps:
        groups.setdefault(o % SUBLANES, []).append((j, o // SUBLANES))
    return sorted(groups.items())


def _taps_apply(src_ref, w_ref, dst_ref, tb, taps, cs):
    nu = tb // SUBLANES
    acc = [None] * nu
    for s, group in _by_shift(taps):
        w = {j: w_ref[j:j + 1, cs] for j, _ in group}
        for t, g in _shifted_tiles(src_ref, cs, s, nu + max(a for _, a in group)):
            for j, a in group:
                if 0 <= t - a < nu:
                    term = w[j] * g
                    acc[t - a] = term if acc[t - a] is None else acc[t - a] + term
    dst_ref[:, cs] = jnp.concatenate(acc, axis=0)


def _taps_reduce(src_ref, d_ref, out_ref, tb, taps, cs):
    nu = tb // SUBLANES
    d = [d_ref[pl.ds(SUBLANES * u, SUBLANES), cs] for u in range(nu)]
    for s, group in _by_shift(taps):
        part = {j: None for j, _ in group}
        for t, g in _shifted_tiles(src_ref, cs, s, nu + max(a for _, a in group)):
            for j, a in group:
                if 0 <= t - a < nu:
                    term = d[t - a] * g
                    part[j] = term if part[j] is None else part[j] + term
        for j, _ in group:
            out_ref[j:j + 1, cs] += jnp.sum(part[j], axis=0, keepdims=True)


def _conv_glu_dw(a_ref, ah_ref, b_ref, bh_ref, dw_ref, gbuf, ybuf, i, tb):
    H = CONV_HALO
    gbuf[H:, :] = a_ref[...].astype(F32) * _sig(b_ref[...].astype(F32))
    gh = ah_ref[...].astype(F32) * _sig(bh_ref[...].astype(F32))
    gbuf[:H, :] = jnp.where(i > 0, gh, 0.0)
    _lane_chunks(lambda cs: _taps_apply(gbuf, dw_ref, ybuf, tb, CONV_TAPS, cs))


def _layer_norm_fwd(y, g, b):
    mu = jnp.mean(y, axis=-1, keepdims=True)
    yc = y - mu
    rstd = lax.rsqrt(jnp.mean(yc * yc, axis=-1, keepdims=True) + EPS)
    xh = yc * rstd
    return xh, rstd, xh * g + b


def _conv_fwd(proj, dw, dw_b, ln_g, ln_b, pw, tb=256):
    S = proj.shape[0]

    def body(a_ref, ah_ref, b_ref, bh_ref, z_ref, dw_ref, dwb_ref, lg_ref, lb_ref, pw_ref, y_ref, gbuf, ybuf):
        i = pl.program_id(0)
        _conv_glu_dw(a_ref, ah_ref, b_ref, bh_ref, dw_ref, gbuf, ybuf, i, tb)
        _, _, yn = _layer_norm_fwd(ybuf[...] + dwb_ref[...], lg_ref[...], lb_ref[...])
        out = _dot(_silu(yn).astype(BF16), pw_ref[...])
        y_ref[...] = (out * _silu(z_ref[...].astype(F32))).astype(BF16)

    vec = _full((1, 1024))
    return pl.pallas_call(
        body, name="conv_fwd", grid=(S // tb,),
        in_specs=[*_conv_specs(tb), _full((32, 1024)), vec, vec, vec, _full((1024, 1024))],
        out_specs=pl.BlockSpec((tb, 1024), lambda i: (i, 0)),
        out_shape=jax.ShapeDtypeStruct((S, 1024), BF16),
        scratch_shapes=[pltpu.VMEM((tb + CONV_HALO, 1024), F32), pltpu.VMEM((tb, 1024), F32)],
        compiler_params=_cp(("parallel",)),
    )(proj, proj, proj, proj, proj, dw, dw_b, ln_g, ln_b, pw)


def _merge_fwd(yp, ya, yc, wbp, wba, wbc, proj, tm=512, tn=1024):
    S = yp.shape[0]

    def body(yp_ref, ya_ref, yc_ref, wp_ref, wa_ref, wc_ref, gp_ref, ga_ref, gc_ref, m_ref, pp_ref, pa_ref, pc_ref):
        pp = _dot(yp_ref[...], wp_ref[...])
        pa = _dot(ya_ref[...], wa_ref[...])
        pc = _dot(yc_ref[...], wc_ref[...])
        m = (_sig(gp_ref[...].astype(F32)) * pp + _sig(ga_ref[...].astype(F32)) * pa
             + _sig(gc_ref[...].astype(F32)) * pc)
        m_ref[...] = m.astype(BF16)
        pp_ref[...], pa_ref[...], pc_ref[...] = pp.astype(BF16), pa.astype(BF16), pc.astype(BF16)

    yb = pl.BlockSpec((tm, 1024), lambda j, i: (i, 0))
    wb = pl.BlockSpec((1024, tn), lambda j, i: (0, j))
    out = pl.BlockSpec((tm, tn), lambda j, i: (i, j))
    return pl.pallas_call(
        body, name="merge_fwd", grid=(D // tn, S // tm),
        in_specs=[yb, yb, yb, wb, wb, wb, _seg2(tm, tn, OFF_GP), _seg2(tm, tn, OFF_GA), _seg2(tm, tn, OFF_GC)],
        out_specs=[out] * 4, out_shape=[jax.ShapeDtypeStruct((S, D), BF16)] * 4,
        compiler_params=_cp(("parallel", "parallel")),
    )(yp, ya, yc, wbp, wba, wbc, proj, proj, proj)


def _out_fwd(x, merged, w_out, gate, tm=512, tn=1024):
    S = x.shape[0]

    def body(x_ref, m_ref, w_ref, g_ref, o_ref):
        o_ref[...] = x_ref[...] + g_ref[...] * _dot(m_ref[...], w_ref[...])

    xb = pl.BlockSpec((tm, tn), lambda j, i: (i, j))
    return pl.pallas_call(
        body, name="out_fwd", grid=(D // tn, S // tm),
        in_specs=[xb, pl.BlockSpec((tm, D), lambda j, i: (i, 0)), pl.BlockSpec((D, tn), lambda j, i: (0, j)),
                  pl.BlockSpec((1, tn), lambda j, i: (0, j))],
        out_specs=xb, out_shape=jax.ShapeDtypeStruct((S, D), F32),
        compiler_params=_cp(("parallel", "parallel")),
    )(x, merged, w_out, gate)


def _final_loss(x, target, final_g, tb=256):
    S = x.shape[0]

    def body(x_ref, t_ref, g_ref, dx_ref, gg_ref, ls_ref):
        i = pl.program_id(0)
        xv, g = x_ref[...], g_ref[...]
        r = lax.rsqrt(jnp.mean(xv * xv, axis=-1, keepdims=True) + EPS)
        xh = xv * r
        e = xh * g - t_ref[...]
        dy = e * (1.0 / D)
        gy = dy * g
        dx_ref[...] = r * (gy - xh * jnp.mean(gy * xh, axis=-1, keepdims=True))
        gg = jnp.sum(dy * xh, axis=0, keepdims=True)
        ls = jnp.sum(e * e, axis=0, keepdims=True) * (0.5 / D)

        @pl.when(i == 0)
        def _():
            gg_ref[...], ls_ref[...] = gg, ls

        @pl.when(i > 0)
        def _():
            gg_ref[...] += gg
            ls_ref[...] += ls

    row = pl.BlockSpec((tb, D), lambda i: (i, 0))
    vec = _full((1, D))
    return pl.pallas_call(
        body, name="final_loss", grid=(S // tb,), in_specs=[row, row, vec], out_specs=[row, vec, vec],
        out_shape=[jax.ShapeDtypeStruct((S, D), F32), jax.ShapeDtypeStruct((1, D), F32),
                   jax.ShapeDtypeStruct((1, D), F32)],
        compiler_params=_cp(("arbitrary",)),
    )(x, target, final_g)


def _out_bwd(dx, gate, w_out, pp, pa, pc, proj, tm=512, tn=1024):
    S = dx.shape[0]

    def body(dx_ref, g_ref, w_ref, pp_ref, pa_ref, pc_ref, gp_ref, ga_ref, gc_ref,
             dpp_ref, dpa_ref, dpc_ref, dgp_ref, dga_ref, dgc_ref):
        dm = _dot_nt((dx_ref[...] * g_ref[...]).astype(BF16), w_ref[...])
        for p_ref, gl_ref, dp_ref, dg_ref in ((pp_ref, gp_ref, dpp_ref, dgp_ref), (pa_ref, ga_ref, dpa_ref, dga_ref),
                                              (pc_ref, gc_ref, dpc_ref, dgc_ref)):
            s = _sig(gl_ref[...].astype(F32))
            dp_ref[...] = (dm * s).astype(BF16)
            dg_ref[...] = (dm * p_ref[...].astype(F32) * s * (1.0 - s)).astype(BF16)

    out = pl.BlockSpec((tm, tn), lambda j, i: (i, j))
    return pl.pallas_call(
        body, name="out_bwd", grid=(D // tn, S // tm),
        in_specs=[pl.BlockSpec((tm, D), lambda j, i: (i, 0)), _full((1, D)), pl.BlockSpec((tn, D), lambda j, i: (j, 0)),
                  out, out, out, _seg2(tm, tn, OFF_GP), _seg2(tm, tn, OFF_GA), _seg2(tm, tn, OFF_GC)],
        out_specs=[out] * 6, out_shape=[jax.ShapeDtypeStruct((S, D), BF16)] * 6,
        compiler_params=_cp(("parallel", "parallel")),
    )(dx, gate, w_out, pp, pa, pc, proj, proj, proj)


def _wout_post(gmat, w_out, gate, tr=256):
    def body(g_ref, w_ref, gate_ref, dw_ref, dg_ref):
        i = pl.program_id(0)
        gm = g_ref[...]
        dw_ref[...] = gm * gate_ref[...]
        part = jnp.sum(gm * w_ref[...].astype(F32), axis=0, keepdims=True)

        @pl.when(i == 0)
        def _():
            dg_ref[...] = part

        @pl.when(i > 0)
        def _():
            dg_ref[...] += part

    row = pl.BlockSpec((tr, D), lambda i: (i, 0))
    return pl.pallas_call(
        body, name="wout_post", grid=(D // tr,), in_specs=[row, row, _full((1, D))], out_specs=[row, _full((1, D))],
        out_shape=[jax.ShapeDtypeStruct((D, D), F32), jax.ShapeDtypeStruct((1, D), F32)],
        compiler_params=_cp(("arbitrary",)),
    )(gmat, w_out, gate)


def _pool_bwd_a(dy, proj, pool_w, pool_scale, dproj, tb=256):
    S = proj.shape[0]

    def body(dy_ref, u_ref, uh_ref, z_ref, w_ref, sc_ref, _, dmix_ref, dz_ref, dsc_ref, dw_ref, ubuf):
        i = pl.program_id(0)
        mixed = [m.astype(BF16) for m in _pool_mix(u_ref, uh_ref, ubuf, i, tb)]
        m = jnp.concatenate([_dot(mixed[g], w_ref[g]) for g in range(4)], axis=1)
        dyv, z, sc = dy_ref[...].astype(F32), z_ref[...].astype(F32), sc_ref[...]
        dyp = dyv * _silu(z)
        dz_ref[...] = (dyv * (m * sc) * _dsilu(z)).astype(BF16)
        dsc = jnp.sum(dyp * m, axis=0, keepdims=True)
        dmm = (dyp * sc).astype(BF16)
        dws = []
        for g in range(4):
            cs = slice(g * POOL_GROUP, (g + 1) * POOL_GROUP)
            dmix_ref[:, cs] = _dot_nt(dmm[:, cs], w_ref[g])
            dws.append(_dot_tn(mixed[g], dmm[:, cs]))

        @pl.when(i == 0)
        def _():
            dsc_ref[...] = dsc
            for g in range(4):
                dw_ref[g] = dws[g]

        @pl.when(i > 0)
        def _():
            dsc_ref[...] += dsc
            for g in range(4):
                dw_ref[g] += dws[g]

    u, uh, z = _pool_specs(tb)
    row = pl.BlockSpec((tb, 1024), lambda i: (i, 0))
    wfull = _full((4, 256, 256))
    return pl.pallas_call(
        body, name="pool_bwd_a", grid=(S // tb,),
        in_specs=[row, u, uh, z, wfull, _full((1, 1024)), ANY],
        out_specs=[row, _seg(tb, 1024, OFF_Z), _full((1, 1024)), wfull],
        out_shape=[jax.ShapeDtypeStruct((S, 1024), F32), jax.ShapeDtypeStruct(dproj.shape, BF16),
                   jax.ShapeDtypeStruct((1, 1024), F32), jax.ShapeDtypeStruct((4, 256, 256), F32)],
        scratch_shapes=[pltpu.VMEM((tb + POOL_HALO, 1024), F32)], input_output_aliases={6: 1},
        compiler_params=_cp(("arbitrary",)),
    )(dy, proj, proj, proj, pool_w, pool_scale, dproj)


def _pool_bwd_b(dmix, dproj, tb=256):
    S = dmix.shape[0]
    H = POOL_HALO
    nb = S // tb

    def body(dm_ref, dh_ref, _, du_ref, ebuf):
        i = pl.program_id(0)
        t = i * tb + lax.broadcasted_iota(jnp.int32, (tb, 1), 0)
        th = (i + 1) * tb + lax.broadcasted_iota(jnp.int32, (H, 1), 0)
        for g, w in enumerate(POOL_WINDOWS):
            cs = slice(g * POOL_GROUP, (g + 1) * POOL_GROUP)
            ebuf[:tb, cs] = dm_ref[:, cs] / jnp.minimum(t + 1, w).astype(F32)
            eh = dh_ref[:, cs] / jnp.minimum(th + 1, w).astype(F32)
            ebuf[tb:, cs] = jnp.where(i < nb - 1, eh, 0.0)
        for g, w in enumerate(POOL_WINDOWS):
            cs = slice(g * POOL_GROUP, (g + 1) * POOL_GROUP)
            acc = ebuf[:tb, cs]
            for j in range(1, w):
                acc = acc + ebuf[pl.ds(j, tb), cs]
            du_ref[:, cs] = (acc - dm_ref[:, cs]).astype(BF16)

    row = pl.BlockSpec((tb, 1024), lambda i: (i, 0))
    nxt = pl.BlockSpec((H, 1024), lambda i: (jnp.minimum((i + 1) * (tb // H), S // H - 1), 0))
    return pl.pallas_call(
        body, name="pool_bwd_b", grid=(nb,), in_specs=[row, nxt, ANY], out_specs=_seg(tb, 1024, OFF_U),
        out_shape=jax.ShapeDtypeStruct(dproj.shape, BF16),
        scratch_shapes=[pltpu.VMEM((tb + H, 1024), F32)], input_output_aliases={2: 0},
        compiler_params=_cp(("parallel",)),
    )(dmix, dmix, dproj)


def _attn_bwd(dy, o, proj, sink_b, dproj):
    S = proj.shape[0]
    TQ, NK = ATTN_TQ, ATTN_TQ + ATTN_BACK
    nb = S // TQ
    G = N_HEADS // N_KV

    QZ = OFF_AZ + 1024 - OFF_Q

    def body(dy_ref, o_ref, q_ref, z_ref, k_ref, kh_ref, v_ref, vh_ref, sink_ref, _, dqz_ref, dk_hbm, dv_hbm, ds_ref,
             dk_acc, dv_acc, kbuf, vbuf):
        i = pl.program_id(0)
        dq_ref, dz_ref = dqz_ref.at[:, :1024], dqz_ref.at[:, OFF_AZ - OFF_Q:]
        dqz_ref[:, 1024:OFF_AZ - OFF_Q] = jnp.zeros((TQ, OFF_AZ - OFF_Q - 1024), BF16)
        _kv_window(k_ref, kh_ref, v_ref, vh_ref, kbuf, vbuf, i)

        @pl.when(i == 0)
        def _():
            dk_acc[...] = jnp.zeros_like(dk_acc)
            dv_acc[...] = jnp.zeros_like(dv_acc)
            ds_ref[...] = jnp.zeros_like(ds_ref)

        start = pl.multiple_of(i * TQ, TQ)
        valid = _attn_mask(i)
        dks, dvs = [], []
        for kh in range(N_KV):
            ks = slice(kh * HEAD_DIM, (kh + 1) * HEAD_DIM)
            kw = kbuf[:, ks]
            vw = vbuf[:, ks]
            dk_sum = jnp.zeros((NK, HEAD_DIM), F32)
            dv_sum = jnp.zeros((NK, HEAD_DIM), F32)
            dqs = []
            for gi in range(G):
                h = kh * G + gi
                hs = slice(h * HEAD_DIM, (h + 1) * HEAD_DIM)
                pn, psink = _attn_probs(q_ref, kw, sink_ref, valid, h)
                ov = o_ref[:, hs].astype(F32)
                do = dy_ref[:, hs].astype(F32) * _silu(z_ref[:, hs].astype(F32))
                delta = jnp.sum(do * ov, axis=-1, keepdims=True)
                dob = do.astype(BF16)
                dp = _dot_nt(dob, vw)
                ds = (pn * (dp - delta)).astype(BF16)
                dsink = -jnp.sum(psink * delta, axis=0, keepdims=True)
                ds_ref[h:h + 1, :] += jnp.broadcast_to(dsink, (1, LANE))
                dqs.append(_dot(ds, kw) * (HEAD_DIM ** -0.5))
                dk_sum = dk_sum + _dot_tn(ds, q_ref[:, hs])
                dv_sum = dv_sum + _dot_tn(pn.astype(BF16), dob)
            gs = slice(kh * G * HEAD_DIM, (kh + 1) * G * HEAD_DIM)
            dq_ref[:, gs] = jnp.concatenate(dqs, axis=1).astype(BF16)
            z = z_ref[:, gs].astype(F32)
            dz_ref[:, gs] = (dy_ref[:, gs].astype(F32) * o_ref[:, gs].astype(F32) * _dsilu(z)).astype(BF16)
            dks.append(dk_sum * (HEAD_DIM ** -0.5))
            dvs.append(dv_sum)
        dk_acc[pl.ds(start, NK), :] += jnp.concatenate(dks, axis=1)
        dv_acc[pl.ds(start, NK), :] += jnp.concatenate(dvs, axis=1)

        @pl.when(i == nb - 1)
        def _():
            pltpu.sync_copy(dk_acc, dk_hbm)
            pltpu.sync_copy(dv_acc, dv_hbm)

    row = pl.BlockSpec((TQ, 1024), lambda i: (i, 0))
    return pl.pallas_call(
        body, name="attn_bwd", grid=(nb,),
        in_specs=[row, row, _seg(TQ, 1024, OFF_Q), _seg(TQ, 1024, OFF_AZ), *_kv_specs(), _full((N_HEADS, LANE)), ANY],
        out_specs=[_seg(TQ, QZ, OFF_Q), ANY, ANY, _full((N_HEADS, LANE))],
        out_shape=[jax.ShapeDtypeStruct(dproj.shape, BF16),
                   jax.ShapeDtypeStruct((S + ATTN_BACK, 256), F32), jax.ShapeDtypeStruct((S + ATTN_BACK, 256), F32),
                   jax.ShapeDtypeStruct((N_HEADS, LANE), F32)],
        scratch_shapes=[pltpu.VMEM((S + ATTN_BACK, 256), F32)] * 2 + [pltpu.VMEM((NK, 256), BF16)] * 2,
        input_output_aliases={9: 0}, compiler_params=_cp(("arbitrary",), VMEM_BIG),
    )(dy, o, proj, proj, proj, proj, proj, proj, sink_b, dproj)


def _conv_bwd_a(dy, proj, dw, dw_b, ln_g, ln_b, pw, dproj, tb=256):
    S = proj.shape[0]
    H = CONV_HALO

    def body(dy_ref, a_ref, ah_ref, b_ref, bh_ref, z_ref, dw_ref, dwb_ref, lg_ref, lb_ref, pw_ref, _,
             dcv_ref, dz_ref, dpw_ref, dlg_ref, dlb_ref, ddwb_ref, ddw_ref, gbuf, ybuf):
        i = pl.program_id(0)
        _conv_glu_dw(a_ref, ah_ref, b_ref, bh_ref, dw_ref, gbuf, ybuf, i, tb)
        lg = lg_ref[...]
        xh, rstd, yn = _layer_norm_fwd(ybuf[...] + dwb_ref[...], lg, lb_ref[...])
        u = _silu(yn).astype(BF16)
        out = _dot(u, pw_ref[...])
        dyv, z = dy_ref[...].astype(F32), z_ref[...].astype(F32)
        dz_ref[...] = (dyv * out * _dsilu(z)).astype(BF16)
        dout = (dyv * _silu(z)).astype(BF16)
        dpw = _dot_tn(u, dout)
        dyn = _dot_nt(dout, pw_ref[...]) * _dsilu(yn)
        dlg = jnp.sum(dyn * xh, axis=0, keepdims=True)
        dlb = jnp.sum(dyn, axis=0, keepdims=True)
        dxh = dyn * lg
        dcv = rstd * (dxh - jnp.mean(dxh, axis=-1, keepdims=True) - xh * jnp.mean(dxh * xh, axis=-1, keepdims=True))
        dcv_ref[...] = dcv
        ddwb = jnp.sum(dcv, axis=0, keepdims=True)
        ybuf[...] = dcv

        @pl.when(i == 0)
        def _():
            dpw_ref[...], dlg_ref[...], dlb_ref[...], ddwb_ref[...] = dpw, dlg, dlb, ddwb
            ddw_ref[...] = jnp.zeros_like(ddw_ref)

        @pl.when(i > 0)
        def _():
            dpw_ref[...] += dpw
            dlg_ref[...] += dlg
            dlb_ref[...] += dlb
            ddwb_ref[...] += ddwb

        _lane_chunks(lambda cs: _taps_reduce(gbuf, ybuf, ddw_ref, tb, CONV_TAPS, cs))

    vec = _full((1, 1024))
    row = pl.BlockSpec((tb, 1024), lambda i: (i, 0))
    big = _full((1024, 1024))
    return pl.pallas_call(
        body, name="conv_bwd_a", grid=(S // tb,),
        in_specs=[row, *_conv_specs(tb), _full((32, 1024)), vec, vec, vec, big, ANY],
        out_specs=[row, _seg(tb, 1024, OFF_CZ), big, vec, vec, vec, _full((32, 1024))],
        out_shape=[jax.ShapeDtypeStruct((S, 1024), F32), jax.ShapeDtypeStruct(dproj.shape, BF16),
                   jax.ShapeDtypeStruct((1024, 1024), F32), jax.ShapeDtypeStruct((1, 1024), F32),
                   jax.ShapeDtypeStruct((1, 1024), F32), jax.ShapeDtypeStruct((1, 1024), F32),
                   jax.ShapeDtypeStruct((32, 1024), F32)],
        scratch_shapes=[pltpu.VMEM((tb + H, 1024), F32), pltpu.VMEM((tb, 1024), F32)],
        input_output_aliases={11: 1}, compiler_params=_cp(("arbitrary",)),
    )(dy, proj, proj, proj, proj, proj, dw, dw_b, ln_g, ln_b, pw, dproj)


def _conv_bwd_b(dcv, proj, dw, dproj, tb=256):
    S = proj.shape[0]
    H = CONV_HALO
    nb = S // tb

    def body(d_ref, dn_ref, a_ref, b_ref, dw_ref, _, dab_ref, dbuf, gbuf):
        i = pl.program_id(0)
        dbuf[:tb, :] = d_ref[...]
        dbuf[tb:, :] = jnp.where(i < nb - 1, dn_ref[...], 0.0)
        _lane_chunks(lambda cs: _taps_apply(dbuf, dw_ref, gbuf, tb, CONV_TAPS_T, cs))
        dg = gbuf[...]
        a, s = a_ref[...].astype(F32), _sig(b_ref[...].astype(F32))
        dab_ref[:, :1024] = (dg * s).astype(BF16)
        dab_ref[:, 1024:] = (dg * a * s * (1.0 - s)).astype(BF16)

    row = pl.BlockSpec((tb, 1024), lambda i: (i, 0))
    nxt = pl.BlockSpec((H, 1024), lambda i: (jnp.minimum((i + 1) * (tb // H), S // H - 1), 0))
    return pl.pallas_call(
        body, name="conv_bwd_b", grid=(nb,),
        in_specs=[row, nxt, _seg(tb, 1024, OFF_CA), _seg(tb, 1024, OFF_CB), _full((32, 1024)), ANY],
        out_specs=_seg(tb, 2048, OFF_CA), out_shape=jax.ShapeDtypeStruct(dproj.shape, BF16),
        scratch_shapes=[pltpu.VMEM((tb + H, 1024), F32), pltpu.VMEM((tb, 1024), F32)],
        input_output_aliases={5: 0}, compiler_params=_cp(("parallel",)),
    )(dcv, dcv, proj, proj, dw, dproj)


def _norm_bwd(dh, x, dx_out, g, scale, tb=256):
    S = x.shape[0]

    def body(dh_ref, x_ref, dxo_ref, g_ref, sc_ref, dx_ref, dsh_ref, da_ref):
        i = pl.program_id(0)
        xv, dhv = x_ref[...], dh_ref[...]
        r = lax.rsqrt(jnp.mean(xv * xv, axis=-1, keepdims=True) + EPS)
        xh = xv * r
        gy = dhv * (g_ref[...] * (1.0 + sc_ref[...]))
        dx_ref[...] = dxo_ref[...] + r * (gy - xh * jnp.mean(gy * xh, axis=-1, keepdims=True))
        dsh = jnp.sum(dhv, axis=0, keepdims=True)
        da = jnp.sum(dhv * xh, axis=0, keepdims=True)

        @pl.when(i == 0)
        def _():
            dsh_ref[...], da_ref[...] = dsh, da

        @pl.when(i > 0)
        def _():
            dsh_ref[...] += dsh
            da_ref[...] += da

    row = pl.BlockSpec((tb, D), lambda i: (i, 0))
    vec = _full((1, D))
    return pl.pallas_call(
        body, name="norm_bwd", grid=(S // tb,), in_specs=[row, row, row, vec, vec], out_specs=[row, vec, vec],
        out_shape=[jax.ShapeDtypeStruct((S, D), F32), jax.ShapeDtypeStruct((1, D), F32), jax.ShapeDtypeStruct((1, D), F32)],
        compiler_params=_cp(("arbitrary",)),
    )(dh, x, dx_out, g, scale)


def _mod_bwd(d_a, norm_g, scale):
    def body(da_ref, g_ref, sc_ref, dg_ref, dsc_ref):
        dg_ref[...] = da_ref[...] * (1.0 + sc_ref[...])
        dsc_ref[...] = da_ref[...] * g_ref[...]

    return pl.pallas_call(body, name="mod_bwd", out_shape=[jax.ShapeDtypeStruct(d_a.shape, F32)] * 2)(d_a, norm_g, scale)


def _reduce_adamw(parts, w, m, v, name):
    L, rows, C = w.shape
    tr = rows if rows % 64 else 64

    def body(*refs):
        p_refs, (w_ref, m_ref, v_ref, g_ref, dl_ref, m2_ref, v2_ref) = refs[:L * N_DEV], refs[L * N_DEV:]
        for l in range(L):
            g = p_refs[l * N_DEV][0].astype(F32)
            for k in range(1, N_DEV):
                g = g + p_refs[l * N_DEV + k][0].astype(F32)
            g_ref[l] = g
            dl_ref[l], m2_ref[l], v2_ref[l] = _adam_math(g, w_ref[l], m_ref[l], v_ref[l])

    slot = lambda k: pl.BlockSpec((1, tr, C), lambda i: (k, i, 0))
    blk = pl.BlockSpec((L, tr, C), lambda i: (0, i, 0))
    return pl.pallas_call(
        body, name=name, grid=(rows // tr,), in_specs=[slot(k) for _ in range(L) for k in range(N_DEV)] + [blk] * 3,
        out_specs=[blk] * 4, out_shape=[jax.ShapeDtypeStruct((L, rows, C), F32)] * 4,
        compiler_params=_cp(("parallel",)),
    )(*[p for p in parts for _ in range(N_DEV)], w, m, v)


def _small_final(parts, w, m, v):
    R = w.shape[0]

    def body(p_ref, w_ref, m_ref, v_ref, g_ref, dl_ref, m2_ref, v2_ref):
        g = p_ref[0]
        for k in range(1, N_DEV):
            g = g + p_ref[k]
        delta, m2, v2 = _adam_math(g, w_ref[...], m_ref[...], v_ref[...])
        g_ref[...], dl_ref[...], m2_ref[...], v2_ref[...] = g, delta, m2, v2

    return pl.pallas_call(body, name="small_final", out_shape=[jax.ShapeDtypeStruct((R, LANE), F32)] * 4)(parts, w, m, v)


def _layer_fwd(x, mod, small, W, more_w):
    shift, scale, gate = mod
    h = _norm_mod(x, small["norm_g"], scale, shift)
    proj = _mm(h, W["w_in_t"], BF16, 512, 1536, name="proj_mm", nt=True)
    W.update(more_w(proj))
    y_pool = _pool_fwd(proj, W["pool_w"], small["pool_scale"])
    o, y_attn = _attn_fwd(proj, small["sink_b"])
    y_conv = _conv_fwd(proj, W["conv_dw"], small["conv_dw_b"], small["conv_ln_g"], small["conv_ln_b"], W["conv_pw"])
    merged, pp, pa, pc = _merge_fwd(y_pool, y_attn, y_conv, W["wbp"], W["wba"], W["wbc"], proj)
    x_new = _out_fwd(x, merged, W["w_out"], gate)
    stash = dict(x=x, h=h, proj=proj, o=o, y_pool=y_pool, y_attn=y_attn, y_conv=y_conv,
                 merged=merged, pp=pp, pa=pa, pc=pc)
    return x_new, stash


def _layer_bwd(dx, st, mod, small, W, put):
    shift, scale, gate = mod
    proj = st["proj"]
    gmat = _mm_tn(st["merged"], dx, 1024, 1024, 1024, name="wout_tn")
    d_w_out, d_gate = _wout_post(gmat, W["w_out"], gate)
    dpp, dpa, dpc, dgp, dga, dgc = _out_bwd(dx, gate, W["w_out"], st["pp"], st["pa"], st["pc"], proj)
    dy_pool = _mm(dpp, W["wbp"], BF16, 512, 1024, name="branch_bwd_mm", nt=True)
    dy_attn = _mm(dpa, W["wba"], BF16, 512, 1024, name="branch_bwd_mm", nt=True)
    dy_conv = _mm(dpc, W["wbc"], BF16, 512, 1024, name="branch_bwd_mm", nt=True)
    d_wbp = _mm_tn(st["y_pool"], dpp, 1024, 1024, 1024, name="branch_tn")
    d_wba = _mm_tn(st["y_attn"], dpa, 1024, 1024, 1024, name="branch_tn")
    d_wbc = _mm_tn(st["y_conv"], dpc, 1024, 1024, 1024, name="branch_tn")

    dproj = lax.empty(proj.shape, BF16)
    dmix, dproj, d_pool_scale, d_pool_w = _pool_bwd_a(dy_pool, proj, W["pool_w"], small["pool_scale"], dproj)
    dproj = _pool_bwd_b(dmix, dproj)
    dproj, dk, dv, d_sink = _attn_bwd(dy_attn, st["o"], proj, small["sink_b"], dproj)
    dcv, dproj, d_pw, d_ln_g, d_ln_b, d_dw_b, d_dw = _conv_bwd_a(
        dy_conv, proj, W["conv_dw"], small["conv_dw_b"], small["conv_ln_g"], small["conv_ln_b"], W["conv_pw"], dproj)
    dproj = _conv_bwd_b(dcv, proj, W["conv_dw"], dproj)
    for piece, off in ((dk[ATTN_BACK:], OFF_K), (dv[ATTN_BACK:], OFF_V), (dgp, OFF_GP), (dga, OFF_GA), (dgc, OFF_GC)):
        dproj = lax.dynamic_update_slice(dproj, piece.astype(BF16), (0, off))

    tok = put(dict(pool_w=d_pool_w, conv_dw=d_dw[:CONV_K], conv_pw=d_pw, wbp=d_wbp, wba=d_wba, wbc=d_wbc, w_out=d_w_out))
    d_w_in_t = _mm_tn(dproj, st["h"], 768, 2048, 1024, name="win_tn", vmem=VMEM_BIG, after=tok, out_dtype=BF16)
    tok = put(dict(w_in=d_w_in_t))
    dh = _mm(dproj, W["w_in_t"], F32, 1024, 2048, 1536, name="dh_mm", vmem=VMEM_BIG, after=tok)
    dx_in, d_shift, d_a = _norm_bwd(dh, st["x"], dx, small["norm_g"], scale)
    sm = dict(d_a=d_a, d_shift=d_shift, d_gate=d_gate, pool_scale=d_pool_scale, attn_sink=d_sink[:, 0],
              conv_dw_b=d_dw_b, conv_ln_g=d_ln_g, conv_ln_b=d_ln_b)
    return dx_in, sm


def _local_step(x, target, mods, smalls, get_w, final_g, put_g, end_layer):
    stashes, Ws = [], []
    for l in range(DEPTH):
        w, more_w = get_w(l, x)
        Ws.append(w)
        x, st = _layer_fwd(x, mods[l], smalls[l], w, more_w)
        stashes.append(st)
    dx, d_final_g, loss_lanes = _final_loss(x, target, final_g)
    sms = [None] * DEPTH
    for l in reversed(range(DEPTH)):
        dx, sms[l] = _layer_bwd(dx, stashes[l], mods[l], smalls[l], Ws[l], functools.partial(put_g, l))
        end_layer(l, dx)
    return loss_lanes, dx, d_final_g, sms


BIG = ("w_in", "pool_w", "conv_pw", "wbp", "wba", "wbc", "w_out")
GRADS = BIG + ("conv_dw",)


def _full_w_in(g):
    return dict(w_in_t=g.reshape(IN_WIDTH, D))


def _full_weights(g, conv_dw):
    cols = lambda a: jnp.transpose(a, (1, 0, 2)).reshape(a.shape[1], -1)
    pool_w = jnp.transpose(g["pool_w"], (1, 0, 2, 3)).reshape(4, 256, 256)
    conv_pw = g["conv_pw"].reshape(1024, 1024)
    wbp, wba, wbc = cols(g["wbp"]), cols(g["wba"]), cols(g["wbc"])
    w_out = g["w_out"].reshape(D, D)
    conv_dw = jnp.pad(cols(conv_dw), ((0, 32 - CONV_K), (0, 0)))
    return dict(pool_w=pool_w, conv_pw=conv_pw, wbp=wbp, wba=wba, wbc=wbc, w_out=w_out, conv_dw=conv_dw)


def _pieces(name, g):
    if name == "w_in":
        return g.reshape(8, IN_WIDTH // 8, D)
    if name == "pool_w":
        return jnp.transpose(g.reshape(4, 8, 32, 256), (1, 0, 2, 3))
    if name == "conv_dw":
        return jnp.transpose(g.reshape(CONV_K, 8, 128), (1, 0, 2))
    if name == "conv_pw":
        return g.reshape(8, 128, 1024)
    if name in ("wbp", "wba", "wbc"):
        return jnp.transpose(g.reshape(1024, 8, 256), (1, 0, 2))
    return g.reshape(8, 256, D)


def _pack_small(items, rows):
    flat = jnp.concatenate([a.reshape(-1).astype(F32) for a in items])
    return jnp.pad(flat, (0, rows * LANE - flat.shape[0])).reshape(rows, LANE)


def _unpack_small(packed, shapes):
    flat, out, off = packed.reshape(-1), [], 0
    for s in shapes:
        n = 1
        for d in s:
            n *= d
        out.append(flat[off:off + n].reshape(s))
        off += n
    return out


def kernel(x, c, norm_g, w_ada, b_ada, w_in, pool_w, pool_scale, attn_sink, conv_dw, conv_dw_b, conv_ln_g, conv_ln_b, conv_pw, w_branch_pool, w_branch_attn, w_branch_conv, w_out, final_g, loss_target, m_norm_g, m_w_ada, m_b_ada, m_w_in, m_pool_w, m_pool_scale, m_attn_sink, m_conv_dw, m_conv_dw_b, m_conv_ln_g, m_conv_ln_b, m_conv_pw, m_w_branch_pool, m_w_branch_attn, m_w_branch_conv, m_w_out, m_final_g, v_norm_g, v_w_ada, v_b_ada, v_w_in, v_pool_w, v_pool_scale, v_attn_sink, v_conv_dw, v_conv_dw_b, v_conv_ln_g, v_conv_ln_b, v_conv_pw, v_w_branch_pool, v_w_branch_attn, v_w_branch_conv, v_w_out, v_final_g):
    L = DEPTH
    me = 4 * lax.axis_index("x") + 2 * lax.axis_index("y") + lax.axis_index("c")
    tr = lambda a: jnp.swapaxes(a, 1, 2)
    shards = dict(w_in=tr(w_in), pool_w=pool_w, conv_dw=conv_dw, conv_pw=conv_pw, wbp=w_branch_pool, wba=w_branch_attn,
                  wbc=w_branch_conv, w_out=w_out)
    moms = dict(w_in=(tr(m_w_in), tr(v_w_in)), pool_w=(m_pool_w, v_pool_w), conv_dw=(m_conv_dw, v_conv_dw),
                conv_pw=(m_conv_pw, v_conv_pw), wbp=(m_w_branch_pool, v_w_branch_pool),
                wba=(m_w_branch_attn, v_w_branch_attn), wbc=(m_w_branch_conv, v_w_branch_conv), w_out=(m_w_out, v_w_out))

    n_cd = L * CONV_K * 128
    first = _all_gather([_pack_small([c, conv_dw], 144)], "gather_c")[0].reshape(N_DEV, -1)
    c_all = first[:, :D]
    conv_dw_all = first[:, D:D + n_cd].reshape(N_DEV, L, CONV_K, 128)

    mod_part = _mod_fwd(c_all, w_ada)
    mod_all = _all_gather([mod_part.reshape(-1, LANE)], "gather_mod")[0].reshape(N_DEV, L, N_DEV, -1)
    mod = jnp.transpose(lax.dynamic_index_in_dim(mod_all, me, axis=2, keepdims=False), (1, 0, 2)).reshape(L, 3 * D)
    mod = mod + b_ada
    mods = [(mod[l:l + 1, :D], mod[l:l + 1, D:2 * D], mod[l:l + 1, 2 * D:]) for l in range(L)]

    w_in_0 = _all_gather([(shards["w_in"][0] + mod[0, 0] * 0.0).astype(BF16)], "gather_w_in_0")[0]
    gathers, tok = [], w_in_0[0, 0, 0].astype(F32) * 0.0
    for l in range(L):
        first = lambda a: (a + tok).astype(BF16)
        mine = [first(shards["w_in"][l])], [first(shards[k][l]) if k == BIG[1] else shards[k][l].astype(BF16) for k in BIG[1:]]
        started = [None if (l, n) == (0, 0) else _spread_start(v, False, f"gather_start_{l}_{n}") for n, v in enumerate(mine)]
        gathers.append((mine, started))
        tok = sum(st[-1][0, 0] for st in started if st is not None)
    mods[0] = (mods[0][0] + tok,) + mods[0][1:]

    sink_b = jnp.broadcast_to(attn_sink[:, :, None], (L, N_HEADS, LANE))
    smalls = [dict(norm_g=norm_g[l:l + 1], pool_scale=pool_scale[l:l + 1], sink_b=sink_b[l], conv_dw_b=conv_dw_b[l:l + 1],
                   conv_ln_g=conv_ln_g[l:l + 1], conv_ln_b=conv_ln_b[l:l + 1]) for l in range(L)]

    def with_mine(landed, mine):
        return lax.dynamic_update_slice(landed, mine, (me,) + (0,) * (landed.ndim - 1))

    def get_w(l, x_in):
        mine, started = gathers[l]
        if l == 0:
            w_in = w_in_0
        else:
            sent, landed = _spread_wait(started[0], x_in, False, f"gather_wait_{l}_0")
            w_in = with_mine(landed[0], sent[0][None])

        def more_w(proj):
            sent, landed = _spread_wait(started[1], proj, False, f"gather_wait_{l}_1")
            g = {k: with_mine(a, b[None]) for k, a, b in zip(BIG[1:], landed, sent)}
            return _full_weights(g, conv_dw_all[:, l])

        return _full_w_in(w_in), more_w

    pending, parts = {l: [] for l in range(L)}, [dict() for _ in range(L)]

    def put_g(l, grads):
        names = tuple(k for k in GRADS if k in grads)
        pieces = [_pieces(k, grads[k]).astype(BF16) for k in names]
        started = _spread_start(pieces, True, f"scatter_start_{l}_{len(pending[l])}")
        pending[l].append((names, started))
        return started[-1][0:1, 0:1]

    def finish(l, after):
        for n, (names, started) in enumerate(pending.pop(l)):
            sent, landed = _spread_wait(started, after, True, f"scatter_wait_{l}_{n}")
            for k, a, b in zip(names, landed, sent):
                parts[l][k] = with_mine(a, lax.dynamic_slice_in_dim(b, me, 1, axis=0))

    def end_layer(l, dx):
        if l + 1 in pending:
            finish(l + 1, dx)

    loss_lanes, grad_x, d_final_g, sms = _local_step(x[0], loss_target[0], mods, smalls, get_w, final_g.reshape(1, D),
                                                     put_g, end_layer)
    finish(0, grad_x)

    stack = lambda k: jnp.concatenate([sms[l][k].reshape(1, -1) for l in range(L)], axis=0)
    scale_all = jnp.concatenate([mods[l][1] for l in range(L)], axis=0)
    d_norm_g, d_scale = _mod_bwd(stack("d_a"), norm_g, scale_all)
    dmod = jnp.concatenate([stack("d_shift"), d_scale, stack("d_gate")], axis=1)
    small_names = ("norm_g", "b_ada", "pool_scale", "attn_sink", "conv_dw_b", "conv_ln_g", "conv_ln_b", "final_g")
    small_g = (d_norm_g, dmod, stack("pool_scale"), stack("attn_sink"), stack("conv_dw_b"), stack("conv_ln_g"),
               stack("conv_ln_b"), d_final_g.reshape(D))
    small_w = (norm_g, b_ada, pool_scale, attn_sink, conv_dw_b, conv_ln_g, conv_ln_b, final_g)
    small_m = (m_norm_g, m_b_ada, m_pool_scale, m_attn_sink, m_conv_dw_b, m_conv_ln_g, m_conv_ln_b, m_final_g)
    small_v = (v_norm_g, v_b_ada, v_pool_scale, v_attn_sink, v_conv_dw_b, v_conv_ln_g, v_conv_ln_b, v_final_g)
    shapes = [a.shape for a in small_w] + [(D,)]
    n_small = sum(a.size for a in small_w) + D
    R = -(-n_small // (8 * LANE)) * 8
    zero = jnp.zeros((D,), F32)
    small_parts = _all_gather([_pack_small(small_g + (loss_lanes,), R)], "gather_small")[0]
    sg, sd, sm2, sv2 = _small_final(small_parts, _pack_small(small_w + (zero,), R), _pack_small(small_m + (zero,), R),
                                    _pack_small(small_v + (zero + 1.0,), R))
    sg, sd, sm2, sv2 = (_unpack_small(a, shapes) for a in (sg, sd, sm2, sv2))
    loss = jnp.sum(sg[-1])
    res = {n: (sg[i], sd[i], sm2[i], sv2[i]) for i, n in enumerate(small_names)}

    off = norm_g.size
    dmod_all = small_parts.reshape(N_DEV, -1)[:, off:off + L * 3 * D].reshape(N_DEV, L, 3 * D)
    dmod_mine = jnp.transpose(lax.dynamic_slice_in_dim(dmod_all, me * (3 * D // N_DEV), 3 * D // N_DEV, axis=2), (1, 0, 2))
    res["w_ada"] = _wada_bwd(c_all.T, dmod_mine, w_ada, m_w_ada, v_w_ada)

    for k in GRADS:
        shp = shards[k].shape
        to3d = lambda a: a.reshape(L, -1, shp[-1])
        out = _reduce_adamw([parts[l][k].reshape(N_DEV, -1, shp[-1]) for l in range(L)], to3d(shards[k]),
                            to3d(moms[k][0]), to3d(moms[k][1]), "adamw_" + k)
        res[k] = tuple(a.reshape(shp) for a in out)
    res["w_in"] = tuple(tr(a) for a in res["w_in"])

    order = ("norm_g", "w_ada", "b_ada", "w_in", "pool_w", "pool_scale", "attn_sink", "conv_dw", "conv_dw_b", "conv_ln_g",
             "conv_ln_b", "conv_pw", "wbp", "wba", "wbc", "w_out", "final_g")
    outs = [loss, grad_x[None]]
    for j in range(4):
        outs += [res[n][j] for n in order]
    return tuple(outs)
```

```python
import functools

import jax
import jax.numpy as jnp
from jax import lax
from jax.experimental import pallas as pl
from jax.experimental.pallas import tpu as pltpu

F32, BF16 = jnp.float32, jnp.bfloat16
MESH = pl.DeviceIdType.MESH
ANY = pl.BlockSpec(memory_space=pl.ANY)

N_DEV = 8
D = 2048
DEPTH = 4
EPS = 1e-6
IN_WIDTH = 13824
POOL_WINDOWS = (2, 4, 8, 16)
POOL_GROUP = 256
POOL_HALO = 16
CONV_K = 31
CONV_HALO = 32
N_HEADS, N_KV, HEAD_DIM = 16, 4, 64
ATTN_TQ = 256
ATTN_BACK = 128
LANE = 128
VMEM_BIG = 56 * 1024 * 1024

OFF_U, OFF_Z, OFF_Q, OFF_K, OFF_V, OFF_AZ, OFF_CA, OFF_CB, OFF_CZ = 0, 1024, 2048, 3072, 3328, 3584, 4608, 5632, 6656
OFF_GP, OFF_GA, OFF_GC = 7680, 9728, 11776


def _seg(rows, width, off, first_row=None):
    start = (lambda i: i * rows) if first_row is None else first_row
    return pl.BlockSpec((pl.Element(rows), pl.Element(width)), lambda i: (pl.multiple_of(start(i), rows), off))


def _seg2(tm, tn, off):
    return pl.BlockSpec((pl.Element(tm), pl.Element(tn)), lambda j, i: (i * tm, pl.multiple_of(off + j * tn, LANE)))

ADAM_LR, ADAM_B1, ADAM_B2, ADAM_EPS, ADAM_WD, ADAM_STEP = 0.001, 0.9, 0.999, 1e-08, 0.01, 10


def _cp(sem=None, vmem=None):
    return pltpu.CompilerParams(dimension_semantics=sem, vmem_limit_bytes=vmem)


def _sig(x):
    return jax.nn.sigmoid(x)


def _silu(x):
    return x * _sig(x)


def _dsilu(x):
    s = _sig(x)
    return s * (1.0 + x * (1.0 - s))


def _dot(a, b):
    return jnp.dot(a, b, preferred_element_type=F32)


def _dot_tn(a, b):
    return lax.dot_general(a, b, (((0,), (0,)), ((), ())), preferred_element_type=F32)


def _dot_nt(a, b):
    return lax.dot_general(a, b, (((1,), (1,)), ((), ())), preferred_element_type=F32)


def _full(shape):
    n = len(shape)
    return pl.BlockSpec(shape, lambda *_: (0,) * n)


def _my_pos():
    return lax.axis_index("x"), lax.axis_index("y"), lax.axis_index("c")


def _all_gather(xs, name):
    n = len(xs)

    def body(*refs):
        x_refs, o_refs = refs[:n], refs[n:2 * n]
        send_sems, recv_sems, local_sems = refs[2 * n:]
        x, y, c = _my_pos()
        sibling = (x, y, 1 - c)
        chips = [(1 - x, y), (x, 1 - y), (1 - x, 1 - y)]
        me = 4 * x + 2 * y + c

        def slot(px, py, pc):
            return 4 * px + 2 * py + pc

        def copy(t, k, block, to, src=None):
            dst = o_refs[t].at[block]
            return pltpu.make_async_remote_copy(
                src_ref=dst if src is None else src, dst_ref=dst,
                send_sem=send_sems.at[t, k], recv_sem=recv_sems.at[t, k],
                device_id=to, device_id_type=MESH)

        mine = [pltpu.make_async_copy(x_refs[t], o_refs[t].at[me], local_sems.at[t]) for t in range(n)]
        for cp in mine:
            cp.start()
        first = []
        for t in range(n):
            first.append(copy(t, 0, me, sibling, src=x_refs[t]))
            for j, chip in enumerate(chips):
                first.append(copy(t, 1 + j, me, (*chip, c), src=x_refs[t]))
        for cp in first:
            cp.start()
        passed = []
        for j, chip in enumerate(chips):
            for t in range(n):
                copy(t, 1 + j, slot(*chip, c), (x, y, c)).wait_recv()
                fwd = copy(t, 4 + j, slot(*chip, c), sibling)
                fwd.start()
                passed.append(fwd)
        for t in range(n):
            copy(t, 0, slot(x, y, 1 - c), (x, y, c)).wait_recv()
            for j, chip in enumerate(chips):
                copy(t, 4 + j, slot(*chip, 1 - c), (x, y, c)).wait_recv()
        for cp in first + passed:
            cp.wait_send()
        for cp in mine:
            cp.wait()

    return pl.pallas_call(
        body, name=name,
        out_shape=[jax.ShapeDtypeStruct((N_DEV,) + a.shape, a.dtype) for a in xs],
        in_specs=[ANY] * n, out_specs=[ANY] * n,
        scratch_shapes=[pltpu.SemaphoreType.DMA((n, 7)), pltpu.SemaphoreType.DMA((n, 7)),
                        pltpu.SemaphoreType.DMA((n,))],
    )(*xs)


N_PEER = N_DEV - 1
HBM = pl.BlockSpec(memory_space=pltpu.HBM)
SEM = pl.BlockSpec(memory_space=pltpu.SEMAPHORE)
EFFECT = pltpu.SideEffectType.DATAFLOW_SIDE_EFFECTING


def _peer(k):
    x, y, c = _my_pos()
    flip = lambda v, bit: 1 - v if bit else v
    return flip(x, (k >> 2) & 1), flip(y, (k >> 1) & 1), flip(c, k & 1)


def _spread_copies(v_ref, land_ref, send_sems, recv_sems, per_peer):
    x, y, c = _my_pos()
    me = 4 * x + 2 * y + c
    copies = []
    for k in range(1, N_DEV):
        px, py, pc = _peer(k)
        src = v_ref.at[4 * px + 2 * py + pc] if per_peer else v_ref
        copies.append(pltpu.make_async_remote_copy(
            src_ref=src, dst_ref=land_ref.at[me], send_sem=send_sems[k - 1], recv_sem=recv_sems[k - 1],
            device_id=(px, py, pc), device_id_type=MESH))
    return copies


def _spread_start(vs, per_peer, name):
    n = len(vs)
    lands = [(N_DEV,) + (v.shape[1:] if per_peer else v.shape) for v in vs]
    n_sem = 2 * N_PEER * n

    def body(*refs):
        v_refs, land_refs, outs = refs[:n], refs[n:2 * n], refs[2 * n:]
        for t in range(n):
            sems = outs[2 * N_PEER * t:2 * N_PEER * (t + 1)]
            for cp in _spread_copies(v_refs[t], land_refs[t], sems[:N_PEER], sems[N_PEER:], per_peer):
                cp.start()
        token = outs[n_sem + 2 * n]
        token[...] = jnp.zeros_like(token)

    hbm = lambda a: pltpu.with_memory_space_constraint(a, pltpu.HBM)
    return pl.pallas_call(
        body, name=name,
        out_shape=((pltpu.SemaphoreType.DMA(()),) * n_sem + tuple(pltpu.HBM(v.shape, v.dtype) for v in vs)
                   + tuple(pltpu.HBM(s, v.dtype) for s, v in zip(lands, vs)) + (jax.ShapeDtypeStruct((8, LANE), F32),)),
        in_specs=(HBM,) * (2 * n), out_specs=(SEM,) * n_sem + (HBM,) * (2 * n) + (pl.BlockSpec(memory_space=pltpu.VMEM),),
        input_output_aliases={t: n_sem + t for t in range(2 * n)},
        compiler_params=pltpu.CompilerParams(has_side_effects=EFFECT),
    )(*[hbm(v) for v in vs], *[hbm(lax.empty(s, v.dtype)) for s, v in zip(lands, vs)])


def _spread_wait(started, after, per_peer, name):
    n = (len(started) - 1) // (2 * N_PEER + 2)
    n_sem = 2 * N_PEER * n
    sems, thru = started[:n_sem], started[n_sem:n_sem + 2 * n]

    def body(*refs):
        v_refs, land_refs, rest = refs[:n], refs[n:2 * n], refs[2 * n:]
        for t in range(n):
            s = rest[2 * N_PEER * t:2 * N_PEER * (t + 1)]
            for cp in _spread_copies(v_refs[t], land_refs[t], s[:N_PEER], s[N_PEER:], per_peer):
                cp.wait_send()
                cp.wait_recv()

    out = pl.pallas_call(
        body, name=name,
        out_shape=tuple(pltpu.HBM(a.shape, a.dtype) for a in thru),
        in_specs=(HBM,) * (2 * n) + (SEM,) * n_sem + (ANY,), out_specs=(HBM,) * (2 * n),
        input_output_aliases={t: t for t in range(2 * n)},
        compiler_params=pltpu.CompilerParams(has_side_effects=EFFECT),
    )(*thru, *sems, after)
    return out[:n], out[n:]


def _mm(a, b, out_dtype, tm, tn, tk=None, name="mm", vmem=None, after=None, nt=False):
    M, K = a.shape
    N = b.shape[0] if nt else b.shape[1]
    tk = K if tk is None else tk
    nk = K // tk
    assert M % tm == 0 and N % tn == 0 and K % tk == 0
    dep = () if after is None else (after,)

    def body(*refs):
        a_ref, b_ref, o_ref, *acc = refs[len(dep):]
        prod = (_dot_nt if nt else _dot)(a_ref[...].astype(BF16), b_ref[...])
        if nk == 1:
            o_ref[...] = prod.astype(out_dtype)
            return
        acc_ref = acc[0] if acc else o_ref
        k = pl.program_id(2)

        @pl.when(k == 0)
        def _():
            acc_ref[...] = prod

        @pl.when(k > 0)
        def _():
            acc_ref[...] += prod

        if acc:
            @pl.when(k == nk - 1)
            def _():
                o_ref[...] = acc_ref[...].astype(out_dtype)

    scratch = [pltpu.VMEM((tm, tn), F32)] if (nk > 1 and out_dtype != F32) else []
    b_spec = pl.BlockSpec((tn, tk), lambda j, i, k: (j, k)) if nt else pl.BlockSpec((tk, tn), lambda j, i, k: (k, j))
    return pl.pallas_call(
        body, name=name, grid=(N // tn, M // tm, nk),
        in_specs=[_full((1, 1))] * len(dep) + [pl.BlockSpec((tm, tk), lambda j, i, k: (i, k)), b_spec],
        out_specs=pl.BlockSpec((tm, tn), lambda j, i, k: (i, j)),
        out_shape=jax.ShapeDtypeStruct((M, N), out_dtype), scratch_shapes=scratch,
        compiler_params=_cp(("parallel", "parallel", "arbitrary"), vmem),
    )(*dep, a, b)


def _mm_tn(a, b, tm, tn, ts, name="mm_tn", vmem=None, after=None, out_dtype=F32):
    S, Ka = a.shape
    _, N = b.shape
    assert Ka % tm == 0 and N % tn == 0 and S % ts == 0
    dep = () if after is None else (after,)

    nk = S // ts

    def body(*refs):
        a_ref, b_ref, o_ref, *acc = refs[len(dep):]
        acc_ref = acc[0] if acc else o_ref
        prod = _dot_tn(a_ref[...].astype(BF16), b_ref[...].astype(BF16))
        k = pl.program_id(2)

        @pl.when(k == 0)
        def _():
            acc_ref[...] = prod

        @pl.when(k > 0)
        def _():
            acc_ref[...] += prod

        if acc:
            @pl.when(k == nk - 1)
            def _():
                o_ref[...] = acc_ref[...].astype(out_dtype)

    return pl.pallas_call(
        body, name=name, grid=(Ka // tm, N // tn, S // ts),
        in_specs=[_full((1, 1))] * len(dep) + [pl.BlockSpec((ts, tm), lambda i, j, k: (k, i)),
                                               pl.BlockSpec((ts, tn), lambda i, j, k: (k, j))],
        out_specs=pl.BlockSpec((tm, tn), lambda i, j, k: (i, j)),
        out_shape=jax.ShapeDtypeStruct((Ka, N), out_dtype),
        scratch_shapes=[] if out_dtype == F32 else [pltpu.VMEM((tm, tn), F32)],
        compiler_params=_cp(("parallel", "parallel", "arbitrary"), vmem),
    )(*dep, a, b)


def _mod_fwd(c_all, w_ada):
    L, _, n = w_ada.shape

    def body(c_ref, w_ref, o_ref):
        ca = _silu(c_ref[...])
        o_ref[0] = jnp.dot(ca, w_ref[0], preferred_element_type=F32, precision=lax.Precision.HIGHEST)

    return pl.pallas_call(
        body, name="mod_fwd", grid=(L,),
        in_specs=[_full((N_DEV, D)), pl.BlockSpec((1, D, n), lambda l: (l, 0, 0))],
        out_specs=pl.BlockSpec((1, N_DEV, n), lambda l: (l, 0, 0)),
        out_shape=jax.ShapeDtypeStruct((L, N_DEV, n), F32),
        compiler_params=_cp(("parallel",)),
    )(c_all, w_ada)


def _adam_math(g, w, m, v):
    m2 = ADAM_B1 * m + (1.0 - ADAM_B1) * g
    v2 = ADAM_B2 * v + (1.0 - ADAM_B2) * (g * g)
    m_hat = m2 / (1.0 - ADAM_B1 ** ADAM_STEP)
    v_hat = v2 / (1.0 - ADAM_B2 ** ADAM_STEP)
    delta = -ADAM_LR * (m_hat / (jnp.sqrt(v_hat) + ADAM_EPS) + ADAM_WD * w)
    return delta, m2, v2


def _wada_bwd(c_all_t, dmod, w, m, v, tr=256):
    L, _, n = w.shape

    def body(c_ref, d_ref, w_ref, m_ref, v_ref, g_ref, dl_ref, m2_ref, v2_ref):
        ca = _silu(c_ref[...])
        dm = d_ref[0]
        g = ca[:, 0:1] * dm[0:1, :]
        for b in range(1, N_DEV):
            g = g + ca[:, b:b + 1] * dm[b:b + 1, :]
        delta, m2, v2 = _adam_math(g, w_ref[0], m_ref[0], v_ref[0])
        g_ref[0], dl_ref[0], m2_ref[0], v2_ref[0] = g, delta, m2, v2

    blk = pl.BlockSpec((1, tr, n), lambda l, i: (l, i, 0))
    return pl.pallas_call(
        body, name="wada_bwd", grid=(L, D // tr),
        in_specs=[pl.BlockSpec((tr, N_DEV), lambda l, i: (i, 0)), pl.BlockSpec((1, N_DEV, n), lambda l, i: (l, 0, 0)),
                  blk, blk, blk],
        out_specs=[blk] * 4, out_shape=[jax.ShapeDtypeStruct(w.shape, F32)] * 4,
        compiler_params=_cp(("parallel", "parallel")),
    )(c_all_t, dmod, w, m, v)


def _norm_mod(x, g, scale, shift, tb=256):
    S = x.shape[0]

    def body(x_ref, g_ref, sc_ref, sh_ref, h_ref):
        xv = x_ref[...]
        r = lax.rsqrt(jnp.mean(xv * xv, axis=-1, keepdims=True) + EPS)
        h_ref[...] = (xv * r * (g_ref[...] * (1.0 + sc_ref[...])) + sh_ref[...]).astype(BF16)

    row = pl.BlockSpec((tb, D), lambda i: (i, 0))
    vec = _full((1, D))
    return pl.pallas_call(
        body, name="norm_mod", grid=(S // tb,), in_specs=[row, vec, vec, vec], out_specs=row,
        out_shape=jax.ShapeDtypeStruct((S, D), BF16), compiler_params=_cp(("parallel",)),
    )(x, g, scale, shift)


def _pool_mix(u_ref, uh_ref, ubuf, i, tb):
    H = POOL_HALO
    ubuf[H:, :] = u_ref[...].astype(F32)
    ubuf[:H, :] = jnp.where(i > 0, uh_ref[...].astype(F32), 0.0)
    t = i * tb + lax.broadcasted_iota(jnp.int32, (tb, 1), 0)
    mixed = []
    for g, w in enumerate(POOL_WINDOWS):
        cs = slice(g * POOL_GROUP, (g + 1) * POOL_GROUP)
        cur = ubuf[H:, cs]
        acc = cur
        for j in range(1, w):
            acc = acc + ubuf[pl.ds(H - j, tb), cs]
        cnt = jnp.minimum(t + 1, w).astype(F32)
        mixed.append(acc / cnt - cur)
    return mixed


def _pool_specs(tb):
    H = POOL_HALO
    u = _seg(tb, 1024, OFF_U)
    uh = _seg(H, 1024, OFF_U, lambda i: jnp.maximum(i * tb - H, 0))
    z = _seg(tb, 1024, OFF_Z)
    return u, uh, z


def _pool_fwd(proj, pool_w, pool_scale, tb=256):
    S = proj.shape[0]

    def body(u_ref, uh_ref, z_ref, w_ref, sc_ref, y_ref, ubuf):
        i = pl.program_id(0)
        mixed = _pool_mix(u_ref, uh_ref, ubuf, i, tb)
        m = jnp.concatenate([_dot(mixed[g].astype(BF16), w_ref[g]) for g in range(4)], axis=1)
        y_ref[...] = (m * sc_ref[...] * _silu(z_ref[...].astype(F32))).astype(BF16)

    u, uh, z = _pool_specs(tb)
    return pl.pallas_call(
        body, name="pool_fwd", grid=(S // tb,),
        in_specs=[u, uh, z, _full((4, 256, 256)), _full((1, 1024))],
        out_specs=pl.BlockSpec((tb, 1024), lambda i: (i, 0)),
        out_shape=jax.ShapeDtypeStruct((S, 1024), BF16),
        scratch_shapes=[pltpu.VMEM((tb + POOL_HALO, 1024), F32)],
        compiler_params=_cp(("parallel",)),
    )(proj, proj, proj, pool_w, pool_scale)


def _attn_mask(i):
    TQ, NK = ATTN_TQ, ATTN_TQ + ATTN_BACK
    qc = lax.broadcasted_iota(jnp.int32, (TQ, NK), 0) // 64
    col = lax.broadcasted_iota(jnp.int32, (TQ, NK), 1)
    kc = col // 64
    return (kc >= qc) & (kc <= qc + 2) & ((col >= ATTN_BACK) | (i > 0))


def _block_diag(kbuf, vbuf, kbd, vbd, kh):
    NK = ATTN_TQ + ATTN_BACK
    ks = slice(kh * HEAD_DIM, (kh + 1) * HEAD_DIM)
    for g in range(N_HEADS // N_KV):
        kbd[g * NK:(g + 1) * NK, g * HEAD_DIM:(g + 1) * HEAD_DIM] = kbuf[:, ks]
        vbd[g * NK:(g + 1) * NK, g * HEAD_DIM:(g + 1) * HEAD_DIM] = vbuf[:, ks]


def _attn_probs(q_ref, kw, sink_ref, valid, h):
    qh = q_ref[:, h * HEAD_DIM:(h + 1) * HEAD_DIM]
    return _softmax_sink(_dot_nt(qh, kw) * (HEAD_DIM ** -0.5), sink_ref, valid, h)


def _softmax_sink(s, sink_ref, valid, h):
    s = jnp.where(valid, s, -jnp.inf)
    sk = sink_ref[h:h + 1, 0:1]
    mx = jnp.maximum(jnp.max(s, axis=-1, keepdims=True), sk)
    p = jnp.exp(s - mx)
    es = jnp.exp(sk - mx)
    den = jnp.sum(p, axis=-1, keepdims=True) + es
    return p / den, es / den


def _kv_specs():
    TQ, B = ATTN_TQ, ATTN_BACK
    blk = lambda off: _seg(TQ, 256, off)
    halo = lambda off: _seg(B, 256, off, lambda i: jnp.maximum(i * TQ - B, 0))
    return [blk(OFF_K), halo(OFF_K), blk(OFF_V), halo(OFF_V)]


def _kv_window(k_ref, kh_ref, v_ref, vh_ref, kbuf, vbuf, i):
    B = ATTN_BACK
    for buf, ref, href in ((kbuf, k_ref, kh_ref), (vbuf, v_ref, vh_ref)):
        buf[:B, :] = jnp.where(i > 0, href[...], jnp.zeros_like(href))
        buf[B:, :] = ref[...]


def _attn_fwd(proj, sink_b):
    S = proj.shape[0]
    TQ, NK = ATTN_TQ, ATTN_TQ + ATTN_BACK

    G = N_HEADS // N_KV

    def body(q_ref, z_ref, k_ref, kh_ref, v_ref, vh_ref, sink_ref, o_ref, y_ref, kbuf, vbuf, kbd, vbd):
        i = pl.program_id(0)
        _kv_window(k_ref, kh_ref, v_ref, vh_ref, kbuf, vbuf, i)
        valid = _attn_mask(i)
        kbd[...] = jnp.zeros_like(kbd)
        vbd[...] = jnp.zeros_like(vbd)
        for kh in range(N_KV):
            _block_diag(kbuf, vbuf, kbd, vbd, kh)
            gs = slice(kh * G * HEAD_DIM, (kh + 1) * G * HEAD_DIM)
            s_all = _dot_nt(q_ref[:, gs], kbd[...]) * (HEAD_DIM ** -0.5)
            pn = [_softmax_sink(s_all[:, g * NK:(g + 1) * NK], sink_ref, valid, kh * G + g)[0].astype(BF16) for g in range(G)]
            o = _dot(jnp.concatenate(pn, axis=1), vbd[...])
            o_ref[:, gs] = o.astype(BF16)
            y_ref[:, gs] = (o * _silu(z_ref[:, gs].astype(F32))).astype(BF16)

    out = pl.BlockSpec((TQ, 1024), lambda i: (i, 0))
    return pl.pallas_call(
        body, name="attn_fwd", grid=(S // TQ,),
        in_specs=[_seg(TQ, 1024, OFF_Q), _seg(TQ, 1024, OFF_AZ), *_kv_specs(), _full((N_HEADS, LANE))],
        out_specs=[out, out], out_shape=[jax.ShapeDtypeStruct((S, 1024), BF16)] * 2,
        scratch_shapes=[pltpu.VMEM((NK, 256), BF16)] * 2 + [pltpu.VMEM((G * NK, G * HEAD_DIM), BF16)] * 2,
        compiler_params=_cp(("parallel",)),
    )(proj, proj, proj, proj, proj, proj, sink_b)


def _conv_specs(tb):
    H = CONV_HALO
    prev = lambda i: jnp.maximum(i * tb - H, 0)
    a = _seg(tb, 1024, OFF_CA)
    ah = _seg(H, 1024, OFF_CA, prev)
    b = _seg(tb, 1024, OFF_CB)
    bh = _seg(H, 1024, OFF_CB, prev)
    z = _seg(tb, 1024, OFF_CZ)
    return a, ah, b, bh, z


SUBLANES = 8
CONV_TAPS = tuple((j, CONV_HALO - (CONV_K - 1) + j) for j in range(CONV_K))
CONV_TAPS_T = tuple((j, (CONV_K - 1) - j) for j in range(CONV_K))


def _lane_chunks(fn):
    def body(c, carry):
        fn(pl.ds(pl.multiple_of(c * LANE, LANE), LANE))
        return carry

    lax.fori_loop(0, 1024 // LANE, body, 0)


def _shifted_tiles(src_ref, cs, s, n):
    row = lax.broadcasted_iota(jnp.int32, (SUBLANES, LANE), 0)
    prev = None
    for t in range(n + (1 if s else 0)):
        v = src_ref[pl.ds(SUBLANES * t, SUBLANES), cs]
        if s == 0:
            yield t, v
            continue
        x = pltpu.roll(v, SUBLANES - s, 0)
        if prev is not None:
            yield t - 1, jnp.where(row < SUBLANES - s, prev, x)
        prev = x


def _by_shift(taps):
    groups = {}
    for j, o in taps:
        groups.setdefault(o % SUBLANES, []).append((j, o // SUBLANES))
    return sorted(groups.items())


def _taps_apply(src_ref, w_ref, dst_ref, tb, taps, cs):
    nu = tb // SUBLANES
    acc = [None] * nu
    for s, group in _by_shift(taps):
        w = {j: w_ref[j:j + 1, cs] for j, _ in group}
        for t, g in _shifted_tiles(src_ref, cs, s, nu + max(a for _, a in group)):
            for j, a in group:
                if 0 <= t - a < nu:
                    term = w[j] * g
                    acc[t - a] = term if acc[t - a] is None else acc[t - a] + term
    dst_ref[:, cs] = jnp.concatenate(acc, axis=0)


def _taps_reduce(src_ref, d_ref, out_ref, tb, taps, cs):
    nu = tb // SUBLANES
    d = [d_ref[pl.ds(SUBLANES * u, SUBLANES), cs] for u in range(nu)]
    for s, group in _by_shift(taps):
        part = {j: None for j, _ in group}
        for t, g in _shifted_tiles(src_ref, cs, s, nu + max(a for _, a in group)):
            for j, a in group:
                if 0 <= t - a < nu:
                    term = d[t - a] * g
                    part[j] = term if part[j] is None else part[j] + term
        for j, _ in group:
            out_ref[j:j + 1, cs] += jnp.sum(part[j], axis=0, keepdims=True)


def _conv_glu_dw(a_ref, ah_ref, b_ref, bh_ref, dw_ref, gbuf, ybuf, i, tb):
    H = CONV_HALO
    gbuf[H:, :] = a_ref[...].astype(F32) * _sig(b_ref[...].astype(F32))
    gh = ah_ref[...].astype(F32) * _sig(bh_ref[...].astype(F32))
    gbuf[:H, :] = jnp.where(i > 0, gh, 0.0)
    _lane_chunks(lambda cs: _taps_apply(gbuf, dw_ref, ybuf, tb, CONV_TAPS, cs))


def _layer_norm_fwd(y, g, b):
    mu = jnp.mean(y, axis=-1, keepdims=True)
    yc = y - mu
    rstd = lax.rsqrt(jnp.mean(yc * yc, axis=-1, keepdims=True) + EPS)
    xh = yc * rstd
    return xh, rstd, xh * g + b


def _conv_fwd(proj, dw, dw_b, ln_g, ln_b, pw, tb=256):
    S = proj.shape[0]

    def body(a_ref, ah_ref, b_ref, bh_ref, z_ref, dw_ref, dwb_ref, lg_ref, lb_ref, pw_ref, y_ref, gbuf, ybuf):
        i = pl.program_id(0)
        _conv_glu_dw(a_ref, ah_ref, b_ref, bh_ref, dw_ref, gbuf, ybuf, i, tb)
        _, _, yn = _layer_norm_fwd(ybuf[...] + dwb_ref[...], lg_ref[...], lb_ref[...])
        out = _dot(_silu(yn).astype(BF16), pw_ref[...])
        y_ref[...] = (out * _silu(z_ref[...].astype(F32))).astype(BF16)

    vec = _full((1, 1024))
    return pl.pallas_call(
        body, name="conv_fwd", grid=(S // tb,),
        in_specs=[*_conv_specs(tb), _full((32, 1024)), vec, vec, vec, _full((1024, 1024))],
        out_specs=pl.BlockSpec((tb, 1024), lambda i: (i, 0)),
        out_shape=jax.ShapeDtypeStruct((S, 1024), BF16),
        scratch_shapes=[pltpu.VMEM((tb + CONV_HALO, 1024), F32), pltpu.VMEM((tb, 1024), F32)],
        compiler_params=_cp(("parallel",)),
    )(proj, proj, proj, proj, proj, dw, dw_b, ln_g, ln_b, pw)


def _merge_fwd(yp, ya, yc, wbp, wba, wbc, proj, tm=512, tn=1024):
    S = yp.shape[0]

    def body(yp_ref, ya_ref, yc_ref, wp_ref, wa_ref, wc_ref, gp_ref, ga_ref, gc_ref, m_ref, pp_ref, pa_ref, pc_ref):
        pp = _dot(yp_ref[...], wp_ref[...])
        pa = _dot(ya_ref[...], wa_ref[...])
        pc = _dot(yc_ref[...], wc_ref[...])
        m = (_sig(gp_ref[...].astype(F32)) * pp + _sig(ga_ref[...].astype(F32)) * pa
             + _sig(gc_ref[...].astype(F32)) * pc)
        m_ref[...] = m.astype(BF16)
        pp_ref[...], pa_ref[...], pc_ref[...] = pp.astype(BF16), pa.astype(BF16), pc.astype(BF16)

    yb = pl.BlockSpec((tm, 1024), lambda j, i: (i, 0))
    wb = pl.BlockSpec((1024, tn), lambda j, i: (0, j))
    out = pl.BlockSpec((tm, tn), lambda j, i: (i, j))
    return pl.pallas_call(
        body, name="merge_fwd", grid=(D // tn, S // tm),
        in_specs=[yb, yb, yb, wb, wb, wb, _seg2(tm, tn, OFF_GP), _seg2(tm, tn, OFF_GA), _seg2(tm, tn, OFF_GC)],
        out_specs=[out] * 4, out_shape=[jax.ShapeDtypeStruct((S, D), BF16)] * 4,
        compiler_params=_cp(("parallel", "parallel")),
    )(yp, ya, yc, wbp, wba, wbc, proj, proj, proj)


def _out_fwd(x, merged, w_out, gate, tm=512, tn=1024):
    S = x.shape[0]

    def body(x_ref, m_ref, w_ref, g_ref, o_ref):
        o_ref[...] = x_ref[...] + g_ref[...] * _dot(m_ref[...], w_ref[...])

    xb = pl.BlockSpec((tm, tn), lambda j, i: (i, j))
    return pl.pallas_call(
        body, name="out_fwd", grid=(D // tn, S // tm),
        in_specs=[xb, pl.BlockSpec((tm, D), lambda j, i: (i, 0)), pl.BlockSpec((D, tn), lambda j, i: (0, j)),
                  pl.BlockSpec((1, tn), lambda j, i: (0, j))],
        out_specs=xb, out_shape=jax.ShapeDtypeStruct((S, D), F32),
        compiler_params=_cp(("parallel", "parallel")),
    )(x, merged, w_out, gate)


def _final_loss(x, target, final_g, tb=256):
    S = x.shape[0]

    def body(x_ref, t_ref, g_ref, dx_ref, gg_ref, ls_ref):
        i = pl.program_id(0)
        xv, g = x_ref[...], g_ref[...]
        r = lax.rsqrt(jnp.mean(xv * xv, axis=-1, keepdims=True) + EPS)
        xh = xv * r
        e = xh * g - t_ref[...]
        dy = e * (1.0 / D)
        gy = dy * g
        dx_ref[...] = r * (gy - xh * jnp.mean(gy * xh, axis=-1, keepdims=True))
        gg = jnp.sum(dy * xh, axis=0, keepdims=True)
        ls = jnp.sum(e * e, axis=0, keepdims=True) * (0.5 / D)

        @pl.when(i == 0)
        def _():
            gg_ref[...], ls_ref[...] = gg, ls

        @pl.when(i > 0)
        def _():
            gg_ref[...] += gg
            ls_ref[...] += ls

    row = pl.BlockSpec((tb, D), lambda i: (i, 0))
    vec = _full((1, D))
    return pl.pallas_call(
        body, name="final_loss", grid=(S // tb,), in_specs=[row, row, vec], out_specs=[row, vec, vec],
        out_shape=[jax.ShapeDtypeStruct((S, D), F32), jax.ShapeDtypeStruct((1, D), F32),
                   jax.ShapeDtypeStruct((1, D), F32)],
        compiler_params=_cp(("arbitrary",)),
    )(x, target, final_g)


def _out_bwd(dx, gate, w_out, pp, pa, pc, proj, tm=512, tn=1024):
    S = dx.shape[0]

    def body(dx_ref, g_ref, w_ref, pp_ref, pa_ref, pc_ref, gp_ref, ga_ref, gc_ref,
             dpp_ref, dpa_ref, dpc_ref, dgp_ref, dga_ref, dgc_ref):
        dm = _dot_nt((dx_ref[...] * g_ref[...]).astype(BF16), w_ref[...])
        for p_ref, gl_ref, dp_ref, dg_ref in ((pp_ref, gp_ref, dpp_ref, dgp_ref), (pa_ref, ga_ref, dpa_ref, dga_ref),
                                              (pc_ref, gc_ref, dpc_ref, dgc_ref)):
            s = _sig(gl_ref[...].astype(F32))
            dp_ref[...] = (dm * s).astype(BF16)
            dg_ref[...] = (dm * p_ref[...].astype(F32) * s * (1.0 - s)).astype(BF16)

    out = pl.BlockSpec((tm, tn), lambda j, i: (i, j))
    return pl.pallas_call(
        body, name="out_bwd", grid=(D // tn, S // tm),
        in_specs=[pl.BlockSpec((tm, D), lambda j, i: (i, 0)), _full((1, D)), pl.BlockSpec((tn, D), lambda j, i: (j, 0)),
                  out, out, out, _seg2(tm, tn, OFF_GP), _seg2(tm, tn, OFF_GA), _seg2(tm, tn, OFF_GC)],
        out_specs=[out] * 6, out_shape=[jax.ShapeDtypeStruct((S, D), BF16)] * 6,
        compiler_params=_cp(("parallel", "parallel")),
    )(dx, gate, w_out, pp, pa, pc, proj, proj, proj)


def _wout_post(gmat, w_out, gate, tr=256):
    def body(g_ref, w_ref, gate_ref, dw_ref, dg_ref):
        i = pl.program_id(0)
        gm = g_ref[...]
        dw_ref[...] = gm * gate_ref[...]
        part = jnp.sum(gm * w_ref[...].astype(F32), axis=0, keepdims=True)

        @pl.when(i == 0)
        def _():
            dg_ref[...] = part

        @pl.when(i > 0)
        def _():
            dg_ref[...] += part

    row = pl.BlockSpec((tr, D), lambda i: (i, 0))
    return pl.pallas_call(
        body, name="wout_post", grid=(D // tr,), in_specs=[row, row, _full((1, D))], out_specs=[row, _full((1, D))],
        out_shape=[jax.ShapeDtypeStruct((D, D), F32), jax.ShapeDtypeStruct((1, D), F32)],
        compiler_params=_cp(("arbitrary",)),
    )(gmat, w_out, gate)


def _pool_bwd_a(dy, proj, pool_w, pool_scale, dproj, tb=256):
    S = proj.shape[0]

    def body(dy_ref, u_ref, uh_ref, z_ref, w_ref, sc_ref, _, dmix_ref, dz_ref, dsc_ref, dw_ref, ubuf):
        i = pl.program_id(0)
        mixed = [m.astype(BF16) for m in _pool_mix(u_ref, uh_ref, ubuf, i, tb)]
        m = jnp.concatenate([_dot(mixed[g], w_ref[g]) for g in range(4)], axis=1)
        dyv, z, sc = dy_ref[...].astype(F32), z_ref[...].astype(F32), sc_ref[...]
        dyp = dyv * _silu(z)
        dz_ref[...] = (dyv * (m * sc) * _dsilu(z)).astype(BF16)
        dsc = jnp.sum(dyp * m, axis=0, keepdims=True)
        dmm = (dyp * sc).astype(BF16)
        dws = []
        for g in range(4):
            cs = slice(g * POOL_GROUP, (g + 1) * POOL_GROUP)
            dmix_ref[:, cs] = _dot_nt(dmm[:, cs], w_ref[g])
            dws.append(_dot_tn(mixed[g], dmm[:, cs]))

        @pl.when(i == 0)
        def _():
            dsc_ref[...] = dsc
            for g in range(4):
                dw_ref[g] = dws[g]

        @pl.when(i > 0)
        def _():
            dsc_ref[...] += dsc
            for g in range(4):
                dw_ref[g] += dws[g]

    u, uh, z = _pool_specs(tb)
    row = pl.BlockSpec((tb, 1024), lambda i: (i, 0))
    wfull = _full((4, 256, 256))
    return pl.pallas_call(
        body, name="pool_bwd_a", grid=(S // tb,),
        in_specs=[row, u, uh, z, wfull, _full((1, 1024)), ANY],
        out_specs=[row, _seg(tb, 1024, OFF_Z), _full((1, 1024)), wfull],
        out_shape=[jax.ShapeDtypeStruct((S, 1024), F32), jax.ShapeDtypeStruct(dproj.shape, BF16),
                   jax.ShapeDtypeStruct((1, 1024), F32), jax.ShapeDtypeStruct((4, 256, 256), F32)],
        scratch_shapes=[pltpu.VMEM((tb + POOL_HALO, 1024), F32)], input_output_aliases={6: 1},
        compiler_params=_cp(("arbitrary",)),
    )(dy, proj, proj, proj, pool_w, pool_scale, dproj)


def _pool_bwd_b(dmix, dproj, tb=256):
    S = dmix.shape[0]
    H = POOL_HALO
    nb = S // tb

    def body(dm_ref, dh_ref, _, du_ref, ebuf):
        i = pl.program_id(0)
        t = i * tb + lax.broadcasted_iota(jnp.int32, (tb, 1), 0)
        th = (i + 1) * tb + lax.broadcasted_iota(jnp.int32, (H, 1), 0)
        for g, w in enumerate(POOL_WINDOWS):
            cs = slice(g * POOL_GROUP, (g + 1) * POOL_GROUP)
            ebuf[:tb, cs] = dm_ref[:, cs] / jnp.minimum(t + 1, w).astype(F32)
            eh = dh_ref[:, cs] / jnp.minimum(th + 1, w).astype(F32)
            ebuf[tb:, cs] = jnp.where(i < nb - 1, eh, 0.0)
        for g, w in enumerate(POOL_WINDOWS):
            cs = slice(g * POOL_GROUP, (g + 1) * POOL_GROUP)
            acc = ebuf[:tb, cs]
            for j in range(1, w):
                acc = acc + ebuf[pl.ds(j, tb), cs]
            du_ref[:, cs] = (acc - dm_ref[:, cs]).astype(BF16)

    row = pl.BlockSpec((tb, 1024), lambda i: (i, 0))
    nxt = pl.BlockSpec((H, 1024), lambda i: (jnp.minimum((i + 1) * (tb // H), S // H - 1), 0))
    return pl.pallas_call(
        body, name="pool_bwd_b", grid=(nb,), in_specs=[row, nxt, ANY], out_specs=_seg(tb, 1024, OFF_U),
        out_shape=jax.ShapeDtypeStruct(dproj.shape, BF16),
        scratch_shapes=[pltpu.VMEM((tb + H, 1024), F32)], input_output_aliases={2: 0},
        compiler_params=_cp(("parallel",)),
    )(dmix, dmix, dproj)


def _attn_bwd(dy, o, proj, sink_b, dproj):
    S = proj.shape[0]
    TQ, NK = ATTN_TQ, ATTN_TQ + ATTN_BACK
    nb = S // TQ
    G = N_HEADS // N_KV

    QZ = OFF_AZ + 1024 - OFF_Q

    def body(dy_ref, o_ref, q_ref, z_ref, k_ref, kh_ref, v_ref, vh_ref, sink_ref, _, dqz_ref, dk_hbm, dv_hbm, ds_ref,
             dk_acc, dv_acc, kbuf, vbuf):
        i = pl.program_id(0)
        dq_ref, dz_ref = dqz_ref.at[:, :1024], dqz_ref.at[:, OFF_AZ - OFF_Q:]
        dqz_ref[:, 1024:OFF_AZ - OFF_Q] = jnp.zeros((TQ, OFF_AZ - OFF_Q - 1024), BF16)
        _kv_window(k_ref, kh_ref, v_ref, vh_ref, kbuf, vbuf, i)

        @pl.when(i == 0)
        def _():
            dk_acc[...] = jnp.zeros_like(dk_acc)
            dv_acc[...] = jnp.zeros_like(dv_acc)
            ds_ref[...] = jnp.zeros_like(ds_ref)

        start = pl.multiple_of(i * TQ, TQ)
        valid = _attn_mask(i)
        dks, dvs = [], []
        for kh in range(N_KV):
            ks = slice(kh * HEAD_DIM, (kh + 1) * HEAD_DIM)
            kw = kbuf[:, ks]
            vw = vbuf[:, ks]
            dk_sum = jnp.zeros((NK, HEAD_DIM), F32)
            dv_sum = jnp.zeros((NK, HEAD_DIM), F32)
            dqs = []
            for gi in range(G):
                h = kh * G + gi
                hs = slice(h * HEAD_DIM, (h + 1) * HEAD_DIM)
                pn, psink = _attn_probs(q_ref, kw, sink_ref, valid, h)
                ov = o_ref[:, hs].astype(F32)
                do = dy_ref[:, hs].astype(F32) * _silu(z_ref[:, hs].astype(F32))
                delta = jnp.sum(do * ov, axis=-1, keepdims=True)
                dob = do.astype(BF16)
                dp = _dot_nt(dob, vw)
                ds = (pn * (dp - delta)).astype(BF16)
                dsink = -jnp.sum(psink * delta, axis=0, keepdims=True)
                ds_ref[h:h + 1, :] += jnp.broadcast_to(dsink, (1, LANE))
                dqs.append(_dot(ds, kw) * (HEAD_DIM ** -0.5))
                dk_sum = dk_sum + _dot_tn(ds, q_ref[:, hs])
                dv_sum = dv_sum + _dot_tn(pn.astype(BF16), dob)
            gs = slice(kh * G * HEAD_DIM, (kh + 1) * G * HEAD_DIM)
            dq_ref[:, gs] = jnp.concatenate(dqs, axis=1).astype(BF16)
            z = z_ref[:, gs].astype(F32)
            dz_ref[:, gs] = (dy_ref[:, gs].astype(F32) * o_ref[:, gs].astype(F32) * _dsilu(z)).astype(BF16)
            dks.append(dk_sum * (HEAD_DIM ** -0.5))
            dvs.append(dv_sum)
        dk_acc[pl.ds(start, NK), :] += jnp.concatenate(dks, axis=1)
        dv_acc[pl.ds(start, NK), :] += jnp.concatenate(dvs, axis=1)

        @pl.when(i == nb - 1)
        def _():
            pltpu.sync_copy(dk_acc, dk_hbm)
            pltpu.sync_copy(dv_acc, dv_hbm)

    row = pl.BlockSpec((TQ, 1024), lambda i: (i, 0))
    return pl.pallas_call(
        body, name="attn_bwd", grid=(nb,),
        in_specs=[row, row, _seg(TQ, 1024, OFF_Q), _seg(TQ, 1024, OFF_AZ), *_kv_specs(), _full((N_HEADS, LANE)), ANY],
        out_specs=[_seg(TQ, QZ, OFF_Q), ANY, ANY, _full((N_HEADS, LANE))],
        out_shape=[jax.ShapeDtypeStruct(dproj.shape, BF16),
                   jax.ShapeDtypeStruct((S + ATTN_BACK, 256), F32), jax.ShapeDtypeStruct((S + ATTN_BACK, 256), F32),
                   jax.ShapeDtypeStruct((N_HEADS, LANE), F32)],
        scratch_shapes=[pltpu.VMEM((S + ATTN_BACK, 256), F32)] * 2 + [pltpu.VMEM((NK, 256), BF16)] * 2,
        input_output_aliases={9: 0}, compiler_params=_cp(("arbitrary",), VMEM_BIG),
    )(dy, o, proj, proj, proj, proj, proj, proj, sink_b, dproj)


def _conv_bwd_a(dy, proj, dw, dw_b, ln_g, ln_b, pw, dproj, tb=256):
    S = proj.shape[0]
    H = CONV_HALO

    def body(dy_ref, a_ref, ah_ref, b_ref, bh_ref, z_ref, dw_ref, dwb_ref, lg_ref, lb_ref, pw_ref, _,
             dcv_ref, dz_ref, dpw_ref, dlg_ref, dlb_ref, ddwb_ref, ddw_ref, gbuf, ybuf):
        i = pl.program_id(0)
        _conv_glu_dw(a_ref, ah_ref, b_ref, bh_ref, dw_ref, gbuf, ybuf, i, tb)
        lg = lg_ref[...]
        xh, rstd, yn = _layer_norm_fwd(ybuf[...] + dwb_ref[...], lg, lb_ref[...])
        u = _silu(yn).astype(BF16)
        out = _dot(u, pw_ref[...])
        dyv, z = dy_ref[...].astype(F32), z_ref[...].astype(F32)
        dz_ref[...] = (dyv * out * _dsilu(z)).astype(BF16)
        dout = (dyv * _silu(z)).astype(BF16)
        dpw = _dot_tn(u, dout)
        dyn = _dot_nt(dout, pw_ref[...]) * _dsilu(yn)
        dlg = jnp.sum(dyn * xh, axis=0, keepdims=True)
        dlb = jnp.sum(dyn, axis=0, keepdims=True)
        dxh = dyn * lg
        dcv = rstd * (dxh - jnp.mean(dxh, axis=-1, keepdims=True) - xh * jnp.mean(dxh * xh, axis=-1, keepdims=True))
        dcv_ref[...] = dcv
        ddwb = jnp.sum(dcv, axis=0, keepdims=True)
        ybuf[...] = dcv

        @pl.when(i == 0)
        def _():
            dpw_ref[...], dlg_ref[...], dlb_ref[...], ddwb_ref[...] = dpw, dlg, dlb, ddwb
            ddw_ref[...] = jnp.zeros_like(ddw_ref)

        @pl.when(i > 0)
        def _():
            dpw_ref[...] += dpw
            dlg_ref[...] += dlg
            dlb_ref[...] += dlb
            ddwb_ref[...] += ddwb

        _lane_chunks(lambda cs: _taps_reduce(gbuf, ybuf, ddw_ref, tb, CONV_TAPS, cs))

    vec = _full((1, 1024))
    row = pl.BlockSpec((tb, 1024), lambda i: (i, 0))
    big = _full((1024, 1024))
    return pl.pallas_call(
        body, name="conv_bwd_a", grid=(S // tb,),
        in_specs=[row, *_conv_specs(tb), _full((32, 1024)), vec, vec, vec, big, ANY],
        out_specs=[row, _seg(tb, 1024, OFF_CZ), big, vec, vec, vec, _full((32, 1024))],
        out_shape=[jax.ShapeDtypeStruct((S, 1024), F32), jax.ShapeDtypeStruct(dproj.shape, BF16),
                   jax.ShapeDtypeStruct((1024, 1024), F32), jax.ShapeDtypeStruct((1, 1024), F32),
                   jax.ShapeDtypeStruct((1, 1024), F32), jax.ShapeDtypeStruct((1, 1024), F32),
                   jax.ShapeDtypeStruct((32, 1024), F32)],
        scratch_shapes=[pltpu.VMEM((tb + H, 1024), F32), pltpu.VMEM((tb, 1024), F32)],
        input_output_aliases={11: 1}, compiler_params=_cp(("arbitrary",)),
    )(dy, proj, proj, proj, proj, proj, dw, dw_b, ln_g, ln_b, pw, dproj)


def _conv_bwd_b(dcv, proj, dw, dproj, tb=256):
    S = proj.shape[0]
    H = CONV_HALO
    nb = S // tb

    def body(d_ref, dn_ref, a_ref, b_ref, dw_ref, _, dab_ref, dbuf, gbuf):
        i = pl.program_id(0)
        dbuf[:tb, :] = d_ref[...]
        dbuf[tb:, :] = jnp.where(i < nb - 1, dn_ref[...], 0.0)
        _lane_chunks(lambda cs: _taps_apply(dbuf, dw_ref, gbuf, tb, CONV_TAPS_T, cs))
        dg = gbuf[...]
        a, s = a_ref[...].astype(F32), _sig(b_ref[...].astype(F32))
        dab_ref[:, :1024] = (dg * s).astype(BF16)
        dab_ref[:, 1024:] = (dg * a * s * (1.0 - s)).astype(BF16)

    row = pl.BlockSpec((tb, 1024), lambda i: (i, 0))
    nxt = pl.BlockSpec((H, 1024), lambda i: (jnp.minimum((i + 1) * (tb // H), S // H - 1), 0))
    return pl.pallas_call(
        body, name="conv_bwd_b", grid=(nb,),
        in_specs=[row, nxt, _seg(tb, 1024, OFF_CA), _seg(tb, 1024, OFF_CB), _full((32, 1024)), ANY],
        out_specs=_seg(tb, 2048, OFF_CA), out_shape=jax.ShapeDtypeStruct(dproj.shape, BF16),
        scratch_shapes=[pltpu.VMEM((tb + H, 1024), F32), pltpu.VMEM((tb, 1024), F32)],
        input_output_aliases={5: 0}, compiler_params=_cp(("parallel",)),
    )(dcv, dcv, proj, proj, dw, dproj)


def _norm_bwd(dh, x, dx_out, g, scale, tb=256):
    S = x.shape[0]

    def body(dh_ref, x_ref, dxo_ref, g_ref, sc_ref, dx_ref, dsh_ref, da_ref):
        i = pl.program_id(0)
        xv, dhv = x_ref[...], dh_ref[...]
        r = lax.rsqrt(jnp.mean(xv * xv, axis=-1, keepdims=True) + EPS)
        xh = xv * r
        gy = dhv * (g_ref[...] * (1.0 + sc_ref[...]))
        dx_ref[...] = dxo_ref[...] + r * (gy - xh * jnp.mean(gy * xh, axis=-1, keepdims=True))
        dsh = jnp.sum(dhv, axis=0, keepdims=True)
        da = jnp.sum(dhv * xh, axis=0, keepdims=True)

        @pl.when(i == 0)
        def _():
            dsh_ref[...], da_ref[...] = dsh, da

        @pl.when(i > 0)
        def _():
            dsh_ref[...] += dsh
            da_ref[...] += da

    row = pl.BlockSpec((tb, D), lambda i: (i, 0))
    vec = _full((1, D))
    return pl.pallas_call(
        body, name="norm_bwd", grid=(S // tb,), in_specs=[row, row, row, vec, vec], out_specs=[row, vec, vec],
        out_shape=[jax.ShapeDtypeStruct((S, D), F32), jax.ShapeDtypeStruct((1, D), F32), jax.ShapeDtypeStruct((1, D), F32)],
        compiler_params=_cp(("arbitrary",)),
    )(dh, x, dx_out, g, scale)


def _mod_bwd(d_a, norm_g, scale):
    def body(da_ref, g_ref, sc_ref, dg_ref, dsc_ref):
        dg_ref[...] = da_ref[...] * (1.0 + sc_ref[...])
        dsc_ref[...] = da_ref[...] * g_ref[...]

    return pl.pallas_call(body, name="mod_bwd", out_shape=[jax.ShapeDtypeStruct(d_a.shape, F32)] * 2)(d_a, norm_g, scale)


def _reduce_adamw(parts, w, m, v, name):
    L, rows, C = w.shape
    tr = rows if rows % 64 else 64

    def body(*refs):
        p_refs, (w_ref, m_ref, v_ref, g_ref, dl_ref, m2_ref, v2_ref) = refs[:L * N_DEV], refs[L * N_DEV:]
        for l in range(L):
            g = p_refs[l * N_DEV][0].astype(F32)
            for k in range(1, N_DEV):
                g = g + p_refs[l * N_DEV + k][0].astype(F32)
            g_ref[l] = g
            dl_ref[l], m2_ref[l], v2_ref[l] = _adam_math(g, w_ref[l], m_ref[l], v_ref[l])

    slot = lambda k: pl.BlockSpec((1, tr, C), lambda i: (k, i, 0))
    blk = pl.BlockSpec((L, tr, C), lambda i: (0, i, 0))
    return pl.pallas_call(
        body, name=name, grid=(rows // tr,), in_specs=[slot(k) for _ in range(L) for k in range(N_DEV)] + [blk] * 3,
        out_specs=[blk] * 4, out_shape=[jax.ShapeDtypeStruct((L, rows, C), F32)] * 4,
        compiler_params=_cp(("parallel",)),
    )(*[p for p in parts for _ in range(N_DEV)], w, m, v)


def _small_final(parts, w, m, v):
    R = w.shape[0]

    def body(p_ref, w_ref, m_ref, v_ref, g_ref, dl_ref, m2_ref, v2_ref):
        g = p_ref[0]
        for k in range(1, N_DEV):
            g = g + p_ref[k]
        delta, m2, v2 = _adam_math(g, w_ref[...], m_ref[...], v_ref[...])
        g_ref[...], dl_ref[...], m2_ref[...], v2_ref[...] = g, delta, m2, v2

    return pl.pallas_call(body, name="small_final", out_shape=[jax.ShapeDtypeStruct((R, LANE), F32)] * 4)(parts, w, m, v)


def _layer_fwd(x, mod, small, W, more_w):
    shift, scale, gate = mod
    h = _norm_mod(x, small["norm_g"], scale, shift)
    proj = _mm(h, W["w_in_t"], BF16, 512, 1536, name="proj_mm", nt=True)
    W.update(more_w(proj))
    y_pool = _pool_fwd(proj, W["pool_w"], small["pool_scale"])
    o, y_attn = _attn_fwd(proj, small["sink_b"])
    y_conv = _conv_fwd(proj, W["conv_dw"], small["conv_dw_b"], small["conv_ln_g"], small["conv_ln_b"], W["conv_pw"])
    merged, pp, pa, pc = _merge_fwd(y_pool, y_attn, y_conv, W["wbp"], W["wba"], W["wbc"], proj)
    x_new = _out_fwd(x, merged, W["w_out"], gate)
    stash = dict(x=x, h=h, proj=proj, o=o, y_pool=y_pool, y_attn=y_attn, y_conv=y_conv,
                 merged=merged, pp=pp, pa=pa, pc=pc)
    return x_new, stash


def _layer_bwd(dx, st, mod, small, W, put):
    shift, scale, gate = mod
    proj = st["proj"]
    gmat = _mm_tn(st["merged"], dx, 1024, 1024, 2048, name="wout_tn", vmem=VMEM_BIG)
    d_w_out, d_gate = _wout_post(gmat, W["w_out"], gate)
    dpp, dpa, dpc, dgp, dga, dgc = _out_bwd(dx, gate, W["w_out"], st["pp"], st["pa"], st["pc"], proj)
    dy_pool = _mm(dpp, W["wbp"], BF16, 512, 1024, name="branch_bwd_mm", nt=True)
    dy_attn = _mm(dpa, W["wba"], BF16, 512, 1024, name="branch_bwd_mm", nt=True)
    dy_conv = _mm(dpc, W["wbc"], BF16, 512, 1024, name="branch_bwd_mm", nt=True)
    d_wbp = _mm_tn(st["y_pool"], dpp, 1024, 1024, 4096, name="branch_tn", vmem=VMEM_BIG)
    d_wba = _mm_tn(st["y_attn"], dpa, 1024, 1024, 4096, name="branch_tn", vmem=VMEM_BIG)
    d_wbc = _mm_tn(st["y_conv"], dpc, 1024, 1024, 4096, name="branch_tn", vmem=VMEM_BIG)

    dproj = lax.empty(proj.shape, BF16)
    dmix, dproj, d_pool_scale, d_pool_w = _pool_bwd_a(dy_pool, proj, W["pool_w"], small["pool_scale"], dproj)
    dproj = _pool_bwd_b(dmix, dproj)
    dproj, dk, dv, d_sink = _attn_bwd(dy_attn, st["o"], proj, small["sink_b"], dproj)
    dcv, dproj, d_pw, d_ln_g, d_ln_b, d_dw_b, d_dw = _conv_bwd_a(
        dy_conv, proj, W["conv_dw"], small["conv_dw_b"], small["conv_ln_g"], small["conv_ln_b"], W["conv_pw"], dproj)
    dproj = _conv_bwd_b(dcv, proj, W["conv_dw"], dproj)
    for piece, off in ((dk[ATTN_BACK:], OFF_K), (dv[ATTN_BACK:], OFF_V), (dgp, OFF_GP), (dga, OFF_GA), (dgc, OFF_GC)):
        dproj = lax.dynamic_update_slice(dproj, piece.astype(BF16), (0, off))

    tok = put(dict(pool_w=d_pool_w, conv_dw=d_dw[:CONV_K], conv_pw=d_pw, wbp=d_wbp, wba=d_wba, wbc=d_wbc, w_out=d_w_out))
    d_w_in_t = _mm_tn(dproj, st["h"], 768, 2048, 2048, name="win_tn", vmem=VMEM_BIG, after=tok, out_dtype=BF16)
    tok = put(dict(w_in=d_w_in_t))
    dh = _mm(dproj, W["w_in_t"], F32, 1024, 2048, 1536, name="dh_mm", vmem=VMEM_BIG, after=tok)
    dx_in, d_shift, d_a = _norm_bwd(dh, st["x"], dx, small["norm_g"], scale)
    sm = dict(d_a=d_a, d_shift=d_shift, d_gate=d_gate, pool_scale=d_pool_scale, attn_sink=d_sink[:, 0],
              conv_dw_b=d_dw_b, conv_ln_g=d_ln_g, conv_ln_b=d_ln_b)
    return dx_in, sm


def _local_step(x, target, mods, smalls, get_w, final_g, put_g, end_layer):
    stashes, Ws = [], []
    for l in range(DEPTH):
        w, more_w = get_w(l, x)
        Ws.append(w)
        x, st = _layer_fwd(x, mods[l], smalls[l], w, more_w)
        stashes.append(st)
    dx, d_final_g, loss_lanes = _final_loss(x, target, final_g)
    sms = [None] * DEPTH
    for l in reversed(range(DEPTH)):
        dx, sms[l] = _layer_bwd(dx, stashes[l], mods[l], smalls[l], Ws[l], functools.partial(put_g, l))
        end_layer(l, dx)
    return loss_lanes, dx, d_final_g, sms


BIG = ("w_in", "pool_w", "conv_pw", "wbp", "wba", "wbc", "w_out")
GRADS = BIG + ("conv_dw",)


def _full_w_in(g):
    return dict(w_in_t=g.reshape(IN_WIDTH, D))


def _full_weights(g, conv_dw):
    cols = lambda a: jnp.transpose(a, (1, 0, 2)).reshape(a.shape[1], -1)
    pool_w = jnp.transpose(g["pool_w"], (1, 0, 2, 3)).reshape(4, 256, 256)
    conv_pw = g["conv_pw"].reshape(1024, 1024)
    wbp, wba, wbc = cols(g["wbp"]), cols(g["wba"]), cols(g["wbc"])
    w_out = g["w_out"].reshape(D, D)
    conv_dw = jnp.pad(cols(conv_dw), ((0, 32 - CONV_K), (0, 0)))
    return dict(pool_w=pool_w, conv_pw=conv_pw, wbp=wbp, wba=wba, wbc=wbc, w_out=w_out, conv_dw=conv_dw)


def _pieces(name, g):
    if name == "w_in":
        return g.reshape(8, IN_WIDTH // 8, D)
    if name == "pool_w":
        return jnp.transpose(g.reshape(4, 8, 32, 256), (1, 0, 2, 3))
    if name == "conv_dw":
        return jnp.transpose(g.reshape(CONV_K, 8, 128), (1, 0, 2))
    if name == "conv_pw":
        return g.reshape(8, 128, 1024)
    if name in ("wbp", "wba", "wbc"):
        return jnp.transpose(g.reshape(1024, 8, 256), (1, 0, 2))
    return g.reshape(8, 256, D)


def _pack_small(items, rows):
    flat = jnp.concatenate([a.reshape(-1).astype(F32) for a in items])
    return jnp.pad(flat, (0, rows * LANE - flat.shape[0])).reshape(rows, LANE)


def _unpack_small(packed, shapes):
    flat, out, off = packed.reshape(-1), [], 0
    for s in shapes:
        n = 1
        for d in s:
            n *= d
        out.append(flat[off:off + n].reshape(s))
        off += n
    return out


def kernel(x, c, norm_g, w_ada, b_ada, w_in, pool_w, pool_scale, attn_sink, conv_dw, conv_dw_b, conv_ln_g, conv_ln_b, conv_pw, w_branch_pool, w_branch_attn, w_branch_conv, w_out, final_g, loss_target, m_norm_g, m_w_ada, m_b_ada, m_w_in, m_pool_w, m_pool_scale, m_attn_sink, m_conv_dw, m_conv_dw_b, m_conv_ln_g, m_conv_ln_b, m_conv_pw, m_w_branch_pool, m_w_branch_attn, m_w_branch_conv, m_w_out, m_final_g, v_norm_g, v_w_ada, v_b_ada, v_w_in, v_pool_w, v_pool_scale, v_attn_sink, v_conv_dw, v_conv_dw_b, v_conv_ln_g, v_conv_ln_b, v_conv_pw, v_w_branch_pool, v_w_branch_attn, v_w_branch_conv, v_w_out, v_final_g):
    L = DEPTH
    me = 4 * lax.axis_index("x") + 2 * lax.axis_index("y") + lax.axis_index("c")
    tr = lambda a: jnp.swapaxes(a, 1, 2)
    shards = dict(w_in=tr(w_in), pool_w=pool_w, conv_dw=conv_dw, conv_pw=conv_pw, wbp=w_branch_pool, wba=w_branch_attn,
                  wbc=w_branch_conv, w_out=w_out)
    moms = dict(w_in=(tr(m_w_in), tr(v_w_in)), pool_w=(m_pool_w, v_pool_w), conv_dw=(m_conv_dw, v_conv_dw),
                conv_pw=(m_conv_pw, v_conv_pw), wbp=(m_w_branch_pool, v_w_branch_pool),
                wba=(m_w_branch_attn, v_w_branch_attn), wbc=(m_w_branch_conv, v_w_branch_conv), w_out=(m_w_out, v_w_out))

    n_cd = L * CONV_K * 128
    first = _all_gather([_pack_small([c, conv_dw], 144)], "gather_c")[0].reshape(N_DEV, -1)
    c_all = first[:, :D]
    conv_dw_all = first[:, D:D + n_cd].reshape(N_DEV, L, CONV_K, 128)

    mod_part = _mod_fwd(c_all, w_ada)
    mod_all = _all_gather([mod_part.reshape(-1, LANE)], "gather_mod")[0].reshape(N_DEV, L, N_DEV, -1)
    mod = jnp.transpose(lax.dynamic_index_in_dim(mod_all, me, axis=2, keepdims=False), (1, 0, 2)).reshape(L, 3 * D)
    mod = mod + b_ada
    mods = [(mod[l:l + 1, :D], mod[l:l + 1, D:2 * D], mod[l:l + 1, 2 * D:]) for l in range(L)]

    w_in_0 = _all_gather([(shards["w_in"][0] + mod[0, 0] * 0.0).astype(BF16)], "gather_w_in_0")[0]
    gathers, tok = [], w_in_0[0, 0, 0].astype(F32) * 0.0
    for l in range(L):
        first = lambda a: (a + tok).astype(BF16)
        mine = [first(shards["w_in"][l])], [first(shards[k][l]) if k == BIG[1] else shards[k][l].astype(BF16) for k in BIG[1:]]
        started = [None if (l, n) == (0, 0) else _spread_start(v, False, f"gather_start_{l}_{n}") for n, v in enumerate(mine)]
        gathers.append((mine, started))
        tok = sum(st[-1][0, 0] for st in started if st is not None)
    mods[0] = (mods[0][0] + tok,) + mods[0][1:]

    sink_b = jnp.broadcast_to(attn_sink[:, :, None], (L, N_HEADS, LANE))
    smalls = [dict(norm_g=norm_g[l:l + 1], pool_scale=pool_scale[l:l + 1], sink_b=sink_b[l], conv_dw_b=conv_dw_b[l:l + 1],
                   conv_ln_g=conv_ln_g[l:l + 1], conv_ln_b=conv_ln_b[l:l + 1]) for l in range(L)]

    def with_mine(landed, mine):
        return lax.dynamic_update_slice(landed, mine, (me,) + (0,) * (landed.ndim - 1))

    def get_w(l, x_in):
        mine, started = gathers[l]
        if l == 0:
            w_in = w_in_0
        else:
            sent, landed = _spread_wait(started[0], x_in, False, f"gather_wait_{l}_0")
            w_in = with_mine(landed[0], sent[0][None])

        def more_w(proj):
            sent, landed = _spread_wait(started[1], proj, False, f"gather_wait_{l}_1")
            g = {k: with_mine(a, b[None]) for k, a, b in zip(BIG[1:], landed, sent)}
            return _full_weights(g, conv_dw_all[:, l])

        return _full_w_in(w_in), more_w

    pending, parts = {l: [] for l in range(L)}, [dict() for _ in range(L)]

    def put_g(l, grads):
        names = tuple(k for k in GRADS if k in grads)
        pieces = [_pieces(k, grads[k]).astype(BF16) for k in names]
        started = _spread_start(pieces, True, f"scatter_start_{l}_{len(pending[l])}")
        pending[l].append((names, started))
        return started[-1][0:1, 0:1]

    def finish(l, after):
        for n, (names, started) in enumerate(pending.pop(l)):
            sent, landed = _spread_wait(started, after, True, f"scatter_wait_{l}_{n}")
            for k, a, b in zip(names, landed, sent):
                parts[l][k] = with_mine(a, lax.dynamic_slice_in_dim(b, me, 1, axis=0))

    def end_layer(l, dx):
        if l + 1 in pending:
            finish(l + 1, dx)

    loss_lanes, grad_x, d_final_g, sms = _local_step(x[0], loss_target[0], mods, smalls, get_w, final_g.reshape(1, D),
                                                     put_g, end_layer)
    finish(0, grad_x)

    stack = lambda k: jnp.concatenate([sms[l][k].reshape(1, -1) for l in range(L)], axis=0)
    scale_all = jnp.concatenate([mods[l][1] for l in range(L)], axis=0)
    d_norm_g, d_scale = _mod_bwd(stack("d_a"), norm_g, scale_all)
    dmod = jnp.concatenate([stack("d_shift"), d_scale, stack("d_gate")], axis=1)
    small_names = ("norm_g", "b_ada", "pool_scale", "attn_sink", "conv_dw_b", "conv_ln_g", "conv_ln_b", "final_g")
    small_g = (d_norm_g, dmod, stack("pool_scale"), stack("attn_sink"), stack("conv_dw_b"), stack("conv_ln_g"),
               stack("conv_ln_b"), d_final_g.reshape(D))
    small_w = (norm_g, b_ada, pool_scale, attn_sink, conv_dw_b, conv_ln_g, conv_ln_b, final_g)
    small_m = (m_norm_g, m_b_ada, m_pool_scale, m_attn_sink, m_conv_dw_b, m_conv_ln_g, m_conv_ln_b, m_final_g)
    small_v = (v_norm_g, v_b_ada, v_pool_scale, v_attn_sink, v_conv_dw_b, v_conv_ln_g, v_conv_ln_b, v_final_g)
    shapes = [a.shape for a in small_w] + [(D,)]
    n_small = sum(a.size for a in small_w) + D
    R = -(-n_small // (8 * LANE)) * 8
    zero = jnp.zeros((D,), F32)
    small_parts = _all_gather([_pack_small(small_g + (loss_lanes,), R)], "gather_small")[0]
    sg, sd, sm2, sv2 = _small_final(small_parts, _pack_small(small_w + (zero,), R), _pack_small(small_m + (zero,), R),
                                    _pack_small(small_v + (zero + 1.0,), R))
    sg, sd, sm2, sv2 = (_unpack_small(a, shapes) for a in (sg, sd, sm2, sv2))
    loss = jnp.sum(sg[-1])
    res = {n: (sg[i], sd[i], sm2[i], sv2[i]) for i, n in enumerate(small_names)}

    off = norm_g.size
    dmod_all = small_parts.reshape(N_DEV, -1)[:, off:off + L * 3 * D].reshape(N_DEV, L, 3 * D)
    dmod_mine = jnp.transpose(lax.dynamic_slice_in_dim(dmod_all, me * (3 * D // N_DEV), 3 * D // N_DEV, axis=2), (1, 0, 2))
    res["w_ada"] = _wada_bwd(c_all.T, dmod_mine, w_ada, m_w_ada, v_w_ada)

    for k in GRADS:
        shp = shards[k].shape
        to3d = lambda a: a.reshape(L, -1, shp[-1])
        out = _reduce_adamw([parts[l][k].reshape(N_DEV, -1, shp[-1]) for l in range(L)], to3d(shards[k]),
                            to3d(moms[k][0]), to3d(moms[k][1]), "adamw_" + k)
        res[k] = tuple(a.reshape(shp) for a in out)
    res["w_in"] = tuple(tr(a) for a in res["w_in"])

    order = ("norm_g", "w_ada", "b_ada", "w_in", "pool_w", "pool_scale", "attn_sink", "conv_dw", "conv_dw_b", "conv_ln_g",
             "conv_ln_b", "conv_pw", "wbp", "wba", "wbc", "w_out", "final_g")
    outs = [loss, grad_x[None]]
    for j in range(4):
        outs += [res[n][j] for n in order]
    return tuple(outs)
```

```python
import functools

import jax
import jax.numpy as jnp
from jax import lax
from jax.experimental import pallas as pl
from jax.experimental.pallas import tpu as pltpu

F32, BF16 = jnp.float32, jnp.bfloat16
MESH = pl.DeviceIdType.MESH
ANY = pl.BlockSpec(memory_space=pl.ANY)

N_DEV = 8
D = 2048
DEPTH = 4
EPS = 1e-6
IN_WIDTH = 13824
POOL_WINDOWS = (2, 4, 8, 16)
POOL_GROUP = 256
POOL_HALO = 16
CONV_K = 31
CONV_HALO = 32
N_HEADS, N_KV, HEAD_DIM = 16, 4, 64
ATTN_TQ = 256
ATTN_BACK = 128
LANE = 128
VMEM_BIG = 56 * 1024 * 1024

OFF_U, OFF_Z, OFF_Q, OFF_K, OFF_V, OFF_AZ, OFF_CA, OFF_CB, OFF_CZ = 0, 1024, 2048, 3072, 3328, 3584, 4608, 5632, 6656
OFF_GP, OFF_GA, OFF_GC = 7680, 9728, 11776


def _seg(rows, width, off, first_row=None):
    start = (lambda i: i * rows) if first_row is None else first_row
    return pl.BlockSpec((pl.Element(rows), pl.Element(width)), lambda i: (pl.multiple_of(start(i), rows), off))


def _seg2(tm, tn, off):
    return pl.BlockSpec((pl.Element(tm), pl.Element(tn)), lambda j, i: (i * tm, pl.multiple_of(off + j * tn, LANE)))

ADAM_LR, ADAM_B1, ADAM_B2, ADAM_EPS, ADAM_WD, ADAM_STEP = 0.001, 0.9, 0.999, 1e-08, 0.01, 10


def _cp(sem=None, vmem=None):
    return pltpu.CompilerParams(dimension_semantics=sem, vmem_limit_bytes=vmem)


def _sig(x):
    return jax.nn.sigmoid(x)


def _silu(x):
    return x * _sig(x)


def _dsilu(x):
    s = _sig(x)
    return s * (1.0 + x * (1.0 - s))


def _dot(a, b):
    return jnp.dot(a, b, preferred_element_type=F32)


def _dot_tn(a, b):
    return lax.dot_general(a, b, (((0,), (0,)), ((), ())), preferred_element_type=F32)


def _dot_nt(a, b):
    return lax.dot_general(a, b, (((1,), (1,)), ((), ())), preferred_element_type=F32)


def _full(shape):
    n = len(shape)
    return pl.BlockSpec(shape, lambda *_: (0,) * n)


def _my_pos():
    return lax.axis_index("x"), lax.axis_index("y"), lax.axis_index("c")


def _all_gather(xs, name):
    n = len(xs)

    def body(*refs):
        x_refs, o_refs = refs[:n], refs[n:2 * n]
        send_sems, recv_sems, local_sems = refs[2 * n:]
        x, y, c = _my_pos()
        sibling = (x, y, 1 - c)
        chips = [(1 - x, y), (x, 1 - y), (1 - x, 1 - y)]
        me = 4 * x + 2 * y + c

        def slot(px, py, pc):
            return 4 * px + 2 * py + pc

        def copy(t, k, block, to, src=None):
            dst = o_refs[t].at[block]
            return pltpu.make_async_remote_copy(
                src_ref=dst if src is None else src, dst_ref=dst,
                send_sem=send_sems.at[t, k], recv_sem=recv_sems.at[t, k],
                device_id=to, device_id_type=MESH)

        mine = [pltpu.make_async_copy(x_refs[t], o_refs[t].at[me], local_sems.at[t]) for t in range(n)]
        for cp in mine:
            cp.start()
        first = []
        for t in range(n):
            first.append(copy(t, 0, me, sibling, src=x_refs[t]))
            for j, chip in enumerate(chips):
                first.append(copy(t, 1 + j, me, (*chip, c), src=x_refs[t]))
        for cp in first:
            cp.start()
        passed = []
        for j, chip in enumerate(chips):
            for t in range(n):
                copy(t, 1 + j, slot(*chip, c), (x, y, c)).wait_recv()
                fwd = copy(t, 4 + j, slot(*chip, c), sibling)
                fwd.start()
                passed.append(fwd)
        for t in range(n):
            copy(t, 0, slot(x, y, 1 - c), (x, y, c)).wait_recv()
            for j, chip in enumerate(chips):
                copy(t, 4 + j, slot(*chip, 1 - c), (x, y, c)).wait_recv()
        for cp in first + passed:
            cp.wait_send()
        for cp in mine:
            cp.wait()

    return pl.pallas_call(
        body, name=name,
        out_shape=[jax.ShapeDtypeStruct((N_DEV,) + a.shape, a.dtype) for a in xs],
        in_specs=[ANY] * n, out_specs=[ANY] * n,
        scratch_shapes=[pltpu.SemaphoreType.DMA((n, 7)), pltpu.SemaphoreType.DMA((n, 7)),
                        pltpu.SemaphoreType.DMA((n,))],
    )(*xs)


N_PEER = N_DEV - 1
HBM = pl.BlockSpec(memory_space=pltpu.HBM)
SEM = pl.BlockSpec(memory_space=pltpu.SEMAPHORE)
EFFECT = pltpu.SideEffectType.DATAFLOW_SIDE_EFFECTING


def _peer(k):
    x, y, c = _my_pos()
    flip = lambda v, bit: 1 - v if bit else v
    return flip(x, (k >> 2) & 1), flip(y, (k >> 1) & 1), flip(c, k & 1)


def _spread_copies(v_ref, land_ref, send_sems, recv_sems, per_peer):
    x, y, c = _my_pos()
    me = 4 * x + 2 * y + c
    copies = []
    for k in range(1, N_DEV):
        px, py, pc = _peer(k)
        src = v_ref.at[4 * px + 2 * py + pc] if per_peer else v_ref
        copies.append(pltpu.make_async_remote_copy(
            src_ref=src, dst_ref=land_ref.at[me], send_sem=send_sems[k - 1], recv_sem=recv_sems[k - 1],
            device_id=(px, py, pc), device_id_type=MESH))
    return copies


def _spread_start(vs, per_peer, name):
    n = len(vs)
    lands = [(N_DEV,) + (v.shape[1:] if per_peer else v.shape) for v in vs]
    n_sem = 2 * N_PEER * n

    def body(*refs):
        v_refs, land_refs, outs = refs[:n], refs[n:2 * n], refs[2 * n:]
        for t in range(n):
            sems = outs[2 * N_PEER * t:2 * N_PEER * (t + 1)]
            for cp in _spread_copies(v_refs[t], land_refs[t], sems[:N_PEER], sems[N_PEER:], per_peer):
                cp.start()
        token = outs[n_sem + 2 * n]
        token[...] = jnp.zeros_like(token)

    hbm = lambda a: pltpu.with_memory_space_constraint(a, pltpu.HBM)
    return pl.pallas_call(
        body, name=name,
        out_shape=((pltpu.SemaphoreType.DMA(()),) * n_sem + tuple(pltpu.HBM(v.shape, v.dtype) for v in vs)
                   + tuple(pltpu.HBM(s, v.dtype) for s, v in zip(lands, vs)) + (jax.ShapeDtypeStruct((8, LANE), F32),)),
        in_specs=(HBM,) * (2 * n), out_specs=(SEM,) * n_sem + (HBM,) * (2 * n) + (pl.BlockSpec(memory_space=pltpu.VMEM),),
        input_output_aliases={t: n_sem + t for t in range(2 * n)},
        compiler_params=pltpu.CompilerParams(has_side_effects=EFFECT),
    )(*[hbm(v) for v in vs], *[hbm(lax.empty(s, v.dtype)) for s, v in zip(lands, vs)])


def _spread_wait(started, after, per_peer, name):
    n = (len(started) - 1) // (2 * N_PEER + 2)
    n_sem = 2 * N_PEER * n
    sems, thru = started[:n_sem], started[n_sem:n_sem + 2 * n]

    def body(*refs):
        v_refs, land_refs, rest = refs[:n], refs[n:2 * n], refs[2 * n:]
        for t in range(n):
            s = rest[2 * N_PEER * t:2 * N_PEER * (t + 1)]
            for cp in _spread_copies(v_refs[t], land_refs[t], s[:N_PEER], s[N_PEER:], per_peer):
                cp.wait_send()
                cp.wait_recv()

    out = pl.pallas_call(
        body, name=name,
        out_shape=tuple(pltpu.HBM(a.shape, a.dtype) for a in thru),
        in_specs=(HBM,) * (2 * n) + (SEM,) * n_sem + (ANY,), out_specs=(HBM,) * (2 * n),
        input_output_aliases={t: t for t in range(2 * n)},
        compiler_params=pltpu.CompilerParams(has_side_effects=EFFECT),
    )(*thru, *sems, after)
    return out[:n], out[n:]


def _mm(a, b, out_dtype, tm, tn, tk=None, name="mm", vmem=None, after=None, nt=False):
    M, K = a.shape
    N = b.shape[0] if nt else b.shape[1]
    tk = K if tk is None else tk
    nk = K // tk
    assert M % tm == 0 and N % tn == 0 and K % tk == 0
    dep = () if after is None else (after,)

    def body(*refs):
        a_ref, b_ref, o_ref, *acc = refs[len(dep):]
        prod = (_dot_nt if nt else _dot)(a_ref[...].astype(BF16), b_ref[...])
        if nk == 1:
            o_ref[...] = prod.astype(out_dtype)
            return
        acc_ref = acc[0] if acc else o_ref
        k = pl.program_id(2)

        @pl.when(k == 0)
        def _():
            acc_ref[...] = prod

        @pl.when(k > 0)
        def _():
            acc_ref[...] += prod

        if acc:
            @pl.when(k == nk - 1)
            def _():
                o_ref[...] = acc_ref[...].astype(out_dtype)

    scratch = [pltpu.VMEM((tm, tn), F32)] if (nk > 1 and out_dtype != F32) else []
    b_spec = pl.BlockSpec((tn, tk), lambda j, i, k: (j, k)) if nt else pl.BlockSpec((tk, tn), lambda j, i, k: (k, j))
    return pl.pallas_call(
        body, name=name, grid=(N // tn, M // tm, nk),
        in_specs=[_full((1, 1))] * len(dep) + [pl.BlockSpec((tm, tk), lambda j, i, k: (i, k)), b_spec],
        out_specs=pl.BlockSpec((tm, tn), lambda j, i, k: (i, j)),
        out_shape=jax.ShapeDtypeStruct((M, N), out_dtype), scratch_shapes=scratch,
        compiler_params=_cp(("parallel", "parallel", "arbitrary"), vmem),
    )(*dep, a, b)


def _mm_tn(a, b, tm, tn, ts, name="mm_tn", vmem=None, after=None, out_dtype=F32):
    S, Ka = a.shape
    _, N = b.shape
    assert Ka % tm == 0 and N % tn == 0 and S % ts == 0
    dep = () if after is None else (after,)

    nk = S // ts

    def body(*refs):
        a_ref, b_ref, o_ref, *acc = refs[len(dep):]
        acc_ref = acc[0] if acc else o_ref
        prod = _dot_tn(a_ref[...].astype(BF16), b_ref[...].astype(BF16))
        k = pl.program_id(2)

        @pl.when(k == 0)
        def _():
            acc_ref[...] = prod

        @pl.when(k > 0)
        def _():
            acc_ref[...] += prod

        if acc:
            @pl.when(k == nk - 1)
            def _():
                o_ref[...] = acc_ref[...].astype(out_dtype)

    return pl.pallas_call(
        body, name=name, grid=(Ka // tm, N // tn, S // ts),
        in_specs=[_full((1, 1))] * len(dep) + [pl.BlockSpec((ts, tm), lambda i, j, k: (k, i)),
                                               pl.BlockSpec((ts, tn), lambda i, j, k: (k, j))],
        out_specs=pl.BlockSpec((tm, tn), lambda i, j, k: (i, j)),
        out_shape=jax.ShapeDtypeStruct((Ka, N), out_dtype),
        scratch_shapes=[] if out_dtype == F32 else [pltpu.VMEM((tm, tn), F32)],
        compiler_params=_cp(("parallel", "parallel", "arbitrary"), vmem),
    )(*dep, a, b)


def _mod_fwd(c_all, w_ada):
    L, _, n = w_ada.shape

    def body(c_ref, w_ref, o_ref):
        ca = _silu(c_ref[...])
        o_ref[0] = jnp.dot(ca, w_ref[0], preferred_element_type=F32, precision=lax.Precision.HIGHEST)

    return pl.pallas_call(
        body, name="mod_fwd", grid=(L,),
        in_specs=[_full((N_DEV, D)), pl.BlockSpec((1, D, n), lambda l: (l, 0, 0))],
        out_specs=pl.BlockSpec((1, N_DEV, n), lambda l: (l, 0, 0)),
        out_shape=jax.ShapeDtypeStruct((L, N_DEV, n), F32),
        compiler_params=_cp(("parallel",)),
    )(c_all, w_ada)


def _adam_math(g, w, m, v):
    m2 = ADAM_B1 * m + (1.0 - ADAM_B1) * g
    v2 = ADAM_B2 * v + (1.0 - ADAM_B2) * (g * g)
    m_hat = m2 / (1.0 - ADAM_B1 ** ADAM_STEP)
    v_hat = v2 / (1.0 - ADAM_B2 ** ADAM_STEP)
    delta = -ADAM_LR * (m_hat / (jnp.sqrt(v_hat) + ADAM_EPS) + ADAM_WD * w)
    return delta, m2, v2


def _wada_bwd(c_all_t, dmod, w, m, v, tr=256):
    L, _, n = w.shape

    def body(c_ref, d_ref, w_ref, m_ref, v_ref, g_ref, dl_ref, m2_ref, v2_ref):
        ca = _silu(c_ref[...])
        dm = d_ref[0]
        g = ca[:, 0:1] * dm[0:1, :]
        for b in range(1, N_DEV):
            g = g + ca[:, b:b + 1] * dm[b:b + 1, :]
        delta, m2, v2 = _adam_math(g, w_ref[0], m_ref[0], v_ref[0])
        g_ref[0], dl_ref[0], m2_ref[0], v2_ref[0] = g, delta, m2, v2

    blk = pl.BlockSpec((1, tr, n), lambda l, i: (l, i, 0))
    return pl.pallas_call(
        body, name="wada_bwd", grid=(L, D // tr),
        in_specs=[pl.BlockSpec((tr, N_DEV), lambda l, i: (i, 0)), pl.BlockSpec((1, N_DEV, n), lambda l, i: (l, 0, 0)),
                  blk, blk, blk],
        out_specs=[blk] * 4, out_shape=[jax.ShapeDtypeStruct(w.shape, F32)] * 4,
        compiler_params=_cp(("parallel", "parallel")),
    )(c_all_t, dmod, w, m, v)


def _norm_mod(x, g, scale, shift, tb=256):
    S = x.shape[0]

    def body(x_ref, g_ref, sc_ref, sh_ref, h_ref):
        xv = x_ref[...]
        r = lax.rsqrt(jnp.mean(xv * xv, axis=-1, keepdims=True) + EPS)
        h_ref[...] = (xv * r * (g_ref[...] * (1.0 + sc_ref[...])) + sh_ref[...]).astype(BF16)

    row = pl.BlockSpec((tb, D), lambda i: (i, 0))
    vec = _full((1, D))
    return pl.pallas_call(
        body, name="norm_mod", grid=(S // tb,), in_specs=[row, vec, vec, vec], out_specs=row,
        out_shape=jax.ShapeDtypeStruct((S, D), BF16), compiler_params=_cp(("parallel",)),
    )(x, g, scale, shift)


def _pool_mix(u_ref, uh_ref, ubuf, i, tb):
    H = POOL_HALO
    ubuf[H:, :] = u_ref[...].astype(F32)
    ubuf[:H, :] = jnp.where(i > 0, uh_ref[...].astype(F32), 0.0)
    t = i * tb + lax.broadcasted_iota(jnp.int32, (tb, 1), 0)
    mixed = []
    for g, w in enumerate(POOL_WINDOWS):
        cs = slice(g * POOL_GROUP, (g + 1) * POOL_GROUP)
        cur = ubuf[H:, cs]
        acc = cur
        for j in range(1, w):
            acc = acc + ubuf[pl.ds(H - j, tb), cs]
        cnt = jnp.minimum(t + 1, w).astype(F32)
        mixed.append(acc / cnt - cur)
    return mixed


def _pool_specs(tb):
    H = POOL_HALO
    u = _seg(tb, 1024, OFF_U)
    uh = _seg(H, 1024, OFF_U, lambda i: jnp.maximum(i * tb - H, 0))
    z = _seg(tb, 1024, OFF_Z)
    return u, uh, z


def _pool_fwd(proj, pool_w, pool_scale, tb=256):
    S = proj.shape[0]

    def body(u_ref, uh_ref, z_ref, w_ref, sc_ref, y_ref, ubuf):
        i = pl.program_id(0)
        mixed = _pool_mix(u_ref, uh_ref, ubuf, i, tb)
        m = jnp.concatenate([_dot(mixed[g].astype(BF16), w_ref[g]) for g in range(4)], axis=1)
        y_ref[...] = (m * sc_ref[...] * _silu(z_ref[...].astype(F32))).astype(BF16)

    u, uh, z = _pool_specs(tb)
    return pl.pallas_call(
        body, name="pool_fwd", grid=(S // tb,),
        in_specs=[u, uh, z, _full((4, 256, 256)), _full((1, 1024))],
        out_specs=pl.BlockSpec((tb, 1024), lambda i: (i, 0)),
        out_shape=jax.ShapeDtypeStruct((S, 1024), BF16),
        scratch_shapes=[pltpu.VMEM((tb + POOL_HALO, 1024), F32)],
        compiler_params=_cp(("parallel",)),
    )(proj, proj, proj, pool_w, pool_scale)


def _attn_mask(i):
    TQ, NK = ATTN_TQ, ATTN_TQ + ATTN_BACK
    qc = lax.broadcasted_iota(jnp.int32, (TQ, NK), 0) // 64
    col = lax.broadcasted_iota(jnp.int32, (TQ, NK), 1)
    kc = col // 64
    return (kc >= qc) & (kc <= qc + 2) & ((col >= ATTN_BACK) | (i > 0))


def _block_diag(kbuf, vbuf, kbd, vbd, kh):
    NK = ATTN_TQ + ATTN_BACK
    ks = slice(kh * HEAD_DIM, (kh + 1) * HEAD_DIM)
    for g in range(N_HEADS // N_KV):
        kbd[g * NK:(g + 1) * NK, g * HEAD_DIM:(g + 1) * HEAD_DIM] = kbuf[:, ks]
        vbd[g * NK:(g + 1) * NK, g * HEAD_DIM:(g + 1) * HEAD_DIM] = vbuf[:, ks]


def _attn_probs(q_ref, kw, sink_ref, valid, h):
    qh = q_ref[:, h * HEAD_DIM:(h + 1) * HEAD_DIM]
    return _softmax_sink(_dot_nt(qh, kw) * (HEAD_DIM ** -0.5), sink_ref, valid, h)


def _softmax_sink(s, sink_ref, valid, h):
    s = jnp.where(valid, s, -jnp.inf)
    sk = sink_ref[h:h + 1, 0:1]
    mx = jnp.maximum(jnp.max(s, axis=-1, keepdims=True), sk)
    p = jnp.exp(s - mx)
    es = jnp.exp(sk - mx)
    den = jnp.sum(p, axis=-1, keepdims=True) + es
    return p / den, es / den


def _kv_specs():
    TQ, B = ATTN_TQ, ATTN_BACK
    blk = lambda off: _seg(TQ, 256, off)
    halo = lambda off: _seg(B, 256, off, lambda i: jnp.maximum(i * TQ - B, 0))
    return [blk(OFF_K), halo(OFF_K), blk(OFF_V), halo(OFF_V)]


def _kv_window(k_ref, kh_ref, v_ref, vh_ref, kbuf, vbuf, i):
    B = ATTN_BACK
    for buf, ref, href in ((kbuf, k_ref, kh_ref), (vbuf, v_ref, vh_ref)):
        buf[:B, :] = jnp.where(i > 0, href[...], jnp.zeros_like(href))
        buf[B:, :] = ref[...]


def _attn_fwd(proj, sink_b):
    S = proj.shape[0]
    TQ, NK = ATTN_TQ, ATTN_TQ + ATTN_BACK

    G = N_HEADS // N_KV

    def body(q_ref, z_ref, k_ref, kh_ref, v_ref, vh_ref, sink_ref, o_ref, y_ref, kbuf, vbuf, kbd, vbd):
        i = pl.program_id(0)
        _kv_window(k_ref, kh_ref, v_ref, vh_ref, kbuf, vbuf, i)
        valid = _attn_mask(i)
        kbd[...] = jnp.zeros_like(kbd)
        vbd[...] = jnp.zeros_like(vbd)
        for kh in range(N_KV):
            _block_diag(kbuf, vbuf, kbd, vbd, kh)
            gs = slice(kh * G * HEAD_DIM, (kh + 1) * G * HEAD_DIM)
            s_all = _dot_nt(q_ref[:, gs], kbd[...]) * (HEAD_DIM ** -0.5)
            pn = [_softmax_sink(s_all[:, g * NK:(g + 1) * NK], sink_ref, valid, kh * G + g)[0].astype(BF16) for g in range(G)]
            o = _dot(jnp.concatenate(pn, axis=1), vbd[...])
            o_ref[:, gs] = o.astype(BF16)
            y_ref[:, gs] = (o * _silu(z_ref[:, gs].astype(F32))).astype(BF16)

    out = pl.BlockSpec((TQ, 1024), lambda i: (i, 0))
    return pl.pallas_call(
        body, name="attn_fwd", grid=(S // TQ,),
        in_specs=[_seg(TQ, 1024, OFF_Q), _seg(TQ, 1024, OFF_AZ), *_kv_specs(), _full((N_HEADS, LANE))],
        out_specs=[out, out], out_shape=[jax.ShapeDtypeStruct((S, 1024), BF16)] * 2,
        scratch_shapes=[pltpu.VMEM((NK, 256), BF16)] * 2 + [pltpu.VMEM((G * NK, G * HEAD_DIM), BF16)] * 2,
        compiler_params=_cp(("parallel",)),
    )(proj, proj, proj, proj, proj, proj, sink_b)


def _conv_specs(tb):
    H = CONV_HALO
    prev = lambda i: jnp.maximum(i * tb - H, 0)
    a = _seg(tb, 1024, OFF_CA)
    ah = _seg(H, 1024, OFF_CA, prev)
    b = _seg(tb, 1024, OFF_CB)
    bh = _seg(H, 1024, OFF_CB, prev)
    z = _seg(tb, 1024, OFF_CZ)
    return a, ah, b, bh, z


SUBLANES = 8
CONV_TAPS = tuple((j, CONV_HALO - (CONV_K - 1) + j) for j in range(CONV_K))
CONV_TAPS_T = tuple((j, (CONV_K - 1) - j) for j in range(CONV_K))


def _lane_chunks(fn):
    def body(c, carry):
        fn(pl.ds(pl.multiple_of(c * LANE, LANE), LANE))
        return carry

    lax.fori_loop(0, 1024 // LANE, body, 0)


def _shifted_tiles(src_ref, cs, s, n):
    row = lax.broadcasted_iota(jnp.int32, (SUBLANES, LANE), 0)
    prev = None
    for t in range(n + (1 if s else 0)):
        v = src_ref[pl.ds(SUBLANES * t, SUBLANES), cs]
        if s == 0:
            yield t, v
            continue
        x = pltpu.roll(v, SUBLANES - s, 0)
        if prev is not None:
            yield t - 1, jnp.where(row < SUBLANES - s, prev, x)
        prev = x


def _by_shift(taps):
    groups = {}
    for j, o in taps:
        groups.setdefault(o % SUBLANES, []).append((j, o // SUBLANES))
    return sorted(groups.items())


def _taps_apply(src_ref, w_ref, dst_ref, tb, taps, cs):
    nu = tb // SUBLANES
    acc = [None] * nu
    for s, group in _by_shift(taps):
        w = {j: w_ref[j:j + 1, cs] for j, _ in group}
        for t, g in _shifted_tiles(src_ref, cs, s, nu + max(a for _, a in group)):
            for j, a in group:
                if 0 <= t - a < nu:
                    term = w[j] * g
                    acc[t - a] = term if acc[t - a] is None else acc[t - a] + term
    dst_ref[:, cs] = jnp.concatenate(acc, axis=0)


def _taps_reduce(src_ref, d_ref, out_ref, tb, taps, cs):
    nu = tb // SUBLANES
    d = [d_ref[pl.ds(SUBLANES * u, SUBLANES), cs] for u in range(nu)]
    for s, group in _by_shift(taps):
        part = {j: None for j, _ in group}
        for t, g in _shifted_tiles(src_ref, cs, s, nu + max(a for _, a in group)):
            for j, a in group:
                if 0 <= t - a < nu:
                    term = d[t - a] * g
                    part[j] = term if part[j] is None else part[j] + term
        for j, _ in group:
            out_ref[j:j + 1, cs] += jnp.sum(part[j], axis=0, keepdims=True)


def _conv_glu_dw(a_ref, ah_ref, b_ref, bh_ref, dw_ref, gbuf, ybuf, i, tb):
    H = CONV_HALO
    gbuf[H:, :] = a_ref[...].astype(F32) * _sig(b_ref[...].astype(F32))
    gh = ah_ref[...].astype(F32) * _sig(bh_ref[...].astype(F32))
    gbuf[:H, :] = jnp.where(i > 0, gh, 0.0)
    _lane_chunks(lambda cs: _taps_apply(gbuf, dw_ref, ybuf, tb, CONV_TAPS, cs))


def _layer_norm_fwd(y, g, b):
    mu = jnp.mean(y, axis=-1, keepdims=True)
    yc = y - mu
    rstd = lax.rsqrt(jnp.mean(yc * yc, axis=-1, keepdims=True) + EPS)
    xh = yc * rstd
    return xh, rstd, xh * g + b


def _conv_fwd(proj, dw, dw_b, ln_g, ln_b, pw, tb=256):
    S = proj.shape[0]

    def body(a_ref, ah_ref, b_ref, bh_ref, z_ref, dw_ref, dwb_ref, lg_ref, lb_ref, pw_ref, y_ref, gbuf, ybuf):
        i = pl.program_id(0)
        _conv_glu_dw(a_ref, ah_ref, b_ref, bh_ref, dw_ref, gbuf, ybuf, i, tb)
        _, _, yn = _layer_norm_fwd(ybuf[...] + dwb_ref[...], lg_ref[...], lb_ref[...])
        out = _dot(_silu(yn).astype(BF16), pw_ref[...])
        y_ref[...] = (out * _silu(z_ref[...].astype(F32))).astype(BF16)

    vec = _full((1, 1024))
    return pl.pallas_call(
        body, name="conv_fwd", grid=(S // tb,),
        in_specs=[*_conv_specs(tb), _full((32, 1024)), vec, vec, vec, _full((1024, 1024))],
        out_specs=pl.BlockSpec((tb, 1024), lambda i: (i, 0)),
        out_shape=jax.ShapeDtypeStruct((S, 1024), BF16),
        scratch_shapes=[pltpu.VMEM((tb + CONV_HALO, 1024), F32), pltpu.VMEM((tb, 1024), F32)],
        compiler_params=_cp(("parallel",)),
    )(proj, proj, proj, proj, proj, dw, dw_b, ln_g, ln_b, pw)


def _merge_fwd(yp, ya, yc, wbp, wba, wbc, proj, tm=512, tn=1024):
    S = yp.shape[0]

    def body(yp_ref, ya_ref, yc_ref, wp_ref, wa_ref, wc_ref, gp_ref, ga_ref, gc_ref, m_ref, pp_ref, pa_ref, pc_ref):
        pp = _dot(yp_ref[...], wp_ref[...])
        pa = _dot(ya_ref[...], wa_ref[...])
        pc = _dot(yc_ref[...], wc_ref[...])
        m = (_sig(gp_ref[...].astype(F32)) * pp + _sig(ga_ref[...].astype(F32)) * pa
             + _sig(gc_ref[...].astype(F32)) * pc)
        m_ref[...] = m.astype(BF16)
        pp_ref[...], pa_ref[...], pc_ref[...] = pp.astype(BF16), pa.astype(BF16), pc.astype(BF16)

    yb = pl.BlockSpec((tm, 1024), lambda j, i: (i, 0))
    wb = pl.BlockSpec((1024, tn), lambda j, i: (0, j))
    out = pl.BlockSpec((tm, tn), lambda j, i: (i, j))
    return pl.pallas_call(
        body, name="merge_fwd", grid=(D // tn, S // tm),
        in_specs=[yb, yb, yb, wb, wb, wb, _seg2(tm, tn, OFF_GP), _seg2(tm, tn, OFF_GA), _seg2(tm, tn, OFF_GC)],
        out_specs=[out] * 4, out_shape=[jax.ShapeDtypeStruct((S, D), BF16)] * 4,
        compiler_params=_cp(("parallel", "parallel")),
    )(yp, ya, yc, wbp, wba, wbc, proj, proj, proj)


def _out_fwd(x, merged, w_out, gate, tm=512, tn=1024):
    S = x.shape[0]

    def body(x_ref, m_ref, w_ref, g_ref, o_ref):
        o_ref[...] = x_ref[...] + g_ref[...] * _dot(m_ref[...], w_ref[...])

    xb = pl.BlockSpec((tm, tn), lambda j, i: (i, j))
    return pl.pallas_call(
        body, name="out_fwd", grid=(D // tn, S // tm),
        in_specs=[xb, pl.BlockSpec((tm, D), lambda j, i: (i, 0)), pl.BlockSpec((D, tn), lambda j, i: (0, j)),
                  pl.BlockSpec((1, tn), lambda j, i: (0, j))],
        out_specs=xb, out_shape=jax.ShapeDtypeStruct((S, D), F32),
        compiler_params=_cp(("parallel", "parallel")),
    )(x, merged, w_out, gate)


def _final_loss(x, target, final_g, tb=256):
    S = x.shape[0]

    def body(x_ref, t_ref, g_ref, dx_ref, gg_ref, ls_ref):
        i = pl.program_id(0)
        xv, g = x_ref[...], g_ref[...]
        r = lax.rsqrt(jnp.mean(xv * xv, axis=-1, keepdims=True) + EPS)
        xh = xv * r
        e = xh * g - t_ref[...]
        dy = e * (1.0 / D)
        gy = dy * g
        dx_ref[...] = r * (gy - xh * jnp.mean(gy * xh, axis=-1, keepdims=True))
        gg = jnp.sum(dy * xh, axis=0, keepdims=True)
        ls = jnp.sum(e * e, axis=0, keepdims=True) * (0.5 / D)

        @pl.when(i == 0)
        def _():
            gg_ref[...], ls_ref[...] = gg, ls

        @pl.when(i > 0)
        def _():
            gg_ref[...] += gg
            ls_ref[...] += ls

    row = pl.BlockSpec((tb, D), lambda i: (i, 0))
    vec = _full((1, D))
    return pl.pallas_call(
        body, name="final_loss", grid=(S // tb,), in_specs=[row, row, vec], out_specs=[row, vec, vec],
        out_shape=[jax.ShapeDtypeStruct((S, D), F32), jax.ShapeDtypeStruct((1, D), F32),
                   jax.ShapeDtypeStruct((1, D), F32)],
        compiler_params=_cp(("arbitrary",)),
    )(x, target, final_g)


def _out_bwd(dx, gate, w_out, pp, pa, pc, proj, tm=512, tn=1024):
    S = dx.shape[0]

    def body(dx_ref, g_ref, w_ref, pp_ref, pa_ref, pc_ref, gp_ref, ga_ref, gc_ref,
             dpp_ref, dpa_ref, dpc_ref, dgp_ref, dga_ref, dgc_ref):
        dm = _dot_nt((dx_ref[...] * g_ref[...]).astype(BF16), w_ref[...])
        for p_ref, gl_ref, dp_ref, dg_ref in ((pp_ref, gp_ref, dpp_ref, dgp_ref), (pa_ref, ga_ref, dpa_ref, dga_ref),
                                              (pc_ref, gc_ref, dpc_ref, dgc_ref)):
            s = _sig(gl_ref[...].astype(F32))
            dp_ref[...] = (dm * s).astype(BF16)
            dg_ref[...] = (dm * p_ref[...].astype(F32) * s * (1.0 - s)).astype(BF16)

    out = pl.BlockSpec((tm, tn), lambda j, i: (i, j))
    return pl.pallas_call(
        body, name="out_bwd", grid=(D // tn, S // tm),
        in_specs=[pl.BlockSpec((tm, D), lambda j, i: (i, 0)), _full((1, D)), pl.BlockSpec((tn, D), lambda j, i: (j, 0)),
                  out, out, out, _seg2(tm, tn, OFF_GP), _seg2(tm, tn, OFF_GA), _seg2(tm, tn, OFF_GC)],
        out_specs=[out] * 6, out_shape=[jax.ShapeDtypeStruct((S, D), BF16)] * 6,
        compiler_params=_cp(("parallel", "parallel")),
    )(dx, gate, w_out, pp, pa, pc, proj, proj, proj)


def _wout_post(gmat, w_out, gate, tr=256):
    def body(g_ref, w_ref, gate_ref, dw_ref, dg_ref):
        i = pl.program_id(0)
        gm = g_ref[...]
        dw_ref[...] = gm * gate_ref[...]
        part = jnp.sum(gm * w_ref[...].astype(F32), axis=0, keepdims=True)

        @pl.when(i == 0)
        def _():
            dg_ref[...] = part

        @pl.when(i > 0)
        def _():
            dg_ref[...] += part

    row = pl.BlockSpec((tr, D), lambda i: (i, 0))
    return pl.pallas_call(
        body, name="wout_post", grid=(D // tr,), in_specs=[row, row, _full((1, D))], out_specs=[row, _full((1, D))],
        out_shape=[jax.ShapeDtypeStruct((D, D), F32), jax.ShapeDtypeStruct((1, D), F32)],
        compiler_params=_cp(("arbitrary",)),
    )(gmat, w_out, gate)


def _pool_bwd_a(dy, proj, pool_w, pool_scale, dproj, tb=256):
    S = proj.shape[0]

    def body(dy_ref, u_ref, uh_ref, z_ref, w_ref, sc_ref, _, dmix_ref, dz_ref, dsc_ref, dw_ref, ubuf):
        i = pl.program_id(0)
        mixed = [m.astype(BF16) for m in _pool_mix(u_ref, uh_ref, ubuf, i, tb)]
        m = jnp.concatenate([_dot(mixed[g], w_ref[g]) for g in range(4)], axis=1)
        dyv, z, sc = dy_ref[...].astype(F32), z_ref[...].astype(F32), sc_ref[...]
        dyp = dyv * _silu(z)
        dz_ref[...] = (dyv * (m * sc) * _dsilu(z)).astype(BF16)
        dsc = jnp.sum(dyp * m, axis=0, keepdims=True)
        dmm = (dyp * sc).astype(BF16)
        dws = []
        for g in range(4):
            cs = slice(g * POOL_GROUP, (g + 1) * POOL_GROUP)
            dmix_ref[:, cs] = _dot_nt(dmm[:, cs], w_ref[g])
            dws.append(_dot_tn(mixed[g], dmm[:, cs]))

        @pl.when(i == 0)
        def _():
            dsc_ref[...] = dsc
            for g in range(4):
                dw_ref[g] = dws[g]

        @pl.when(i > 0)
        def _():
            dsc_ref[...] += dsc
            for g in range(4):
                dw_ref[g] += dws[g]

    u, uh, z = _pool_specs(tb)
    row = pl.BlockSpec((tb, 1024), lambda i: (i, 0))
    wfull = _full((4, 256, 256))
    return pl.pallas_call(
        body, name="pool_bwd_a", grid=(S // tb,),
        in_specs=[row, u, uh, z, wfull, _full((1, 1024)), ANY],
        out_specs=[row, _seg(tb, 1024, OFF_Z), _full((1, 1024)), wfull],
        out_shape=[jax.ShapeDtypeStruct((S, 1024), F32), jax.ShapeDtypeStruct(dproj.shape, BF16),
                   jax.ShapeDtypeStruct((1, 1024), F32), jax.ShapeDtypeStruct((4, 256, 256), F32)],
        scratch_shapes=[pltpu.VMEM((tb + POOL_HALO, 1024), F32)], input_output_aliases={6: 1},
        compiler_params=_cp(("arbitrary",)),
    )(dy, proj, proj, proj, pool_w, pool_scale, dproj)


def _pool_bwd_b(dmix, dproj, tb=256):
    S = dmix.shape[0]
    H = POOL_HALO
    nb = S // tb

    def body(dm_ref, dh_ref, _, du_ref, ebuf):
        i = pl.program_id(0)
        t = i * tb + lax.broadcasted_iota(jnp.int32, (tb, 1), 0)
        th = (i + 1) * tb + lax.broadcasted_iota(jnp.int32, (H, 1), 0)
        for g, w in enumerate(POOL_WINDOWS):
            cs = slice(g * POOL_GROUP, (g + 1) * POOL_GROUP)
            ebuf[:tb, cs] = dm_ref[:, cs] / jnp.minimum(t + 1, w).astype(F32)
            eh = dh_ref[:, cs] / jnp.minimum(th + 1, w).astype(F32)
            ebuf[tb:, cs] = jnp.where(i < nb - 1, eh, 0.0)
        for g, w in enumerate(POOL_WINDOWS):
            cs = slice(g * POOL_GROUP, (g + 1) * POOL_GROUP)
            acc = ebuf[:tb, cs]
            for j in range(1, w):
                acc = acc + ebuf[pl.ds(j, tb), cs]
            du_ref[:, cs] = (acc - dm_ref[:, cs]).astype(BF16)

    row = pl.BlockSpec((tb, 1024), lambda i: (i, 0))
    nxt = pl.BlockSpec((H, 1024), lambda i: (jnp.minimum((i + 1) * (tb // H), S // H - 1), 0))
    return pl.pallas_call(
        body, name="pool_bwd_b", grid=(nb,), in_specs=[row, nxt, ANY], out_specs=_seg(tb, 1024, OFF_U),
        out_shape=jax.ShapeDtypeStruct(dproj.shape, BF16),
        scratch_shapes=[pltpu.VMEM((tb + H, 1024), F32)], input_output_aliases={2: 0},
        compiler_params=_cp(("parallel",)),
    )(dmix, dmix, dproj)


def _attn_bwd(dy, o, proj, sink_b, dproj):
    S = proj.shape[0]
    TQ, NK = ATTN_TQ, ATTN_TQ + ATTN_BACK
    nb = S // TQ
    G = N_HEADS // N_KV

    QZ = OFF_AZ + 1024 - OFF_Q

    def body(dy_ref, o_ref, q_ref, z_ref, k_ref, kh_ref, v_ref, vh_ref, sink_ref, _, dqz_ref, dk_hbm, dv_hbm, ds_ref,
             dk_acc, dv_acc, kbuf, vbuf):
        i = pl.program_id(0)
        dq_ref, dz_ref = dqz_ref.at[:, :1024], dqz_ref.at[:, OFF_AZ - OFF_Q:]
        dqz_ref[:, 1024:OFF_AZ - OFF_Q] = jnp.zeros((TQ, OFF_AZ - OFF_Q - 1024), BF16)
        _kv_window(k_ref, kh_ref, v_ref, vh_ref, kbuf, vbuf, i)

        @pl.when(i == 0)
        def _():
            dk_acc[...] = jnp.zeros_like(dk_acc)
            dv_acc[...] = jnp.zeros_like(dv_acc)
            ds_ref[...] = jnp.zeros_like(ds_ref)

        start = pl.multiple_of(i * TQ, TQ)
        valid = _attn_mask(i)
        dks, dvs = [], []
        for kh in range(N_KV):
            ks = slice(kh * HEAD_DIM, (kh + 1) * HEAD_DIM)
            kw = kbuf[:, ks]
            vw = vbuf[:, ks]
            dk_sum = jnp.zeros((NK, HEAD_DIM), F32)
            dv_sum = jnp.zeros((NK, HEAD_DIM), F32)
            dqs = []
            for gi in range(G):
                h = kh * G + gi
                hs = slice(h * HEAD_DIM, (h + 1) * HEAD_DIM)
                pn, psink = _attn_probs(q_ref, kw, sink_ref, valid, h)
                ov = o_ref[:, hs].astype(F32)
                do = dy_ref[:, hs].astype(F32) * _silu(z_ref[:, hs].astype(F32))
                delta = jnp.sum(do * ov, axis=-1, keepdims=True)
                dob = do.astype(BF16)
                dp = _dot_nt(dob, vw)
                ds = (pn * (dp - delta)).astype(BF16)
                dsink = -jnp.sum(psink * delta, axis=0, keepdims=True)
                ds_ref[h:h + 1, :] += jnp.broadcast_to(dsink, (1, LANE))
                dqs.append(_dot(ds, kw) * (HEAD_DIM ** -0.5))
                dk_sum = dk_sum + _dot_tn(ds, q_ref[:, hs])
                dv_sum = dv_sum + _dot_tn(pn.astype(BF16), dob)
            gs = slice(kh * G * HEAD_DIM, (kh + 1) * G * HEAD_DIM)
            dq_ref[:, gs] = jnp.concatenate(dqs, axis=1).astype(BF16)
            z = z_ref[:, gs].astype(F32)
            dz_ref[:, gs] = (dy_ref[:, gs].astype(F32) * o_ref[:, gs].astype(F32) * _dsilu(z)).astype(BF16)
            dks.append(dk_sum * (HEAD_DIM ** -0.5))
            dvs.append(dv_sum)
        dk_acc[pl.ds(start, NK), :] += jnp.concatenate(dks, axis=1)
        dv_acc[pl.ds(start, NK), :] += jnp.concatenate(dvs, axis=1)

        @pl.when(i == nb - 1)
        def _():
            pltpu.sync_copy(dk_acc, dk_hbm)
            pltpu.sync_copy(dv_acc, dv_hbm)

    row = pl.BlockSpec((TQ, 1024), lambda i: (i, 0))
    return pl.pallas_call(
        body, name="attn_bwd", grid=(nb,),
        in_specs=[row, row, _seg(TQ, 1024, OFF_Q), _seg(TQ, 1024, OFF_AZ), *_kv_specs(), _full((N_HEADS, LANE)), ANY],
        out_specs=[_seg(TQ, QZ, OFF_Q), ANY, ANY, _full((N_HEADS, LANE))],
        out_shape=[jax.ShapeDtypeStruct(dproj.shape, BF16),
                   jax.ShapeDtypeStruct((S + ATTN_BACK, 256), F32), jax.ShapeDtypeStruct((S + ATTN_BACK, 256), F32),
                   jax.ShapeDtypeStruct((N_HEADS, LANE), F32)],
        scratch_shapes=[pltpu.VMEM((S + ATTN_BACK, 256), F32)] * 2 + [pltpu.VMEM((NK, 256), BF16)] * 2,
        input_output_aliases={9: 0}, compiler_params=_cp(("arbitrary",), VMEM_BIG),
    )(dy, o, proj, proj, proj, proj, proj, proj, sink_b, dproj)


def _conv_bwd_a(dy, proj, dw, dw_b, ln_g, ln_b, pw, dproj, tb=256):
    S = proj.shape[0]
    H = CONV_HALO

    def body(dy_ref, a_ref, ah_ref, b_ref, bh_ref, z_ref, dw_ref, dwb_ref, lg_ref, lb_ref, pw_ref, _,
             dcv_ref, dz_ref, dpw_ref, dlg_ref, dlb_ref, ddwb_ref, ddw_ref, gbuf, ybuf):
        i = pl.program_id(0)
        _conv_glu_dw(a_ref, ah_ref, b_ref, bh_ref, dw_ref, gbuf, ybuf, i, tb)
        lg = lg_ref[...]
        xh, rstd, yn = _layer_norm_fwd(ybuf[...] + dwb_ref[...], lg, lb_ref[...])
        u = _silu(yn).astype(BF16)
        out = _dot(u, pw_ref[...])
        dyv, z = dy_ref[...].astype(F32), z_ref[...].astype(F32)
        dz_ref[...] = (dyv * out * _dsilu(z)).astype(BF16)
        dout = (dyv * _silu(z)).astype(BF16)
        dpw = _dot_tn(u, dout)
        dyn = _dot_nt(dout, pw_ref[...]) * _dsilu(yn)
        dlg = jnp.sum(dyn * xh, axis=0, keepdims=True)
        dlb = jnp.sum(dyn, axis=0, keepdims=True)
        dxh = dyn * lg
        dcv = rstd * (dxh - jnp.mean(dxh, axis=-1, keepdims=True) - xh * jnp.mean(dxh * xh, axis=-1, keepdims=True))
        dcv_ref[...] = dcv
        ddwb = jnp.sum(dcv, axis=0, keepdims=True)
        ybuf[...] = dcv

        @pl.when(i == 0)
        def _():
            dpw_ref[...], dlg_ref[...], dlb_ref[...], ddwb_ref[...] = dpw, dlg, dlb, ddwb
            ddw_ref[...] = jnp.zeros_like(ddw_ref)

        @pl.when(i > 0)
        def _():
            dpw_ref[...] += dpw
            dlg_ref[...] += dlg
            dlb_ref[...] += dlb
            ddwb_ref[...] += ddwb

        _lane_chunks(lambda cs: _taps_reduce(gbuf, ybuf, ddw_ref, tb, CONV_TAPS, cs))

    vec = _full((1, 1024))
    row = pl.BlockSpec((tb, 1024), lambda i: (i, 0))
    big = _full((1024, 1024))
    return pl.pallas_call(
        body, name="conv_bwd_a", grid=(S // tb,),
        in_specs=[row, *_conv_specs(tb), _full((32, 1024)), vec, vec, vec, big, ANY],
        out_specs=[row, _seg(tb, 1024, OFF_CZ), big, vec, vec, vec, _full((32, 1024))],
        out_shape=[jax.ShapeDtypeStruct((S, 1024), F32), jax.ShapeDtypeStruct(dproj.shape, BF16),
                   jax.ShapeDtypeStruct((1024, 1024), F32), jax.ShapeDtypeStruct((1, 1024), F32),
                   jax.ShapeDtypeStruct((1, 1024), F32), jax.ShapeDtypeStruct((1, 1024), F32),
                   jax.ShapeDtypeStruct((32, 1024), F32)],
        scratch_shapes=[pltpu.VMEM((tb + H, 1024), F32), pltpu.VMEM((tb, 1024), F32)],
        input_output_aliases={11: 1}, compiler_params=_cp(("arbitrary",)),
    )(dy, proj, proj, proj, proj, proj, dw, dw_b, ln_g, ln_b, pw, dproj)


def _conv_bwd_b(dcv, proj, dw, dproj, tb=256):
    S = proj.shape[0]
    H = CONV_HALO
    nb = S // tb

    def body(d_ref, dn_ref, a_ref, b_ref, dw_ref, _, dab_ref, dbuf, gbuf):
        i = pl.program_id(0)
        dbuf[:tb, :] = d_ref[...]
        dbuf[tb:, :] = jnp.where(i < nb - 1, dn_ref[...], 0.0)
        _lane_chunks(lambda cs: _taps_apply(dbuf, dw_ref, gbuf, tb, CONV_TAPS_T, cs))
        dg = gbuf[...]
        a, s = a_ref[...].astype(F32), _sig(b_ref[...].astype(F32))
        dab_ref[:, :1024] = (dg * s).astype(BF16)
        dab_ref[:, 1024:] = (dg * a * s * (1.0 - s)).astype(BF16)

    row = pl.BlockSpec((tb, 1024), lambda i: (i, 0))
    nxt = pl.BlockSpec((H, 1024), lambda i: (jnp.minimum((i + 1) * (tb // H), S // H - 1), 0))
    return pl.pallas_call(
        body, name="conv_bwd_b", grid=(nb,),
        in_specs=[row, nxt, _seg(tb, 1024, OFF_CA), _seg(tb, 1024, OFF_CB), _full((32, 1024)), ANY],
        out_specs=_seg(tb, 2048, OFF_CA), out_shape=jax.ShapeDtypeStruct(dproj.shape, BF16),
        scratch_shapes=[pltpu.VMEM((tb + H, 1024), F32), pltpu.VMEM((tb, 1024), F32)],
        input_output_aliases={5: 0}, compiler_params=_cp(("parallel",)),
    )(dcv, dcv, proj, proj, dw, dproj)


def _norm_bwd(dh, x, dx_out, g, scale, tb=256):
    S = x.shape[0]

    def body(dh_ref, x_ref, dxo_ref, g_ref, sc_ref, dx_ref, dsh_ref, da_ref):
        i = pl.program_id(0)
        xv, dhv = x_ref[...], dh_ref[...]
        r = lax.rsqrt(jnp.mean(xv * xv, axis=-1, keepdims=True) + EPS)
        xh = xv * r
        gy = dhv * (g_ref[...] * (1.0 + sc_ref[...]))
        dx_ref[...] = dxo_ref[...] + r * (gy - xh * jnp.mean(gy * xh, axis=-1, keepdims=True))
        dsh = jnp.sum(dhv, axis=0, keepdims=True)
        da = jnp.sum(dhv * xh, axis=0, keepdims=True)

        @pl.when(i == 0)
        def _():
            dsh_ref[...], da_ref[...] = dsh, da

        @pl.when(i > 0)
        def _():
            dsh_ref[...] += dsh
            da_ref[...] += da

    row = pl.BlockSpec((tb, D), lambda i: (i, 0))
    vec = _full((1, D))
    return pl.pallas_call(
        body, name="norm_bwd", grid=(S // tb,), in_specs=[row, row, row, vec, vec], out_specs=[row, vec, vec],
        out_shape=[jax.ShapeDtypeStruct((S, D), F32), jax.ShapeDtypeStruct((1, D), F32), jax.ShapeDtypeStruct((1, D), F32)],
        compiler_params=_cp(("arbitrary",)),
    )(dh, x, dx_out, g, scale)


def _mod_bwd(d_a, norm_g, scale):
    def body(da_ref, g_ref, sc_ref, dg_ref, dsc_ref):
        dg_ref[...] = da_ref[...] * (1.0 + sc_ref[...])
        dsc_ref[...] = da_ref[...] * g_ref[...]

    return pl.pallas_call(body, name="mod_bwd", out_shape=[jax.ShapeDtypeStruct(d_a.shape, F32)] * 2)(d_a, norm_g, scale)


def _reduce_adamw(parts, w, m, v, name):
    L, rows, C = w.shape
    tr = rows if rows % 64 else 64

    def body(*refs):
        p_refs, (w_ref, m_ref, v_ref, g_ref, dl_ref, m2_ref, v2_ref) = refs[:L * N_DEV], refs[L * N_DEV:]
        for l in range(L):
            g = p_refs[l * N_DEV][0].astype(F32)
            for k in range(1, N_DEV):
                g = g + p_refs[l * N_DEV + k][0].astype(F32)
            g_ref[l] = g
            dl_ref[l], m2_ref[l], v2_ref[l] = _adam_math(g, w_ref[l], m_ref[l], v_ref[l])

    slot = lambda k: pl.BlockSpec((1, tr, C), lambda i: (k, i, 0))
    blk = pl.BlockSpec((L, tr, C), lambda i: (0, i, 0))
    return pl.pallas_call(
        body, name=name, grid=(rows // tr,), in_specs=[slot(k) for _ in range(L) for k in range(N_DEV)] + [blk] * 3,
        out_specs=[blk] * 4, out_shape=[jax.ShapeDtypeStruct((L, rows, C), F32)] * 4,
        compiler_params=_cp(("parallel",)),
    )(*[p for p in parts for _ in range(N_DEV)], w, m, v)


def _small_final(parts, w, m, v):
    R = w.shape[0]

    def body(p_ref, w_ref, m_ref, v_ref, g_ref, dl_ref, m2_ref, v2_ref):
        g = p_ref[0]
        for k in range(1, N_DEV):
            g = g + p_ref[k]
        delta, m2, v2 = _adam_math(g, w_ref[...], m_ref[...], v_ref[...])
        g_ref[...], dl_ref[...], m2_ref[...], v2_ref[...] = g, delta, m2, v2

    return pl.pallas_call(body, name="small_final", out_shape=[jax.ShapeDtypeStruct((R, LANE), F32)] * 4)(parts, w, m, v)


def _layer_fwd(x, mod, small, W, more_w):
    shift, scale, gate = mod
    h = _norm_mod(x, small["norm_g"], scale, shift)
    proj = _mm(h, W["w_in_t"], BF16, 512, 1536, name="proj_mm", nt=True)
    W.update(more_w(proj))
    y_pool = _pool_fwd(proj, W["pool_w"], small["pool_scale"])
    o, y_attn = _attn_fwd(proj, small["sink_b"])
    y_conv = _conv_fwd(proj, W["conv_dw"], small["conv_dw_b"], small["conv_ln_g"], small["conv_ln_b"], W["conv_pw"])
    merged, pp, pa, pc = _merge_fwd(y_pool, y_attn, y_conv, W["wbp"], W["wba"], W["wbc"], proj)
    x_new = _out_fwd(x, merged, W["w_out"], gate)
    stash = dict(x=x, h=h, proj=proj, o=o, y_pool=y_pool, y_attn=y_attn, y_conv=y_conv,
                 merged=merged, pp=pp, pa=pa, pc=pc)
    return x_new, stash


def _layer_bwd(dx, st, mod, small, W, put):
    shift, scale, gate = mod
    proj = st["proj"]
    gmat = _mm_tn(st["merged"], dx, 1024, 1024, 2048, name="wout_tn", vmem=VMEM_BIG)
    d_w_out, d_gate = _wout_post(gmat, W["w_out"], gate)
    dpp, dpa, dpc, dgp, dga, dgc = _out_bwd(dx, gate, W["w_out"], st["pp"], st["pa"], st["pc"], proj)
    dy_pool = _mm(dpp, W["wbp"], BF16, 512, 1024, name="branch_bwd_mm", nt=True)
    dy_attn = _mm(dpa, W["wba"], BF16, 512, 1024, name="branch_bwd_mm", nt=True)
    dy_conv = _mm(dpc, W["wbc"], BF16, 512, 1024, name="branch_bwd_mm", nt=True)
    d_wbp = _mm_tn(st["y_pool"], dpp, 1024, 1024, 4096, name="branch_tn", vmem=VMEM_BIG)
    d_wba = _mm_tn(st["y_attn"], dpa, 1024, 1024, 4096, name="branch_tn", vmem=VMEM_BIG)
    d_wbc = _mm_tn(st["y_conv"], dpc, 1024, 1024, 4096, name="branch_tn", vmem=VMEM_BIG)

    dproj = lax.empty(proj.shape, BF16)
    dmix, dproj, d_pool_scale, d_pool_w = _pool_bwd_a(dy_pool, proj, W["pool_w"], small["pool_scale"], dproj)
    dproj = _pool_bwd_b(dmix, dproj)
    dproj, dk, dv, d_sink = _attn_bwd(dy_attn, st["o"], proj, small["sink_b"], dproj)
    dcv, dproj, d_pw, d_ln_g, d_ln_b, d_dw_b, d_dw = _conv_bwd_a(
        dy_conv, proj, W["conv_dw"], small["conv_dw_b"], small["conv_ln_g"], small["conv_ln_b"], W["conv_pw"], dproj)
    dproj = _conv_bwd_b(dcv, proj, W["conv_dw"], dproj)
    for piece, off in ((dk[ATTN_BACK:], OFF_K), (dv[ATTN_BACK:], OFF_V), (dgp, OFF_GP), (dga, OFF_GA), (dgc, OFF_GC)):
        dproj = lax.dynamic_update_slice(dproj, piece.astype(BF16), (0, off))

    tok = put(dict(pool_w=d_pool_w, conv_dw=d_dw[:CONV_K], conv_pw=d_pw, wbp=d_wbp, wba=d_wba, wbc=d_wbc, w_out=d_w_out))
    d_w_in_t = _mm_tn(dproj, st["h"], 768, 2048, 2048, name="win_tn", vmem=VMEM_BIG, after=tok, out_dtype=BF16)
    tok = put(dict(w_in=d_w_in_t))
    dh = _mm(dproj, W["w_in_t"], F32, 1024, 2048, 2304, name="dh_mm", vmem=VMEM_BIG, after=tok)
    dx_in, d_shift, d_a = _norm_bwd(dh, st["x"], dx, small["norm_g"], scale)
    sm = dict(d_a=d_a, d_shift=d_shift, d_gate=d_gate, pool_scale=d_pool_scale, attn_sink=d_sink[:, 0],
              conv_dw_b=d_dw_b, conv_ln_g=d_ln_g, conv_ln_b=d_ln_b)
    return dx_in, sm


def _local_step(x, target, mods, smalls, get_w, final_g, put_g, end_layer):
    stashes, Ws = [], []
    for l in range(DEPTH):
        w, more_w = get_w(l, x)
        Ws.append(w)
        x, st = _layer_fwd(x, mods[l], smalls[l], w, more_w)
        stashes.append(st)
    dx, d_final_g, loss_lanes = _final_loss(x, target, final_g)
    sms = [None] * DEPTH
    for l in reversed(range(DEPTH)):
        dx, sms[l] = _layer_bwd(dx, stashes[l], mods[l], smalls[l], Ws[l], functools.partial(put_g, l))
        end_layer(l, dx)
    return loss_lanes, dx, d_final_g, sms


BIG = ("w_in", "pool_w", "conv_pw", "wbp", "wba", "wbc", "w_out")
GRADS = BIG + ("conv_dw",)


def _full_w_in(g):
    return dict(w_in_t=g.reshape(IN_WIDTH, D))


def _full_weights(g, conv_dw):
    cols = lambda a: jnp.transpose(a, (1, 0, 2)).reshape(a.shape[1], -1)
    pool_w = jnp.transpose(g["pool_w"], (1, 0, 2, 3)).reshape(4, 256, 256)
    conv_pw = g["conv_pw"].reshape(1024, 1024)
    wbp, wba, wbc = cols(g["wbp"]), cols(g["wba"]), cols(g["wbc"])
    w_out = g["w_out"].reshape(D, D)
    conv_dw = jnp.pad(cols(conv_dw), ((0, 32 - CONV_K), (0, 0)))
    return dict(pool_w=pool_w, conv_pw=conv_pw, wbp=wbp, wba=wba, wbc=wbc, w_out=w_out, conv_dw=conv_dw)


def _pieces(name, g):
    if name == "w_in":
        return g.reshape(8, IN_WIDTH // 8, D)
    if name == "pool_w":
        return jnp.transpose(g.reshape(4, 8, 32, 256), (1, 0, 2, 3))
    if name == "conv_dw":
        return jnp.transpose(g.reshape(CONV_K, 8, 128), (1, 0, 2))
    if name == "conv_pw":
        return g.reshape(8, 128, 1024)
    if name in ("wbp", "wba", "wbc"):
        return jnp.transpose(g.reshape(1024, 8, 256), (1, 0, 2))
    return g.reshape(8, 256, D)


def _pack_small(items, rows):
    flat = jnp.concatenate([a.reshape(-1).astype(F32) for a in items])
    return jnp.pad(flat, (0, rows * LANE - flat.shape[0])).reshape(rows, LANE)


def _unpack_small(packed, shapes):
    flat, out, off = packed.reshape(-1), [], 0
    for s in shapes:
        n = 1
        for d in s:
            n *= d
        out.append(flat[off:off + n].reshape(s))
        off += n
    return out


def kernel(x, c, norm_g, w_ada, b_ada, w_in, pool_w, pool_scale, attn_sink, conv_dw, conv_dw_b, conv_ln_g, conv_ln_b, conv_pw, w_branch_pool, w_branch_attn, w_branch_conv, w_out, final_g, loss_target, m_norm_g, m_w_ada, m_b_ada, m_w_in, m_pool_w, m_pool_scale, m_attn_sink, m_conv_dw, m_conv_dw_b, m_conv_ln_g, m_conv_ln_b, m_conv_pw, m_w_branch_pool, m_w_branch_attn, m_w_branch_conv, m_w_out, m_final_g, v_norm_g, v_w_ada, v_b_ada, v_w_in, v_pool_w, v_pool_scale, v_attn_sink, v_conv_dw, v_conv_dw_b, v_conv_ln_g, v_conv_ln_b, v_conv_pw, v_w_branch_pool, v_w_branch_attn, v_w_branch_conv, v_w_out, v_final_g):
    L = DEPTH
    me = 4 * lax.axis_index("x") + 2 * lax.axis_index("y") + lax.axis_index("c")
    tr = lambda a: jnp.swapaxes(a, 1, 2)
    shards = dict(w_in=tr(w_in), pool_w=pool_w, conv_dw=conv_dw, conv_pw=conv_pw, wbp=w_branch_pool, wba=w_branch_attn,
                  wbc=w_branch_conv, w_out=w_out)
    moms = dict(w_in=(tr(m_w_in), tr(v_w_in)), pool_w=(m_pool_w, v_pool_w), conv_dw=(m_conv_dw, v_conv_dw),
                conv_pw=(m_conv_pw, v_conv_pw), wbp=(m_w_branch_pool, v_w_branch_pool),
                wba=(m_w_branch_attn, v_w_branch_attn), wbc=(m_w_branch_conv, v_w_branch_conv), w_out=(m_w_out, v_w_out))

    n_cd = L * CONV_K * 128
    first = _all_gather([_pack_small([c, conv_dw], 144)], "gather_c")[0].reshape(N_DEV, -1)
    c_all = first[:, :D]
    conv_dw_all = first[:, D:D + n_cd].reshape(N_DEV, L, CONV_K, 128)

    mod_part = _mod_fwd(c_all, w_ada)
    mod_all = _all_gather([mod_part.reshape(-1, LANE)], "gather_mod")[0].reshape(N_DEV, L, N_DEV, -1)
    mod = jnp.transpose(lax.dynamic_index_in_dim(mod_all, me, axis=2, keepdims=False), (1, 0, 2)).reshape(L, 3 * D)
    mod = mod + b_ada
    mods = [(mod[l:l + 1, :D], mod[l:l + 1, D:2 * D], mod[l:l + 1, 2 * D:]) for l in range(L)]

    w_in_0 = _all_gather([(shards["w_in"][0] + mod[0, 0] * 0.0).astype(BF16)], "gather_w_in_0")[0]
    gathers, tok = [], w_in_0[0, 0, 0].astype(F32) * 0.0
    for l in range(L):
        first = lambda a: (a + tok).astype(BF16)
        mine = [first(shards["w_in"][l])], [first(shards[k][l]) if k == BIG[1] else shards[k][l].astype(BF16) for k in BIG[1:]]
        started = [None if (l, n) == (0, 0) else _spread_start(v, False, f"gather_start_{l}_{n}") for n, v in enumerate(mine)]
        gathers.append((mine, started))
        tok = sum(st[-1][0, 0] for st in started if st is not None)
    mods[0] = (mods[0][0] + tok,) + mods[0][1:]

    sink_b = jnp.broadcast_to(attn_sink[:, :, None], (L, N_HEADS, LANE))
    smalls = [dict(norm_g=norm_g[l:l + 1], pool_scale=pool_scale[l:l + 1], sink_b=sink_b[l], conv_dw_b=conv_dw_b[l:l + 1],
                   conv_ln_g=conv_ln_g[l:l + 1], conv_ln_b=conv_ln_b[l:l + 1]) for l in range(L)]

    def with_mine(landed, mine):
        return lax.dynamic_update_slice(landed, mine, (me,) + (0,) * (landed.ndim - 1))

    def get_w(l, x_in):
        mine, started = gathers[l]
        if l == 0:
            w_in = w_in_0
        else:
            sent, landed = _spread_wait(started[0], x_in, False, f"gather_wait_{l}_0")
            w_in = with_mine(landed[0], sent[0][None])

        def more_w(proj):
            sent, landed = _spread_wait(started[1], proj, False, f"gather_wait_{l}_1")
            g = {k: with_mine(a, b[None]) for k, a, b in zip(BIG[1:], landed, sent)}
            return _full_weights(g, conv_dw_all[:, l])

        return _full_w_in(w_in), more_w

    pending, parts = {l: [] for l in range(L)}, [dict() for _ in range(L)]

    def put_g(l, grads):
        names = tuple(k for k in GRADS if k in grads)
        pieces = [_pieces(k, grads[k]).astype(BF16) for k in names]
        started = _spread_start(pieces, True, f"scatter_start_{l}_{len(pending[l])}")
        pending[l].append((names, started))
        return started[-1][0:1, 0:1]

    def finish(l, after):
        for n, (names, started) in enumerate(pending.pop(l)):
            sent, landed = _spread_wait(started, after, True, f"scatter_wait_{l}_{n}")
            for k, a, b in zip(names, landed, sent):
                parts[l][k] = with_mine(a, lax.dynamic_slice_in_dim(b, me, 1, axis=0))

    def end_layer(l, dx):
        if l + 1 in pending:
            finish(l + 1, dx)

    loss_lanes, grad_x, d_final_g, sms = _local_step(x[0], loss_target[0], mods, smalls, get_w, final_g.reshape(1, D),
                                                     put_g, end_layer)
    finish(0, grad_x)

    stack = lambda k: jnp.concatenate([sms[l][k].reshape(1, -1) for l in range(L)], axis=0)
    scale_all = jnp.concatenate([mods[l][1] for l in range(L)], axis=0)
    d_norm_g, d_scale = _mod_bwd(stack("d_a"), norm_g, scale_all)
    dmod = jnp.concatenate([stack("d_shift"), d_scale, stack("d_gate")], axis=1)
    small_names = ("norm_g", "b_ada", "pool_scale", "attn_sink", "conv_dw_b", "conv_ln_g", "conv_ln_b", "final_g")
    small_g = (d_norm_g, dmod, stack("pool_scale"), stack("attn_sink"), stack("conv_dw_b"), stack("conv_ln_g"),
               stack("conv_ln_b"), d_final_g.reshape(D))
    small_w = (norm_g, b_ada, pool_scale, attn_sink, conv_dw_b, conv_ln_g, conv_ln_b, final_g)
    small_m = (m_norm_g, m_b_ada, m_pool_scale, m_attn_sink, m_conv_dw_b, m_conv_ln_g, m_conv_ln_b, m_final_g)
    small_v = (v_norm_g, v_b_ada, v_pool_scale, v_attn_sink, v_conv_dw_b, v_conv_ln_g, v_conv_ln_b, v_final_g)
    shapes = [a.shape for a in small_w] + [(D,)]
    n_small = sum(a.size for a in small_w) + D
    R = -(-n_small // (8 * LANE)) * 8
    zero = jnp.zeros((D,), F32)
    small_parts = _all_gather([_pack_small(small_g + (loss_lanes,), R)], "gather_small")[0]
    sg, sd, sm2, sv2 = _small_final(small_parts, _pack_small(small_w + (zero,), R), _pack_small(small_m + (zero,), R),
                                    _pack_small(small_v + (zero + 1.0,), R))
    sg, sd, sm2, sv2 = (_unpack_small(a, shapes) for a in (sg, sd, sm2, sv2))
    loss = jnp.sum(sg[-1])
    res = {n: (sg[i], sd[i], sm2[i], sv2[i]) for i, n in enumerate(small_names)}

    off = norm_g.size
    dmod_all = small_parts.reshape(N_DEV, -1)[:, off:off + L * 3 * D].reshape(N_DEV, L, 3 * D)
    dmod_mine = jnp.transpose(lax.dynamic_slice_in_dim(dmod_all, me * (3 * D // N_DEV), 3 * D // N_DEV, axis=2), (1, 0, 2))
    res["w_ada"] = _wada_bwd(c_all.T, dmod_mine, w_ada, m_w_ada, v_w_ada)

    for k in GRADS:
        shp = shards[k].shape
        to3d = lambda a: a.reshape(L, -1, shp[-1])
        out = _reduce_adamw([parts[l][k].reshape(N_DEV, -1, shp[-1]) for l in range(L)], to3d(shards[k]),
                            to3d(moms[k][0]), to3d(moms[k][1]), "adamw_" + k)
        res[k] = tuple(a.reshape(shp) for a in out)
    res["w_in"] = tuple(tr(a) for a in res["w_in"])

    order = ("norm_g", "w_ada", "b_ada", "w_in", "pool_w", "pool_scale", "attn_sink", "conv_dw", "conv_dw_b", "conv_ln_g",
             "conv_ln_b", "conv_pw", "wbp", "wba", "wbc", "w_out", "final_g")
    outs = [loss, grad_x[None]]
    for j in range(4):
        outs += [res[n][j] for n in order]
    return tuple(outs)
```
